```python
import math
import jax, jax.numpy as jnp
from jax import lax
import numpy as np

D_MODEL = 1024
BATCH = 8
SEQ = 2048
DEPTH = 1
DEC_BATCH = 128
DEC_SEQ = 8
PAST_LEN = 16384
PAGE_SIZE = 128

PLE_DIM = 256
MIX_WIDTH = D_MODEL
MLSTM_WIDTH = MIX_WIDTH // 2
MLSTM_HEADS = 4
MLSTM_DH = MLSTM_WIDTH // MLSTM_HEADS
MLSTM_CHUNK = 64
POOL_WIDTH = MIX_WIDTH - MLSTM_WIDTH
POOL_WINDOWS = (2, 4, 8, 16)
POOL_GROUPS = len(POOL_WINDOWS)
POOL_GDIM = POOL_WIDTH // POOL_GROUPS
POOL_BUF = max(POOL_WINDOWS) - 1
IN_COLS = 4 * MLSTM_WIDTH + 2 * MLSTM_HEADS + POOL_WIDTH
N_EXPERTS = 64
TOP_K = 8
N_EXPERT_GROUPS = 8
TOPK_GROUPS = 4
EXPERT_FF = 256
SHARED_FF = 256
ROUTED_SCALE = 2.5
MOE_BLOCK = 1024
NORM_EPS = 1e-6

kernel_name = "hymba_mlstm_pool_moe_ple_step"


def _rmsnorm(x, g):
    xf = x.astype(jnp.float32)
    y = xf * lax.rsqrt(jnp.mean(xf * xf, axis=-1, keepdims=True) + NORM_EPS)
    return (y * g.astype(jnp.float32)).astype(x.dtype)


def _mlstm(q, k, v, i_pre, log_f, C0, n0, m0):
    B, H, T, DH = q.shape
    L = math.gcd(T, MLSTM_CHUNK)
    NC = T // L
    chunk4 = lambda t: t.reshape(B, H, NC, L, DH).transpose(2, 0, 1, 3, 4)
    chunk3 = lambda t: t.reshape(B, H, NC, L).transpose(2, 0, 1, 3)
    causal = jnp.tril(jnp.ones((L, L), dtype=bool))

    def step(carry, xs):
        C, n, m = carry
        qc, kc, vc, ic, fc = xs
        b = jnp.cumsum(fc, axis=-1)
        d = b[..., :, None] - b[..., None, :] + ic[..., None, :]
        d = jnp.where(causal, d, -jnp.inf)
        inter = b + m[..., None]
        m_t = jnp.maximum(inter, jnp.max(d, axis=-1))
        w_inter = jnp.exp(inter - m_t)
        s = jnp.einsum('bhtd,bhsd->bhts', qc, kc) * jnp.exp(d - m_t[..., None])
        num = w_inter[..., None] * jnp.einsum('bhtd,bhde->bhte', qc, C) + jnp.einsum('bhts,bhse->bhte', s, vc)
        nq = w_inter * jnp.einsum('bhtd,bhd->bht', qc, n) + jnp.sum(s, axis=-1)
        h = num / jnp.maximum(jnp.abs(nq), jnp.exp(-m_t))[..., None]
        b_last = b[..., -1]
        ls = b_last[..., None] - b + ic
        m_new = jnp.maximum(b_last + m, jnp.max(ls, axis=-1))
        fw = jnp.exp(b_last + m - m_new)
        iw = jnp.exp(ls - m_new[..., None])
        C_new = fw[..., None, None] * C + jnp.einsum('bhs,bhsd,bhse->bhde', iw, kc, vc)
        n_new = fw[..., None] * n + jnp.einsum('bhs,bhsd->bhd', iw, kc)
        return (C_new, n_new, m_new), h

    (C, n, m), hs = lax.scan(step, (C0, n0, m0),
                             (chunk4(q), chunk4(k), chunk4(v), chunk3(i_pre), chunk3(log_f)))
    h = hs.transpose(1, 0, 3, 2, 4).reshape(B, T, H, DH)
    return h, C, n, m


def _pool_mixer(u, buf, start_pos, w_pool, pool_scale):
    B, T, C = u.shape
    R = POOL_BUF
    ext = jnp.concatenate([buf, u], axis=1)
    cs = jnp.concatenate([jnp.zeros((B, 1, C), jnp.float32), jnp.cumsum(ext, axis=1)], axis=1)
    pos = start_pos + jnp.arange(T, dtype=jnp.int32)
    outs = []
    for g, w in enumerate(POOL_WINDOWS):
        sl = slice(g * POOL_GDIM, (g + 1) * POOL_GDIM)
        win_sum = cs[:, R + 1:R + 1 + T, sl] - cs[:, R + 1 - w:R + 1 - w + T, sl]
        cnt = jnp.minimum(pos + 1, w).astype(jnp.float32)
        outs.append(win_sum / cnt[None, :, None] - u[..., sl])
    pooled = jnp.stack(outs, axis=2)
    mixed = jnp.einsum('btgc,gcd->btgd', pooled, w_pool.astype(jnp.float32)).reshape(B, T, C)
    return mixed * pool_scale.astype(jnp.float32), ext[:, -R:]


def _moe(x, router_w, router_bias, ex_w1, ex_w3, ex_w2, sh_w1, sh_w3, sh_w2):
    N, D = x.shape
    E, G = N_EXPERTS, N_EXPERT_GROUPS
    scores = jax.nn.sigmoid(jnp.einsum('nd,de->ne', x.astype(jnp.float32), router_w.astype(jnp.float32)))
    biased = scores + router_bias.astype(jnp.float32)
    group_score = jnp.sum(lax.top_k(biased.reshape(N, G, E // G), 2)[0], axis=-1)
    _, gidx = lax.top_k(group_score, TOPK_GROUPS)
    gmask = jnp.sum(jax.nn.one_hot(gidx, G, dtype=jnp.float32), axis=1)
    emask = jnp.repeat(gmask, E // G, axis=1)
    _, eidx = lax.top_k(jnp.where(emask > 0, biased, -jnp.inf), TOP_K)
    sel = jnp.take_along_axis(scores, eidx, axis=-1)
    wts = sel / jnp.sum(sel, axis=-1, keepdims=True) * ROUTED_SCALE
    gates = jnp.sum(jax.nn.one_hot(eidx, E, dtype=jnp.float32) * wts[..., None], axis=1).astype(x.dtype)
    blk = math.gcd(N, MOE_BLOCK)

    def block(args):
        xb, gb = args
        hb = jax.nn.silu(jnp.einsum('nd,edf->nef', xb, ex_w1)) * jnp.einsum('nd,edf->nef', xb, ex_w3)
        return jnp.einsum('nef,efd->nd', hb * gb[..., None], ex_w2)

    routed = lax.map(block, (x.reshape(N // blk, blk, D), gates.reshape(N // blk, blk, E))).reshape(N, D)
    shared = jnp.einsum('nf,fd->nd', jax.nn.silu(x @ sh_w1) * (x @ sh_w3), sh_w2)
    return routed + shared


def _layer(x, p, C0, n0, m0, buf0, start_pos, norm_mix_g, w_in, b_igate, b_fgate, head_norm_g,
           w_pool, pool_scale, w_out, norm_ffn_g, router_w, router_bias, ex_w1, ex_w3, ex_w2,
           sh_w1, sh_w3, sh_w2, norm_ple_g, w_ple_gate, w_ple_proj):
    B, T, _ = x.shape
    H, DH, W = MLSTM_HEADS, MLSTM_DH, MLSTM_WIDTH
    f32 = jnp.float32
    h = _rmsnorm(x, norm_mix_g)
    z = jnp.einsum('btd,dc->btc', h, w_in).astype(f32)
    q, k, v, o = (z[..., i * W:(i + 1) * W] for i in range(4))
    g0 = 4 * W
    i_pre = z[..., g0:g0 + H] + b_igate.astype(f32)
    log_f = jax.nn.log_sigmoid(z[..., g0 + H:g0 + 2 * H] + b_fgate.astype(f32))
    u = z[..., g0 + 2 * H:]
    heads = lambda t: t.reshape(B, T, H, DH).transpose(0, 2, 1, 3)
    h_a, C, n, m = _mlstm(heads(q) * DH ** -0.5, heads(k), heads(v),
                          i_pre.transpose(0, 2, 1), log_f.transpose(0, 2, 1),
                          C0.astype(f32), n0.astype(f32), m0.astype(f32))
    h_a = h_a * lax.rsqrt(jnp.mean(h_a * h_a, axis=-1, keepdims=True) + NORM_EPS)
    h_a = h_a.reshape(B, T, W) * head_norm_g.astype(f32) * jax.nn.sigmoid(o)
    h_b, buf = _pool_mixer(u, buf0.astype(f32), start_pos, w_pool, pool_scale)
    mix = jnp.concatenate([h_a, h_b], axis=-1).astype(x.dtype)
    x = x + jnp.einsum('btc,cd->btd', mix, w_out)
    y = _moe(_rmsnorm(x, norm_ffn_g).reshape(B * T, -1), router_w, router_bias,
             ex_w1, ex_w3, ex_w2, sh_w1, sh_w3, sh_w2).reshape(B, T, -1)
    x = x + y
    gate = jax.nn.sigmoid(jnp.einsum('btd,de->bte', _rmsnorm(x, norm_ple_g), w_ple_gate))
    x = x + jnp.einsum('btp,pd->btd', p, w_ple_proj) * gate
    return x, C, n, m, buf


def setup_inputs(seed: int = 0) -> dict:
    key = jax.random.key(seed)
    ks = jax.random.split(key, 32)
    f32 = jnp.float32
    H, DH = MLSTM_HEADS, MLSTM_DH

    def nrm(k, shape, scale):
        return jax.random.normal(k, shape, f32) * scale

    def gain(k, shape):
        return 1.0 + 0.02 * jax.random.normal(k, shape, f32)

    return {
        "x_prompt": nrm(ks[0], (BATCH, SEQ, D_MODEL), 1.0),
        "x_sample": nrm(ks[1], (DEC_BATCH, DEC_SEQ, D_MODEL), 1.0),
        "p_prompt": nrm(ks[2], (DEPTH, BATCH, SEQ, PLE_DIM), 1.0),
        "p_sample": nrm(ks[3], (DEPTH, DEC_BATCH, DEC_SEQ, PLE_DIM), 1.0),
        "state_C": nrm(ks[4], (DEPTH, DEC_BATCH, H, DH, DH), DH ** -0.5),
        "state_n": nrm(ks[5], (DEPTH, DEC_BATCH, H, DH), 1.0),
        "state_m": nrm(ks[6], (DEPTH, DEC_BATCH, H), 1.0),
        "state_pool": nrm(ks[7], (DEPTH, DEC_BATCH, POOL_BUF, POOL_WIDTH), 1.0),
        "norm_mix_g": gain(ks[8], (DEPTH, D_MODEL)),
        "w_in": nrm(ks[9], (DEPTH, D_MODEL, IN_COLS), D_MODEL ** -0.5),
        "b_igate": nrm(ks[10], (DEPTH, H), 0.1),
        "b_fgate": jnp.linspace(3.0, 6.0, H, dtype=f32)[None, :] + nrm(ks[11], (DEPTH, H), 0.1),
        "head_norm_g": gain(ks[12], (DEPTH, MLSTM_WIDTH)),
        "w_pool": nrm(ks[13], (DEPTH, POOL_GROUPS, POOL_GDIM, POOL_GDIM), POOL_GDIM ** -0.5),
        "pool_scale": gain(ks[14], (DEPTH, POOL_WIDTH)),
        "w_out": nrm(ks[15], (DEPTH, MIX_WIDTH, D_MODEL), MIX_WIDTH ** -0.5),
        "norm_ffn_g": gain(ks[16], (DEPTH, D_MODEL)),
        "router_w": nrm(ks[17], (DEPTH, D_MODEL, N_EXPERTS), D_MODEL ** -0.5),
        "router_bias": nrm(ks[18], (DEPTH, N_EXPERTS), 0.01),
        "ex_w1": nrm(ks[19], (DEPTH, N_EXPERTS, D_MODEL, EXPERT_FF), D_MODEL ** -0.5),
        "ex_w3": nrm(ks[20], (DEPTH, N_EXPERTS, D_MODEL, EXPERT_FF), D_MODEL ** -0.5),
        "ex_w2": nrm(ks[21], (DEPTH, N_EXPERTS, EXPERT_FF, D_MODEL), EXPERT_FF ** -0.5),
        "sh_w1": nrm(ks[22], (DEPTH, D_MODEL, SHARED_FF), D_MODEL ** -0.5),
        "sh_w3": nrm(ks[23], (DEPTH, D_MODEL, SHARED_FF), D_MODEL ** -0.5),
        "sh_w2": nrm(ks[24], (DEPTH, SHARED_FF, D_MODEL), SHARED_FF ** -0.5),
        "norm_ple_g": gain(ks[25], (DEPTH, D_MODEL)),
        "w_ple_gate": nrm(ks[26], (DEPTH, D_MODEL, D_MODEL), D_MODEL ** -0.5),
        "w_ple_proj": nrm(ks[27], (DEPTH, PLE_DIM, D_MODEL), PLE_DIM ** -0.5),
        "final_norm_g": gain(ks[28], (D_MODEL,)),
    }


def reference(x_prompt, x_sample, p_prompt, p_sample, state_C, state_n, state_m, state_pool,
              norm_mix_g, w_in, b_igate, b_fgate, head_norm_g, w_pool, pool_scale, w_out,
              norm_ffn_g, router_w, router_bias, ex_w1, ex_w3, ex_w2, sh_w1, sh_w3, sh_w2,
              norm_ple_g, w_ple_gate, w_ple_proj, final_norm_g):
    f32 = jnp.float32
    H, DH = MLSTM_HEADS, MLSTM_DH
    B = x_prompt.shape[0]
    xp, xs = x_prompt, x_sample
    Cp, Np, Mp, Bp = [], [], [], []
    Cs, Ns, Ms, Bs = [], [], [], []
    for l in range(DEPTH):
        lw = (norm_mix_g[l], w_in[l], b_igate[l], b_fgate[l], head_norm_g[l], w_pool[l], pool_scale[l],
              w_out[l], norm_ffn_g[l], router_w[l], router_bias[l], ex_w1[l], ex_w3[l], ex_w2[l],
              sh_w1[l], sh_w3[l], sh_w2[l], norm_ple_g[l], w_ple_gate[l], w_ple_proj[l])
        xp, c, n, m, bf = _layer(xp, p_prompt[l], jnp.zeros((B, H, DH, DH), f32), jnp.zeros((B, H, DH), f32),
                                 jnp.zeros((B, H), f32), jnp.zeros((B, POOL_BUF, POOL_WIDTH), f32), 0, *lw)
        Cp.append(c); Np.append(n); Mp.append(m); Bp.append(bf)
        xs, c, n, m, bf = _layer(xs, p_sample[l], state_C[l], state_n[l], state_m[l], state_pool[l],
                                 PAST_LEN, *lw)
        Cs.append(c); Ns.append(n); Ms.append(m); Bs.append(bf)
    y_prompt = _rmsnorm(xp, final_norm_g)
    y_sample = _rmsnorm(xs, final_norm_g)
    return (y_prompt, y_sample, jnp.stack(Cp), jnp.stack(Np), jnp.stack(Mp), jnp.stack(Bp),
            jnp.stack(Cs), jnp.stack(Ns), jnp.stack(Ms), jnp.stack(Bs))
```

```python
import functools

import jax
import jax.numpy as jnp
from jax import lax
from jax.experimental import pallas as pl
from jax.experimental.pallas import tpu as pltpu

D_MODEL = 1024
HEADS = 4
DH = 128
MLSTM_WIDTH = HEADS * DH
POOL_WIDTH = 512
POOL_WINDOWS = (2, 4, 8, 16)
POOL_GDIM = 128
POOL_BUF = 15
POOL_PAD = 16
N_EXPERTS = 64
TOP_K = 8
N_GROUPS = 8
GROUP_SIZE = N_EXPERTS // N_GROUPS
TOPK_GROUPS = 4
EXPERT_FF = 256
ROUTED_SCALE = 2.5
NORM_EPS = 1e-6
PLE_DIM = 256
PAST_LEN = 16384

COL_Q, COL_K, COL_V, COL_O, COL_U, COL_G = 0, 512, 1024, 1536, 2048, 2560
IN_COLS_PAD = 2688

VMEM_LIMIT = 56 * 1024 * 1024
F32 = jnp.float32
BF16 = jnp.bfloat16
NEG_INF = float("-inf")


def _rms(x, g):
    return x * lax.rsqrt(jnp.mean(x * x, axis=-1, keepdims=True) + NORM_EPS) * g


def _log_sigmoid(x):
    return jnp.minimum(x, 0.0) - jnp.log1p(jnp.exp(-jnp.abs(x)))


def _dot(a, b):
    return jnp.dot(a, b, preferred_element_type=F32)


def _mlstm_chunk(q, k, v, b_col, i_col, C, n, m, causal, eye):
    L = q.shape[0]
    r_col = i_col - b_col
    r_row = jnp.sum(jnp.where(eye, r_col, 0.0), axis=0, keepdims=True)
    d = jnp.where(causal, b_col + r_row, NEG_INF)
    inter = b_col + m
    m_t = jnp.maximum(inter, jnp.max(d, axis=-1, keepdims=True))
    w_inter = jnp.exp(inter - m_t)
    qb, kb, vb = q.astype(BF16), k.astype(BF16), v.astype(BF16)
    qk = lax.dot_general(qb, kb, (((1,), (1,)), ((), ())), preferred_element_type=F32)
    s = qk * jnp.exp(d - m_t)
    num = w_inter * _dot(qb, C.astype(BF16)) + _dot(s.astype(BF16), vb)
    nq = w_inter * jnp.sum(q * n, axis=-1, keepdims=True) + jnp.sum(s, axis=-1, keepdims=True)
    h = num / jnp.maximum(jnp.abs(nq), jnp.exp(-m_t))
    b_last = b_col[L - 1:L, :]
    m_new = jnp.maximum(b_last + m, jnp.max(b_last + r_row, axis=-1, keepdims=True))
    fw = jnp.exp(b_last + m - m_new)
    iw_col = jnp.exp(b_last + r_col - m_new)
    kw = iw_col * k
    C_new = fw * C + lax.dot_general(kw.astype(BF16), vb, (((0,), (0,)), ((), ())),
                                     preferred_element_type=F32)
    n_new = fw * n + jnp.sum(kw, axis=0, keepdims=True)
    return h, C_new, n_new, m_new


def _mixer_kernel(*refs, BB, TT, L, start_pos, zero_state):
    if zero_state:
        (x_ref, wcat_ref, gmix_ref, gbias_ref, hng_ref, wpool_ref, pscale_ref, wout_ref,
         x1_ref, c_ref, n_ref, m_ref, buf_ref,
         z_ref, gi_ref, lf_ref, mix_ref, ext_ref) = refs
    else:
        (x_ref, wcat_ref, gmix_ref, gbias_ref, hng_ref, wpool_ref, pscale_ref, wout_ref,
         c0_ref, n0_ref, m0_ref, buf0_ref,
         x1_ref, c_ref, n_ref, m_ref, buf_ref,
         z_ref, gi_ref, lf_ref, mix_ref, ext_ref) = refs
    t = pl.program_id(1)
    R = BB * TT
    n_chunks = TT // L

    @pl.when(t == 0)
    def _init():
        ext_ref[:, 0:POOL_PAD, :] = jnp.zeros((BB, POOL_PAD, POOL_WIDTH), F32)
        if zero_state:
            c_ref[...] = jnp.zeros(c_ref.shape, F32)
            n_ref[...] = jnp.zeros(n_ref.shape, F32)
            m_ref[...] = jnp.zeros(m_ref.shape, F32)
        else:
            c_ref[...] = c0_ref[...]
            n_ref[...] = n0_ref[...]
            m_ref[...] = m0_ref[...]
            ext_ref[:, 1:POOL_PAD, :] = buf0_ref[...]

    x = x_ref[...].reshape(R, D_MODEL)
    hn = _rms(x, gmix_ref[...])
    z_ref[...] = _dot(hn.astype(BF16), wcat_ref[...])
    g = z_ref[:, COL_G:COL_G + 128] + gbias_ref[...]
    gi_ref[...] = g
    lf_ref[...] = _log_sigmoid(g)

    row = lax.broadcasted_iota(jnp.int32, (L, L), 0)
    col = lax.broadcasted_iota(jnp.int32, (L, L), 1)
    causal = row >= col
    eye = row == col
    tril = causal.astype(F32)
    hng = hng_ref[...]

    def chunk_body(s, carry):
        row0 = pl.multiple_of(s * L, L)
        bb = s // n_chunks
        rows = pl.ds(row0, L)
        bcum = jnp.dot(tril, lf_ref[rows, :], preferred_element_type=F32,
                       precision=lax.Precision.HIGHEST)
        gi = gi_ref[rows, :]
        for hd in range(HEADS):
            lanes = slice(hd * DH, (hd + 1) * DH)
            q = z_ref[rows, COL_Q + hd * DH:COL_Q + (hd + 1) * DH] * (DH ** -0.5)
            k = z_ref[rows, COL_K + hd * DH:COL_K + (hd + 1) * DH]
            v = z_ref[rows, COL_V + hd * DH:COL_V + (hd + 1) * DH]
            o = z_ref[rows, COL_O + hd * DH:COL_O + (hd + 1) * DH]
            C = c_ref[bb, hd]
            n = n_ref[bb, hd:hd + 1, :]
            m = m_ref[bb, hd:hd + 1, :]
            h, C_new, n_new, m_new = _mlstm_chunk(
                q, k, v, bcum[:, HEADS + hd:HEADS + hd + 1], gi[:, hd:hd + 1], C, n, m, causal, eye)
            c_ref[bb, hd] = C_new
            n_ref[bb, hd:hd + 1, :] = n_new
            m_ref[bb, hd:hd + 1, :] = m_new
            h = h * lax.rsqrt(jnp.mean(h * h, axis=-1, keepdims=True) + NORM_EPS)
            mix_ref[rows, lanes] = h * hng[:, lanes] * jax.nn.sigmoid(o)
        return carry

    lax.fori_loop(0, BB * n_chunks, chunk_body, 0)

    ext_ref[:, POOL_PAD:POOL_PAD + TT, :] = z_ref[:, COL_U:COL_U + POOL_WIDTH].reshape(BB, TT, POOL_WIDTH)
    pos = start_pos + t * TT + lax.broadcasted_iota(jnp.int32, (1, TT, 1), 1)
    pscale = pscale_ref[...]
    for gidx, w in enumerate(POOL_WINDOWS):
        lanes = slice(gidx * POOL_GDIM, (gidx + 1) * POOL_GDIM)
        u_g = ext_ref[:, POOL_PAD:POOL_PAD + TT, lanes]
        acc = u_g
        for j in range(1, w):
            acc = acc + ext_ref[:, POOL_PAD - j:POOL_PAD - j + TT, lanes]
        cnt = jnp.minimum(pos + 1, w).astype(F32)
        pooled = (acc / cnt - u_g).reshape(R, POOL_GDIM)
        mixed = _dot(pooled.astype(BF16), wpool_ref[gidx]) * pscale[:, lanes]
        mix_ref[:, MLSTM_WIDTH + gidx * POOL_GDIM:MLSTM_WIDTH + (gidx + 1) * POOL_GDIM] = mixed
    new_buf = ext_ref[:, TT + 1:TT + POOL_PAD, :]
    buf_ref[...] = new_buf
    ext_ref[:, 1:POOL_PAD, :] = new_buf

    out = x_ref[...].reshape(R, D_MODEL) + _dot(mix_ref[...].astype(BF16), wout_ref[...])
    x1_ref[...] = out.reshape(BB, TT, D_MODEL)


def _const_spec(shape):
    nd = len(shape)
    return pl.BlockSpec(shape, lambda b, t, _nd=nd: (0,) * _nd)


def _mixer(x, wcat, gmix, gbias, hng, wpool, pscale, wout, state, *, BB, TT, L, start_pos):
    B, T, _ = x.shape
    zero_state = state is None
    R = BB * TT
    grid = (B // BB, T // TT)
    x_spec = pl.BlockSpec((BB, TT, D_MODEL), lambda b, t: (b, t, 0))
    c_spec = pl.BlockSpec((BB, HEADS, DH, DH), lambda b, t: (b, 0, 0, 0))
    n_spec = pl.BlockSpec((BB, HEADS, DH), lambda b, t: (b, 0, 0))
    m_spec = pl.BlockSpec((BB, HEADS, 1), lambda b, t: (b, 0, 0))
    buf_spec = pl.BlockSpec((BB, POOL_BUF, POOL_WIDTH), lambda b, t: (b, 0, 0))
    weights = (wcat, gmix, gbias, hng, wpool, pscale, wout)
    in_specs = [x_spec] + [_const_spec(w.shape) for w in weights]
    args = [x, *weights]
    if not zero_state:
        in_specs += [c_spec, n_spec, m_spec, buf_spec]
        args += list(state)
    out_shape = (
        jax.ShapeDtypeStruct((B, T, D_MODEL), F32),
        jax.ShapeDtypeStruct((B, HEADS, DH, DH), F32),
        jax.ShapeDtypeStruct((B, HEADS, DH), F32),
        jax.ShapeDtypeStruct((B, HEADS, 1), F32),
        jax.ShapeDtypeStruct((B, POOL_BUF, POOL_WIDTH), F32),
    )
    kern = functools.partial(_mixer_kernel, BB=BB, TT=TT, L=L, start_pos=start_pos, zero_state=zero_state)
    return pl.pallas_call(
        kern,
        grid=grid,
        in_specs=in_specs,
        out_specs=(x_spec, c_spec, n_spec, m_spec, buf_spec),
        out_shape=out_shape,
        scratch_shapes=[
            pltpu.VMEM((R, IN_COLS_PAD), F32),
            pltpu.VMEM((R, 128), F32),
            pltpu.VMEM((R, 128), F32),
            pltpu.VMEM((R, D_MODEL), F32),
            pltpu.VMEM((BB, POOL_PAD + TT, POOL_WIDTH), F32),
        ],
        compiler_params=pltpu.CompilerParams(
            dimension_semantics=("arbitrary", "arbitrary"), vmem_limit_bytes=VMEM_LIMIT),
        name="mixer_zero" if zero_state else "mixer_state",
    )(*args)


def _first_index_of_max(work, idx, n, axis):
    mx = jnp.max(work, axis=axis, keepdims=True)
    first = jnp.min(jnp.where(work == mx, idx, float(n)), axis=axis, keepdims=True)
    return mx, first


def _route(scores_t, bias_t):
    tm = scores_t.shape[1]
    biased = scores_t + bias_t
    b3 = biased.reshape(N_GROUPS, GROUP_SIZE, tm)
    sub = lax.broadcasted_iota(jnp.int32, b3.shape, 1).astype(F32)
    m1, first = _first_index_of_max(b3, sub, GROUP_SIZE, 1)
    m2 = jnp.max(jnp.where(sub == first, NEG_INF, b3), axis=1, keepdims=True)
    gs = (m1 + m2).reshape(N_GROUPS, tm)
    gidx = lax.broadcasted_iota(jnp.int32, gs.shape, 0).astype(F32)
    gsel = jnp.zeros(gs.shape, F32)
    work = gs
    for _ in range(TOPK_GROUPS):
        _, first = _first_index_of_max(work, gidx, N_GROUPS, 0)
        pick = gidx == first
        gsel = jnp.where(pick, 1.0, gsel)
        work = jnp.where(pick, NEG_INF, work)
    emask = jnp.broadcast_to(gsel.reshape(N_GROUPS, 1, tm), b3.shape).reshape(N_EXPERTS, tm)
    work = jnp.where(emask > 0, biased, NEG_INF)
    eidx = lax.broadcasted_iota(jnp.int32, work.shape, 0).astype(F32)
    sel = jnp.zeros(work.shape, F32)
    for _ in range(TOP_K):
        _, first = _first_index_of_max(work, eidx, N_EXPERTS, 0)
        pick = eidx == first
        sel = jnp.where(pick, scores_t, sel)
        work = jnp.where(pick, NEG_INF, work)
    return sel / jnp.sum(sel, axis=0, keepdims=True) * ROUTED_SCALE


def _router_kernel(x1_ref, gffn_ref, rwt_ref, rbias_ref, w1_ref, w3_ref, w2_ref,
                   xn_ref, gates_ref, x1s_ref):
    x1 = x1_ref[...]
    xn = _rms(x1, gffn_ref[...])
    xb = xn.astype(BF16)
    xn_ref[...] = xb
    logits_t = lax.dot_general(rwt_ref[...], xn, (((1,), (1,)), ((), ())),
                               preferred_element_type=F32, precision=lax.Precision.HIGHEST)
    gates_ref[...] = _route(jax.nn.sigmoid(logits_t), rbias_ref[...])
    a = _dot(xb, w1_ref[...])
    hsh = (a * jax.nn.sigmoid(a)) * _dot(xb, w3_ref[...])
    x1s_ref[...] = x1 + _dot(hsh.astype(BF16), w2_ref[...])


def _router(x1, gffn, rwt, rbias, w1, w3, w2, *, TM):
    N = x1.shape[0]
    tok = pl.BlockSpec((TM, D_MODEL), lambda i: (i, 0))
    consts = (gffn, rwt, rbias, w1, w3, w2)
    return pl.pallas_call(
        _router_kernel,
        grid=(N // TM,),
        in_specs=[tok] + [pl.BlockSpec(c.shape, lambda i: (0, 0)) for c in consts],
        out_specs=(tok, pl.BlockSpec((N_EXPERTS, TM), lambda i: (0, i)), tok),
        out_shape=(jax.ShapeDtypeStruct((N, D_MODEL), BF16),
                   jax.ShapeDtypeStruct((N_EXPERTS, N), F32),
                   jax.ShapeDtypeStruct((N, D_MODEL), F32)),
        compiler_params=pltpu.CompilerParams(
            dimension_semantics=("arbitrary",), vmem_limit_bytes=VMEM_LIMIT),
        name="router_shared",
    )(x1, *consts)


def _moe_kernel(xn_ref, gates_ref, w1_ref, w3_ref, w2_ref, out_ref, acc_ref):
    e = pl.program_id(1)

    @pl.when(e == 0)
    def _zero():
        acc_ref[...] = jnp.zeros(acc_ref.shape, F32)

    xb = xn_ref[...]
    a = _dot(xb, w1_ref[0].astype(BF16))
    hb = (a * jax.nn.sigmoid(a)) * _dot(xb, w3_ref[0].astype(BF16))
    lane = lax.broadcasted_iota(jnp.int32, gates_ref.shape, 1)
    gcol = jnp.sum(jnp.where(lane == e, gates_ref[...], 0.0), axis=-1, keepdims=True)
    acc_ref[...] += _dot((hb * gcol).astype(BF16), w2_ref[0].astype(BF16))

    @pl.when(e == N_EXPERTS - 1)
    def _store():
        out_ref[...] = acc_ref[...]


def _moe_dense(xn, gates, w1, w3, w2, *, TM):
    N = xn.shape[0]
    tok = pl.BlockSpec((TM, D_MODEL), lambda i, e: (i, 0))
    return pl.pallas_call(
        _moe_kernel,
        grid=(N // TM, N_EXPERTS),
        in_specs=[tok,
                  pl.BlockSpec((TM, 128), lambda i, e: (i, 0)),
                  pl.BlockSpec((1, D_MODEL, EXPERT_FF), lambda i, e: (e, 0, 0)),
                  pl.BlockSpec((1, D_MODEL, EXPERT_FF), lambda i, e: (e, 0, 0)),
                  pl.BlockSpec((1, EXPERT_FF, D_MODEL), lambda i, e: (e, 0, 0))],
        out_specs=tok,
        out_shape=jax.ShapeDtypeStruct((N, D_MODEL), F32),
        scratch_shapes=[pltpu.VMEM((TM, D_MODEL), F32)],
        compiler_params=pltpu.CompilerParams(
            dimension_semantics=("arbitrary", "arbitrary"), vmem_limit_bytes=VMEM_LIMIT),
        name="moe_dense",
    )(xn, gates, w1, w3, w2)


def _final_kernel(x1s_ref, routed_ref, p_ref, gple_ref, wgate_ref, wproj_ref, gfin_ref, y_ref):
    x2 = x1s_ref[...] + routed_ref[...]
    r = _rms(x2, gple_ref[...])
    gate = jax.nn.sigmoid(_dot(r.astype(BF16), wgate_ref[...]))
    x3 = x2 + _dot(p_ref[...].astype(BF16), wproj_ref[...]) * gate
    y_ref[...] = _rms(x3, gfin_ref[...])


def _final(x1s, routed, p, gple, wgate, wproj, gfin, *, TM):
    N = x1s.shape[0]
    tok = pl.BlockSpec((TM, D_MODEL), lambda i: (i, 0))
    consts = (gple, wgate, wproj, gfin)
    return pl.pallas_call(
        _final_kernel,
        grid=(N // TM,),
        in_specs=[tok, tok, pl.BlockSpec((TM, PLE_DIM), lambda i: (i, 0))]
        + [pl.BlockSpec(c.shape, lambda i: (0, 0)) for c in consts],
        out_specs=tok,
        out_shape=jax.ShapeDtypeStruct((N, D_MODEL), F32),
        compiler_params=pltpu.CompilerParams(
            dimension_semantics=("arbitrary",), vmem_limit_bytes=VMEM_LIMIT),
        name="ple_final",
    )(x1s, routed, p, *consts)


def kernel(x_prompt, x_sample, p_prompt, p_sample, state_C, state_n, state_m, state_pool, norm_mix_g, w_in, b_igate, b_fgate, head_norm_g, w_pool, pool_scale, w_out, norm_ffn_g, router_w, router_bias, ex_w1, ex_w3, ex_w2, sh_w1, sh_w3, sh_w2, norm_ple_g, w_ple_gate, w_ple_proj, final_norm_g):
    depth = norm_mix_g.shape[0]
    assert depth == 1
    l = 0
    B, T, _ = x_prompt.shape
    Bs, Ts, _ = x_sample.shape
    g0 = 4 * MLSTM_WIDTH
    w = w_in[l]
    wcat = jnp.concatenate(
        [w[:, :g0], w[:, g0 + 2 * HEADS:], w[:, g0:g0 + 2 * HEADS],
         jnp.zeros((D_MODEL, 128 - 2 * HEADS), F32)], axis=1).astype(BF16)
    gbias = jnp.concatenate([b_igate[l], b_fgate[l], jnp.zeros((128 - 2 * HEADS,), F32)])[None, :]
    mixer_w = (wcat, norm_mix_g[l][None, :], gbias, head_norm_g[l][None, :], w_pool[l].astype(BF16),
               pool_scale[l][None, :], w_out[l].astype(BF16))

    x1p, Cp, Np, Mp, Bp = _mixer(x_prompt, *mixer_w, None, BB=1, TT=512, L=128, start_pos=0)
    state = (state_C[l], state_n[l], state_m[l][..., None], state_pool[l])
    x1s_, Cs, Ns, Ms, Bs_ = _mixer(x_sample, *mixer_w, state, BB=32, TT=Ts, L=Ts, start_pos=PAST_LEN)

    x1 = jnp.concatenate([x1p.reshape(B * T, D_MODEL), x1s_.reshape(Bs * Ts, D_MODEL)], axis=0)
    p = jnp.concatenate([p_prompt[l].reshape(B * T, PLE_DIM), p_sample[l].reshape(Bs * Ts, PLE_DIM)], axis=0)
    N = x1.shape[0]

    xn, gates_t, x1sh = _router(
        x1, norm_ffn_g[l][None, :], router_w[l].T, router_bias[l][:, None],
        sh_w1[l].astype(BF16), sh_w3[l].astype(BF16), sh_w2[l].astype(BF16), TM=512)
    gates = jnp.pad(gates_t.T, ((0, 0), (0, 128 - N_EXPERTS)))
    routed = _moe_dense(xn, gates, ex_w1[l], ex_w3[l], ex_w2[l], TM=N // 8)
    y = _final(x1sh, routed, p, norm_ple_g[l][None, :], w_ple_gate[l].astype(BF16),
               w_ple_proj[l].astype(BF16), final_norm_g[None, :], TM=512)

    y_prompt = y[:B * T].reshape(B, T, D_MODEL)
    y_sample = y[B * T:].reshape(Bs, Ts, D_MODEL)
    return (y_prompt, y_sample,
            Cp[None], Np[None], Mp[..., 0][None], Bp[None],
            Cs[None], Ns[None], Ms[..., 0][None], Bs_[None])
```

```python
import functools

import jax
import jax.numpy as jnp
from jax import lax
from jax.experimental import pallas as pl
from jax.experimental.pallas import tpu as pltpu

D_MODEL = 1024
HALF = D_MODEL // 2
SLAB = HALF // 128
HEADS = 4
DH = 128
MLSTM_WIDTH = HEADS * DH
POOL_WIDTH = 512
POOL_WINDOWS = (2, 4, 8, 16)
POOL_GDIM = 128
POOL_BUF = 15
POOL_PAD = 16
N_EXPERTS = 64
TOP_K = 8
N_GROUPS = 8
GROUP_SIZE = N_EXPERTS // N_GROUPS
TOPK_GROUPS = 4
EXPERT_FF = 256
ROUTED_SCALE = 2.5
NORM_EPS = 1e-6
PLE_DIM = 256
PAST_LEN = 16384

COL_Q, COL_K, COL_V, COL_O, COL_U, COL_G = 0, 512, 1024, 1536, 2048, 2560
IN_COLS_PAD = 2688

TOKEN_TILE = 512
EXPERT_TILE = 512
PAIR_BITS = 18

VMEM_LIMIT = 56 * 1024 * 1024
F32 = jnp.float32
BF16 = jnp.bfloat16
U32 = jnp.uint32
NEG_INF = float("-inf")


def _rms(x, g):
    return x * lax.rsqrt(jnp.mean(x * x, axis=-1, keepdims=True) + NORM_EPS) * g


def _log_sigmoid(x):
    return jnp.minimum(x, 0.0) - jnp.log1p(jnp.exp(-jnp.abs(x)))


def _dot(a, b):
    return jnp.dot(a, b, preferred_element_type=F32)


def _store_packed_rows(ref, x):
    rows = x.shape[0]
    lo = pltpu.bitcast(x[:, :HALF].astype(BF16).astype(F32), U32)
    hi = pltpu.bitcast(x[:, HALF:].astype(BF16).astype(F32), U32)
    words = lax.shift_right_logical(lo, jnp.uint32(16)) | (hi & jnp.uint32(0xFFFF0000))
    for s in range(SLAB):
        ref[pl.ds(s, rows, stride=SLAB), :] = words[:, s * 128:(s + 1) * 128]


def _load_packed_rows(ref):
    rows = ref.shape[0] // SLAB
    los, his = [], []
    for s in range(SLAB):
        w = ref[pl.ds(s, rows, stride=SLAB), :]
        los.append(pltpu.bitcast(lax.shift_left(w, jnp.uint32(16)), F32))
        his.append(pltpu.bitcast(w & jnp.uint32(0xFFFF0000), F32))
    return los + his


def _mlstm_chunk(q, k, v, b_col, i_col, C, n, m, causal, eye):
    L = q.shape[0]
    r_col = i_col - b_col
    r_row = jnp.sum(jnp.where(eye, r_col, 0.0), axis=0, keepdims=True)
    d = jnp.where(causal, b_col + r_row, NEG_INF)
    inter = b_col + m
    m_t = jnp.maximum(inter, jnp.max(d, axis=-1, keepdims=True))
    w_inter = jnp.exp(inter - m_t)
    qb, kb, vb = q.astype(BF16), k.astype(BF16), v.astype(BF16)
    qk = lax.dot_general(qb, kb, (((1,), (1,)), ((), ())), preferred_element_type=F32)
    s = qk * jnp.exp(d - m_t)
    num = w_inter * _dot(qb, C.astype(BF16)) + _dot(s.astype(BF16), vb)
    nq = w_inter * jnp.sum(q * n, axis=-1, keepdims=True) + jnp.sum(s, axis=-1, keepdims=True)
    h = num / jnp.maximum(jnp.abs(nq), jnp.exp(-m_t))
    b_last = b_col[L - 1:L, :]
    m_new = jnp.maximum(b_last + m, jnp.max(b_last + r_row, axis=-1, keepdims=True))
    fw = jnp.exp(b_last + m - m_new)
    iw_col = jnp.exp(b_last + r_col - m_new)
    kw = iw_col * k
    C_new = fw * C + lax.dot_general(kw.astype(BF16), vb, (((0,), (0,)), ((), ())),
                                     preferred_element_type=F32)
    n_new = fw * n + jnp.sum(kw, axis=0, keepdims=True)
    return h, C_new, n_new, m_new


def _mixer_kernel(*refs, BB, TT, L, start_pos, zero_state):
    if zero_state:
        (x_ref, wcat_ref, gmix_ref, gbias_ref, hng_ref, wpool_ref, pscale_ref, wout_ref,
         x1_ref, c_ref, n_ref, m_ref, buf_ref,
         z_ref, gi_ref, lf_ref, mix_ref, ext_ref) = refs
    else:
        (x_ref, wcat_ref, gmix_ref, gbias_ref, hng_ref, wpool_ref, pscale_ref, wout_ref,
         c0_ref, n0_ref, m0_ref, buf0_ref,
         x1_ref, c_ref, n_ref, m_ref, buf_ref,
         z_ref, gi_ref, lf_ref, mix_ref, ext_ref) = refs
    t = pl.program_id(1)
    R = BB * TT
    n_chunks = TT // L

    @pl.when(t == 0)
    def _init():
        ext_ref[:, 0:POOL_PAD, :] = jnp.zeros((BB, POOL_PAD, POOL_WIDTH), F32)
        if zero_state:
            c_ref[...] = jnp.zeros(c_ref.shape, F32)
            n_ref[...] = jnp.zeros(n_ref.shape, F32)
            m_ref[...] = jnp.zeros(m_ref.shape, F32)
        else:
            c_ref[...] = c0_ref[...]
            n_ref[...] = n0_ref[...]
            m_ref[...] = m0_ref[...]
            ext_ref[:, 1:POOL_PAD, :] = buf0_ref[...]

    x = x_ref[...].reshape(R, D_MODEL)
    hn = _rms(x, gmix_ref[...])
    z_ref[...] = _dot(hn.astype(BF16), wcat_ref[...])
    g = z_ref[:, COL_G:COL_G + 128] + gbias_ref[...]
    gi_ref[...] = g
    lf_ref[...] = _log_sigmoid(g)

    row = lax.broadcasted_iota(jnp.int32, (L, L), 0)
    col = lax.broadcasted_iota(jnp.int32, (L, L), 1)
    causal = row >= col
    eye = row == col
    tril = causal.astype(F32)
    hng = hng_ref[...]

    def chunk_body(s, carry):
        row0 = pl.multiple_of(s * L, L)
        bb = s // n_chunks
        rows = pl.ds(row0, L)
        bcum = jnp.dot(tril, lf_ref[rows, :], preferred_element_type=F32,
                       precision=lax.Precision.HIGHEST)
        gi = gi_ref[rows, :]
        for hd in range(HEADS):
            lanes = slice(hd * DH, (hd + 1) * DH)
            q = z_ref[rows, COL_Q + hd * DH:COL_Q + (hd + 1) * DH] * (DH ** -0.5)
            k = z_ref[rows, COL_K + hd * DH:COL_K + (hd + 1) * DH]
            v = z_ref[rows, COL_V + hd * DH:COL_V + (hd + 1) * DH]
            o = z_ref[rows, COL_O + hd * DH:COL_O + (hd + 1) * DH]
            C = c_ref[bb, hd]
            n = n_ref[bb, hd:hd + 1, :]
            m = m_ref[bb, hd:hd + 1, :]
            h, C_new, n_new, m_new = _mlstm_chunk(
                q, k, v, bcum[:, HEADS + hd:HEADS + hd + 1], gi[:, hd:hd + 1], C, n, m, causal, eye)
            c_ref[bb, hd] = C_new
            n_ref[bb, hd:hd + 1, :] = n_new
            m_ref[bb, hd:hd + 1, :] = m_new
            h = h * lax.rsqrt(jnp.mean(h * h, axis=-1, keepdims=True) + NORM_EPS)
            mix_ref[rows, lanes] = h * hng[:, lanes] * jax.nn.sigmoid(o)
        return carry

    lax.fori_loop(0, BB * n_chunks, chunk_body, 0)

    ext_ref[:, POOL_PAD:POOL_PAD + TT, :] = z_ref[:, COL_U:COL_U + POOL_WIDTH].reshape(BB, TT, POOL_WIDTH)
    pos = start_pos + t * TT + lax.broadcasted_iota(jnp.int32, (1, TT, 1), 1)
    pscale = pscale_ref[...]
    for gidx, w in enumerate(POOL_WINDOWS):
        lanes = slice(gidx * POOL_GDIM, (gidx + 1) * POOL_GDIM)
        u_g = ext_ref[:, POOL_PAD:POOL_PAD + TT, lanes]
        acc = u_g
        for j in range(1, w):
            acc = acc + ext_ref[:, POOL_PAD - j:POOL_PAD - j + TT, lanes]
        cnt = jnp.minimum(pos + 1, w).astype(F32)
        pooled = (acc / cnt - u_g).reshape(R, POOL_GDIM)
        mixed = _dot(pooled.astype(BF16), wpool_ref[gidx]) * pscale[:, lanes]
        mix_ref[:, MLSTM_WIDTH + gidx * POOL_GDIM:MLSTM_WIDTH + (gidx + 1) * POOL_GDIM] = mixed
    new_buf = ext_ref[:, TT + 1:TT + POOL_PAD, :]
    buf_ref[...] = new_buf
    ext_ref[:, 1:POOL_PAD, :] = new_buf

    out = x_ref[...].reshape(R, D_MODEL) + _dot(mix_ref[...].astype(BF16), wout_ref[...])
    x1_ref[...] = out.reshape(BB, TT, D_MODEL)


def _const_spec(shape):
    nd = len(shape)
    return pl.BlockSpec(shape, lambda b, t, _nd=nd: (0,) * _nd)


def _mixer(x, wcat, gmix, gbias, hng, wpool, pscale, wout, state, *, BB, TT, L, start_pos):
    B, T, _ = x.shape
    zero_state = state is None
    R = BB * TT
    grid = (B // BB, T // TT)
    x_spec = pl.BlockSpec((BB, TT, D_MODEL), lambda b, t: (b, t, 0))
    c_spec = pl.BlockSpec((BB, HEADS, DH, DH), lambda b, t: (b, 0, 0, 0))
    n_spec = pl.BlockSpec((BB, HEADS, DH), lambda b, t: (b, 0, 0))
    m_spec = pl.BlockSpec((BB, HEADS, 1), lambda b, t: (b, 0, 0))
    buf_spec = pl.BlockSpec((BB, POOL_BUF, POOL_WIDTH), lambda b, t: (b, 0, 0))
    weights = (wcat, gmix, gbias, hng, wpool, pscale, wout)
    in_specs = [x_spec] + [_const_spec(w.shape) for w in weights]
    args = [x, *weights]
    if not zero_state:
        in_specs += [c_spec, n_spec, m_spec, buf_spec]
        args += list(state)
    out_shape = (
        jax.ShapeDtypeStruct((B, T, D_MODEL), F32),
        jax.ShapeDtypeStruct((B, HEADS, DH, DH), F32),
        jax.ShapeDtypeStruct((B, HEADS, DH), F32),
        jax.ShapeDtypeStruct((B, HEADS, 1), F32),
        jax.ShapeDtypeStruct((B, POOL_BUF, POOL_WIDTH), F32),
    )
    kern = functools.partial(_mixer_kernel, BB=BB, TT=TT, L=L, start_pos=start_pos, zero_state=zero_state)
    return pl.pallas_call(
        kern,
        grid=grid,
        in_specs=in_specs,
        out_specs=(x_spec, c_spec, n_spec, m_spec, buf_spec),
        out_shape=out_shape,
        scratch_shapes=[
            pltpu.VMEM((R, IN_COLS_PAD), F32),
            pltpu.VMEM((R, 128), F32),
            pltpu.VMEM((R, 128), F32),
            pltpu.VMEM((R, D_MODEL), F32),
            pltpu.VMEM((BB, POOL_PAD + TT, POOL_WIDTH), F32),
        ],
        compiler_params=pltpu.CompilerParams(
            dimension_semantics=("arbitrary", "arbitrary"), vmem_limit_bytes=VMEM_LIMIT),
        name="mixer_zero" if zero_state else "mixer_state",
    )(*args)


def _first_index_of_max(work, idx, n, axis):
    mx = jnp.max(work, axis=axis, keepdims=True)
    return jnp.min(jnp.where(work == mx, idx, float(n)), axis=axis, keepdims=True)


def _route(scores_t, bias_t):
    tm = scores_t.shape[1]
    biased = scores_t + bias_t
    b3 = biased.reshape(N_GROUPS, GROUP_SIZE, tm)
    sub = lax.broadcasted_iota(jnp.int32, b3.shape, 1).astype(F32)
    m1 = jnp.max(b3, axis=1, keepdims=True)
    first = jnp.min(jnp.where(b3 == m1, sub, float(GROUP_SIZE)), axis=1, keepdims=True)
    m2 = jnp.max(jnp.where(sub == first, NEG_INF, b3), axis=1, keepdims=True)
    gs = (m1 + m2).reshape(N_GROUPS, tm)
    gidx = lax.broadcasted_iota(jnp.int32, gs.shape, 0).astype(F32)
    gsel = jnp.zeros(gs.shape, F32)
    work = gs
    for _ in range(TOPK_GROUPS):
        pick = gidx == _first_index_of_max(work, gidx, N_GROUPS, 0)
        gsel = jnp.where(pick, 1.0, gsel)
        work = jnp.where(pick, NEG_INF, work)
    emask = jnp.broadcast_to(gsel.reshape(N_GROUPS, 1, tm), b3.shape).reshape(N_EXPERTS, tm)
    work = jnp.where(emask > 0, biased, NEG_INF)
    eidx = lax.broadcasted_iota(jnp.int32, work.shape, 0).astype(F32)
    ids, sel = [], []
    for _ in range(TOP_K):
        first = _first_index_of_max(work, eidx, N_EXPERTS, 0)
        pick = eidx == first
        ids.append(first)
        sel.append(jnp.sum(jnp.where(pick, scores_t, 0.0), axis=0, keepdims=True))
        work = jnp.where(pick, NEG_INF, work)
    ids = jnp.concatenate(ids, axis=0)
    sel = jnp.concatenate(sel, axis=0)
    wts = sel / jnp.sum(sel, axis=0, keepdims=True) * ROUTED_SCALE
    return ids.astype(jnp.int32), wts


def _router_kernel(x1_ref, gffn_ref, rwt_ref, rbias_ref, w1_ref, w3_ref, w2_ref,
                   xnp_ref, ids_ref, wts_ref, x1s_ref):
    x1 = x1_ref[...]
    xn = _rms(x1, gffn_ref[...])
    xb = xn.astype(BF16)
    _store_packed_rows(xnp_ref, xn)
    logits_t = lax.dot_general(rwt_ref[...], xn, (((1,), (1,)), ((), ())),
                               preferred_element_type=F32, precision=lax.Precision.HIGHEST)
    ids, wts = _route(jax.nn.sigmoid(logits_t), rbias_ref[...])
    ids_ref[...] = ids
    wts_ref[...] = wts
    a = _dot(xb, w1_ref[...])
    hsh = (a * jax.nn.sigmoid(a)) * _dot(xb, w3_ref[...])
    x1s_ref[...] = x1 + _dot(hsh.astype(BF16), w2_ref[...])


def _router(x1, gffn, rwt, rbias, w1, w3, w2, *, TM):
    N = x1.shape[0]
    tok = pl.BlockSpec((TM, D_MODEL), lambda i: (i, 0))
    slot = pl.BlockSpec((TOP_K, TM), lambda i: (0, i))
    consts = (gffn, rwt, rbias, w1, w3, w2)
    return pl.pallas_call(
        _router_kernel,
        grid=(N // TM,),
        in_specs=[tok] + [pl.BlockSpec(c.shape, lambda i: (0, 0)) for c in consts],
        out_specs=(pl.BlockSpec((TM * SLAB, 128), lambda i: (i, 0)), slot, slot, tok),
        out_shape=(jax.ShapeDtypeStruct((N * SLAB, 128), U32),
                   jax.ShapeDtypeStruct((TOP_K, N), jnp.int32),
                   jax.ShapeDtypeStruct((TOP_K, N), F32),
                   jax.ShapeDtypeStruct((N, D_MODEL), F32)),
        compiler_params=pltpu.CompilerParams(
            dimension_semantics=("arbitrary",), vmem_limit_bytes=VMEM_LIMIT),
        name="router_shared",
    )(x1, *consts)


def _sorted_pair_plan(ids_t, n_tokens, tile, n_tiles_max):
    n_pairs = TOP_K * n_tokens
    e_flat = ids_t.reshape(n_pairs)
    key = (e_flat << PAIR_BITS) | jnp.arange(n_pairs, dtype=jnp.int32)
    skey = jnp.sort(key)
    order = skey & ((1 << PAIR_BITS) - 1)
    experts = jnp.arange(N_EXPERTS, dtype=jnp.int32)
    starts = jnp.searchsorted(skey, experts << PAIR_BITS).astype(jnp.int32)
    ends = jnp.concatenate([starts[1:], jnp.array([n_pairs], jnp.int32)])
    counts = ends - starts
    tiles_per_e = (counts + tile - 1) // tile
    tile_end = jnp.cumsum(tiles_per_e)
    tile_begin = tile_end - tiles_per_e
    n_tiles = tile_end[-1]
    t_ids = jnp.arange(n_tiles_max, dtype=jnp.int32)
    tile_e = jnp.minimum(jnp.searchsorted(tile_end, t_ids, side="right"), N_EXPERTS - 1).astype(jnp.int32)
    q = jnp.arange(n_tiles_max * tile, dtype=jnp.int32)
    q_e = jnp.repeat(tile_e, tile)
    j = q - tile_begin[q_e] * tile
    valid = (q < n_tiles * tile) & (j < counts[q_e])
    pair = order[jnp.clip(starts[q_e] + j, 0, n_pairs - 1)]
    src = jnp.where(valid, pair % n_tokens, 0)
    dst = jnp.where(valid, pair, n_pairs + q % (2 * tile))
    src = (src * SLAB).astype(jnp.int32).reshape(n_tiles_max, tile)
    dst = (dst * SLAB).astype(jnp.int32).reshape(n_tiles_max, tile)
    return src, dst, tile_e, n_tiles.astype(jnp.int32).reshape(1)


def _moe_kernel(tile_e_ref, n_tiles_ref, src_hbm, dst_hbm, xnp_hbm, w1_ref, w3_ref, w2_ref, y8_hbm,
                src0, src1, dst0, dst1, xbuf0, xbuf1, obuf0, obuf1, w1b, w3b, w2b,
                src_sem, dst_sem, gsem, ssem, *, TILE, T_MAX, N_PAIRS):
    t = pl.program_id(0)
    n_tiles = n_tiles_ref[0]
    valid = t < n_tiles
    src_smem, dst_smem = (src0, src1), (dst0, dst1)
    xbuf, obuf = (xbuf0, xbuf1), (obuf0, obuf1)

    def src_copy(tile_idx, slot):
        return pltpu.make_async_copy(src_hbm.at[jnp.minimum(tile_idx, T_MAX - 1)], src_smem[slot],
                                     src_sem.at[slot])

    def dst_copy(tile_idx, slot):
        return pltpu.make_async_copy(dst_hbm.at[jnp.minimum(tile_idx, T_MAX - 1)], dst_smem[slot],
                                     dst_sem.at[slot])

    def gather_all(slot):
        return pltpu.make_async_copy(xnp_hbm.at[pl.ds(0, TILE * SLAB)], xbuf[slot], gsem.at[slot])

    def scatter_all(slot):
        return pltpu.make_async_copy(obuf[slot], y8_hbm.at[pl.ds(0, TILE * SLAB)], ssem.at[slot])

    def issue_gather(slot):
        for j in range(TILE):
            line = src_smem[slot][j]
            pltpu.make_async_copy(xnp_hbm.at[pl.ds(line, SLAB)], xbuf[slot].at[pl.ds(j * SLAB, SLAB)],
                                  gsem.at[slot]).start(priority=j % 2)

    def issue_scatter(slot):
        for j in range(TILE):
            line = dst_smem[slot][j]
            pltpu.make_async_copy(obuf[slot].at[pl.ds(j * SLAB, SLAB)], y8_hbm.at[pl.ds(line, SLAB)],
                                  ssem.at[slot]).start(priority=j % 2)

    @pl.when(t == 0)
    def _prologue():
        src_copy(0, 0).start()
        dst_copy(0, 0).start()
        src_copy(1, 1).start()
        for p in (0, 1):
            obuf[p][...] = jnp.zeros(obuf[p].shape, U32)
            dump = pltpu.make_async_copy(
                obuf[p], y8_hbm.at[pl.ds((N_PAIRS + p * TILE) * SLAB, TILE * SLAB)], ssem.at[p])
            dump.start()
            dump.wait()
        src_copy(0, 0).wait()
        issue_gather(0)

    for p in (0, 1):
        @pl.when(valid & (t % 2 == p))
        def _arrivals(p=p):
            @pl.when(t >= 2)
            def _():
                scatter_all(p).wait()
            src_copy(0, 1 - p).wait()
            dst_copy(0, p).wait()
            gather_all(p).wait()

    prev_e = tile_e_ref[jnp.maximum(t - 1, 0)]

    @pl.when(valid & ((t == 0) | (tile_e_ref[t] != prev_e)))
    def _cast_weights():
        w1b[...] = w1_ref[0].astype(BF16)
        w3b[...] = w3_ref[0].astype(BF16)
        w2b[...] = w2_ref[0].astype(BF16)

    for p in (0, 1):
        @pl.when(valid & (t % 2 == p))
        def _main(p=p):
            src_copy(t + 2, p).start()
            dst_copy(t + 1, 1 - p).start()
            issue_gather(1 - p)
            xb = jnp.concatenate(_load_packed_rows(xbuf[p]), axis=1).astype(BF16)
            a = _dot(xb, w1b[...])
            hb = (a * jax.nn.sigmoid(a)) * _dot(xb, w3b[...])
            _store_packed_rows(obuf[p], _dot(hb.astype(BF16), w2b[...]))
            issue_scatter(p)

    for p in (0, 1):
        @pl.when((t == n_tiles - 1) & (t % 2 == p))
        def _drain(p=p):
            src_copy(0, p).wait()
            dst_copy(0, 1 - p).wait()
            gather_all(1 - p).wait()
            scatter_all(p).wait()

            @pl.when(t >= 1)
            def _():
                scatter_all(1 - p).wait()


def _moe_sparse(xnp, src, dst, tile_e, n_tiles, w1, w3, w2, *, n_pairs):
    t_max, tile = src.shape
    wspec_in = pl.BlockSpec((1, D_MODEL, EXPERT_FF), lambda t, te, nt: (te[t], 0, 0))
    wspec_out = pl.BlockSpec((1, EXPERT_FF, D_MODEL), lambda t, te, nt: (te[t], 0, 0))
    any_spec = pl.BlockSpec(memory_space=pl.ANY)
    kern = functools.partial(_moe_kernel, TILE=tile, T_MAX=t_max, N_PAIRS=n_pairs)
    row_buf = pltpu.VMEM((tile * SLAB, 128), U32)
    row_list = pltpu.SMEM((tile,), jnp.int32)
    return pl.pallas_call(
        kern,
        grid_spec=pltpu.PrefetchScalarGridSpec(
            num_scalar_prefetch=2,
            grid=(t_max,),
            in_specs=[any_spec, any_spec, any_spec, wspec_in, wspec_in, wspec_out],
            out_specs=any_spec,
            scratch_shapes=[
                row_list, row_list, row_list, row_list,
                row_buf, row_buf, row_buf, row_buf,
                pltpu.VMEM((D_MODEL, EXPERT_FF), BF16),
                pltpu.VMEM((D_MODEL, EXPERT_FF), BF16),
                pltpu.VMEM((EXPERT_FF, D_MODEL), BF16),
                pltpu.SemaphoreType.DMA((2,)),
                pltpu.SemaphoreType.DMA((2,)),
                pltpu.SemaphoreType.DMA((2,)),
                pltpu.SemaphoreType.DMA((2,)),
            ],
        ),
        out_shape=jax.ShapeDtypeStruct(((n_pairs + 2 * tile) * SLAB, 128), U32),
        compiler_params=pltpu.CompilerParams(
            dimension_semantics=("arbitrary",), vmem_limit_bytes=VMEM_LIMIT),
        name="moe_grouped",
    )(tile_e, n_tiles, src, dst, xnp, w1, w3, w2)


def _final_kernel(*refs):
    x1s_ref, wcol_ref = refs[0], refs[1]
    y8_refs = refs[2:2 + TOP_K]
    p_ref, gple_ref, wgate_ref, wproj_ref, gfin_ref, y_ref = refs[2 + TOP_K:]
    wcol = wcol_ref[...]
    acc = None
    for r in range(TOP_K):
        w = wcol[:, r:r + 1]
        cols = [w * c for c in _load_packed_rows(y8_refs[r])]
        acc = cols if acc is None else [a + c for a, c in zip(acc, cols)]
    x2 = x1s_ref[...] + jnp.concatenate(acc, axis=1)
    r_ = _rms(x2, gple_ref[...])
    gate = jax.nn.sigmoid(_dot(r_.astype(BF16), wgate_ref[...]))
    x3 = x2 + _dot(p_ref[...].astype(BF16), wproj_ref[...]) * gate
    y_ref[...] = _rms(x3, gfin_ref[...])


def _final(x1s, wcol, y8, p, gple, wgate, wproj, gfin, *, TM):
    N = x1s.shape[0]
    nb = N // TM
    tok = pl.BlockSpec((TM, D_MODEL), lambda i: (i, 0))
    consts = (gple, wgate, wproj, gfin)
    y8_specs = [pl.BlockSpec((TM * SLAB, 128), lambda i, _r=r: (_r * nb + i, 0)) for r in range(TOP_K)]
    return pl.pallas_call(
        _final_kernel,
        grid=(nb,),
        in_specs=[tok, pl.BlockSpec((TM, TOP_K), lambda i: (i, 0))] + y8_specs
        + [pl.BlockSpec((TM, PLE_DIM), lambda i: (i, 0))]
        + [pl.BlockSpec(c.shape, lambda i: (0, 0)) for c in consts],
        out_specs=tok,
        out_shape=jax.ShapeDtypeStruct((N, D_MODEL), F32),
        compiler_params=pltpu.CompilerParams(
            dimension_semantics=("arbitrary",), vmem_limit_bytes=VMEM_LIMIT),
        name="ple_final",
    )(x1s, wcol, *([y8] * TOP_K), p, *consts)


def kernel(x_prompt, x_sample, p_prompt, p_sample, state_C, state_n, state_m, state_pool, norm_mix_g, w_in, b_igate, b_fgate, head_norm_g, w_pool, pool_scale, w_out, norm_ffn_g, router_w, router_bias, ex_w1, ex_w3, ex_w2, sh_w1, sh_w3, sh_w2, norm_ple_g, w_ple_gate, w_ple_proj, final_norm_g):
    depth = norm_mix_g.shape[0]
    assert depth == 1
    l = 0
    B, T, _ = x_prompt.shape
    Bs, Ts, _ = x_sample.shape
    g0 = 4 * MLSTM_WIDTH
    w = w_in[l]
    wcat = jnp.concatenate(
        [w[:, :g0], w[:, g0 + 2 * HEADS:], w[:, g0:g0 + 2 * HEADS],
         jnp.zeros((D_MODEL, 128 - 2 * HEADS), F32)], axis=1).astype(BF16)
    gbias = jnp.concatenate([b_igate[l], b_fgate[l], jnp.zeros((128 - 2 * HEADS,), F32)])[None, :]
    mixer_w = (wcat, norm_mix_g[l][None, :], gbias, head_norm_g[l][None, :], w_pool[l].astype(BF16),
               pool_scale[l][None, :], w_out[l].astype(BF16))

    x1p, Cp, Np, Mp, Bp = _mixer(x_prompt, *mixer_w, None, BB=1, TT=512, L=128, start_pos=0)
    state = (state_C[l], state_n[l], state_m[l][..., None], state_pool[l])
    x1s_, Cs, Ns, Ms, Bs_ = _mixer(x_sample, *mixer_w, state, BB=32, TT=Ts, L=Ts, start_pos=PAST_LEN)

    x1 = jnp.concatenate([x1p.reshape(B * T, D_MODEL), x1s_.reshape(Bs * Ts, D_MODEL)], axis=0)
    p = jnp.concatenate([p_prompt[l].reshape(B * T, PLE_DIM), p_sample[l].reshape(Bs * Ts, PLE_DIM)], axis=0)
    N = x1.shape[0]
    assert N % TOKEN_TILE == 0 and TOP_K * N < (1 << PAIR_BITS)

    xnp, ids_t, wts_t, x1sh = _router(
        x1, norm_ffn_g[l][None, :], router_w[l].T, router_bias[l][:, None],
        sh_w1[l].astype(BF16), sh_w3[l].astype(BF16), sh_w2[l].astype(BF16), TM=TOKEN_TILE)

    n_pairs = TOP_K * N
    t_max = n_pairs // EXPERT_TILE + N_EXPERTS
    src, dst, tile_e, n_tiles = _sorted_pair_plan(ids_t, N, EXPERT_TILE, t_max)
    y8 = _moe_sparse(xnp, src, dst, tile_e, n_tiles, ex_w1[l], ex_w3[l], ex_w2[l], n_pairs=n_pairs)
    y = _final(x1sh, wts_t.T, y8, p, norm_ple_g[l][None, :], w_ple_gate[l].astype(BF16),
               w_ple_proj[l].astype(BF16), final_norm_g[None, :], TM=TOKEN_TILE)

    y_prompt = y[:B * T].reshape(B, T, D_MODEL)
    y_sample = y[B * T:].reshape(Bs, Ts, D_MODEL)
    return (y_prompt, y_sample,
            Cp[None], Np[None], Mp[..., 0][None], Bp[None],
            Cs[None], Ns[None], Ms[..., 0][None], Bs_[None])
```

```python
import functools

import jax
import jax.numpy as jnp
from jax import lax
from jax.experimental import pallas as pl
from jax.experimental.pallas import tpu as pltpu

D_MODEL = 1024
HEADS = 4
DH = 128
MLSTM_WIDTH = HEADS * DH
POOL_WIDTH = 512
POOL_WINDOWS = (2, 4, 8, 16)
POOL_GDIM = 128
POOL_BUF = 15
POOL_PAD = 16
N_EXPERTS = 64
TOP_K = 8
N_GROUPS = 8
GROUP_SIZE = N_EXPERTS // N_GROUPS
TOPK_GROUPS = 4
EXPERT_FF = 256
ROUTED_SCALE = 2.5
NORM_EPS = 1e-6
PLE_DIM = 256
PAST_LEN = 16384

COL_Q, COL_K, COL_V, COL_O, COL_U, COL_G = 0, 512, 1024, 1536, 2048, 2560
IN_COLS_PAD = 2688

ROUTE_TILE = 256
EXPERT_TILE = 512
SEG_ALIGN = 16
CAP = 64
CHUNK_E = 8
N_CHUNKS = N_EXPERTS // CHUNK_E

VMEM_LIMIT = 56 * 1024 * 1024
F32 = jnp.float32
BF16 = jnp.bfloat16
NEG_INF = float("-inf")


def _rms(x, g):
    return x * lax.rsqrt(jnp.mean(x * x, axis=-1, keepdims=True) + NORM_EPS) * g


def _log_sigmoid(x):
    return jnp.minimum(x, 0.0) - jnp.log1p(jnp.exp(-jnp.abs(x)))


def _dot(a, b):
    return jnp.dot(a, b, preferred_element_type=F32)


def _mlstm_chunk(q, k, v, b_col, i_col, C, n, m, causal, eye):
    L = q.shape[0]
    r_col = i_col - b_col
    r_row = jnp.sum(jnp.where(eye, r_col, 0.0), axis=0, keepdims=True)
    d = jnp.where(causal, b_col + r_row, NEG_INF)
    inter = b_col + m
    m_t = jnp.maximum(inter, jnp.max(d, axis=-1, keepdims=True))
    w_inter = jnp.exp(inter - m_t)
    qb, kb, vb = q.astype(BF16), k.astype(BF16), v.astype(BF16)
    qk = lax.dot_general(qb, kb, (((1,), (1,)), ((), ())), preferred_element_type=F32)
    s = qk * jnp.exp(d - m_t)
    num = w_inter * _dot(qb, C.astype(BF16)) + _dot(s.astype(BF16), vb)
    nq = w_inter * jnp.sum(q * n, axis=-1, keepdims=True) + jnp.sum(s, axis=-1, keepdims=True)
    h = num / jnp.maximum(jnp.abs(nq), jnp.exp(-m_t))
    b_last = b_col[L - 1:L, :]
    m_new = jnp.maximum(b_last + m, jnp.max(b_last + r_row, axis=-1, keepdims=True))
    fw = jnp.exp(b_last + m - m_new)
    iw_col = jnp.exp(b_last + r_col - m_new)
    kw = iw_col * k
    C_new = fw * C + lax.dot_general(kw.astype(BF16), vb, (((0,), (0,)), ((), ())),
                                     preferred_element_type=F32)
    n_new = fw * n + jnp.sum(kw, axis=0, keepdims=True)
    return h, C_new, n_new, m_new


def _mixer_kernel(*refs, BB, TT, L, start_pos, zero_state):
    if zero_state:
        (x_ref, wcat_ref, gmix_ref, gbias_ref, hng_ref, wpool_ref, pscale_ref, wout_ref,
         x1_ref, c_ref, n_ref, m_ref, buf_ref,
         z_ref, gi_ref, lf_ref, mix_ref, ext_ref) = refs
    else:
        (x_ref, wcat_ref, gmix_ref, gbias_ref, hng_ref, wpool_ref, pscale_ref, wout_ref,
         c0_ref, n0_ref, m0_ref, buf0_ref,
         x1_ref, c_ref, n_ref, m_ref, buf_ref,
         z_ref, gi_ref, lf_ref, mix_ref, ext_ref) = refs
    t = pl.program_id(1)
    R = BB * TT
    n_chunks = TT // L

    @pl.when(t == 0)
    def _init():
        ext_ref[:, 0:POOL_PAD, :] = jnp.zeros((BB, POOL_PAD, POOL_WIDTH), F32)
        if zero_state:
            c_ref[...] = jnp.zeros(c_ref.shape, F32)
            n_ref[...] = jnp.zeros(n_ref.shape, F32)
            m_ref[...] = jnp.zeros(m_ref.shape, F32)
        else:
            c_ref[...] = c0_ref[...]
            n_ref[...] = n0_ref[...]
            m_ref[...] = m0_ref[...]
            ext_ref[:, 1:POOL_PAD, :] = buf0_ref[...]

    x = x_ref[...].reshape(R, D_MODEL)
    hn = _rms(x, gmix_ref[...])
    z_ref[...] = _dot(hn.astype(BF16), wcat_ref[...])
    g = z_ref[:, COL_G:COL_G + 128] + gbias_ref[...]
    gi_ref[...] = g
    lf_ref[...] = _log_sigmoid(g)

    row = lax.broadcasted_iota(jnp.int32, (L, L), 0)
    col = lax.broadcasted_iota(jnp.int32, (L, L), 1)
    causal = row >= col
    eye = row == col
    tril = causal.astype(F32)
    hng = hng_ref[...]

    def chunk_body(s, carry):
        row0 = pl.multiple_of(s * L, L)
        bb = s // n_chunks
        rows = pl.ds(row0, L)
        bcum = jnp.dot(tril, lf_ref[rows, :], preferred_element_type=F32,
                       precision=lax.Precision.HIGHEST)
        gi = gi_ref[rows, :]
        for hd in range(HEADS):
            lanes = slice(hd * DH, (hd + 1) * DH)
            q = z_ref[rows, COL_Q + hd * DH:COL_Q + (hd + 1) * DH] * (DH ** -0.5)
            k = z_ref[rows, COL_K + hd * DH:COL_K + (hd + 1) * DH]
            v = z_ref[rows, COL_V + hd * DH:COL_V + (hd + 1) * DH]
            o = z_ref[rows, COL_O + hd * DH:COL_O + (hd + 1) * DH]
            C = c_ref[bb, hd]
            n = n_ref[bb, hd:hd + 1, :]
            m = m_ref[bb, hd:hd + 1, :]
            h, C_new, n_new, m_new = _mlstm_chunk(
                q, k, v, bcum[:, HEADS + hd:HEADS + hd + 1], gi[:, hd:hd + 1], C, n, m, causal, eye)
            c_ref[bb, hd] = C_new
            n_ref[bb, hd:hd + 1, :] = n_new
            m_ref[bb, hd:hd + 1, :] = m_new
            h = h * lax.rsqrt(jnp.mean(h * h, axis=-1, keepdims=True) + NORM_EPS)
            mix_ref[rows, lanes] = h * hng[:, lanes] * jax.nn.sigmoid(o)
        return carry

    lax.fori_loop(0, BB * n_chunks, chunk_body, 0)

    ext_ref[:, POOL_PAD:POOL_PAD + TT, :] = z_ref[:, COL_U:COL_U + POOL_WIDTH].reshape(BB, TT, POOL_WIDTH)
    pos = start_pos + t * TT + lax.broadcasted_iota(jnp.int32, (1, TT, 1), 1)
    pscale = pscale_ref[...]
    for gidx, w in enumerate(POOL_WINDOWS):
        lanes = slice(gidx * POOL_GDIM, (gidx + 1) * POOL_GDIM)
        u_g = ext_ref[:, POOL_PAD:POOL_PAD + TT, lanes]
        acc = u_g
        for j in range(1, w):
            acc = acc + ext_ref[:, POOL_PAD - j:POOL_PAD - j + TT, lanes]
        cnt = jnp.minimum(pos + 1, w).astype(F32)
        pooled = (acc / cnt - u_g).reshape(R, POOL_GDIM)
        mixed = _dot(pooled.astype(BF16), wpool_ref[gidx]) * pscale[:, lanes]
        mix_ref[:, MLSTM_WIDTH + gidx * POOL_GDIM:MLSTM_WIDTH + (gidx + 1) * POOL_GDIM] = mixed
    new_buf = ext_ref[:, TT + 1:TT + POOL_PAD, :]
    buf_ref[...] = new_buf
    ext_ref[:, 1:POOL_PAD, :] = new_buf

    out = x_ref[...].reshape(R, D_MODEL) + _dot(mix_ref[...].astype(BF16), wout_ref[...])
    x1_ref[...] = out.reshape(BB, TT, D_MODEL)


def _const_spec(shape):
    nd = len(shape)
    return pl.BlockSpec(shape, lambda b, t, _nd=nd: (0,) * _nd)


def _mixer(x, wcat, gmix, gbias, hng, wpool, pscale, wout, state, *, BB, TT, L, start_pos):
    B, T, _ = x.shape
    zero_state = state is None
    R = BB * TT
    grid = (B // BB, T // TT)
    x_spec = pl.BlockSpec((BB, TT, D_MODEL), lambda b, t: (b, t, 0))
    c_spec = pl.BlockSpec((BB, HEADS, DH, DH), lambda b, t: (b, 0, 0, 0))
    n_spec = pl.BlockSpec((BB, HEADS, DH), lambda b, t: (b, 0, 0))
    m_spec = pl.BlockSpec((BB, HEADS, 1), lambda b, t: (b, 0, 0))
    buf_spec = pl.BlockSpec((BB, POOL_BUF, POOL_WIDTH), lambda b, t: (b, 0, 0))
    weights = (wcat, gmix, gbias, hng, wpool, pscale, wout)
    in_specs = [x_spec] + [_const_spec(w.shape) for w in weights]
    args = [x, *weights]
    if not zero_state:
        in_specs += [c_spec, n_spec, m_spec, buf_spec]
        args += list(state)
    out_shape = (
        jax.ShapeDtypeStruct((B, T, D_MODEL), F32),
        jax.ShapeDtypeStruct((B, HEADS, DH, DH), F32),
        jax.ShapeDtypeStruct((B, HEADS, DH), F32),
        jax.ShapeDtypeStruct((B, HEADS, 1), F32),
        jax.ShapeDtypeStruct((B, POOL_BUF, POOL_WIDTH), F32),
    )
    kern = functools.partial(_mixer_kernel, BB=BB, TT=TT, L=L, start_pos=start_pos, zero_state=zero_state)
    return pl.pallas_call(
        kern,
        grid=grid,
        in_specs=in_specs,
        out_specs=(x_spec, c_spec, n_spec, m_spec, buf_spec),
        out_shape=out_shape,
        scratch_shapes=[
            pltpu.VMEM((R, IN_COLS_PAD), F32),
            pltpu.VMEM((R, 128), F32),
            pltpu.VMEM((R, 128), F32),
            pltpu.VMEM((R, D_MODEL), F32),
            pltpu.VMEM((BB, POOL_PAD + TT, POOL_WIDTH), F32),
        ],
        compiler_params=pltpu.CompilerParams(
            dimension_semantics=("arbitrary", "arbitrary"), vmem_limit_bytes=VMEM_LIMIT),
        name="mixer_zero" if zero_state else "mixer_state",
    )(*args)


def _first_index_of_max(work, idx, n, axis):
    mx = jnp.max(work, axis=axis, keepdims=True)
    return jnp.min(jnp.where(work == mx, idx, float(n)), axis=axis, keepdims=True)


def _route(scores_t, bias_t):
    tm = scores_t.shape[1]
    biased = scores_t + bias_t
    b3 = biased.reshape(N_GROUPS, GROUP_SIZE, tm)
    sub = lax.broadcasted_iota(jnp.int32, b3.shape, 1).astype(F32)
    m1 = jnp.max(b3, axis=1, keepdims=True)
    first = jnp.min(jnp.where(b3 == m1, sub, float(GROUP_SIZE)), axis=1, keepdims=True)
    m2 = jnp.max(jnp.where(sub == first, NEG_INF, b3), axis=1, keepdims=True)
    gs = (m1 + m2).reshape(N_GROUPS, tm)
    gidx = lax.broadcasted_iota(jnp.int32, gs.shape, 0).astype(F32)
    gsel = jnp.zeros(gs.shape, F32)
    work = gs
    for _ in range(TOPK_GROUPS):
        pick = gidx == _first_index_of_max(work, gidx, N_GROUPS, 0)
        gsel = jnp.where(pick, 1.0, gsel)
        work = jnp.where(pick, NEG_INF, work)
    emask = jnp.broadcast_to(gsel.reshape(N_GROUPS, 1, tm), b3.shape).reshape(N_EXPERTS, tm)
    work = jnp.where(emask > 0, biased, NEG_INF)
    eidx = lax.broadcasted_iota(jnp.int32, work.shape, 0).astype(F32)
    mask = jnp.zeros(work.shape, F32)
    for _ in range(TOP_K):
        pick = eidx == _first_index_of_max(work, eidx, N_EXPERTS, 0)
        mask = jnp.where(pick, 1.0, mask)
        work = jnp.where(pick, NEG_INF, work)
    sel = mask * scores_t
    return mask, sel / jnp.sum(sel, axis=0, keepdims=True) * ROUTED_SCALE


def _router_kernel(x1_ref, gffn_ref, rwt_ref, rbias_ref, w1_ref, w3_ref, w2_ref,
                   xn_ref, rank_ref, gates_ref, cnt_ref, x1s_ref):
    x1 = x1_ref[...]
    tm = x1.shape[0]
    xn = _rms(x1, gffn_ref[...])
    xb = xn.astype(BF16)
    xn_ref[...] = xb
    logits_t = lax.dot_general(rwt_ref[...], xn, (((1,), (1,)), ((), ())),
                               preferred_element_type=F32, precision=lax.Precision.HIGHEST)
    mask, gates = _route(jax.nn.sigmoid(logits_t), rbias_ref[...])
    gates_ref[...] = gates
    before = (lax.broadcasted_iota(jnp.int32, (tm, tm), 0)
              < lax.broadcasted_iota(jnp.int32, (tm, tm), 1)).astype(BF16)
    rank = _dot(mask.astype(BF16), before)
    rank_ref[...] = jnp.where(mask > 0, rank, -1.0)
    cnt_ref[...] = jnp.broadcast_to(jnp.sum(mask, axis=1, keepdims=True), cnt_ref.shape)
    a = _dot(xb, w1_ref[...])
    hsh = (a * jax.nn.sigmoid(a)) * _dot(xb, w3_ref[...])
    x1s_ref[...] = x1 + _dot(hsh.astype(BF16), w2_ref[...])


def _router(x1, gffn, rwt, rbias, w1, w3, w2, *, TM):
    N = x1.shape[0]
    tok = pl.BlockSpec((TM, D_MODEL), lambda i: (i, 0))
    per_e = pl.BlockSpec((N_EXPERTS, TM), lambda i: (0, i))
    consts = (gffn, rwt, rbias, w1, w3, w2)
    return pl.pallas_call(
        _router_kernel,
        grid=(N // TM,),
        in_specs=[tok] + [pl.BlockSpec(c.shape, lambda i: (0, 0)) for c in consts],
        out_specs=(tok, per_e, per_e, pl.BlockSpec((N_EXPERTS, 128), lambda i: (i, 0)), tok),
        out_shape=(jax.ShapeDtypeStruct((N, D_MODEL), BF16),
                   jax.ShapeDtypeStruct((N_EXPERTS, N), F32),
                   jax.ShapeDtypeStruct((N_EXPERTS, N), F32),
                   jax.ShapeDtypeStruct((N // TM * N_EXPERTS, 128), F32),
                   jax.ShapeDtypeStruct((N, D_MODEL), F32)),
        compiler_params=pltpu.CompilerParams(
            dimension_semantics=("arbitrary",), vmem_limit_bytes=VMEM_LIMIT),
        name="router_shared",
    )(x1, *consts)


def _segment_plan(cnt, n_tiles_max):
    seg = (cnt + SEG_ALIGN - 1) // SEG_ALIGN * SEG_ALIGN
    used = jnp.sum(seg, axis=0)
    size = (used + CAP + EXPERT_TILE - 1) // EXPERT_TILE * EXPERT_TILE
    row_end = jnp.cumsum(size)
    row_start = row_end - size
    base = row_start[None, :] + jnp.cumsum(seg, axis=0) - seg
    nwin = jnp.maximum((seg + CAP - 1) // CAP, 1)
    tile_end = row_end // EXPERT_TILE
    n_tiles = tile_end[-1]
    t_ids = jnp.arange(n_tiles_max, dtype=jnp.int32)
    tile_e = jnp.minimum(jnp.sum(t_ids[:, None] >= tile_end[None, :], axis=1), N_EXPERTS - 1)
    i32 = lambda a: a.astype(jnp.int32)
    return (i32(base).reshape(-1), i32(nwin).reshape(-1), i32(row_start), i32(row_end),
            i32(tile_e), i32(n_tiles).reshape(1))


def _one_hot_rows(rank_ref, chunk, first_row, values_ref=None):
    tm = rank_ref.shape[1]
    j = (lax.broadcasted_iota(jnp.int32, (CAP, tm), 0) + first_row).astype(F32)
    rows = []
    for k in range(CHUNK_E):
        e = chunk * CHUNK_E + k
        hit = j == rank_ref[e:e + 1, :]
        val = 1.0 if values_ref is None else values_ref[e:e + 1, :]
        rows.append(jnp.where(hit, val, 0.0).astype(BF16))
    return jnp.concatenate(rows, axis=0)


def _window(hbm, base_ref, idx, w):
    start = pl.multiple_of(base_ref[idx] + w * CAP, SEG_ALIGN)
    return hbm.at[pl.ds(start, CAP)]


def _dispatch_kernel(base_ref, nwin_ref, rstart_ref, rend_ref, xn_ref, rank_ref, xs_hbm, *scratch):
    stage = scratch[:N_CHUNKS]
    ostage, zbuf, sem, osem, zsem = scratch[N_CHUNKS:]
    i = pl.program_id(0)
    last = pl.num_programs(0) - 1

    def window_copy(c, k, tile_idx):
        return pltpu.make_async_copy(stage[c].at[pl.ds(k * CAP, CAP)],
                                     _window(xs_hbm, base_ref, tile_idx * N_EXPERTS + c * CHUNK_E + k, 0),
                                     sem.at[c])

    @pl.when(i == 0)
    def _zero_tails():
        zbuf[...] = jnp.zeros(zbuf.shape, BF16)
        copies = []
        for e in range(N_EXPERTS):
            end = rend_ref[e]
            a0 = pl.multiple_of(end - EXPERT_TILE, SEG_ALIGN)
            b0 = pl.multiple_of(jnp.maximum(end - EXPERT_TILE - CAP, rstart_ref[e]), SEG_ALIGN)
            copies.append(pltpu.make_async_copy(zbuf, xs_hbm.at[pl.ds(a0, EXPERT_TILE)], zsem))
            copies.append(pltpu.make_async_copy(zbuf.at[pl.ds(0, CAP)], xs_hbm.at[pl.ds(b0, CAP)], zsem))
        for cp in copies:
            cp.start()
        for cp in copies:
            cp.wait()

    xb = xn_ref[...]
    for c in range(N_CHUNKS):
        @pl.when(i > 0)
        def _reuse(c=c):
            for k in range(CHUNK_E):
                window_copy(c, k, i - 1).wait()

        stage[c][...] = _dot(_one_hot_rows(rank_ref, c, 0), xb).astype(BF16)
        for k in range(CHUNK_E):
            window_copy(c, k, i).start()

        extra = nwin_ref[i * N_EXPERTS + c * CHUNK_E]
        for k in range(1, CHUNK_E):
            extra = jnp.maximum(extra, nwin_ref[i * N_EXPERTS + c * CHUNK_E + k])

        @pl.when(extra > 1)
        def _long_segments(c=c, extra=extra):
            def body(w, carry):
                ostage[...] = _dot(_one_hot_rows(rank_ref, c, w * CAP), xb).astype(BF16)
                for k in range(CHUNK_E):
                    idx = i * N_EXPERTS + c * CHUNK_E + k

                    @pl.when(w < nwin_ref[idx])
                    def _(k=k, idx=idx):
                        cp = pltpu.make_async_copy(ostage.at[pl.ds(k * CAP, CAP)],
                                                   _window(xs_hbm, base_ref, idx, w), osem)
                        cp.start()
                        cp.wait()
                return carry

            lax.fori_loop(1, extra, body, 0)

    @pl.when(i == last)
    def _drain():
        for c in range(N_CHUNKS):
            for k in range(CHUNK_E):
                window_copy(c, k, i).wait()


def _dispatch(xn, rank_t, base, nwin, row_start, row_end, *, n_rows, TM):
    N = xn.shape[0]
    stage = pltpu.VMEM((CHUNK_E * CAP, D_MODEL), BF16)
    return pl.pallas_call(
        _dispatch_kernel,
        grid_spec=pltpu.PrefetchScalarGridSpec(
            num_scalar_prefetch=4,
            grid=(N // TM,),
            in_specs=[pl.BlockSpec((TM, D_MODEL), lambda i, *_: (i, 0)),
                      pl.BlockSpec((N_EXPERTS, TM), lambda i, *_: (0, i))],
            out_specs=pl.BlockSpec(memory_space=pl.ANY),
            scratch_shapes=[stage] * N_CHUNKS + [
                stage,
                pltpu.VMEM((EXPERT_TILE, D_MODEL), BF16),
                pltpu.SemaphoreType.DMA((N_CHUNKS,)),
                pltpu.SemaphoreType.DMA(()),
                pltpu.SemaphoreType.DMA(()),
            ],
        ),
        out_shape=jax.ShapeDtypeStruct((n_rows, D_MODEL), BF16),
        compiler_params=pltpu.CompilerParams(
            dimension_semantics=("arbitrary",), vmem_limit_bytes=VMEM_LIMIT),
        name="moe_dispatch",
    )(base, nwin, row_start, row_end, xn, rank_t)


def _expert_kernel(tile_e_ref, n_tiles_ref, xs_ref, w1_ref, w3_ref, w2_ref, ys_ref, w1b, w3b, w2b):
    t = pl.program_id(0)
    valid = t < n_tiles_ref[0]
    prev_e = tile_e_ref[jnp.maximum(t - 1, 0)]

    @pl.when(valid & ((t == 0) | (tile_e_ref[t] != prev_e)))
    def _cast_weights():
        w1b[...] = w1_ref[0].astype(BF16)
        w3b[...] = w3_ref[0].astype(BF16)
        w2b[...] = w2_ref[0].astype(BF16)

    @pl.when(valid)
    def _compute():
        xb = xs_ref[...]
        a = _dot(xb, w1b[...])
        hb = (a * jax.nn.sigmoid(a)) * _dot(xb, w3b[...])
        ys_ref[...] = _dot(hb.astype(BF16), w2b[...]).astype(BF16)

    @pl.when(jnp.logical_not(valid))
    def _spare():
        ys_ref[...] = jnp.zeros(ys_ref.shape, BF16)


def _experts(xs, tile_e, n_tiles, w1, w3, w2):
    t_max = xs.shape[0] // EXPERT_TILE
    clamp = lambda t, nt: jnp.minimum(t, nt[0] - 1)
    wspec_in = pl.BlockSpec((1, D_MODEL, EXPERT_FF), lambda t, te, nt: (te[clamp(t, nt)], 0, 0))
    wspec_out = pl.BlockSpec((1, EXPERT_FF, D_MODEL), lambda t, te, nt: (te[clamp(t, nt)], 0, 0))
    return pl.pallas_call(
        _expert_kernel,
        grid_spec=pltpu.PrefetchScalarGridSpec(
            num_scalar_prefetch=2,
            grid=(t_max,),
            in_specs=[pl.BlockSpec((EXPERT_TILE, D_MODEL), lambda t, te, nt: (clamp(t, nt), 0)),
                      wspec_in, wspec_in, wspec_out],
            out_specs=pl.BlockSpec((EXPERT_TILE, D_MODEL),
                                   lambda t, te, nt: (jnp.where(t < nt[0], t, t_max), 0)),
            scratch_shapes=[
                pltpu.VMEM((D_MODEL, EXPERT_FF), BF16),
                pltpu.VMEM((D_MODEL, EXPERT_FF), BF16),
                pltpu.VMEM((EXPERT_FF, D_MODEL), BF16),
            ],
        ),
        out_shape=jax.ShapeDtypeStruct(((t_max + 1) * EXPERT_TILE, D_MODEL), BF16),
        compiler_params=pltpu.CompilerParams(
            dimension_semantics=("arbitrary",), vmem_limit_bytes=VMEM_LIMIT),
        name="moe_experts",
    )(tile_e, n_tiles, xs, w1, w3, w2)


def _final_kernel(base_ref, nwin_ref, x1s_ref, rank_ref, gates_ref, ys_hbm, p_ref, gple_ref,
                  wgate_ref, wproj_ref, gfin_ref, y_ref, win0, win1, owin, acc_ref, sem, osem):
    i = pl.program_id(0)
    last = pl.num_programs(0) - 1
    win = (win0, win1)

    def window_copy(c, k, tile_idx):
        slot = c % 2
        return pltpu.make_async_copy(_window(ys_hbm, base_ref, tile_idx * N_EXPERTS + c * CHUNK_E + k, 0),
                                     win[slot].at[pl.ds(k * CAP, CAP)], sem.at[slot])

    def fetch(c, tile_idx):
        for k in range(CHUNK_E):
            window_copy(c, k, tile_idx).start()

    @pl.when(i == 0)
    def _first():
        fetch(0, i)

    contract0 = (((0,), (0,)), ((), ()))

    def add_rows(c, first_row, rows_ref):
        part = lax.dot_general(_one_hot_rows(rank_ref, c, first_row, gates_ref), rows_ref[...], contract0,
                               preferred_element_type=F32)
        acc_ref[...] = part if (c == 0 and isinstance(first_row, int)) else acc_ref[...] + part

    for c in range(N_CHUNKS):
        if c + 1 < N_CHUNKS:
            fetch(c + 1, i)
        else:
            @pl.when(i < last)
            def _next_tile():
                fetch(0, i + 1)
        for k in range(CHUNK_E):
            window_copy(c, k, i).wait()
        add_rows(c, 0, win[c % 2])

        extra = nwin_ref[i * N_EXPERTS + c * CHUNK_E]
        for k in range(1, CHUNK_E):
            extra = jnp.maximum(extra, nwin_ref[i * N_EXPERTS + c * CHUNK_E + k])

        @pl.when(extra > 1)
        def _long_segments(c=c, extra=extra):
            def body(w, carry):
                for k in range(CHUNK_E):
                    idx = i * N_EXPERTS + c * CHUNK_E + k
                    ww = jnp.where(w < nwin_ref[idx], w, 0)
                    cp = pltpu.make_async_copy(_window(ys_hbm, base_ref, idx, ww),
                                               owin.at[pl.ds(k * CAP, CAP)], osem)
                    cp.start()
                    cp.wait()
                add_rows(c, w * CAP, owin)
                return carry

            lax.fori_loop(1, extra, body, 0)

    x2 = x1s_ref[...] + acc_ref[...]
    r_ = _rms(x2, gple_ref[...])
    gate = jax.nn.sigmoid(_dot(r_.astype(BF16), wgate_ref[...]))
    x3 = x2 + _dot(p_ref[...].astype(BF16), wproj_ref[...]) * gate
    y_ref[...] = _rms(x3, gfin_ref[...])


def _final(x1s, rank_t, gates_t, ys, p, gple, wgate, wproj, gfin, base, nwin, *, TM):
    N = x1s.shape[0]
    tok = pl.BlockSpec((TM, D_MODEL), lambda i, *_: (i, 0))
    per_e = pl.BlockSpec((N_EXPERTS, TM), lambda i, *_: (0, i))
    consts = (gple, wgate, wproj, gfin)
    win = pltpu.VMEM((CHUNK_E * CAP, D_MODEL), BF16)
    return pl.pallas_call(
        _final_kernel,
        grid_spec=pltpu.PrefetchScalarGridSpec(
            num_scalar_prefetch=2,
            grid=(N // TM,),
            in_specs=[tok, per_e, per_e, pl.BlockSpec(memory_space=pl.ANY),
                      pl.BlockSpec((TM, PLE_DIM), lambda i, *_: (i, 0))]
            + [pl.BlockSpec(c.shape, lambda i, *_: (0, 0)) for c in consts],
            out_specs=tok,
            scratch_shapes=[win, win, win, pltpu.VMEM((TM, D_MODEL), F32),
                            pltpu.SemaphoreType.DMA((2,)), pltpu.SemaphoreType.DMA(())],
        ),
        out_shape=jax.ShapeDtypeStruct((N, D_MODEL), F32),
        compiler_params=pltpu.CompilerParams(
            dimension_semantics=("arbitrary",), vmem_limit_bytes=VMEM_LIMIT),
        name="combine_ple_final",
    )(base, nwin, x1s, rank_t, gates_t, ys, p, *consts)


def kernel(x_prompt, x_sample, p_prompt, p_sample, state_C, state_n, state_m, state_pool, norm_mix_g, w_in, b_igate, b_fgate, head_norm_g, w_pool, pool_scale, w_out, norm_ffn_g, router_w, router_bias, ex_w1, ex_w3, ex_w2, sh_w1, sh_w3, sh_w2, norm_ple_g, w_ple_gate, w_ple_proj, final_norm_g):
    depth = norm_mix_g.shape[0]
    assert depth == 1
    l = 0
    B, T, _ = x_prompt.shape
    Bs, Ts, _ = x_sample.shape
    g0 = 4 * MLSTM_WIDTH
    w = w_in[l]
    wcat = jnp.concatenate(
        [w[:, :g0], w[:, g0 + 2 * HEADS:], w[:, g0:g0 + 2 * HEADS],
         jnp.zeros((D_MODEL, 128 - 2 * HEADS), F32)], axis=1).astype(BF16)
    gbias = jnp.concatenate([b_igate[l], b_fgate[l], jnp.zeros((128 - 2 * HEADS,), F32)])[None, :]
    mixer_w = (wcat, norm_mix_g[l][None, :], gbias, head_norm_g[l][None, :], w_pool[l].astype(BF16),
               pool_scale[l][None, :], w_out[l].astype(BF16))

    x1p, Cp, Np, Mp, Bp = _mixer(x_prompt, *mixer_w, None, BB=1, TT=512, L=128, start_pos=0)
    state = (state_C[l], state_n[l], state_m[l][..., None], state_pool[l])
    x1s_, Cs, Ns, Ms, Bs_ = _mixer(x_sample, *mixer_w, state, BB=32, TT=Ts, L=Ts, start_pos=PAST_LEN)

    x1 = jnp.concatenate([x1p.reshape(B * T, D_MODEL), x1s_.reshape(Bs * Ts, D_MODEL)], axis=0)
    p = jnp.concatenate([p_prompt[l].reshape(B * T, PLE_DIM), p_sample[l].reshape(Bs * Ts, PLE_DIM)], axis=0)
    N = x1.shape[0]
    assert N % ROUTE_TILE == 0
    n_route_tiles = N // ROUTE_TILE

    xn, rank_t, gates_t, cnt, x1sh = _router(
        x1, norm_ffn_g[l][None, :], router_w[l].T, router_bias[l][:, None],
        sh_w1[l].astype(BF16), sh_w3[l].astype(BF16), sh_w2[l].astype(BF16), TM=ROUTE_TILE)

    max_rows = TOP_K * N + (SEG_ALIGN - 1) * n_route_tiles * N_EXPERTS + N_EXPERTS * (CAP + EXPERT_TILE)
    t_max = -(-max_rows // EXPERT_TILE)
    cnt = cnt[:, 0].reshape(n_route_tiles, N_EXPERTS).astype(jnp.int32)
    base, nwin, row_start, row_end, tile_e, n_tiles = _segment_plan(cnt, t_max)

    xs = _dispatch(xn, rank_t, base, nwin, row_start, row_end, n_rows=t_max * EXPERT_TILE, TM=ROUTE_TILE)
    ys = _experts(xs, tile_e, n_tiles, ex_w1[l], ex_w3[l], ex_w2[l])
    y = _final(x1sh, rank_t, gates_t, ys, p, norm_ple_g[l][None, :], w_ple_gate[l].astype(BF16),
               w_ple_proj[l].astype(BF16), final_norm_g[None, :], base, nwin, TM=ROUTE_TILE)

    y_prompt = y[:B * T].reshape(B, T, D_MODEL)
    y_sample = y[B * T:].reshape(Bs, Ts, D_MODEL)
    return (y_prompt, y_sample,
            Cp[None], Np[None], Mp[..., 0][None], Bp[None],
            Cs[None], Ns[None], Ms[..., 0][None], Bs_[None])
```

```python
import functools

import jax
import jax.numpy as jnp
from jax import lax
from jax.experimental import pallas as pl
from jax.experimental.pallas import tpu as pltpu

D_MODEL = 1024
HEADS = 4
DH = 128
MLSTM_WIDTH = HEADS * DH
POOL_WIDTH = 512
POOL_WINDOWS = (2, 4, 8, 16)
POOL_GDIM = 128
POOL_BUF = 15
POOL_PAD = 16
N_EXPERTS = 64
TOP_K = 8
N_GROUPS = 8
GROUP_SIZE = N_EXPERTS // N_GROUPS
TOPK_GROUPS = 4
EXPERT_FF = 256
ROUTED_SCALE = 2.5
NORM_EPS = 1e-6
PLE_DIM = 256
PAST_LEN = 16384

COL_Q, COL_K, COL_V, COL_O, COL_U, COL_G = 0, 512, 1024, 1536, 2048, 2560
IN_COLS_PAD = 2688

ROUTE_TILE = 256
EXPERT_TILE = 512
SEG_ALIGN = 16
CAP = 64
CHUNK_E = 8
N_CHUNKS = N_EXPERTS // CHUNK_E

VMEM_LIMIT = 56 * 1024 * 1024
F32 = jnp.float32
BF16 = jnp.bfloat16
NEG_INF = float("-inf")


def _rms(x, g):
    return x * lax.rsqrt(jnp.mean(x * x, axis=-1, keepdims=True) + NORM_EPS) * g


def _log_sigmoid(x):
    return jnp.minimum(x, 0.0) - jnp.log1p(jnp.exp(-jnp.abs(x)))


def _dot(a, b):
    return jnp.dot(a, b, preferred_element_type=F32)


def _mlstm_chunk(q, k, v, b_col, i_col, C, n, m, causal, eye):
    L = q.shape[0]
    r_col = i_col - b_col
    r_row = jnp.sum(jnp.where(eye, r_col, 0.0), axis=0, keepdims=True)
    d = jnp.where(causal, b_col + r_row, NEG_INF)
    inter = b_col + m
    m_t = jnp.maximum(inter, jnp.max(d, axis=-1, keepdims=True))
    w_inter = jnp.exp(inter - m_t)
    qb, kb, vb = q.astype(BF16), k.astype(BF16), v.astype(BF16)
    qk = lax.dot_general(qb, kb, (((1,), (1,)), ((), ())), preferred_element_type=F32)
    s = qk * jnp.exp(d - m_t)
    num = w_inter * _dot(qb, C.astype(BF16)) + _dot(s.astype(BF16), vb)
    nq = w_inter * jnp.sum(q * n, axis=-1, keepdims=True) + jnp.sum(s, axis=-1, keepdims=True)
    h = num / jnp.maximum(jnp.abs(nq), jnp.exp(-m_t))
    b_last = b_col[L - 1:L, :]
    m_new = jnp.maximum(b_last + m, jnp.max(b_last + r_row, axis=-1, keepdims=True))
    fw = jnp.exp(b_last + m - m_new)
    iw_col = jnp.exp(b_last + r_col - m_new)
    kw = iw_col * k
    C_new = fw * C + lax.dot_general(kw.astype(BF16), vb, (((0,), (0,)), ((), ())),
                                     preferred_element_type=F32)
    n_new = fw * n + jnp.sum(kw, axis=0, keepdims=True)
    return h, C_new, n_new, m_new


def _mixer_kernel(*refs, BB, TT, L, start_pos, zero_state):
    if zero_state:
        (x_ref, wcat_ref, gmix_ref, gbias_ref, hng_ref, wpool_ref, pscale_ref, wout_ref,
         x1_ref, c_ref, n_ref, m_ref, buf_ref,
         z_ref, gi_ref, lf_ref, mix_ref, ext_ref) = refs
    else:
        (x_ref, wcat_ref, gmix_ref, gbias_ref, hng_ref, wpool_ref, pscale_ref, wout_ref,
         c0_ref, n0_ref, m0_ref, buf0_ref,
         x1_ref, c_ref, n_ref, m_ref, buf_ref,
         z_ref, gi_ref, lf_ref, mix_ref, ext_ref) = refs
    t = pl.program_id(1)
    R = BB * TT
    n_chunks = TT // L

    @pl.when(t == 0)
    def _init():
        ext_ref[:, 0:POOL_PAD, :] = jnp.zeros((BB, POOL_PAD, POOL_WIDTH), F32)
        if zero_state:
            c_ref[...] = jnp.zeros(c_ref.shape, F32)
            n_ref[...] = jnp.zeros(n_ref.shape, F32)
            m_ref[...] = jnp.zeros(m_ref.shape, F32)
        else:
            c_ref[...] = c0_ref[...]
            n_ref[...] = n0_ref[...]
            m_ref[...] = m0_ref[...]
            ext_ref[:, 1:POOL_PAD, :] = buf0_ref[...]

    x = x_ref[...].reshape(R, D_MODEL)
    hn = _rms(x, gmix_ref[...])
    z_ref[...] = _dot(hn.astype(BF16), wcat_ref[...])
    g = z_ref[:, COL_G:COL_G + 128] + gbias_ref[...]
    gi_ref[...] = g
    lf_ref[...] = _log_sigmoid(g)

    row = lax.broadcasted_iota(jnp.int32, (L, L), 0)
    col = lax.broadcasted_iota(jnp.int32, (L, L), 1)
    causal = row >= col
    eye = row == col
    tril = causal.astype(F32)
    hng = hng_ref[...]

    def chunk_body(s, carry):
        row0 = pl.multiple_of(s * L, L)
        bb = s // n_chunks
        rows = pl.ds(row0, L)
        bcum = jnp.dot(tril, lf_ref[rows, :], preferred_element_type=F32,
                       precision=lax.Precision.HIGHEST)
        gi = gi_ref[rows, :]
        for hd in range(HEADS):
            lanes = slice(hd * DH, (hd + 1) * DH)
            q = z_ref[rows, COL_Q + hd * DH:COL_Q + (hd + 1) * DH] * (DH ** -0.5)
            k = z_ref[rows, COL_K + hd * DH:COL_K + (hd + 1) * DH]
            v = z_ref[rows, COL_V + hd * DH:COL_V + (hd + 1) * DH]
            o = z_ref[rows, COL_O + hd * DH:COL_O + (hd + 1) * DH]
            C = c_ref[bb, hd]
            n = n_ref[bb, hd:hd + 1, :]
            m = m_ref[bb, hd:hd + 1, :]
            h, C_new, n_new, m_new = _mlstm_chunk(
                q, k, v, bcum[:, HEADS + hd:HEADS + hd + 1], gi[:, hd:hd + 1], C, n, m, causal, eye)
            c_ref[bb, hd] = C_new
            n_ref[bb, hd:hd + 1, :] = n_new
            m_ref[bb, hd:hd + 1, :] = m_new
            h = h * lax.rsqrt(jnp.mean(h * h, axis=-1, keepdims=True) + NORM_EPS)
            mix_ref[rows, lanes] = h * hng[:, lanes] * jax.nn.sigmoid(o)
        return carry

    lax.fori_loop(0, BB * n_chunks, chunk_body, 0)

    ext_ref[:, POOL_PAD:POOL_PAD + TT, :] = z_ref[:, COL_U:COL_U + POOL_WIDTH].reshape(BB, TT, POOL_WIDTH)
    pos = start_pos + t * TT + lax.broadcasted_iota(jnp.int32, (1, TT, 1), 1)
    pscale = pscale_ref[...]
    for gidx, w in enumerate(POOL_WINDOWS):
        lanes = slice(gidx * POOL_GDIM, (gidx + 1) * POOL_GDIM)
        u_g = ext_ref[:, POOL_PAD:POOL_PAD + TT, lanes]
        acc = u_g
        for j in range(1, w):
            acc = acc + ext_ref[:, POOL_PAD - j:POOL_PAD - j + TT, lanes]
        cnt = jnp.minimum(pos + 1, w).astype(F32)
        pooled = (acc / cnt - u_g).reshape(R, POOL_GDIM)
        mixed = _dot(pooled.astype(BF16), wpool_ref[gidx]) * pscale[:, lanes]
        mix_ref[:, MLSTM_WIDTH + gidx * POOL_GDIM:MLSTM_WIDTH + (gidx + 1) * POOL_GDIM] = mixed
    new_buf = ext_ref[:, TT + 1:TT + POOL_PAD, :]
    buf_ref[...] = new_buf
    ext_ref[:, 1:POOL_PAD, :] = new_buf

    out = x_ref[...].reshape(R, D_MODEL) + _dot(mix_ref[...].astype(BF16), wout_ref[...])
    x1_ref[...] = out.reshape(BB, TT, D_MODEL)


def _const_spec(shape):
    nd = len(shape)
    return pl.BlockSpec(shape, lambda b, t, _nd=nd: (0,) * _nd)


def _mixer(x, wcat, gmix, gbias, hng, wpool, pscale, wout, state, *, BB, TT, L, start_pos):
    B, T, _ = x.shape
    zero_state = state is None
    R = BB * TT
    grid = (B // BB, T // TT)
    x_spec = pl.BlockSpec((BB, TT, D_MODEL), lambda b, t: (b, t, 0))
    c_spec = pl.BlockSpec((BB, HEADS, DH, DH), lambda b, t: (b, 0, 0, 0))
    n_spec = pl.BlockSpec((BB, HEADS, DH), lambda b, t: (b, 0, 0))
    m_spec = pl.BlockSpec((BB, HEADS, 1), lambda b, t: (b, 0, 0))
    buf_spec = pl.BlockSpec((BB, POOL_BUF, POOL_WIDTH), lambda b, t: (b, 0, 0))
    weights = (wcat, gmix, gbias, hng, wpool, pscale, wout)
    in_specs = [x_spec] + [_const_spec(w.shape) for w in weights]
    args = [x, *weights]
    if not zero_state:
        in_specs += [c_spec, n_spec, m_spec, buf_spec]
        args += list(state)
    out_shape = (
        jax.ShapeDtypeStruct((B, T, D_MODEL), F32),
        jax.ShapeDtypeStruct((B, HEADS, DH, DH), F32),
        jax.ShapeDtypeStruct((B, HEADS, DH), F32),
        jax.ShapeDtypeStruct((B, HEADS, 1), F32),
        jax.ShapeDtypeStruct((B, POOL_BUF, POOL_WIDTH), F32),
    )
    kern = functools.partial(_mixer_kernel, BB=BB, TT=TT, L=L, start_pos=start_pos, zero_state=zero_state)
    return pl.pallas_call(
        kern,
        grid=grid,
        in_specs=in_specs,
        out_specs=(x_spec, c_spec, n_spec, m_spec, buf_spec),
        out_shape=out_shape,
        scratch_shapes=[
            pltpu.VMEM((R, IN_COLS_PAD), F32),
            pltpu.VMEM((R, 128), F32),
            pltpu.VMEM((R, 128), F32),
            pltpu.VMEM((R, D_MODEL), F32),
            pltpu.VMEM((BB, POOL_PAD + TT, POOL_WIDTH), F32),
        ],
        compiler_params=pltpu.CompilerParams(
            dimension_semantics=("arbitrary", "arbitrary"), vmem_limit_bytes=VMEM_LIMIT),
        name="mixer_zero" if zero_state else "mixer_state",
    )(*args)


def _first_index_of_max(work, idx, n, axis):
    mx = jnp.max(work, axis=axis, keepdims=True)
    return jnp.min(jnp.where(work == mx, idx, float(n)), axis=axis, keepdims=True)


def _route(scores_t, bias_t):
    tm = scores_t.shape[1]
    biased = scores_t + bias_t
    b3 = biased.reshape(N_GROUPS, GROUP_SIZE, tm)
    sub = lax.broadcasted_iota(jnp.int32, b3.shape, 1).astype(F32)
    m1 = jnp.max(b3, axis=1, keepdims=True)
    first = jnp.min(jnp.where(b3 == m1, sub, float(GROUP_SIZE)), axis=1, keepdims=True)
    m2 = jnp.max(jnp.where(sub == first, NEG_INF, b3), axis=1, keepdims=True)
    gs = (m1 + m2).reshape(N_GROUPS, tm)
    gidx = lax.broadcasted_iota(jnp.int32, gs.shape, 0).astype(F32)
    gsel = jnp.zeros(gs.shape, F32)
    work = gs
    for _ in range(TOPK_GROUPS):
        pick = gidx == _first_index_of_max(work, gidx, N_GROUPS, 0)
        gsel = jnp.where(pick, 1.0, gsel)
        work = jnp.where(pick, NEG_INF, work)
    emask = jnp.broadcast_to(gsel.reshape(N_GROUPS, 1, tm), b3.shape).reshape(N_EXPERTS, tm)
    work = jnp.where(emask > 0, biased, NEG_INF)
    eidx = lax.broadcasted_iota(jnp.int32, work.shape, 0).astype(F32)
    mask = jnp.zeros(work.shape, F32)
    for _ in range(TOP_K):
        pick = eidx == _first_index_of_max(work, eidx, N_EXPERTS, 0)
        mask = jnp.where(pick, 1.0, mask)
        work = jnp.where(pick, NEG_INF, work)
    sel = mask * scores_t
    return mask, sel / jnp.sum(sel, axis=0, keepdims=True) * ROUTED_SCALE


def _group_specs(block, split):
    return (pl.BlockSpec(block, lambda i, *_: (jnp.minimum(i, split - 1), 0)),
            pl.BlockSpec(block, lambda i, *_: (jnp.maximum(i - split, 0), 0)))


def _router_kernel(x1a_ref, x1b_ref, gffn_ref, rwt_ref, rbias_ref, w1_ref, w3_ref, w2_ref,
                   xn_ref, rank_ref, gates_ref, cnt_ref, x1s_ref, *, split):
    x1 = jnp.where(pl.program_id(0) < split, x1a_ref[...], x1b_ref[...])
    tm = x1.shape[0]
    xn = _rms(x1, gffn_ref[...])
    xb = xn.astype(BF16)
    xn_ref[...] = xb
    logits_t = lax.dot_general(rwt_ref[...], xn, (((1,), (1,)), ((), ())),
                               preferred_element_type=F32, precision=lax.Precision.HIGHEST)
    mask, gates = _route(jax.nn.sigmoid(logits_t), rbias_ref[...])
    gates_ref[...] = gates
    before = (lax.broadcasted_iota(jnp.int32, (tm, tm), 0)
              < lax.broadcasted_iota(jnp.int32, (tm, tm), 1)).astype(BF16)
    rank = _dot(mask.astype(BF16), before)
    rank_ref[...] = jnp.where(mask > 0, rank, -1.0)
    cnt_ref[...] = jnp.broadcast_to(jnp.sum(mask, axis=1, keepdims=True), cnt_ref.shape)
    a = _dot(xb, w1_ref[...])
    hsh = (a * jax.nn.sigmoid(a)) * _dot(xb, w3_ref[...])
    x1s_ref[...] = x1 + _dot(hsh.astype(BF16), w2_ref[...])


def _router(x1a, x1b, gffn, rwt, rbias, w1, w3, w2, *, TM):
    N = x1a.shape[0] + x1b.shape[0]
    split = x1a.shape[0] // TM
    tok = pl.BlockSpec((TM, D_MODEL), lambda i: (i, 0))
    per_e = pl.BlockSpec((N_EXPERTS, TM), lambda i: (0, i))
    consts = (gffn, rwt, rbias, w1, w3, w2)
    return pl.pallas_call(
        functools.partial(_router_kernel, split=split),
        grid=(N // TM,),
        in_specs=list(_group_specs((TM, D_MODEL), split)) + [pl.BlockSpec(c.shape, lambda i: (0, 0)) for c in consts],
        out_specs=(tok, per_e, per_e, pl.BlockSpec((N_EXPERTS, 128), lambda i: (i, 0)), tok),
        out_shape=(jax.ShapeDtypeStruct((N, D_MODEL), BF16),
                   jax.ShapeDtypeStruct((N_EXPERTS, N), F32),
                   jax.ShapeDtypeStruct((N_EXPERTS, N), F32),
                   jax.ShapeDtypeStruct((N // TM * N_EXPERTS, 128), F32),
                   jax.ShapeDtypeStruct((N, D_MODEL), F32)),
        compiler_params=pltpu.CompilerParams(
            dimension_semantics=("arbitrary",), vmem_limit_bytes=VMEM_LIMIT),
        name="router_shared",
    )(x1a, x1b, *consts)


def _segment_plan(cnt, n_tiles_max):
    seg = (cnt + SEG_ALIGN - 1) // SEG_ALIGN * SEG_ALIGN
    used = jnp.sum(seg, axis=0)
    size = (used + CAP + EXPERT_TILE - 1) // EXPERT_TILE * EXPERT_TILE
    row_end = jnp.cumsum(size)
    row_start = row_end - size
    base = row_start[None, :] + jnp.cumsum(seg, axis=0) - seg
    nwin = jnp.maximum((seg + CAP - 1) // CAP, 1)
    tile_end = row_end // EXPERT_TILE
    n_tiles = tile_end[-1]
    t_ids = jnp.arange(n_tiles_max, dtype=jnp.int32)
    tile_e = jnp.minimum(jnp.sum(t_ids[:, None] >= tile_end[None, :], axis=1), N_EXPERTS - 1)
    i32 = lambda a: a.astype(jnp.int32)
    return (i32(base).reshape(-1), i32(nwin).reshape(-1), i32(row_start), i32(row_end),
            i32(tile_e), i32(n_tiles).reshape(1))


def _one_hot_rows(rank_ref, chunk, first_row, values_ref=None):
    tm = rank_ref.shape[1]
    j = (lax.broadcasted_iota(jnp.int32, (CAP, tm), 0) + first_row).astype(F32)
    rows = []
    for k in range(CHUNK_E):
        e = chunk * CHUNK_E + k
        hit = j == rank_ref[e:e + 1, :]
        val = 1.0 if values_ref is None else values_ref[e:e + 1, :]
        rows.append(jnp.where(hit, val, 0.0).astype(BF16))
    return jnp.concatenate(rows, axis=0)


def _window(hbm, base_ref, idx, w):
    start = pl.multiple_of(base_ref[idx] + w * CAP, SEG_ALIGN)
    return hbm.at[pl.ds(start, CAP)]


def _chunk_windows(nwin_ref, tile_idx, c):
    extra = nwin_ref[tile_idx * N_EXPERTS + c * CHUNK_E]
    for k in range(1, CHUNK_E):
        extra = jnp.maximum(extra, nwin_ref[tile_idx * N_EXPERTS + c * CHUNK_E + k])
    return extra


def _tile_windows(nwin_ref, tile_idx):
    extra = _chunk_windows(nwin_ref, tile_idx, 0)
    for c in range(1, N_CHUNKS):
        extra = jnp.maximum(extra, _chunk_windows(nwin_ref, tile_idx, c))
    return extra


def _dispatch_kernel(base_ref, nwin_ref, rstart_ref, rend_ref, xn_ref, rank_ref, xs_hbm,
                     stage0, stage1, ostage, zbuf, sem, osem, zsem):
    stage = (stage0, stage1)
    i = pl.program_id(0)
    last = pl.num_programs(0) - 1

    def window_copy(p, e, tile_idx):
        return pltpu.make_async_copy(stage[p].at[pl.ds(e * CAP, CAP)],
                                     _window(xs_hbm, base_ref, tile_idx * N_EXPERTS + e, 0), sem.at[p])

    def wait_tile(p, tile_idx):
        for e in range(N_EXPERTS):
            window_copy(p, e, tile_idx).wait()

    @pl.when(i == 0)
    def _zero_tails():
        zbuf[...] = jnp.zeros(zbuf.shape, BF16)
        copies = []
        for e in range(N_EXPERTS):
            end = rend_ref[e]
            a0 = pl.multiple_of(end - EXPERT_TILE, SEG_ALIGN)
            b0 = pl.multiple_of(jnp.maximum(end - EXPERT_TILE - CAP, rstart_ref[e]), SEG_ALIGN)
            copies.append(pltpu.make_async_copy(zbuf, xs_hbm.at[pl.ds(a0, EXPERT_TILE)], zsem))
            copies.append(pltpu.make_async_copy(zbuf.at[pl.ds(0, CAP)], xs_hbm.at[pl.ds(b0, CAP)], zsem))
        for cp in copies:
            cp.start()
        for cp in copies:
            cp.wait()

    for p in (0, 1):
        @pl.when(i % 2 == p)
        def _step(p=p):
            xb = xn_ref[...]
            rows = CHUNK_E * CAP
            for c in range(N_CHUNKS):
                stage[p][pl.ds(c * rows, rows), :] = _dot(_one_hot_rows(rank_ref, c, 0), xb).astype(BF16)

            @pl.when(i >= 1)
            def _():
                wait_tile(1 - p, i - 1)

            for e in range(N_EXPERTS):
                window_copy(p, e, i).start()

    extra = _tile_windows(nwin_ref, i)

    @pl.when(extra > 1)
    def _long_segments():
        def body(w, carry):
            for c in range(N_CHUNKS):
                @pl.when(w < _chunk_windows(nwin_ref, i, c))
                def _(c=c):
                    ostage[...] = _dot(_one_hot_rows(rank_ref, c, w * CAP), xn_ref[...]).astype(BF16)
                    for k in range(CHUNK_E):
                        idx = i * N_EXPERTS + c * CHUNK_E + k

                        @pl.when(w < nwin_ref[idx])
                        def _(k=k, idx=idx):
                            cp = pltpu.make_async_copy(ostage.at[pl.ds(k * CAP, CAP)],
                                                       _window(xs_hbm, base_ref, idx, w), osem)
                            cp.start()
                            cp.wait()
            return carry

        lax.fori_loop(1, extra, body, 0)

    for p in (0, 1):
        @pl.when((i == last) & (i % 2 == p))
        def _drain(p=p):
            wait_tile(p, i)


def _dispatch(xn, rank_t, base, nwin, row_start, row_end, *, n_rows, TM):
    N = xn.shape[0]
    stage = pltpu.VMEM((N_EXPERTS * CAP, D_MODEL), BF16)
    return pl.pallas_call(
        _dispatch_kernel,
        grid_spec=pltpu.PrefetchScalarGridSpec(
            num_scalar_prefetch=4,
            grid=(N // TM,),
            in_specs=[pl.BlockSpec((TM, D_MODEL), lambda i, *_: (i, 0)),
                      pl.BlockSpec((N_EXPERTS, TM), lambda i, *_: (0, i))],
            out_specs=pl.BlockSpec(memory_space=pl.ANY),
            scratch_shapes=[
                stage, stage,
                pltpu.VMEM((CHUNK_E * CAP, D_MODEL), BF16),
                pltpu.VMEM((EXPERT_TILE, D_MODEL), BF16),
                pltpu.SemaphoreType.DMA((2,)),
                pltpu.SemaphoreType.DMA(()),
                pltpu.SemaphoreType.DMA(()),
            ],
        ),
        out_shape=jax.ShapeDtypeStruct((n_rows, D_MODEL), BF16),
        compiler_params=pltpu.CompilerParams(
            dimension_semantics=("arbitrary",), vmem_limit_bytes=VMEM_LIMIT),
        name="moe_dispatch",
    )(base, nwin, row_start, row_end, xn, rank_t)


def _expert_kernel(tile_e_ref, n_tiles_ref, xs_ref, w1_ref, w3_ref, w2_ref, ys_ref, w1b, w3b, w2b):
    t = pl.program_id(0)
    valid = t < n_tiles_ref[0]
    prev_e = tile_e_ref[jnp.maximum(t - 1, 0)]

    @pl.when(valid & ((t == 0) | (tile_e_ref[t] != prev_e)))
    def _cast_weights():
        w1b[...] = w1_ref[0].astype(BF16)
        w3b[...] = w3_ref[0].astype(BF16)
        w2b[...] = w2_ref[0].astype(BF16)

    @pl.when(valid)
    def _compute():
        xb = xs_ref[...]
        a = _dot(xb, w1b[...])
        hb = (a * jax.nn.sigmoid(a)) * _dot(xb, w3b[...])
        ys_ref[...] = _dot(hb.astype(BF16), w2b[...]).astype(BF16)

    @pl.when(jnp.logical_not(valid))
    def _spare():
        ys_ref[...] = jnp.zeros(ys_ref.shape, BF16)


def _experts(xs, tile_e, n_tiles, w1, w3, w2):
    t_max = xs.shape[0] // EXPERT_TILE
    clamp = lambda t, nt: jnp.minimum(t, nt[0] - 1)
    wspec_in = pl.BlockSpec((1, D_MODEL, EXPERT_FF), lambda t, te, nt: (te[clamp(t, nt)], 0, 0))
    wspec_out = pl.BlockSpec((1, EXPERT_FF, D_MODEL), lambda t, te, nt: (te[clamp(t, nt)], 0, 0))
    return pl.pallas_call(
        _expert_kernel,
        grid_spec=pltpu.PrefetchScalarGridSpec(
            num_scalar_prefetch=2,
            grid=(t_max,),
            in_specs=[pl.BlockSpec((EXPERT_TILE, D_MODEL), lambda t, te, nt: (clamp(t, nt), 0)),
                      wspec_in, wspec_in, wspec_out],
            out_specs=pl.BlockSpec((EXPERT_TILE, D_MODEL),
                                   lambda t, te, nt: (jnp.where(t < nt[0], t, t_max), 0)),
            scratch_shapes=[
                pltpu.VMEM((D_MODEL, EXPERT_FF), BF16),
                pltpu.VMEM((D_MODEL, EXPERT_FF), BF16),
                pltpu.VMEM((EXPERT_FF, D_MODEL), BF16),
            ],
        ),
        out_shape=jax.ShapeDtypeStruct(((t_max + 1) * EXPERT_TILE, D_MODEL), BF16),
        compiler_params=pltpu.CompilerParams(
            dimension_semantics=("arbitrary",), vmem_limit_bytes=VMEM_LIMIT),
        name="moe_experts",
    )(tile_e, n_tiles, xs, w1, w3, w2)


def _final_kernel(base_ref, nwin_ref, x1s_ref, rank_ref, gates_ref, ys_hbm, pa_ref, pb_ref, gple_ref,
                  wgate_ref, wproj_ref, gfin_ref, ya_ref, yb_ref, win0, win1, owin, acc_ref, sem, osem,
                  *, split):
    i = pl.program_id(0)
    last = pl.num_programs(0) - 1
    win = (win0, win1)

    def window_copy(p, e, tile_idx):
        return pltpu.make_async_copy(_window(ys_hbm, base_ref, tile_idx * N_EXPERTS + e, 0),
                                     win[p].at[pl.ds(e * CAP, CAP)], sem.at[p])

    @pl.when(i == 0)
    def _first():
        for e in range(N_EXPERTS):
            window_copy(0, e, i).start()

    contract0 = (((0,), (0,)), ((), ()))

    for p in (0, 1):
        @pl.when(i % 2 == p)
        def _step(p=p):
            for e in range(N_EXPERTS):
                window_copy(p, e, i).wait()

            @pl.when(i < last)
            def _():
                for e in range(N_EXPERTS):
                    window_copy(1 - p, e, i + 1).start()

            one_hot = jnp.concatenate([_one_hot_rows(rank_ref, c, 0, gates_ref) for c in range(N_CHUNKS)], axis=0)
            acc_ref[...] = lax.dot_general(one_hot, win[p][...], contract0, preferred_element_type=F32)

    extra = _tile_windows(nwin_ref, i)

    @pl.when(extra > 1)
    def _long_segments():
        def body(w, carry):
            for c in range(N_CHUNKS):
                @pl.when(w < _chunk_windows(nwin_ref, i, c))
                def _(c=c):
                    for k in range(CHUNK_E):
                        idx = i * N_EXPERTS + c * CHUNK_E + k
                        ww = jnp.where(w < nwin_ref[idx], w, 0)
                        cp = pltpu.make_async_copy(_window(ys_hbm, base_ref, idx, ww),
                                                   owin.at[pl.ds(k * CAP, CAP)], osem)
                        cp.start()
                        cp.wait()
                    acc_ref[...] += lax.dot_general(_one_hot_rows(rank_ref, c, w * CAP, gates_ref), owin[...],
                                                    contract0, preferred_element_type=F32)
            return carry

        lax.fori_loop(1, extra, body, 0)

    x2 = x1s_ref[...] + acc_ref[...]
    r_ = _rms(x2, gple_ref[...])
    gate = jax.nn.sigmoid(_dot(r_.astype(BF16), wgate_ref[...]))
    pt = jnp.where(i < split, pa_ref[...], pb_ref[...])
    x3 = x2 + _dot(pt.astype(BF16), wproj_ref[...]) * gate
    y = _rms(x3, gfin_ref[...])

    @pl.when(i < split)
    def _():
        ya_ref[...] = y

    @pl.when(i >= split)
    def _():
        yb_ref[...] = y


def _final(x1s, rank_t, gates_t, ys, pa, pb, gple, wgate, wproj, gfin, base, nwin, *, TM):
    N = x1s.shape[0]
    split = pa.shape[0] // TM
    tok = pl.BlockSpec((TM, D_MODEL), lambda i, *_: (i, 0))
    per_e = pl.BlockSpec((N_EXPERTS, TM), lambda i, *_: (0, i))
    consts = (gple, wgate, wproj, gfin)
    win = pltpu.VMEM((N_EXPERTS * CAP, D_MODEL), BF16)
    return pl.pallas_call(
        functools.partial(_final_kernel, split=split),
        grid_spec=pltpu.PrefetchScalarGridSpec(
            num_scalar_prefetch=2,
            grid=(N // TM,),
            in_specs=[tok, per_e, per_e, pl.BlockSpec(memory_space=pl.ANY)]
            + list(_group_specs((TM, PLE_DIM), split))
            + [pl.BlockSpec(c.shape, lambda i, *_: (0, 0)) for c in consts],
            out_specs=_group_specs((TM, D_MODEL), split),
            scratch_shapes=[win, win, pltpu.VMEM((CHUNK_E * CAP, D_MODEL), BF16), pltpu.VMEM((TM, D_MODEL), F32),
                            pltpu.SemaphoreType.DMA((2,)), pltpu.SemaphoreType.DMA(())],
        ),
        out_shape=(jax.ShapeDtypeStruct((pa.shape[0], D_MODEL), F32),
                   jax.ShapeDtypeStruct((pb.shape[0], D_MODEL), F32)),
        compiler_params=pltpu.CompilerParams(
            dimension_semantics=("arbitrary",), vmem_limit_bytes=VMEM_LIMIT),
        name="combine_ple_final",
    )(base, nwin, x1s, rank_t, gates_t, ys, pa, pb, *consts)


def kernel(x_prompt, x_sample, p_prompt, p_sample, state_C, state_n, state_m, state_pool, norm_mix_g, w_in, b_igate, b_fgate, head_norm_g, w_pool, pool_scale, w_out, norm_ffn_g, router_w, router_bias, ex_w1, ex_w3, ex_w2, sh_w1, sh_w3, sh_w2, norm_ple_g, w_ple_gate, w_ple_proj, final_norm_g):
    depth = norm_mix_g.shape[0]
    assert depth == 1
    l = 0
    B, T, _ = x_prompt.shape
    Bs, Ts, _ = x_sample.shape
    g0 = 4 * MLSTM_WIDTH
    w = w_in[l]
    wcat = jnp.concatenate(
        [w[:, :g0], w[:, g0 + 2 * HEADS:], w[:, g0:g0 + 2 * HEADS],
         jnp.zeros((D_MODEL, 128 - 2 * HEADS), F32)], axis=1).astype(BF16)
    gbias = jnp.concatenate([b_igate[l], b_fgate[l], jnp.zeros((128 - 2 * HEADS,), F32)])[None, :]
    mixer_w = (wcat, norm_mix_g[l][None, :], gbias, head_norm_g[l][None, :], w_pool[l].astype(BF16),
               pool_scale[l][None, :], w_out[l].astype(BF16))

    x1p, Cp, Np, Mp, Bp = _mixer(x_prompt, *mixer_w, None, BB=1, TT=512, L=128, start_pos=0)
    state = (state_C[l], state_n[l], state_m[l][..., None], state_pool[l])
    x1s_, Cs, Ns, Ms, Bs_ = _mixer(x_sample, *mixer_w, state, BB=32, TT=Ts, L=Ts, start_pos=PAST_LEN)

    N = B * T + Bs * Ts
    assert (B * T) % ROUTE_TILE == 0 and (Bs * Ts) % ROUTE_TILE == 0
    n_route_tiles = N // ROUTE_TILE

    xn, rank_t, gates_t, cnt, x1sh = _router(
        x1p.reshape(B * T, D_MODEL), x1s_.reshape(Bs * Ts, D_MODEL),
        norm_ffn_g[l][None, :], router_w[l].T, router_bias[l][:, None],
        sh_w1[l].astype(BF16), sh_w3[l].astype(BF16), sh_w2[l].astype(BF16), TM=ROUTE_TILE)

    max_rows = TOP_K * N + (SEG_ALIGN - 1) * n_route_tiles * N_EXPERTS + N_EXPERTS * (CAP + EXPERT_TILE)
    t_max = -(-max_rows // EXPERT_TILE)
    cnt = cnt[:, 0].reshape(n_route_tiles, N_EXPERTS).astype(jnp.int32)
    base, nwin, row_start, row_end, tile_e, n_tiles = _segment_plan(cnt, t_max)

    xs = _dispatch(xn, rank_t, base, nwin, row_start, row_end, n_rows=t_max * EXPERT_TILE, TM=ROUTE_TILE)
    ys = _experts(xs, tile_e, n_tiles, ex_w1[l], ex_w3[l], ex_w2[l])
    y_prompt, y_sample = _final(
        x1sh, rank_t, gates_t, ys, p_prompt[l].reshape(B * T, PLE_DIM), p_sample[l].reshape(Bs * Ts, PLE_DIM),
        norm_ple_g[l][None, :], w_ple_gate[l].astype(BF16), w_ple_proj[l].astype(BF16),
        final_norm_g[None, :], base, nwin, TM=ROUTE_TILE)
    return (y_prompt.reshape(B, T, D_MODEL), y_sample.reshape(Bs, Ts, D_MODEL),
            Cp[None], Np[None], Mp[..., 0][None], Bp[None],
            Cs[None], Ns[None], Ms[..., 0][None], Bs_[None])
```

```python
import functools

import jax
import jax.numpy as jnp
from jax import lax
from jax.experimental import pallas as pl
from jax.experimental.pallas import tpu as pltpu

D_MODEL = 1024
HEADS = 4
DH = 128
MLSTM_WIDTH = HEADS * DH
POOL_WIDTH = 512
POOL_WINDOWS = (2, 4, 8, 16)
POOL_GDIM = 128
POOL_BUF = 15
POOL_PAD = 16
N_EXPERTS = 64
TOP_K = 8
N_GROUPS = 8
GROUP_SIZE = N_EXPERTS // N_GROUPS
TOPK_GROUPS = 4
EXPERT_FF = 256
ROUTED_SCALE = 2.5
NORM_EPS = 1e-6
PLE_DIM = 256
PAST_LEN = 16384

COL_Q, COL_K, COL_V, COL_O, COL_U, COL_G = 0, 512, 1024, 1536, 2048, 2560
IN_COLS_PAD = 2688

ROUTE_TILE = 256
EXPERT_TILE = 1024
SEG_ALIGN = 16
CAP = 48
CHUNK_E = 8
N_CHUNKS = N_EXPERTS // CHUNK_E

VMEM_LIMIT = 56 * 1024 * 1024
F32 = jnp.float32
BF16 = jnp.bfloat16
NEG_INF = float("-inf")


def _rms(x, g):
    return x * lax.rsqrt(jnp.mean(x * x, axis=-1, keepdims=True) + NORM_EPS) * g


def _log_sigmoid(x):
    return jnp.minimum(x, 0.0) - jnp.log1p(jnp.exp(-jnp.abs(x)))


def _dot(a, b):
    return jnp.dot(a, b, preferred_element_type=F32)


def _mlstm_chunk(q, k, v, b_col, i_col, C, n, m, causal, eye):
    L = q.shape[0]
    r_col = i_col - b_col
    r_row = jnp.sum(jnp.where(eye, r_col, 0.0), axis=0, keepdims=True)
    d = jnp.where(causal, b_col + r_row, NEG_INF)
    inter = b_col + m
    m_t = jnp.maximum(inter, jnp.max(d, axis=-1, keepdims=True))
    w_inter = jnp.exp(inter - m_t)
    qb, kb, vb = q.astype(BF16), k.astype(BF16), v.astype(BF16)
    qk = lax.dot_general(qb, kb, (((1,), (1,)), ((), ())), preferred_element_type=F32)
    s = qk * jnp.exp(d - m_t)
    num = w_inter * _dot(qb, C.astype(BF16)) + _dot(s.astype(BF16), vb)
    nq = w_inter * jnp.sum(q * n, axis=-1, keepdims=True) + jnp.sum(s, axis=-1, keepdims=True)
    h = num / jnp.maximum(jnp.abs(nq), jnp.exp(-m_t))
    b_last = b_col[L - 1:L, :]
    m_new = jnp.maximum(b_last + m, jnp.max(b_last + r_row, axis=-1, keepdims=True))
    fw = jnp.exp(b_last + m - m_new)
    iw_col = jnp.exp(b_last + r_col - m_new)
    kw = iw_col * k
    C_new = fw * C + lax.dot_general(kw.astype(BF16), vb, (((0,), (0,)), ((), ())),
                                     preferred_element_type=F32)
    n_new = fw * n + jnp.sum(kw, axis=0, keepdims=True)
    return h, C_new, n_new, m_new


def _mixer_kernel(*refs, BB, TT, L, start_pos, zero_state):
    if zero_state:
        (x_ref, wcat_ref, gmix_ref, gbias_ref, hng_ref, wpool_ref, pscale_ref, wout_ref,
         x1_ref, c_ref, n_ref, m_ref, buf_ref,
         z_ref, gi_ref, lf_ref, mix_ref, ext_ref) = refs
    else:
        (x_ref, wcat_ref, gmix_ref, gbias_ref, hng_ref, wpool_ref, pscale_ref, wout_ref,
         c0_ref, n0_ref, m0_ref, buf0_ref,
         x1_ref, c_ref, n_ref, m_ref, buf_ref,
         z_ref, gi_ref, lf_ref, mix_ref, ext_ref) = refs
    t = pl.program_id(1)
    R = BB * TT
    n_chunks = TT // L

    @pl.when(t == 0)
    def _init():
        ext_ref[:, 0:POOL_PAD, :] = jnp.zeros((BB, POOL_PAD, POOL_WIDTH), F32)
        if zero_state:
            c_ref[...] = jnp.zeros(c_ref.shape, F32)
            n_ref[...] = jnp.zeros(n_ref.shape, F32)
            m_ref[...] = jnp.zeros(m_ref.shape, F32)
        else:
            c_ref[...] = c0_ref[...]
            n_ref[...] = n0_ref[...]
            m_ref[...] = m0_ref[...]
            ext_ref[:, 1:POOL_PAD, :] = buf0_ref[...]

    x = x_ref[...].reshape(R, D_MODEL)
    hn = _rms(x, gmix_ref[...])
    z_ref[...] = _dot(hn.astype(BF16), wcat_ref[...])
    g = z_ref[:, COL_G:COL_G + 128] + gbias_ref[...]
    gi_ref[...] = g
    lf_ref[...] = _log_sigmoid(g)

    row = lax.broadcasted_iota(jnp.int32, (L, L), 0)
    col = lax.broadcasted_iota(jnp.int32, (L, L), 1)
    causal = row >= col
    eye = row == col
    tril = causal.astype(F32)
    hng = hng_ref[...]

    def chunk_body(s, carry):
        row0 = pl.multiple_of(s * L, L)
        bb = s // n_chunks
        rows = pl.ds(row0, L)
        bcum = jnp.dot(tril, lf_ref[rows, :], preferred_element_type=F32,
                       precision=lax.Precision.HIGHEST)
        gi = gi_ref[rows, :]
        for hd in range(HEADS):
            lanes = slice(hd * DH, (hd + 1) * DH)
            q = z_ref[rows, COL_Q + hd * DH:COL_Q + (hd + 1) * DH] * (DH ** -0.5)
            k = z_ref[rows, COL_K + hd * DH:COL_K + (hd + 1) * DH]
            v = z_ref[rows, COL_V + hd * DH:COL_V + (hd + 1) * DH]
            o = z_ref[rows, COL_O + hd * DH:COL_O + (hd + 1) * DH]
            C = c_ref[bb, hd]
            n = n_ref[bb, hd:hd + 1, :]
            m = m_ref[bb, hd:hd + 1, :]
            h, C_new, n_new, m_new = _mlstm_chunk(
                q, k, v, bcum[:, HEADS + hd:HEADS + hd + 1], gi[:, hd:hd + 1], C, n, m, causal, eye)
            c_ref[bb, hd] = C_new
            n_ref[bb, hd:hd + 1, :] = n_new
            m_ref[bb, hd:hd + 1, :] = m_new
            h = h * lax.rsqrt(jnp.mean(h * h, axis=-1, keepdims=True) + NORM_EPS)
            mix_ref[rows, lanes] = h * hng[:, lanes] * jax.nn.sigmoid(o)
        return carry

    lax.fori_loop(0, BB * n_chunks, chunk_body, 0)

    ext_ref[:, POOL_PAD:POOL_PAD + TT, :] = z_ref[:, COL_U:COL_U + POOL_WIDTH].reshape(BB, TT, POOL_WIDTH)
    pos = start_pos + t * TT + lax.broadcasted_iota(jnp.int32, (1, TT, 1), 1)
    pscale = pscale_ref[...]
    for gidx, w in enumerate(POOL_WINDOWS):
        lanes = slice(gidx * POOL_GDIM, (gidx + 1) * POOL_GDIM)
        u_g = ext_ref[:, POOL_PAD:POOL_PAD + TT, lanes]
        acc = u_g
        for j in range(1, w):
            acc = acc + ext_ref[:, POOL_PAD - j:POOL_PAD - j + TT, lanes]
        cnt = jnp.minimum(pos + 1, w).astype(F32)
        pooled = (acc / cnt - u_g).reshape(R, POOL_GDIM)
        mixed = _dot(pooled.astype(BF16), wpool_ref[gidx]) * pscale[:, lanes]
        mix_ref[:, MLSTM_WIDTH + gidx * POOL_GDIM:MLSTM_WIDTH + (gidx + 1) * POOL_GDIM] = mixed
    new_buf = ext_ref[:, TT + 1:TT + POOL_PAD, :]
    buf_ref[...] = new_buf
    ext_ref[:, 1:POOL_PAD, :] = new_buf

    out = x_ref[...].reshape(R, D_MODEL) + _dot(mix_ref[...].astype(BF16), wout_ref[...])
    x1_ref[...] = out.reshape(BB, TT, D_MODEL)


def _const_spec(shape):
    nd = len(shape)
    return pl.BlockSpec(shape, lambda b, t, _nd=nd: (0,) * _nd)


def _mixer(x, wcat, gmix, gbias, hng, wpool, pscale, wout, state, *, BB, TT, L, start_pos):
    B, T, _ = x.shape
    zero_state = state is None
    R = BB * TT
    grid = (B // BB, T // TT)
    x_spec = pl.BlockSpec((BB, TT, D_MODEL), lambda b, t: (b, t, 0))
    c_spec = pl.BlockSpec((BB, HEADS, DH, DH), lambda b, t: (b, 0, 0, 0))
    n_spec = pl.BlockSpec((BB, HEADS, DH), lambda b, t: (b, 0, 0))
    m_spec = pl.BlockSpec((BB, HEADS, 1), lambda b, t: (b, 0, 0))
    buf_spec = pl.BlockSpec((BB, POOL_BUF, POOL_WIDTH), lambda b, t: (b, 0, 0))
    weights = (wcat, gmix, gbias, hng, wpool, pscale, wout)
    in_specs = [x_spec] + [_const_spec(w.shape) for w in weights]
    args = [x, *weights]
    if not zero_state:
        in_specs += [c_spec, n_spec, m_spec, buf_spec]
        args += list(state)
    out_shape = (
        jax.ShapeDtypeStruct((B, T, D_MODEL), F32),
        jax.ShapeDtypeStruct((B, HEADS, DH, DH), F32),
        jax.ShapeDtypeStruct((B, HEADS, DH), F32),
        jax.ShapeDtypeStruct((B, HEADS, 1), F32),
        jax.ShapeDtypeStruct((B, POOL_BUF, POOL_WIDTH), F32),
    )
    kern = functools.partial(_mixer_kernel, BB=BB, TT=TT, L=L, start_pos=start_pos, zero_state=zero_state)
    return pl.pallas_call(
        kern,
        grid=grid,
        in_specs=in_specs,
        out_specs=(x_spec, c_spec, n_spec, m_spec, buf_spec),
        out_shape=out_shape,
        scratch_shapes=[
            pltpu.VMEM((R, IN_COLS_PAD), F32),
            pltpu.VMEM((R, 128), F32),
            pltpu.VMEM((R, 128), F32),
            pltpu.VMEM((R, D_MODEL), F32),
            pltpu.VMEM((BB, POOL_PAD + TT, POOL_WIDTH), F32),
        ],
        compiler_params=pltpu.CompilerParams(
            dimension_semantics=("arbitrary", "arbitrary"), vmem_limit_bytes=VMEM_LIMIT),
        name="mixer_zero" if zero_state else "mixer_state",
    )(*args)


def _first_index_of_max(work, idx, n, axis):
    mx = jnp.max(work, axis=axis, keepdims=True)
    return jnp.min(jnp.where(work == mx, idx, float(n)), axis=axis, keepdims=True)


def _route(scores_t, bias_t):
    tm = scores_t.shape[1]
    biased = scores_t + bias_t
    b3 = biased.reshape(N_GROUPS, GROUP_SIZE, tm)
    sub = lax.broadcasted_iota(jnp.int32, b3.shape, 1).astype(F32)
    m1 = jnp.max(b3, axis=1, keepdims=True)
    first = jnp.min(jnp.where(b3 == m1, sub, float(GROUP_SIZE)), axis=1, keepdims=True)
    m2 = jnp.max(jnp.where(sub == first, NEG_INF, b3), axis=1, keepdims=True)
    gs = (m1 + m2).reshape(N_GROUPS, tm)
    gidx = lax.broadcasted_iota(jnp.int32, gs.shape, 0).astype(F32)
    gsel = jnp.zeros(gs.shape, F32)
    work = gs
    for _ in range(TOPK_GROUPS):
        pick = gidx == _first_index_of_max(work, gidx, N_GROUPS, 0)
        gsel = jnp.where(pick, 1.0, gsel)
        work = jnp.where(pick, NEG_INF, work)
    emask = jnp.broadcast_to(gsel.reshape(N_GROUPS, 1, tm), b3.shape).reshape(N_EXPERTS, tm)
    work = jnp.where(emask > 0, biased, NEG_INF)
    eidx = lax.broadcasted_iota(jnp.int32, work.shape, 0).astype(F32)
    mask = jnp.zeros(work.shape, F32)
    for _ in range(TOP_K):
        pick = eidx == _first_index_of_max(work, eidx, N_EXPERTS, 0)
        mask = jnp.where(pick, 1.0, mask)
        work = jnp.where(pick, NEG_INF, work)
    sel = mask * scores_t
    return mask, sel / jnp.sum(sel, axis=0, keepdims=True) * ROUTED_SCALE


def _group_specs(block, split):
    return (pl.BlockSpec(block, lambda i, *_: (jnp.minimum(i, split - 1), 0)),
            pl.BlockSpec(block, lambda i, *_: (jnp.maximum(i - split, 0), 0)))


def _router_kernel(x1a_ref, x1b_ref, gffn_ref, rwt_ref, rbias_ref, w1_ref, w3_ref, w2_ref,
                   xn_ref, rank_ref, gates_ref, cnt_ref, x1s_ref, *, split):
    x1 = jnp.where(pl.program_id(0) < split, x1a_ref[...], x1b_ref[...])
    tm = x1.shape[0]
    xn = _rms(x1, gffn_ref[...])
    xb = xn.astype(BF16)
    xn_ref[...] = xb
    logits_t = lax.dot_general(rwt_ref[...], xn, (((1,), (1,)), ((), ())),
                               preferred_element_type=F32, precision=lax.Precision.HIGHEST)
    mask, gates = _route(jax.nn.sigmoid(logits_t), rbias_ref[...])
    gates_ref[...] = gates
    before = (lax.broadcasted_iota(jnp.int32, (tm, tm), 0)
              < lax.broadcasted_iota(jnp.int32, (tm, tm), 1)).astype(BF16)
    rank = _dot(mask.astype(BF16), before)
    rank_ref[...] = jnp.where(mask > 0, rank, -1.0)
    cnt_ref[...] = jnp.broadcast_to(jnp.sum(mask, axis=1, keepdims=True), cnt_ref.shape)
    a = _dot(xb, w1_ref[...])
    hsh = (a * jax.nn.sigmoid(a)) * _dot(xb, w3_ref[...])
    x1s_ref[...] = x1 + _dot(hsh.astype(BF16), w2_ref[...])


def _router(x1a, x1b, gffn, rwt, rbias, w1, w3, w2, *, TM):
    N = x1a.shape[0] + x1b.shape[0]
    split = x1a.shape[0] // TM
    tok = pl.BlockSpec((TM, D_MODEL), lambda i: (i, 0))
    per_e = pl.BlockSpec((N_EXPERTS, TM), lambda i: (0, i))
    consts = (gffn, rwt, rbias, w1, w3, w2)
    return pl.pallas_call(
        functools.partial(_router_kernel, split=split),
        grid=(N // TM,),
        in_specs=list(_group_specs((TM, D_MODEL), split)) + [pl.BlockSpec(c.shape, lambda i: (0, 0)) for c in consts],
        out_specs=(tok, per_e, per_e, pl.BlockSpec((N_EXPERTS, 128), lambda i: (i, 0)), tok),
        out_shape=(jax.ShapeDtypeStruct((N, D_MODEL), BF16),
                   jax.ShapeDtypeStruct((N_EXPERTS, N), F32),
                   jax.ShapeDtypeStruct((N_EXPERTS, N), F32),
                   jax.ShapeDtypeStruct((N // TM * N_EXPERTS, 128), F32),
                   jax.ShapeDtypeStruct((N, D_MODEL), F32)),
        compiler_params=pltpu.CompilerParams(
            dimension_semantics=("arbitrary",), vmem_limit_bytes=VMEM_LIMIT),
        name="router_shared",
    )(x1a, x1b, *consts)


def _segment_plan(cnt, n_tiles_max):
    seg = (cnt + SEG_ALIGN - 1) // SEG_ALIGN * SEG_ALIGN
    used = jnp.sum(seg, axis=0)
    size = (used + CAP + EXPERT_TILE - 1) // EXPERT_TILE * EXPERT_TILE
    row_end = jnp.cumsum(size)
    row_start = row_end - size
    base = row_start[None, :] + jnp.cumsum(seg, axis=0) - seg
    nwin = jnp.maximum((seg + CAP - 1) // CAP, 1)
    tile_end = row_end // EXPERT_TILE
    n_tiles = tile_end[-1]
    t_ids = jnp.arange(n_tiles_max, dtype=jnp.int32)
    tile_e = jnp.minimum(jnp.sum(t_ids[:, None] >= tile_end[None, :], axis=1), N_EXPERTS - 1)
    i32 = lambda a: a.astype(jnp.int32)
    return (i32(base).reshape(-1), i32(nwin).reshape(-1), i32(row_start), i32(row_end),
            i32(tile_e), i32(n_tiles).reshape(1))


def _one_hot_rows(rank_ref, chunk, first_row, values_ref=None):
    tm = rank_ref.shape[1]
    j = (lax.broadcasted_iota(jnp.int32, (CAP, tm), 0) + first_row).astype(F32)
    rows = []
    for k in range(CHUNK_E):
        e = chunk * CHUNK_E + k
        hit = j == rank_ref[e:e + 1, :]
        val = 1.0 if values_ref is None else values_ref[e:e + 1, :]
        rows.append(jnp.where(hit, val, 0.0).astype(BF16))
    return jnp.concatenate(rows, axis=0)


def _window(hbm, base_ref, idx, w):
    start = pl.multiple_of(base_ref[idx] + w * CAP, SEG_ALIGN)
    return hbm.at[pl.ds(start, CAP)]


def _chunk_windows(nwin_ref, tile_idx, c):
    extra = nwin_ref[tile_idx * N_EXPERTS + c * CHUNK_E]
    for k in range(1, CHUNK_E):
        extra = jnp.maximum(extra, nwin_ref[tile_idx * N_EXPERTS + c * CHUNK_E + k])
    return extra


def _tile_windows(nwin_ref, tile_idx):
    extra = _chunk_windows(nwin_ref, tile_idx, 0)
    for c in range(1, N_CHUNKS):
        extra = jnp.maximum(extra, _chunk_windows(nwin_ref, tile_idx, c))
    return extra


def _dispatch_kernel(base_ref, nwin_ref, rstart_ref, rend_ref, xn_ref, rank_ref, xs_hbm,
                     stage0, stage1, ostage, zbuf, sem, osem, zsem):
    stage = (stage0, stage1)
    i = pl.program_id(0)
    last = pl.num_programs(0) - 1

    def window_copy(p, e, tile_idx):
        return pltpu.make_async_copy(stage[p].at[pl.ds(e * CAP, CAP)],
                                     _window(xs_hbm, base_ref, tile_idx * N_EXPERTS + e, 0), sem.at[p])

    def wait_tile(p, tile_idx):
        for e in range(N_EXPERTS):
            window_copy(p, e, tile_idx).wait()

    @pl.when(i == 0)
    def _zero_tails():
        zbuf[...] = jnp.zeros(zbuf.shape, BF16)
        copies = []
        for e in range(N_EXPERTS):
            end = rend_ref[e]
            a0 = pl.multiple_of(end - EXPERT_TILE, SEG_ALIGN)
            b0 = pl.multiple_of(jnp.maximum(end - EXPERT_TILE - CAP, rstart_ref[e]), SEG_ALIGN)
            copies.append(pltpu.make_async_copy(zbuf, xs_hbm.at[pl.ds(a0, EXPERT_TILE)], zsem))
            copies.append(pltpu.make_async_copy(zbuf.at[pl.ds(0, CAP)], xs_hbm.at[pl.ds(b0, CAP)], zsem))
        for cp in copies:
            cp.start()
        for cp in copies:
            cp.wait()

    for p in (0, 1):
        @pl.when(i % 2 == p)
        def _step(p=p):
            xb = xn_ref[...]
            rows = CHUNK_E * CAP
            for c in range(N_CHUNKS):
                stage[p][pl.ds(c * rows, rows), :] = _dot(_one_hot_rows(rank_ref, c, 0), xb).astype(BF16)

            @pl.when(i >= 1)
            def _():
                wait_tile(1 - p, i - 1)

            for e in range(N_EXPERTS):
                window_copy(p, e, i).start()

    extra = _tile_windows(nwin_ref, i)

    @pl.when(extra > 1)
    def _long_segments():
        def body(w, carry):
            for c in range(N_CHUNKS):
                @pl.when(w < _chunk_windows(nwin_ref, i, c))
                def _(c=c):
                    ostage[...] = _dot(_one_hot_rows(rank_ref, c, w * CAP), xn_ref[...]).astype(BF16)
                    for k in range(CHUNK_E):
                        idx = i * N_EXPERTS + c * CHUNK_E + k

                        @pl.when(w < nwin_ref[idx])
                        def _(k=k, idx=idx):
                            cp = pltpu.make_async_copy(ostage.at[pl.ds(k * CAP, CAP)],
                                                       _window(xs_hbm, base_ref, idx, w), osem)
                            cp.start()
                            cp.wait()
            return carry

        lax.fori_loop(1, extra, body, 0)

    for p in (0, 1):
        @pl.when((i == last) & (i % 2 == p))
        def _drain(p=p):
            wait_tile(p, i)


def _dispatch(xn, rank_t, base, nwin, row_start, row_end, *, n_rows, TM):
    N = xn.shape[0]
    stage = pltpu.VMEM((N_EXPERTS * CAP, D_MODEL), BF16)
    return pl.pallas_call(
        _dispatch_kernel,
        grid_spec=pltpu.PrefetchScalarGridSpec(
            num_scalar_prefetch=4,
            grid=(N // TM,),
            in_specs=[pl.BlockSpec((TM, D_MODEL), lambda i, *_: (i, 0)),
                      pl.BlockSpec((N_EXPERTS, TM), lambda i, *_: (0, i))],
            out_specs=pl.BlockSpec(memory_space=pl.ANY),
            scratch_shapes=[
                stage, stage,
                pltpu.VMEM((CHUNK_E * CAP, D_MODEL), BF16),
                pltpu.VMEM((EXPERT_TILE, D_MODEL), BF16),
                pltpu.SemaphoreType.DMA((2,)),
                pltpu.SemaphoreType.DMA(()),
                pltpu.SemaphoreType.DMA(()),
            ],
        ),
        out_shape=jax.ShapeDtypeStruct((n_rows, D_MODEL), BF16),
        compiler_params=pltpu.CompilerParams(
            dimension_semantics=("arbitrary",), vmem_limit_bytes=VMEM_LIMIT),
        name="moe_dispatch",
    )(base, nwin, row_start, row_end, xn, rank_t)


def _expert_kernel(tile_e_ref, n_tiles_ref, xs_ref, w1_ref, w3_ref, w2_ref, ys_ref, w1b, w3b, w2b):
    t = pl.program_id(0)
    valid = t < n_tiles_ref[0]
    prev_e = tile_e_ref[jnp.maximum(t - 1, 0)]

    @pl.when(valid & ((t == 0) | (tile_e_ref[t] != prev_e)))
    def _cast_weights():
        w1b[...] = w1_ref[0].astype(BF16)
        w3b[...] = w3_ref[0].astype(BF16)
        w2b[...] = w2_ref[0].astype(BF16)

    @pl.when(valid)
    def _compute():
        xb = xs_ref[...]
        a = _dot(xb, w1b[...])
        hb = (a * jax.nn.sigmoid(a)) * _dot(xb, w3b[...])
        ys_ref[...] = _dot(hb.astype(BF16), w2b[...]).astype(BF16)

    @pl.when(jnp.logical_not(valid))
    def _spare():
        ys_ref[...] = jnp.zeros(ys_ref.shape, BF16)


def _experts(xs, tile_e, n_tiles, w1, w3, w2):
    t_max = xs.shape[0] // EXPERT_TILE
    clamp = lambda t, nt: jnp.minimum(t, nt[0] - 1)
    wspec_in = pl.BlockSpec((1, D_MODEL, EXPERT_FF), lambda t, te, nt: (te[clamp(t, nt)], 0, 0))
    wspec_out = pl.BlockSpec((1, EXPERT_FF, D_MODEL), lambda t, te, nt: (te[clamp(t, nt)], 0, 0))
    return pl.pallas_call(
        _expert_kernel,
        grid_spec=pltpu.PrefetchScalarGridSpec(
            num_scalar_prefetch=2,
            grid=(t_max,),
            in_specs=[pl.BlockSpec((EXPERT_TILE, D_MODEL), lambda t, te, nt: (clamp(t, nt), 0)),
                      wspec_in, wspec_in, wspec_out],
            out_specs=pl.BlockSpec((EXPERT_TILE, D_MODEL),
                                   lambda t, te, nt: (jnp.where(t < nt[0], t, t_max), 0)),
            scratch_shapes=[
                pltpu.VMEM((D_MODEL, EXPERT_FF), BF16),
                pltpu.VMEM((D_MODEL, EXPERT_FF), BF16),
                pltpu.VMEM((EXPERT_FF, D_MODEL), BF16),
            ],
        ),
        out_shape=jax.ShapeDtypeStruct(((t_max + 1) * EXPERT_TILE, D_MODEL), BF16),
        compiler_params=pltpu.CompilerParams(
            dimension_semantics=("arbitrary",), vmem_limit_bytes=VMEM_LIMIT),
        name="moe_experts",
    )(tile_e, n_tiles, xs, w1, w3, w2)


def _final_kernel(base_ref, nwin_ref, x1s_ref, rank_ref, gates_ref, ys_hbm, pa_ref, pb_ref, gple_ref,
                  wgate_ref, wproj_ref, gfin_ref, ya_ref, yb_ref, win0, win1, owin, acc_ref, sem, osem,
                  *, split):
    i = pl.program_id(0)
    last = pl.num_programs(0) - 1
    win = (win0, win1)

    def window_copy(p, e, tile_idx):
        return pltpu.make_async_copy(_window(ys_hbm, base_ref, tile_idx * N_EXPERTS + e, 0),
                                     win[p].at[pl.ds(e * CAP, CAP)], sem.at[p])

    @pl.when(i == 0)
    def _first():
        for e in range(N_EXPERTS):
            window_copy(0, e, i).start()

    contract0 = (((0,), (0,)), ((), ()))

    for p in (0, 1):
        @pl.when(i % 2 == p)
        def _step(p=p):
            for e in range(N_EXPERTS):
                window_copy(p, e, i).wait()

            @pl.when(i < last)
            def _():
                for e in range(N_EXPERTS):
                    window_copy(1 - p, e, i + 1).start()

            one_hot = jnp.concatenate([_one_hot_rows(rank_ref, c, 0, gates_ref) for c in range(N_CHUNKS)], axis=0)
            acc_ref[...] = lax.dot_general(one_hot, win[p][...], contract0, preferred_element_type=F32)

    extra = _tile_windows(nwin_ref, i)

    @pl.when(extra > 1)
    def _long_segments():
        def body(w, carry):
            for c in range(N_CHUNKS):
                @pl.when(w < _chunk_windows(nwin_ref, i, c))
                def _(c=c):
                    for k in range(CHUNK_E):
                        idx = i * N_EXPERTS + c * CHUNK_E + k
                        ww = jnp.where(w < nwin_ref[idx], w, 0)
                        cp = pltpu.make_async_copy(_window(ys_hbm, base_ref, idx, ww),
                                                   owin.at[pl.ds(k * CAP, CAP)], osem)
                        cp.start()
                        cp.wait()
                    acc_ref[...] += lax.dot_general(_one_hot_rows(rank_ref, c, w * CAP, gates_ref), owin[...],
                                                    contract0, preferred_element_type=F32)
            return carry

        lax.fori_loop(1, extra, body, 0)

    x2 = x1s_ref[...] + acc_ref[...]
    r_ = _rms(x2, gple_ref[...])
    gate = jax.nn.sigmoid(_dot(r_.astype(BF16), wgate_ref[...]))
    pt = jnp.where(i < split, pa_ref[...], pb_ref[...])
    x3 = x2 + _dot(pt.astype(BF16), wproj_ref[...]) * gate
    y = _rms(x3, gfin_ref[...])

    @pl.when(i < split)
    def _():
        ya_ref[...] = y

    @pl.when(i >= split)
    def _():
        yb_ref[...] = y


def _final(x1s, rank_t, gates_t, ys, pa, pb, gple, wgate, wproj, gfin, base, nwin, *, TM):
    N = x1s.shape[0]
    split = pa.shape[0] // TM
    tok = pl.BlockSpec((TM, D_MODEL), lambda i, *_: (i, 0))
    per_e = pl.BlockSpec((N_EXPERTS, TM), lambda i, *_: (0, i))
    consts = (gple, wgate, wproj, gfin)
    win = pltpu.VMEM((N_EXPERTS * CAP, D_MODEL), BF16)
    return pl.pallas_call(
        functools.partial(_final_kernel, split=split),
        grid_spec=pltpu.PrefetchScalarGridSpec(
            num_scalar_prefetch=2,
            grid=(N // TM,),
            in_specs=[tok, per_e, per_e, pl.BlockSpec(memory_space=pl.ANY)]
            + list(_group_specs((TM, PLE_DIM), split))
            + [pl.BlockSpec(c.shape, lambda i, *_: (0, 0)) for c in consts],
            out_specs=_group_specs((TM, D_MODEL), split),
            scratch_shapes=[win, win, pltpu.VMEM((CHUNK_E * CAP, D_MODEL), BF16), pltpu.VMEM((TM, D_MODEL), F32),
                            pltpu.SemaphoreType.DMA((2,)), pltpu.SemaphoreType.DMA(())],
        ),
        out_shape=(jax.ShapeDtypeStruct((pa.shape[0], D_MODEL), F32),
                   jax.ShapeDtypeStruct((pb.shape[0], D_MODEL), F32)),
        compiler_params=pltpu.CompilerParams(
            dimension_semantics=("arbitrary",), vmem_limit_bytes=VMEM_LIMIT),
        name="combine_ple_final",
    )(base, nwin, x1s, rank_t, gates_t, ys, pa, pb, *consts)


def kernel(x_prompt, x_sample, p_prompt, p_sample, state_C, state_n, state_m, state_pool, norm_mix_g, w_in, b_igate, b_fgate, head_norm_g, w_pool, pool_scale, w_out, norm_ffn_g, router_w, router_bias, ex_w1, ex_w3, ex_w2, sh_w1, sh_w3, sh_w2, norm_ple_g, w_ple_gate, w_ple_proj, final_norm_g):
    depth = norm_mix_g.shape[0]
    assert depth == 1
    l = 0
    B, T, _ = x_prompt.shape
    Bs, Ts, _ = x_sample.shape
    g0 = 4 * MLSTM_WIDTH
    w = w_in[l]
    wcat = jnp.concatenate(
        [w[:, :g0], w[:, g0 + 2 * HEADS:], w[:, g0:g0 + 2 * HEADS],
         jnp.zeros((D_MODEL, 128 - 2 * HEADS), F32)], axis=1).astype(BF16)
    gbias = jnp.concatenate([b_igate[l], b_fgate[l], jnp.zeros((128 - 2 * HEADS,), F32)])[None, :]
    mixer_w = (wcat, norm_mix_g[l][None, :], gbias, head_norm_g[l][None, :], w_pool[l].astype(BF16),
               pool_scale[l][None, :], w_out[l].astype(BF16))

    x1p, Cp, Np, Mp, Bp = _mixer(x_prompt, *mixer_w, None, BB=1, TT=512, L=128, start_pos=0)
    state = (state_C[l], state_n[l], state_m[l][..., None], state_pool[l])
    x1s_, Cs, Ns, Ms, Bs_ = _mixer(x_sample, *mixer_w, state, BB=32, TT=Ts, L=Ts, start_pos=PAST_LEN)

    N = B * T + Bs * Ts
    assert (B * T) % ROUTE_TILE == 0 and (Bs * Ts) % ROUTE_TILE == 0
    n_route_tiles = N // ROUTE_TILE

    xn, rank_t, gates_t, cnt, x1sh = _router(
        x1p.reshape(B * T, D_MODEL), x1s_.reshape(Bs * Ts, D_MODEL),
        norm_ffn_g[l][None, :], router_w[l].T, router_bias[l][:, None],
        sh_w1[l].astype(BF16), sh_w3[l].astype(BF16), sh_w2[l].astype(BF16), TM=ROUTE_TILE)

    max_rows = TOP_K * N + (SEG_ALIGN - 1) * n_route_tiles * N_EXPERTS + N_EXPERTS * (CAP + EXPERT_TILE)
    t_max = -(-max_rows // EXPERT_TILE)
    cnt = cnt[:, 0].reshape(n_route_tiles, N_EXPERTS).astype(jnp.int32)
    base, nwin, row_start, row_end, tile_e, n_tiles = _segment_plan(cnt, t_max)

    xs = _dispatch(xn, rank_t, base, nwin, row_start, row_end, n_rows=t_max * EXPERT_TILE, TM=ROUTE_TILE)
    ys = _experts(xs, tile_e, n_tiles, ex_w1[l], ex_w3[l], ex_w2[l])
    y_prompt, y_sample = _final(
        x1sh, rank_t, gates_t, ys, p_prompt[l].reshape(B * T, PLE_DIM), p_sample[l].reshape(Bs * Ts, PLE_DIM),
        norm_ple_g[l][None, :], w_ple_gate[l].astype(BF16), w_ple_proj[l].astype(BF16),
        final_norm_g[None, :], base, nwin, TM=ROUTE_TILE)
    return (y_prompt.reshape(B, T, D_MODEL), y_sample.reshape(Bs, Ts, D_MODEL),
            Cp[None], Np[None], Mp[..., 0][None], Bp[None],
            Cs[None], Ns[None], Ms[..., 0][None], Bs_[None])
```

```python
import functools

import jax
import jax.numpy as jnp
from jax import lax
from jax.experimental import pallas as pl
from jax.experimental.pallas import tpu as pltpu

D_MODEL = 1024
HEADS = 4
DH = 128
MLSTM_WIDTH = HEADS * DH
POOL_WIDTH = 512
POOL_WINDOWS = (2, 4, 8, 16)
POOL_GDIM = 128
POOL_BUF = 15
POOL_PAD = 16
N_EXPERTS = 64
TOP_K = 8
N_GROUPS = 8
GROUP_SIZE = N_EXPERTS // N_GROUPS
TOPK_GROUPS = 4
EXPERT_FF = 256
ROUTED_SCALE = 2.5
NORM_EPS = 1e-6
PLE_DIM = 256
PAST_LEN = 16384

COL_Q, COL_K, COL_V, COL_O, COL_U, COL_I, COL_F = 0, 512, 1024, 1536, 2048, 2560, 3072
IN_COLS_PAD = 3584
GATE_COLS = HEADS * 128

ROUTE_TILE = 256
EXPERT_TILE = 1024
SEG_ALIGN = 16
CAP = 48
CHUNK_E = 8
N_CHUNKS = N_EXPERTS // CHUNK_E

VMEM_LIMIT = 56 * 1024 * 1024
F32 = jnp.float32
BF16 = jnp.bfloat16
NEG_INF = float("-inf")


def _rms(x, g):
    return x * lax.rsqrt(jnp.mean(x * x, axis=-1, keepdims=True) + NORM_EPS) * g


def _log_sigmoid(x):
    return jnp.minimum(x, 0.0) - jnp.log1p(jnp.exp(-jnp.abs(x)))


def _dot(a, b):
    return jnp.dot(a, b, preferred_element_type=F32)


def _mlstm_chunk(q, k, v, b_col, i_col, C, n, m, causal, eye):
    L = q.shape[0]
    r_col = i_col - b_col
    r_row = jnp.sum(jnp.where(eye, r_col, 0.0), axis=0, keepdims=True)
    d = jnp.where(causal, b_col + r_row, NEG_INF)
    inter = b_col + m
    m_t = jnp.maximum(inter, jnp.max(d, axis=-1, keepdims=True))
    w_inter = jnp.exp(inter - m_t)
    qb, kb, vb = q.astype(BF16), k.astype(BF16), v.astype(BF16)
    qk = lax.dot_general(qb, kb, (((1,), (1,)), ((), ())), preferred_element_type=F32)
    s = qk * jnp.exp(d - m_t)
    num = w_inter * _dot(qb, C.astype(BF16)) + _dot(s.astype(BF16), vb)
    nq = w_inter * jnp.sum(q * n, axis=-1, keepdims=True) + jnp.sum(s, axis=-1, keepdims=True)
    h = num / jnp.maximum(jnp.abs(nq), jnp.exp(-m_t))
    b_last = b_col[L - 1:L, :]
    m_new = jnp.maximum(b_last + m, jnp.max(b_last + r_row, axis=-1, keepdims=True))
    fw = jnp.exp(b_last + m - m_new)
    iw_col = jnp.exp(b_last + r_col - m_new)
    kw = iw_col * k
    C_new = fw * C + lax.dot_general(kw.astype(BF16), vb, (((0,), (0,)), ((), ())),
                                     preferred_element_type=F32)
    n_new = fw * n + jnp.sum(kw, axis=0, keepdims=True)
    return h, C_new, n_new, m_new


def _mlstm_chunk_square(q, k, v, b, i, C, n, m, causal, eye, ones):
    L = q.shape[0]
    r = i - b
    r_row = jnp.sum(jnp.where(eye, r, 0.0), axis=0, keepdims=True)
    g = jnp.maximum(jnp.max(jnp.where(causal, r_row, NEG_INF), axis=-1, keepdims=True), m)
    g = jnp.broadcast_to(g, (L, DH))
    w_inter = jnp.exp(m - g)
    qb, kb, vb = q.astype(BF16), k.astype(BF16), v.astype(BF16)
    qk = lax.dot_general(qb, kb, (((1,), (1,)), ((), ())), preferred_element_type=F32)
    s = (qk * jnp.where(causal, jnp.exp(r_row - g), 0.0)).astype(BF16)
    sv = _dot(s, jnp.concatenate([vb, ones], axis=1))
    num = w_inter * _dot(qb, C.astype(BF16)) + sv[:, :DH]
    nq = w_inter * _dot((q * n).astype(BF16), ones) + sv[:, DH:]
    h = num / jnp.maximum(jnp.abs(nq), jnp.exp(-(b + g)))
    msq = _dot((h * h).astype(BF16), ones) * (1.0 / DH)
    g_last = g[L - 1:L, :]
    m_new = b[L - 1:L, :] + g_last
    fw = jnp.exp(m - g_last)
    kw = jnp.exp(r - g_last) * k
    C_new = fw * C + lax.dot_general(kw.astype(BF16), vb, (((0,), (0,)), ((), ())),
                                     preferred_element_type=F32)
    n_new = fw * n + jnp.sum(kw, axis=0, keepdims=True)
    return h, msq, C_new, n_new, m_new[:, 0:1]


def _mixer_kernel(*refs, BB, TT, L, start_pos, zero_state):
    if zero_state:
        (x_ref, wcat_ref, gmix_ref, gbias_ref, hng_ref, wpool_ref, pscale_ref, wout_ref,
         x1_ref, c_ref, n_ref, m_ref, buf_ref,
         z_ref, gi_ref, lf_ref, mix_ref, ext_ref) = refs
    else:
        (x_ref, wcat_ref, gmix_ref, gbias_ref, hng_ref, wpool_ref, pscale_ref, wout_ref,
         c0_ref, n0_ref, m0_ref, buf0_ref,
         x1_ref, c_ref, n_ref, m_ref, buf_ref,
         z_ref, gi_ref, lf_ref, mix_ref, ext_ref) = refs
    t = pl.program_id(1)
    R = BB * TT
    n_chunks = TT // L

    @pl.when(t == 0)
    def _init():
        ext_ref[:, 0:POOL_PAD, :] = jnp.zeros((BB, POOL_PAD, POOL_WIDTH), F32)
        if zero_state:
            c_ref[...] = jnp.zeros(c_ref.shape, F32)
            n_ref[...] = jnp.zeros(n_ref.shape, F32)
            m_ref[...] = jnp.zeros(m_ref.shape, F32)
        else:
            c_ref[...] = c0_ref[...]
            n_ref[...] = n0_ref[...]
            m_ref[...] = m0_ref[...]
            ext_ref[:, 1:POOL_PAD, :] = buf0_ref[...]

    x = x_ref[...].reshape(R, D_MODEL)
    hn = _rms(x, gmix_ref[...])
    z_ref[...] = _dot(hn.astype(BF16), wcat_ref[...])
    gbias = gbias_ref[...]
    gi_ref[...] = z_ref[:, COL_I:COL_I + GATE_COLS] + gbias[:, :GATE_COLS]
    lf_ref[...] = _log_sigmoid(z_ref[:, COL_F:COL_F + GATE_COLS] + gbias[:, GATE_COLS:])

    row = lax.broadcasted_iota(jnp.int32, (L, L), 0)
    col = lax.broadcasted_iota(jnp.int32, (L, L), 1)
    causal = row >= col
    eye = row == col
    tril = causal.astype(F32)
    ones = jnp.ones((DH, DH), BF16)
    hng = hng_ref[...]

    def chunk_body(s, carry):
        row0 = pl.multiple_of(s * L, L)
        bb = 0 if BB == 1 else s // n_chunks
        rows = pl.ds(row0, L)
        bcum = jnp.dot(tril, lf_ref[rows, :], preferred_element_type=F32,
                       precision=lax.Precision.HIGHEST)
        gi = gi_ref[rows, :]
        for hd in range(HEADS):
            lanes = slice(hd * DH, (hd + 1) * DH)
            q = z_ref[rows, COL_Q + hd * DH:COL_Q + (hd + 1) * DH] * (DH ** -0.5)
            k = z_ref[rows, COL_K + hd * DH:COL_K + (hd + 1) * DH]
            v = z_ref[rows, COL_V + hd * DH:COL_V + (hd + 1) * DH]
            o = z_ref[rows, COL_O + hd * DH:COL_O + (hd + 1) * DH]
            C = c_ref[bb, hd]
            n = n_ref[bb, hd:hd + 1, :]
            m = m_ref[bb, hd:hd + 1, :]
            if L == DH:
                h, msq, C_new, n_new, m_new = _mlstm_chunk_square(
                    q, k, v, bcum[:, lanes], gi[:, lanes], C, n, m, causal, eye, ones)
            else:
                h, C_new, n_new, m_new = _mlstm_chunk(
                    q, k, v, bcum[:, hd * DH:hd * DH + 1], gi[:, hd * DH:hd * DH + 1], C, n, m, causal, eye)
                msq = jnp.mean(h * h, axis=-1, keepdims=True)
            c_ref[bb, hd] = C_new
            n_ref[bb, hd:hd + 1, :] = n_new
            m_ref[bb, hd:hd + 1, :] = m_new
            mix_ref[rows, lanes] = h * lax.rsqrt(msq + NORM_EPS) * hng[:, lanes] * jax.nn.sigmoid(o)
        return carry

    lax.fori_loop(0, BB * n_chunks, chunk_body, 0)

    ext_ref[:, POOL_PAD:POOL_PAD + TT, :] = z_ref[:, COL_U:COL_U + POOL_WIDTH].reshape(BB, TT, POOL_WIDTH)
    pos = start_pos + t * TT + lax.broadcasted_iota(jnp.int32, (1, TT, 1), 1)
    pscale = pscale_ref[...]
    for gidx, w in enumerate(POOL_WINDOWS):
        lanes = slice(gidx * POOL_GDIM, (gidx + 1) * POOL_GDIM)
        u_g = ext_ref[:, POOL_PAD:POOL_PAD + TT, lanes]
        acc = u_g
        for j in range(1, w):
            acc = acc + ext_ref[:, POOL_PAD - j:POOL_PAD - j + TT, lanes]
        cnt = jnp.minimum(pos + 1, w).astype(F32)
        pooled = (acc / cnt - u_g).reshape(R, POOL_GDIM)
        mixed = _dot(pooled.astype(BF16), wpool_ref[gidx]) * pscale[:, lanes]
        mix_ref[:, MLSTM_WIDTH + gidx * POOL_GDIM:MLSTM_WIDTH + (gidx + 1) * POOL_GDIM] = mixed
    new_buf = ext_ref[:, TT + 1:TT + POOL_PAD, :]
    buf_ref[...] = new_buf
    ext_ref[:, 1:POOL_PAD, :] = new_buf

    out = x_ref[...].reshape(R, D_MODEL) + _dot(mix_ref[...].astype(BF16), wout_ref[...])
    x1_ref[...] = out.reshape(BB, TT, D_MODEL)


def _const_spec(shape):
    nd = len(shape)
    return pl.BlockSpec(shape, lambda b, t, _nd=nd: (0,) * _nd, pipeline_mode=pl.Buffered(1))


def _mixer(x, wcat, gmix, gbias, hng, wpool, pscale, wout, state, *, BB, TT, L, start_pos):
    B, T, _ = x.shape
    zero_state = state is None
    R = BB * TT
    grid = (B // BB, T // TT)
    x_spec = pl.BlockSpec((BB, TT, D_MODEL), lambda b, t: (b, t, 0))
    c_spec = pl.BlockSpec((BB, HEADS, DH, DH), lambda b, t: (b, 0, 0, 0))
    n_spec = pl.BlockSpec((BB, HEADS, DH), lambda b, t: (b, 0, 0))
    m_spec = pl.BlockSpec((BB, HEADS, 1), lambda b, t: (b, 0, 0))
    buf_spec = pl.BlockSpec((BB, POOL_BUF, POOL_WIDTH), lambda b, t: (b, 0, 0))
    weights = (wcat, gmix, gbias, hng, wpool, pscale, wout)
    in_specs = [x_spec] + [_const_spec(w.shape) for w in weights]
    args = [x, *weights]
    if not zero_state:
        in_specs += [c_spec, n_spec, m_spec, buf_spec]
        args += list(state)
    out_shape = (
        jax.ShapeDtypeStruct((B, T, D_MODEL), F32),
        jax.ShapeDtypeStruct((B, HEADS, DH, DH), F32),
        jax.ShapeDtypeStruct((B, HEADS, DH), F32),
        jax.ShapeDtypeStruct((B, HEADS, 1), F32),
        jax.ShapeDtypeStruct((B, POOL_BUF, POOL_WIDTH), F32),
    )
    kern = functools.partial(_mixer_kernel, BB=BB, TT=TT, L=L, start_pos=start_pos, zero_state=zero_state)
    return pl.pallas_call(
        kern,
        grid=grid,
        in_specs=in_specs,
        out_specs=(x_spec, c_spec, n_spec, m_spec, buf_spec),
        out_shape=out_shape,
        scratch_shapes=[
            pltpu.VMEM((R, IN_COLS_PAD), F32),
            pltpu.VMEM((R, GATE_COLS), F32),
            pltpu.VMEM((R, GATE_COLS), F32),
            pltpu.VMEM((R, D_MODEL), F32),
            pltpu.VMEM((BB, POOL_PAD + TT, POOL_WIDTH), F32),
        ],
        compiler_params=pltpu.CompilerParams(
            dimension_semantics=("arbitrary", "arbitrary"), vmem_limit_bytes=VMEM_LIMIT),
        name="mixer_zero" if zero_state else "mixer_state",
    )(*args)


def _first_index_of_max(work, idx, n, axis):
    mx = jnp.max(work, axis=axis, keepdims=True)
    return jnp.min(jnp.where(work == mx, idx, float(n)), axis=axis, keepdims=True)


def _route(scores_t, bias_t):
    tm = scores_t.shape[1]
    biased = scores_t + bias_t
    b3 = biased.reshape(N_GROUPS, GROUP_SIZE, tm)
    sub = lax.broadcasted_iota(jnp.int32, b3.shape, 1).astype(F32)
    m1 = jnp.max(b3, axis=1, keepdims=True)
    first = jnp.min(jnp.where(b3 == m1, sub, float(GROUP_SIZE)), axis=1, keepdims=True)
    m2 = jnp.max(jnp.where(sub == first, NEG_INF, b3), axis=1, keepdims=True)
    gs = (m1 + m2).reshape(N_GROUPS, tm)
    gidx = lax.broadcasted_iota(jnp.int32, gs.shape, 0).astype(F32)
    gsel = jnp.zeros(gs.shape, F32)
    work = gs
    for _ in range(TOPK_GROUPS):
        pick = gidx == _first_index_of_max(work, gidx, N_GROUPS, 0)
        gsel = jnp.where(pick, 1.0, gsel)
        work = jnp.where(pick, NEG_INF, work)
    emask = jnp.broadcast_to(gsel.reshape(N_GROUPS, 1, tm), b3.shape).reshape(N_EXPERTS, tm)
    work = jnp.where(emask > 0, biased, NEG_INF)
    eidx = lax.broadcasted_iota(jnp.int32, work.shape, 0).astype(F32)
    mask = jnp.zeros(work.shape, F32)
    for _ in range(TOP_K):
        pick = eidx == _first_index_of_max(work, eidx, N_EXPERTS, 0)
        mask = jnp.where(pick, 1.0, mask)
        work = jnp.where(pick, NEG_INF, work)
    sel = mask * scores_t
    return mask, sel / jnp.sum(sel, axis=0, keepdims=True) * ROUTED_SCALE


def _group_specs(block, split):
    return (pl.BlockSpec(block, lambda i, *_: (jnp.minimum(i, split - 1), 0)),
            pl.BlockSpec(block, lambda i, *_: (jnp.maximum(i - split, 0), 0)))


def _router_kernel(x1a_ref, x1b_ref, gffn_ref, rwt_ref, rbias_ref, w1_ref, w3_ref, w2_ref,
                   xn_ref, rank_ref, gates_ref, cnt_ref, x1s_ref, *, split):
    x1 = jnp.where(pl.program_id(0) < split, x1a_ref[...], x1b_ref[...])
    tm = x1.shape[0]
    xn = _rms(x1, gffn_ref[...])
    xb = xn.astype(BF16)
    xn_ref[...] = xb
    logits_t = lax.dot_general(rwt_ref[...], xn, (((1,), (1,)), ((), ())),
                               preferred_element_type=F32, precision=lax.Precision.HIGHEST)
    mask, gates = _route(jax.nn.sigmoid(logits_t), rbias_ref[...])
    gates_ref[...] = gates
    before = (lax.broadcasted_iota(jnp.int32, (tm, tm), 0)
              < lax.broadcasted_iota(jnp.int32, (tm, tm), 1)).astype(BF16)
    rank = _dot(mask.astype(BF16), before)
    rank_ref[...] = jnp.where(mask > 0, rank, -1.0)
    cnt_ref[...] = jnp.broadcast_to(jnp.sum(mask, axis=1, keepdims=True), cnt_ref.shape)
    a = _dot(xb, w1_ref[...])
    hsh = (a * jax.nn.sigmoid(a)) * _dot(xb, w3_ref[...])
    x1s_ref[...] = x1 + _dot(hsh.astype(BF16), w2_ref[...])


def _router(x1a, x1b, gffn, rwt, rbias, w1, w3, w2, *, TM):
    N = x1a.shape[0] + x1b.shape[0]
    split = x1a.shape[0] // TM
    tok = pl.BlockSpec((TM, D_MODEL), lambda i: (i, 0))
    per_e = pl.BlockSpec((N_EXPERTS, TM), lambda i: (0, i))
    consts = (gffn, rwt, rbias, w1, w3, w2)
    return pl.pallas_call(
        functools.partial(_router_kernel, split=split),
        grid=(N // TM,),
        in_specs=list(_group_specs((TM, D_MODEL), split)) + [pl.BlockSpec(c.shape, lambda i: (0, 0)) for c in consts],
        out_specs=(tok, per_e, per_e, pl.BlockSpec((N_EXPERTS, 128), lambda i: (i, 0)), tok),
        out_shape=(jax.ShapeDtypeStruct((N, D_MODEL), BF16),
                   jax.ShapeDtypeStruct((N_EXPERTS, N), F32),
                   jax.ShapeDtypeStruct((N_EXPERTS, N), F32),
                   jax.ShapeDtypeStruct((N // TM * N_EXPERTS, 128), F32),
                   jax.ShapeDtypeStruct((N, D_MODEL), F32)),
        compiler_params=pltpu.CompilerParams(
            dimension_semantics=("arbitrary",), vmem_limit_bytes=VMEM_LIMIT),
        name="router_shared",
    )(x1a, x1b, *consts)


def _segment_plan(cnt, n_tiles_max):
    seg = (cnt + SEG_ALIGN - 1) // SEG_ALIGN * SEG_ALIGN
    used = jnp.sum(seg, axis=0)
    size = (used + CAP + EXPERT_TILE - 1) // EXPERT_TILE * EXPERT_TILE
    row_end = jnp.cumsum(size)
    row_start = row_end - size
    base = row_start[None, :] + jnp.cumsum(seg, axis=0) - seg
    nwin = jnp.maximum((seg + CAP - 1) // CAP, 1)
    tile_end = row_end // EXPERT_TILE
    n_tiles = tile_end[-1]
    t_ids = jnp.arange(n_tiles_max, dtype=jnp.int32)
    tile_e = jnp.minimum(jnp.sum(t_ids[:, None] >= tile_end[None, :], axis=1), N_EXPERTS - 1)
    i32 = lambda a: a.astype(jnp.int32)
    return (i32(base).reshape(-1), i32(nwin).reshape(-1), i32(row_start), i32(row_end),
            i32(tile_e), i32(n_tiles).reshape(1))


def _one_hot_rows(rank_ref, chunk, first_row, values_ref=None):
    tm = rank_ref.shape[1]
    j = (lax.broadcasted_iota(jnp.int32, (CAP, tm), 0) + first_row).astype(F32)
    rows = []
    for k in range(CHUNK_E):
        e = chunk * CHUNK_E + k
        hit = j == rank_ref[e:e + 1, :]
        val = 1.0 if values_ref is None else values_ref[e:e + 1, :]
        rows.append(jnp.where(hit, val, 0.0).astype(BF16))
    return jnp.concatenate(rows, axis=0)


def _window(hbm, base_ref, idx, w):
    start = pl.multiple_of(base_ref[idx] + w * CAP, SEG_ALIGN)
    return hbm.at[pl.ds(start, CAP)]


def _chunk_windows(nwin_ref, tile_idx, c):
    extra = nwin_ref[tile_idx * N_EXPERTS + c * CHUNK_E]
    for k in range(1, CHUNK_E):
        extra = jnp.maximum(extra, nwin_ref[tile_idx * N_EXPERTS + c * CHUNK_E + k])
    return extra


def _tile_windows(nwin_ref, tile_idx):
    extra = _chunk_windows(nwin_ref, tile_idx, 0)
    for c in range(1, N_CHUNKS):
        extra = jnp.maximum(extra, _chunk_windows(nwin_ref, tile_idx, c))
    return extra


def _dispatch_kernel(base_ref, nwin_ref, rstart_ref, rend_ref, xn_ref, rank_ref, xs_hbm,
                     stage0, stage1, ostage, zbuf, sem, osem, zsem):
    stage = (stage0, stage1)
    i = pl.program_id(0)
    last = pl.num_programs(0) - 1

    def window_copy(p, e, tile_idx):
        return pltpu.make_async_copy(stage[p].at[pl.ds(e * CAP, CAP)],
                                     _window(xs_hbm, base_ref, tile_idx * N_EXPERTS + e, 0), sem.at[p])

    def wait_tile(p, tile_idx):
        for e in range(N_EXPERTS):
            window_copy(p, e, tile_idx).wait()

    @pl.when(i == 0)
    def _zero_tails():
        zbuf[...] = jnp.zeros(zbuf.shape, BF16)
        copies = []
        for e in range(N_EXPERTS):
            end = rend_ref[e]
            a0 = pl.multiple_of(end - EXPERT_TILE, SEG_ALIGN)
            b0 = pl.multiple_of(jnp.maximum(end - EXPERT_TILE - CAP, rstart_ref[e]), SEG_ALIGN)
            copies.append(pltpu.make_async_copy(zbuf, xs_hbm.at[pl.ds(a0, EXPERT_TILE)], zsem))
            copies.append(pltpu.make_async_copy(zbuf.at[pl.ds(0, CAP)], xs_hbm.at[pl.ds(b0, CAP)], zsem))
        for cp in copies:
            cp.start()
        for cp in copies:
            cp.wait()

    for p in (0, 1):
        @pl.when(i % 2 == p)
        def _step(p=p):
            xb = xn_ref[...]
            rows = CHUNK_E * CAP
            for c in range(N_CHUNKS):
                stage[p][pl.ds(c * rows, rows), :] = _dot(_one_hot_rows(rank_ref, c, 0), xb).astype(BF16)

            @pl.when(i >= 1)
            def _():
                wait_tile(1 - p, i - 1)

            for e in range(N_EXPERTS):
                window_copy(p, e, i).start()

    extra = _tile_windows(nwin_ref, i)

    @pl.when(extra > 1)
    def _long_segments():
        def body(w, carry):
            for c in range(N_CHUNKS):
                @pl.when(w < _chunk_windows(nwin_ref, i, c))
                def _(c=c):
                    ostage[...] = _dot(_one_hot_rows(rank_ref, c, w * CAP), xn_ref[...]).astype(BF16)
                    for k in range(CHUNK_E):
                        idx = i * N_EXPERTS + c * CHUNK_E + k

                        @pl.when(w < nwin_ref[idx])
                        def _(k=k, idx=idx):
                            cp = pltpu.make_async_copy(ostage.at[pl.ds(k * CAP, CAP)],
                                                       _window(xs_hbm, base_ref, idx, w), osem)
                            cp.start()
                            cp.wait()
            return carry

        lax.fori_loop(1, extra, body, 0)

    for p in (0, 1):
        @pl.when((i == last) & (i % 2 == p))
        def _drain(p=p):
            wait_tile(p, i)


def _dispatch(xn, rank_t, base, nwin, row_start, row_end, *, n_rows, TM):
    N = xn.shape[0]
    stage = pltpu.VMEM((N_EXPERTS * CAP, D_MODEL), BF16)
    return pl.pallas_call(
        _dispatch_kernel,
        grid_spec=pltpu.PrefetchScalarGridSpec(
            num_scalar_prefetch=4,
            grid=(N // TM,),
            in_specs=[pl.BlockSpec((TM, D_MODEL), lambda i, *_: (i, 0)),
                      pl.BlockSpec((N_EXPERTS, TM), lambda i, *_: (0, i))],
            out_specs=pl.BlockSpec(memory_space=pl.ANY),
            scratch_shapes=[
                stage, stage,
                pltpu.VMEM((CHUNK_E * CAP, D_MODEL), BF16),
                pltpu.VMEM((EXPERT_TILE, D_MODEL), BF16),
                pltpu.SemaphoreType.DMA((2,)),
                pltpu.SemaphoreType.DMA(()),
                pltpu.SemaphoreType.DMA(()),
            ],
        ),
        out_shape=jax.ShapeDtypeStruct((n_rows, D_MODEL), BF16),
        compiler_params=pltpu.CompilerParams(
            dimension_semantics=("arbitrary",), vmem_limit_bytes=VMEM_LIMIT),
        name="moe_dispatch",
    )(base, nwin, row_start, row_end, xn, rank_t)


def _expert_kernel(tile_e_ref, n_tiles_ref, xs_ref, w1_ref, w3_ref, w2_ref, ys_ref, w1b, w3b, w2b):
    t = pl.program_id(0)
    valid = t < n_tiles_ref[0]
    prev_e = tile_e_ref[jnp.maximum(t - 1, 0)]

    @pl.when(valid & ((t == 0) | (tile_e_ref[t] != prev_e)))
    def _cast_weights():
        w1b[...] = w1_ref[0].astype(BF16)
        w3b[...] = w3_ref[0].astype(BF16)
        w2b[...] = w2_ref[0].astype(BF16)

    @pl.when(valid)
    def _compute():
        xb = xs_ref[...]
        a = _dot(xb, w1b[...])
        hb = (a * jax.nn.sigmoid(a)) * _dot(xb, w3b[...])
        ys_ref[...] = _dot(hb.astype(BF16), w2b[...]).astype(BF16)

    @pl.when(jnp.logical_not(valid))
    def _spare():
        ys_ref[...] = jnp.zeros(ys_ref.shape, BF16)


def _experts(xs, tile_e, n_tiles, w1, w3, w2):
    t_max = xs.shape[0] // EXPERT_TILE
    clamp = lambda t, nt: jnp.minimum(t, nt[0] - 1)
    wspec_in = pl.BlockSpec((1, D_MODEL, EXPERT_FF), lambda t, te, nt: (te[clamp(t, nt)], 0, 0))
    wspec_out = pl.BlockSpec((1, EXPERT_FF, D_MODEL), lambda t, te, nt: (te[clamp(t, nt)], 0, 0))
    return pl.pallas_call(
        _expert_kernel,
        grid_spec=pltpu.PrefetchScalarGridSpec(
            num_scalar_prefetch=2,
            grid=(t_max,),
            in_specs=[pl.BlockSpec((EXPERT_TILE, D_MODEL), lambda t, te, nt: (clamp(t, nt), 0)),
                      wspec_in, wspec_in, wspec_out],
            out_specs=pl.BlockSpec((EXPERT_TILE, D_MODEL),
                                   lambda t, te, nt: (jnp.where(t < nt[0], t, t_max), 0)),
            scratch_shapes=[
                pltpu.VMEM((D_MODEL, EXPERT_FF), BF16),
                pltpu.VMEM((D_MODEL, EXPERT_FF), BF16),
                pltpu.VMEM((EXPERT_FF, D_MODEL), BF16),
            ],
        ),
        out_shape=jax.ShapeDtypeStruct(((t_max + 1) * EXPERT_TILE, D_MODEL), BF16),
        compiler_params=pltpu.CompilerParams(
            dimension_semantics=("arbitrary",), vmem_limit_bytes=VMEM_LIMIT),
        name="moe_experts",
    )(tile_e, n_tiles, xs, w1, w3, w2)


def _final_kernel(base_ref, nwin_ref, x1s_ref, rank_ref, gates_ref, ys_hbm, pa_ref, pb_ref, gple_ref,
                  wgate_ref, wproj_ref, gfin_ref, ya_ref, yb_ref, win0, win1, owin, acc_ref, sem, osem,
                  *, split):
    i = pl.program_id(0)
    last = pl.num_programs(0) - 1
    win = (win0, win1)

    def window_copy(p, e, tile_idx):
        return pltpu.make_async_copy(_window(ys_hbm, base_ref, tile_idx * N_EXPERTS + e, 0),
                                     win[p].at[pl.ds(e * CAP, CAP)], sem.at[p])

    @pl.when(i == 0)
    def _first():
        owin[...] = jnp.zeros(owin.shape, BF16)
        for e in range(N_EXPERTS):
            window_copy(0, e, i).start()

    contract0 = (((0,), (0,)), ((), ()))

    for p in (0, 1):
        @pl.when(i % 2 == p)
        def _step(p=p):
            for e in range(N_EXPERTS):
                window_copy(p, e, i).wait()

            @pl.when(i < last)
            def _():
                for e in range(N_EXPERTS):
                    window_copy(1 - p, e, i + 1).start()

            one_hot = jnp.concatenate([_one_hot_rows(rank_ref, c, 0, gates_ref) for c in range(N_CHUNKS)], axis=0)
            acc_ref[...] = lax.dot_general(one_hot, win[p][...], contract0, preferred_element_type=F32)

    extra = _tile_windows(nwin_ref, i)

    @pl.when(extra > 1)
    def _long_segments():
        def body(w, carry):
            for c in range(N_CHUNKS):
                @pl.when(w < _chunk_windows(nwin_ref, i, c))
                def _(c=c):
                    for k in range(CHUNK_E):
                        idx = i * N_EXPERTS + c * CHUNK_E + k

                        @pl.when(w < nwin_ref[idx])
                        def _(k=k, idx=idx):
                            cp = pltpu.make_async_copy(_window(ys_hbm, base_ref, idx, w),
                                                       owin.at[pl.ds(k * CAP, CAP)], osem)
                            cp.start()
                            cp.wait()
                    acc_ref[...] += lax.dot_general(_one_hot_rows(rank_ref, c, w * CAP, gates_ref), owin[...],
                                                    contract0, preferred_element_type=F32)
            return carry

        lax.fori_loop(1, extra, body, 0)

    x2 = x1s_ref[...] + acc_ref[...]
    r_ = _rms(x2, gple_ref[...])
    gate = jax.nn.sigmoid(_dot(r_.astype(BF16), wgate_ref[...]))
    pt = jnp.where(i < split, pa_ref[...], pb_ref[...])
    x3 = x2 + _dot(pt.astype(BF16), wproj_ref[...]) * gate
    y = _rms(x3, gfin_ref[...])

    @pl.when(i < split)
    def _():
        ya_ref[...] = y

    @pl.when(i >= split)
    def _():
        yb_ref[...] = y


def _final(x1s, rank_t, gates_t, ys, pa, pb, gple, wgate, wproj, gfin, base, nwin, *, TM):
    N = x1s.shape[0]
    split = pa.shape[0] // TM
    tok = pl.BlockSpec((TM, D_MODEL), lambda i, *_: (i, 0))
    per_e = pl.BlockSpec((N_EXPERTS, TM), lambda i, *_: (0, i))
    consts = (gple, wgate, wproj, gfin)
    win = pltpu.VMEM((N_EXPERTS * CAP, D_MODEL), BF16)
    return pl.pallas_call(
        functools.partial(_final_kernel, split=split),
        grid_spec=pltpu.PrefetchScalarGridSpec(
            num_scalar_prefetch=2,
            grid=(N // TM,),
            in_specs=[tok, per_e, per_e, pl.BlockSpec(memory_space=pl.ANY)]
            + list(_group_specs((TM, PLE_DIM), split))
            + [pl.BlockSpec(c.shape, lambda i, *_: (0, 0)) for c in consts],
            out_specs=_group_specs((TM, D_MODEL), split),
            scratch_shapes=[win, win, pltpu.VMEM((CHUNK_E * CAP, D_MODEL), BF16), pltpu.VMEM((TM, D_MODEL), F32),
                            pltpu.SemaphoreType.DMA((2,)), pltpu.SemaphoreType.DMA(())],
        ),
        out_shape=(jax.ShapeDtypeStruct((pa.shape[0], D_MODEL), F32),
                   jax.ShapeDtypeStruct((pb.shape[0], D_MODEL), F32)),
        compiler_params=pltpu.CompilerParams(
            dimension_semantics=("arbitrary",), vmem_limit_bytes=VMEM_LIMIT),
        name="combine_ple_final",
    )(base, nwin, x1s, rank_t, gates_t, ys, pa, pb, *consts)


def kernel(x_prompt, x_sample, p_prompt, p_sample, state_C, state_n, state_m, state_pool, norm_mix_g, w_in, b_igate, b_fgate, head_norm_g, w_pool, pool_scale, w_out, norm_ffn_g, router_w, router_bias, ex_w1, ex_w3, ex_w2, sh_w1, sh_w3, sh_w2, norm_ple_g, w_ple_gate, w_ple_proj, final_norm_g):
    depth = norm_mix_g.shape[0]
    assert depth == 1
    l = 0
    B, T, _ = x_prompt.shape
    Bs, Ts, _ = x_sample.shape
    g0 = 4 * MLSTM_WIDTH
    w = w_in[l]
    wcat = jnp.concatenate(
        [w[:, :g0], w[:, g0 + 2 * HEADS:], jnp.repeat(w[:, g0:g0 + 2 * HEADS], 128, axis=1)], axis=1).astype(BF16)
    gbias = jnp.repeat(jnp.concatenate([b_igate[l], b_fgate[l]]), 128)[None, :]
    mixer_w = (wcat, norm_mix_g[l][None, :], gbias, head_norm_g[l][None, :], w_pool[l].astype(BF16),
               pool_scale[l][None, :], w_out[l].astype(BF16))

    x1p, Cp, Np, Mp, Bp = _mixer(x_prompt, *mixer_w, None, BB=1, TT=512, L=128, start_pos=0)
    state = (state_C[l], state_n[l], state_m[l][..., None], state_pool[l])
    x1s_, Cs, Ns, Ms, Bs_ = _mixer(x_sample, *mixer_w, state, BB=16, TT=Ts, L=Ts, start_pos=PAST_LEN)

    N = B * T + Bs * Ts
    assert (B * T) % ROUTE_TILE == 0 and (Bs * Ts) % ROUTE_TILE == 0
    n_route_tiles = N // ROUTE_TILE

    xn, rank_t, gates_t, cnt, x1sh = _router(
        x1p.reshape(B * T, D_MODEL), x1s_.reshape(Bs * Ts, D_MODEL),
        norm_ffn_g[l][None, :], router_w[l].T, router_bias[l][:, None],
        sh_w1[l].astype(BF16), sh_w3[l].astype(BF16), sh_w2[l].astype(BF16), TM=ROUTE_TILE)

    max_rows = TOP_K * N + (SEG_ALIGN - 1) * n_route_tiles * N_EXPERTS + N_EXPERTS * (CAP + EXPERT_TILE)
    t_max = -(-max_rows // EXPERT_TILE)
    cnt = cnt[:, 0].reshape(n_route_tiles, N_EXPERTS).astype(jnp.int32)
    base, nwin, row_start, row_end, tile_e, n_tiles = _segment_plan(cnt, t_max)

    xs = _dispatch(xn, rank_t, base, nwin, row_start, row_end, n_rows=t_max * EXPERT_TILE, TM=ROUTE_TILE)
    ys = _experts(xs, tile_e, n_tiles, ex_w1[l], ex_w3[l], ex_w2[l])
    y_prompt, y_sample = _final(
        x1sh, rank_t, gates_t, ys, p_prompt[l].reshape(B * T, PLE_DIM), p_sample[l].reshape(Bs * Ts, PLE_DIM),
        norm_ple_g[l][None, :], w_ple_gate[l].astype(BF16), w_ple_proj[l].astype(BF16),
        final_norm_g[None, :], base, nwin, TM=ROUTE_TILE)
    return (y_prompt.reshape(B, T, D_MODEL), y_sample.reshape(Bs, Ts, D_MODEL),
            Cp[None], Np[None], Mp[..., 0][None], Bp[None],
            Cs[None], Ns[None], Ms[..., 0][None], Bs_[None])
```

```python
import functools

import jax
import jax.numpy as jnp
from jax import lax
from jax.experimental import pallas as pl
from jax.experimental.pallas import tpu as pltpu

D_MODEL = 1024
HEADS = 4
DH = 128
MLSTM_WIDTH = HEADS * DH
POOL_WIDTH = 512
POOL_WINDOWS = (2, 4, 8, 16)
POOL_GDIM = 128
POOL_BUF = 15
POOL_PAD = 16
N_EXPERTS = 64
TOP_K = 8
N_GROUPS = 8
GROUP_SIZE = N_EXPERTS // N_GROUPS
TOPK_GROUPS = 4
EXPERT_FF = 256
ROUTED_SCALE = 2.5
NORM_EPS = 1e-6
PLE_DIM = 256
PAST_LEN = 16384

COL_Q, COL_K, COL_V, COL_O, COL_U, COL_I, COL_F = 0, 512, 1024, 1536, 2048, 2560, 2688
IN_COLS_PAD = 2816
GATE_COLS = HEADS * 128

ROUTE_TILE = 256
EXPERT_TILE = 1024
SEG_ALIGN = 16
CAP = 48
CHUNK_E = 8
N_CHUNKS = N_EXPERTS // CHUNK_E

VMEM_LIMIT = 56 * 1024 * 1024
F32 = jnp.float32
BF16 = jnp.bfloat16
NEG_INF = float("-inf")


def _rms(x, g):
    return x * lax.rsqrt(jnp.mean(x * x, axis=-1, keepdims=True) + NORM_EPS) * g


def _log_sigmoid(x):
    return jnp.minimum(x, 0.0) - jnp.log1p(jnp.exp(-jnp.abs(x)))


def _dot(a, b):
    return jnp.dot(a, b, preferred_element_type=F32)


def _mlstm_chunk(q, k, v, b_col, i_col, C, n, m, causal, eye):
    L = q.shape[0]
    r_col = i_col - b_col
    r_row = jnp.sum(jnp.where(eye, r_col, 0.0), axis=0, keepdims=True)
    d = jnp.where(causal, b_col + r_row, NEG_INF)
    inter = b_col + m
    m_t = jnp.maximum(inter, jnp.max(d, axis=-1, keepdims=True))
    w_inter = jnp.exp(inter - m_t)
    qb, kb, vb = q.astype(BF16), k.astype(BF16), v.astype(BF16)
    qk = lax.dot_general(qb, kb, (((1,), (1,)), ((), ())), preferred_element_type=F32)
    s = qk * jnp.exp(d - m_t)
    num = w_inter * _dot(qb, C.astype(BF16)) + _dot(s.astype(BF16), vb)
    nq = w_inter * jnp.sum(q * n, axis=-1, keepdims=True) + jnp.sum(s, axis=-1, keepdims=True)
    h = num / jnp.maximum(jnp.abs(nq), jnp.exp(-m_t))
    b_last = b_col[L - 1:L, :]
    m_new = jnp.maximum(b_last + m, jnp.max(b_last + r_row, axis=-1, keepdims=True))
    fw = jnp.exp(b_last + m - m_new)
    iw_col = jnp.exp(b_last + r_col - m_new)
    kw = iw_col * k
    C_new = fw * C + lax.dot_general(kw.astype(BF16), vb, (((0,), (0,)), ((), ())),
                                     preferred_element_type=F32)
    n_new = fw * n + jnp.sum(kw, axis=0, keepdims=True)
    return h, C_new, n_new, m_new


def _mlstm_chunk_square(q, k, v, r, b, mp, CN, m, causal, eye, ones):
    L = q.shape[0]
    r_row = jnp.sum(jnp.where(eye, r, 0.0), axis=0, keepdims=True)
    g = jnp.maximum(mp, m)
    w_inter = jnp.exp(m - g)
    qb, kb = q.astype(BF16), k.astype(BF16)
    v1 = jnp.concatenate([v.astype(BF16), ones], axis=1)
    qk = lax.dot_general(qb, kb, (((1,), (1,)), ((), ())), preferred_element_type=F32)
    s = (qk * jnp.where(causal, jnp.exp(r_row - g), 0.0)).astype(BF16)
    sv = _dot(s, v1)
    qcn = _dot(qb, CN.astype(BF16))
    num = w_inter * qcn[:, :DH] + sv[:, :DH]
    nq = w_inter * qcn[:, DH:] + sv[:, DH:]
    h = num / jnp.maximum(jnp.abs(nq), jnp.exp(-(b + g)))
    g_last = g[L - 1:L, :]
    m_new = b[L - 1:L, :] + g_last
    fw = jnp.exp(m - g_last)
    kw = (jnp.exp(r - g_last) * k).astype(BF16)
    CN_new = jnp.concatenate([fw, fw], axis=1) * CN + lax.dot_general(
        kw, v1, (((0,), (0,)), ((), ())), preferred_element_type=F32)
    return h, CN_new, m_new[:, 0:1]


def _chunk_scan(x, L, op, fill):
    pos = lax.broadcasted_iota(jnp.int32, x.shape, 0) & (L - 1)
    k = 1
    while k < L:
        x = op(x, jnp.where(pos >= k, pltpu.roll(x, k, axis=0), fill))
        k *= 2
    return x


def _mixer_kernel(*refs, BB, TT, L, start_pos, zero_state):
    if zero_state:
        (x_ref, wcat_ref, gmix_ref, gbias_ref, hng_ref, wpool_ref, pscale_ref, wout_ref,
         x1_ref, c_ref, n_ref, m_ref, buf_ref,
         z_ref, a_ref, b_ref, mp_ref, cn_ref, mix_ref, ext_ref) = refs
    else:
        (x_ref, wcat_ref, gmix_ref, gbias_ref, hng_ref, wpool_ref, pscale_ref, wout_ref,
         c0_ref, n0_ref, m0_ref, buf0_ref,
         x1_ref, c_ref, n_ref, m_ref, buf_ref,
         z_ref, a_ref, b_ref, mp_ref, cn_ref, mix_ref, ext_ref) = refs
    t = pl.program_id(1)
    R = BB * TT
    n_chunks = TT // L
    square = L == DH
    assert not square or (zero_state and BB == 1)

    @pl.when(t == 0)
    def _init():
        ext_ref[:, 0:POOL_PAD, :] = jnp.zeros((BB, POOL_PAD, POOL_WIDTH), F32)
        if zero_state:
            cn_ref[...] = jnp.zeros(cn_ref.shape, F32)
            c_ref[...] = jnp.zeros(c_ref.shape, F32)
            n_ref[...] = jnp.zeros(n_ref.shape, F32)
            m_ref[...] = jnp.zeros(m_ref.shape, F32)
        else:
            c_ref[...] = c0_ref[...]
            n_ref[...] = n0_ref[...]
            m_ref[...] = m0_ref[...]
            ext_ref[:, 1:POOL_PAD, :] = buf0_ref[...]

    x = x_ref[...].reshape(R, D_MODEL)
    hn = _rms(x, gmix_ref[...])
    z_ref[...] = _dot(hn.astype(BF16), wcat_ref[...])
    gbias = gbias_ref[...]
    gi = z_ref[:, COL_I:COL_I + 128] + gbias[:, :128]
    b_c = _chunk_scan(_log_sigmoid(z_ref[:, COL_F:COL_F + 128] + gbias[:, 128:]), L, jnp.add, 0.0)
    if square:
        r_c = gi - b_c
        mp_c = _chunk_scan(r_c, L, jnp.maximum, NEG_INF)
        for hd in range(HEADS):
            lanes = slice(hd * DH, (hd + 1) * DH)
            a_ref[:, lanes] = jnp.broadcast_to(r_c[:, hd:hd + 1], (R, DH))
            b_ref[:, lanes] = jnp.broadcast_to(b_c[:, hd:hd + 1], (R, DH))
            mp_ref[:, lanes] = jnp.broadcast_to(mp_c[:, hd:hd + 1], (R, DH))
    else:
        a_ref[:, 0:128] = gi
        b_ref[:, 0:128] = b_c

    row = lax.broadcasted_iota(jnp.int32, (L, L), 0)
    col = lax.broadcasted_iota(jnp.int32, (L, L), 1)
    causal = row >= col
    eye = row == col
    ones = jnp.ones((DH, DH), BF16)
    hng = hng_ref[...]

    def chunk_body(s, carry):
        row0 = pl.multiple_of(s * L, L)
        bb = 0 if BB == 1 else s // n_chunks
        rows = pl.ds(row0, L)
        for hd in range(HEADS):
            lanes = slice(hd * DH, (hd + 1) * DH)
            q = z_ref[rows, COL_Q + hd * DH:COL_Q + (hd + 1) * DH] * (DH ** -0.5)
            k = z_ref[rows, COL_K + hd * DH:COL_K + (hd + 1) * DH]
            v = z_ref[rows, COL_V + hd * DH:COL_V + (hd + 1) * DH]
            o = z_ref[rows, COL_O + hd * DH:COL_O + (hd + 1) * DH]
            m = m_ref[bb, hd:hd + 1, :]
            if square:
                h, CN_new, m_new = _mlstm_chunk_square(
                    q, k, v, a_ref[rows, lanes], b_ref[rows, lanes], mp_ref[rows, lanes], cn_ref[hd], m,
                    causal, eye, ones)
                cn_ref[hd] = CN_new
            else:
                h, C_new, n_new, m_new = _mlstm_chunk(
                    q, k, v, b_ref[rows, hd:hd + 1], a_ref[rows, hd:hd + 1], c_ref[bb, hd],
                    n_ref[bb, hd:hd + 1, :], m, causal, eye)
                c_ref[bb, hd] = C_new
                n_ref[bb, hd:hd + 1, :] = n_new
            m_ref[bb, hd:hd + 1, :] = m_new
            msq = jnp.mean(h * h, axis=-1, keepdims=True)
            mix_ref[rows, lanes] = h * lax.rsqrt(msq + NORM_EPS) * hng[:, lanes] * jax.nn.sigmoid(o)
        return carry

    lax.fori_loop(0, BB * n_chunks, chunk_body, 0)

    if square:
        @pl.when(t == pl.num_programs(1) - 1)
        def _emit_state():
            for hd in range(HEADS):
                cn = cn_ref[hd]
                c_ref[0, hd] = cn[:, :DH]
                n_ref[0, hd:hd + 1, :] = jnp.sum(jnp.where(eye, cn[:, DH:], 0.0), axis=0, keepdims=True)

    ext_ref[:, POOL_PAD:POOL_PAD + TT, :] = z_ref[:, COL_U:COL_U + POOL_WIDTH].reshape(BB, TT, POOL_WIDTH)
    pos = start_pos + t * TT + lax.broadcasted_iota(jnp.int32, (1, TT, 1), 1)
    pscale = pscale_ref[...]
    for gidx, w in enumerate(POOL_WINDOWS):
        lanes = slice(gidx * POOL_GDIM, (gidx + 1) * POOL_GDIM)
        u_g = ext_ref[:, POOL_PAD:POOL_PAD + TT, lanes]
        acc = u_g
        for j in range(1, w):
            acc = acc + ext_ref[:, POOL_PAD - j:POOL_PAD - j + TT, lanes]
        cnt = jnp.minimum(pos + 1, w).astype(F32)
        pooled = (acc / cnt - u_g).reshape(R, POOL_GDIM)
        mixed = _dot(pooled.astype(BF16), wpool_ref[gidx]) * pscale[:, lanes]
        mix_ref[:, MLSTM_WIDTH + gidx * POOL_GDIM:MLSTM_WIDTH + (gidx + 1) * POOL_GDIM] = mixed
    new_buf = ext_ref[:, TT + 1:TT + POOL_PAD, :]
    buf_ref[...] = new_buf
    ext_ref[:, 1:POOL_PAD, :] = new_buf

    out = x_ref[...].reshape(R, D_MODEL) + _dot(mix_ref[...].astype(BF16), wout_ref[...])
    x1_ref[...] = out.reshape(BB, TT, D_MODEL)


def _const_spec(shape):
    nd = len(shape)
    return pl.BlockSpec(shape, lambda b, t, _nd=nd: (0,) * _nd, pipeline_mode=pl.Buffered(1))


def _mixer(x, wcat, gmix, gbias, hng, wpool, pscale, wout, state, *, BB, TT, L, start_pos):
    B, T, _ = x.shape
    zero_state = state is None
    R = BB * TT
    grid = (B // BB, T // TT)
    x_spec = pl.BlockSpec((BB, TT, D_MODEL), lambda b, t: (b, t, 0))
    c_spec = pl.BlockSpec((BB, HEADS, DH, DH), lambda b, t: (b, 0, 0, 0))
    n_spec = pl.BlockSpec((BB, HEADS, DH), lambda b, t: (b, 0, 0))
    m_spec = pl.BlockSpec((BB, HEADS, 1), lambda b, t: (b, 0, 0))
    buf_spec = pl.BlockSpec((BB, POOL_BUF, POOL_WIDTH), lambda b, t: (b, 0, 0))
    weights = (wcat, gmix, gbias, hng, wpool, pscale, wout)
    in_specs = [x_spec] + [_const_spec(w.shape) for w in weights]
    args = [x, *weights]
    if not zero_state:
        in_specs += [c_spec, n_spec, m_spec, buf_spec]
        args += list(state)
    out_shape = (
        jax.ShapeDtypeStruct((B, T, D_MODEL), F32),
        jax.ShapeDtypeStruct((B, HEADS, DH, DH), F32),
        jax.ShapeDtypeStruct((B, HEADS, DH), F32),
        jax.ShapeDtypeStruct((B, HEADS, 1), F32),
        jax.ShapeDtypeStruct((B, POOL_BUF, POOL_WIDTH), F32),
    )
    kern = functools.partial(_mixer_kernel, BB=BB, TT=TT, L=L, start_pos=start_pos, zero_state=zero_state)
    return pl.pallas_call(
        kern,
        grid=grid,
        in_specs=in_specs,
        out_specs=(x_spec, c_spec, n_spec, m_spec, buf_spec),
        out_shape=out_shape,
        scratch_shapes=[
            pltpu.VMEM((R, IN_COLS_PAD), F32),
            pltpu.VMEM((R, GATE_COLS), F32),
            pltpu.VMEM((R, GATE_COLS), F32),
            pltpu.VMEM((R, GATE_COLS), F32),
            pltpu.VMEM((HEADS, DH, 2 * DH), F32),
            pltpu.VMEM((R, D_MODEL), F32),
            pltpu.VMEM((BB, POOL_PAD + TT, POOL_WIDTH), F32),
        ],
        compiler_params=pltpu.CompilerParams(
            dimension_semantics=("arbitrary", "arbitrary"), vmem_limit_bytes=VMEM_LIMIT),
        name="mixer_zero" if zero_state else "mixer_state",
    )(*args)


def _first_index_of_max(work, idx, n, axis):
    mx = jnp.max(work, axis=axis, keepdims=True)
    return jnp.min(jnp.where(work == mx, idx, float(n)), axis=axis, keepdims=True)


def _route(scores_t, bias_t):
    tm = scores_t.shape[1]
    biased = scores_t + bias_t
    b3 = biased.reshape(N_GROUPS, GROUP_SIZE, tm)
    sub = lax.broadcasted_iota(jnp.int32, b3.shape, 1).astype(F32)
    m1 = jnp.max(b3, axis=1, keepdims=True)
    first = jnp.min(jnp.where(b3 == m1, sub, float(GROUP_SIZE)), axis=1, keepdims=True)
    m2 = jnp.max(jnp.where(sub == first, NEG_INF, b3), axis=1, keepdims=True)
    gs = (m1 + m2).reshape(N_GROUPS, tm)
    gidx = lax.broadcasted_iota(jnp.int32, gs.shape, 0).astype(F32)
    gsel = jnp.zeros(gs.shape, F32)
    work = gs
    for _ in range(TOPK_GROUPS):
        pick = gidx == _first_index_of_max(work, gidx, N_GROUPS, 0)
        gsel = jnp.where(pick, 1.0, gsel)
        work = jnp.where(pick, NEG_INF, work)
    emask = jnp.broadcast_to(gsel.reshape(N_GROUPS, 1, tm), b3.shape).reshape(N_EXPERTS, tm)
    work = jnp.where(emask > 0, biased, NEG_INF)
    eidx = lax.broadcasted_iota(jnp.int32, work.shape, 0).astype(F32)
    mask = jnp.zeros(work.shape, F32)
    for _ in range(TOP_K):
        pick = eidx == _first_index_of_max(work, eidx, N_EXPERTS, 0)
        mask = jnp.where(pick, 1.0, mask)
        work = jnp.where(pick, NEG_INF, work)
    sel = mask * scores_t
    return mask, sel / jnp.sum(sel, axis=0, keepdims=True) * ROUTED_SCALE


def _group_specs(block, split):
    return (pl.BlockSpec(block, lambda i, *_: (jnp.minimum(i, split - 1), 0)),
            pl.BlockSpec(block, lambda i, *_: (jnp.maximum(i - split, 0), 0)))


def _router_kernel(x1a_ref, x1b_ref, gffn_ref, rwt_ref, rbias_ref, w1_ref, w3_ref, w2_ref,
                   xn_ref, rank_ref, gates_ref, cnt_ref, x1s_ref, *, split):
    x1 = jnp.where(pl.program_id(0) < split, x1a_ref[...], x1b_ref[...])
    tm = x1.shape[0]
    xn = _rms(x1, gffn_ref[...])
    xb = xn.astype(BF16)
    xn_ref[...] = xb
    logits_t = lax.dot_general(rwt_ref[...], xn, (((1,), (1,)), ((), ())),
                               preferred_element_type=F32, precision=lax.Precision.HIGHEST)
    mask, gates = _route(jax.nn.sigmoid(logits_t), rbias_ref[...])
    gates_ref[...] = gates
    before = (lax.broadcasted_iota(jnp.int32, (tm, tm), 0)
              < lax.broadcasted_iota(jnp.int32, (tm, tm), 1)).astype(BF16)
    rank = _dot(mask.astype(BF16), before)
    rank_ref[...] = jnp.where(mask > 0, rank, -1.0)
    cnt_ref[...] = jnp.broadcast_to(jnp.sum(mask, axis=1, keepdims=True), cnt_ref.shape)
    a = _dot(xb, w1_ref[...])
    hsh = (a * jax.nn.sigmoid(a)) * _dot(xb, w3_ref[...])
    x1s_ref[...] = x1 + _dot(hsh.astype(BF16), w2_ref[...])


def _router(x1a, x1b, gffn, rwt, rbias, w1, w3, w2, *, TM):
    N = x1a.shape[0] + x1b.shape[0]
    split = x1a.shape[0] // TM
    tok = pl.BlockSpec((TM, D_MODEL), lambda i: (i, 0))
    per_e = pl.BlockSpec((N_EXPERTS, TM), lambda i: (0, i))
    consts = (gffn, rwt, rbias, w1, w3, w2)
    return pl.pallas_call(
        functools.partial(_router_kernel, split=split),
        grid=(N // TM,),
        in_specs=list(_group_specs((TM, D_MODEL), split)) + [pl.BlockSpec(c.shape, lambda i: (0, 0)) for c in consts],
        out_specs=(tok, per_e, per_e, pl.BlockSpec((N_EXPERTS, 128), lambda i: (i, 0)), tok),
        out_shape=(jax.ShapeDtypeStruct((N, D_MODEL), BF16),
                   jax.ShapeDtypeStruct((N_EXPERTS, N), F32),
                   jax.ShapeDtypeStruct((N_EXPERTS, N), F32),
                   jax.ShapeDtypeStruct((N // TM * N_EXPERTS, 128), F32),
                   jax.ShapeDtypeStruct((N, D_MODEL), F32)),
        compiler_params=pltpu.CompilerParams(
            dimension_semantics=("arbitrary",), vmem_limit_bytes=VMEM_LIMIT),
        name="router_shared",
    )(x1a, x1b, *consts)


def _segment_plan(cnt, n_tiles_max):
    seg = (cnt + SEG_ALIGN - 1) // SEG_ALIGN * SEG_ALIGN
    used = jnp.sum(seg, axis=0)
    size = (used + CAP + EXPERT_TILE - 1) // EXPERT_TILE * EXPERT_TILE
    row_end = jnp.cumsum(size)
    row_start = row_end - size
    base = row_start[None, :] + jnp.cumsum(seg, axis=0) - seg
    nwin = jnp.maximum((seg + CAP - 1) // CAP, 1)
    tile_end = row_end // EXPERT_TILE
    n_tiles = tile_end[-1]
    t_ids = jnp.arange(n_tiles_max, dtype=jnp.int32)
    tile_e = jnp.minimum(jnp.sum(t_ids[:, None] >= tile_end[None, :], axis=1), N_EXPERTS - 1)
    i32 = lambda a: a.astype(jnp.int32)
    return (i32(base).reshape(-1), i32(nwin).reshape(-1), i32(row_start), i32(row_end),
            i32(tile_e), i32(n_tiles).reshape(1))


def _one_hot_rows(rank_ref, chunk, first_row, values_ref=None):
    tm = rank_ref.shape[1]
    j = (lax.broadcasted_iota(jnp.int32, (CAP, tm), 0) + first_row).astype(F32)
    rows = []
    for k in range(CHUNK_E):
        e = chunk * CHUNK_E + k
        hit = j == rank_ref[e:e + 1, :]
        val = 1.0 if values_ref is None else values_ref[e:e + 1, :]
        rows.append(jnp.where(hit, val, 0.0).astype(BF16))
    return jnp.concatenate(rows, axis=0)


def _window(hbm, base_ref, idx, w):
    start = pl.multiple_of(base_ref[idx] + w * CAP, SEG_ALIGN)
    return hbm.at[pl.ds(start, CAP)]


def _chunk_windows(nwin_ref, tile_idx, c):
    extra = nwin_ref[tile_idx * N_EXPERTS + c * CHUNK_E]
    for k in range(1, CHUNK_E):
        extra = jnp.maximum(extra, nwin_ref[tile_idx * N_EXPERTS + c * CHUNK_E + k])
    return extra


def _tile_windows(nwin_ref, tile_idx):
    extra = _chunk_windows(nwin_ref, tile_idx, 0)
    for c in range(1, N_CHUNKS):
        extra = jnp.maximum(extra, _chunk_windows(nwin_ref, tile_idx, c))
    return extra


def _dispatch_kernel(base_ref, nwin_ref, rstart_ref, rend_ref, xn_ref, rank_ref, xs_hbm,
                     stage0, stage1, ostage, zbuf, sem, osem, zsem):
    stage = (stage0, stage1)
    i = pl.program_id(0)
    last = pl.num_programs(0) - 1

    def window_copy(p, e, tile_idx):
        return pltpu.make_async_copy(stage[p].at[pl.ds(e * CAP, CAP)],
                                     _window(xs_hbm, base_ref, tile_idx * N_EXPERTS + e, 0), sem.at[p])

    def wait_tile(p, tile_idx):
        for e in range(N_EXPERTS):
            window_copy(p, e, tile_idx).wait()

    @pl.when(i == 0)
    def _zero_tails():
        zbuf[...] = jnp.zeros(zbuf.shape, BF16)
        copies = []
        for e in range(N_EXPERTS):
            end = rend_ref[e]
            a0 = pl.multiple_of(end - EXPERT_TILE, SEG_ALIGN)
            b0 = pl.multiple_of(jnp.maximum(end - EXPERT_TILE - CAP, rstart_ref[e]), SEG_ALIGN)
            copies.append(pltpu.make_async_copy(zbuf, xs_hbm.at[pl.ds(a0, EXPERT_TILE)], zsem))
            copies.append(pltpu.make_async_copy(zbuf.at[pl.ds(0, CAP)], xs_hbm.at[pl.ds(b0, CAP)], zsem))
        for cp in copies:
            cp.start()
        for cp in copies:
            cp.wait()

    for p in (0, 1):
        @pl.when(i % 2 == p)
        def _step(p=p):
            xb = xn_ref[...]
            rows = CHUNK_E * CAP
            for c in range(N_CHUNKS):
                stage[p][pl.ds(c * rows, rows), :] = _dot(_one_hot_rows(rank_ref, c, 0), xb).astype(BF16)

            @pl.when(i >= 1)
            def _():
                wait_tile(1 - p, i - 1)

            for e in range(N_EXPERTS):
                window_copy(p, e, i).start()

    extra = _tile_windows(nwin_ref, i)

    @pl.when(extra > 1)
    def _long_segments():
        def body(w, carry):
            for c in range(N_CHUNKS):
                @pl.when(w < _chunk_windows(nwin_ref, i, c))
                def _(c=c):
                    ostage[...] = _dot(_one_hot_rows(rank_ref, c, w * CAP), xn_ref[...]).astype(BF16)
                    for k in range(CHUNK_E):
                        idx = i * N_EXPERTS + c * CHUNK_E + k

                        @pl.when(w < nwin_ref[idx])
                        def _(k=k, idx=idx):
                            cp = pltpu.make_async_copy(ostage.at[pl.ds(k * CAP, CAP)],
                                                       _window(xs_hbm, base_ref, idx, w), osem)
                            cp.start()
                            cp.wait()
            return carry

        lax.fori_loop(1, extra, body, 0)

    for p in (0, 1):
        @pl.when((i == last) & (i % 2 == p))
        def _drain(p=p):
            wait_tile(p, i)


def _dispatch(xn, rank_t, base, nwin, row_start, row_end, *, n_rows, TM):
    N = xn.shape[0]
    stage = pltpu.VMEM((N_EXPERTS * CAP, D_MODEL), BF16)
    return pl.pallas_call(
        _dispatch_kernel,
        grid_spec=pltpu.PrefetchScalarGridSpec(
            num_scalar_prefetch=4,
            grid=(N // TM,),
            in_specs=[pl.BlockSpec((TM, D_MODEL), lambda i, *_: (i, 0)),
                      pl.BlockSpec((N_EXPERTS, TM), lambda i, *_: (0, i))],
            out_specs=pl.BlockSpec(memory_space=pl.ANY),
            scratch_shapes=[
                stage, stage,
                pltpu.VMEM((CHUNK_E * CAP, D_MODEL), BF16),
                pltpu.VMEM((EXPERT_TILE, D_MODEL), BF16),
                pltpu.SemaphoreType.DMA((2,)),
                pltpu.SemaphoreType.DMA(()),
                pltpu.SemaphoreType.DMA(()),
            ],
        ),
        out_shape=jax.ShapeDtypeStruct((n_rows, D_MODEL), BF16),
        compiler_params=pltpu.CompilerParams(
            dimension_semantics=("arbitrary",), vmem_limit_bytes=VMEM_LIMIT),
        name="moe_dispatch",
    )(base, nwin, row_start, row_end, xn, rank_t)


def _expert_kernel(tile_e_ref, n_tiles_ref, xs_ref, w1_ref, w3_ref, w2_ref, ys_ref, w1b, w3b, w2b):
    t = pl.program_id(0)
    valid = t < n_tiles_ref[0]
    prev_e = tile_e_ref[jnp.maximum(t - 1, 0)]

    @pl.when(valid & ((t == 0) | (tile_e_ref[t] != prev_e)))
    def _cast_weights():
        w1b[...] = w1_ref[0].astype(BF16)
        w3b[...] = w3_ref[0].astype(BF16)
        w2b[...] = w2_ref[0].astype(BF16)

    @pl.when(valid)
    def _compute():
        xb = xs_ref[...]
        a = _dot(xb, w1b[...])
        hb = (a * jax.nn.sigmoid(a)) * _dot(xb, w3b[...])
        ys_ref[...] = _dot(hb.astype(BF16), w2b[...]).astype(BF16)

    @pl.when(jnp.logical_not(valid))
    def _spare():
        ys_ref[...] = jnp.zeros(ys_ref.shape, BF16)


def _experts(xs, tile_e, n_tiles, w1, w3, w2):
    t_max = xs.shape[0] // EXPERT_TILE
    clamp = lambda t, nt: jnp.minimum(t, nt[0] - 1)
    wspec_in = pl.BlockSpec((1, D_MODEL, EXPERT_FF), lambda t, te, nt: (te[clamp(t, nt)], 0, 0))
    wspec_out = pl.BlockSpec((1, EXPERT_FF, D_MODEL), lambda t, te, nt: (te[clamp(t, nt)], 0, 0))
    return pl.pallas_call(
        _expert_kernel,
        grid_spec=pltpu.PrefetchScalarGridSpec(
            num_scalar_prefetch=2,
            grid=(t_max,),
            in_specs=[pl.BlockSpec((EXPERT_TILE, D_MODEL), lambda t, te, nt: (clamp(t, nt), 0)),
                      wspec_in, wspec_in, wspec_out],
            out_specs=pl.BlockSpec((EXPERT_TILE, D_MODEL),
                                   lambda t, te, nt: (jnp.where(t < nt[0], t, t_max), 0)),
            scratch_shapes=[
                pltpu.VMEM((D_MODEL, EXPERT_FF), BF16),
                pltpu.VMEM((D_MODEL, EXPERT_FF), BF16),
                pltpu.VMEM((EXPERT_FF, D_MODEL), BF16),
            ],
        ),
        out_shape=jax.ShapeDtypeStruct(((t_max + 1) * EXPERT_TILE, D_MODEL), BF16),
        compiler_params=pltpu.CompilerParams(
            dimension_semantics=("arbitrary",), vmem_limit_bytes=VMEM_LIMIT),
        name="moe_experts",
    )(tile_e, n_tiles, xs, w1, w3, w2)


def _final_kernel(base_ref, nwin_ref, x1s_ref, rank_ref, gates_ref, ys_hbm, pa_ref, pb_ref, gple_ref,
                  wgate_ref, wproj_ref, gfin_ref, ya_ref, yb_ref, win0, win1, owin, acc_ref, sem, osem,
                  *, split):
    i = pl.program_id(0)
    last = pl.num_programs(0) - 1
    win = (win0, win1)

    def window_copy(p, e, tile_idx):
        return pltpu.make_async_copy(_window(ys_hbm, base_ref, tile_idx * N_EXPERTS + e, 0),
                                     win[p].at[pl.ds(e * CAP, CAP)], sem.at[p])

    @pl.when(i == 0)
    def _first():
        owin[...] = jnp.zeros(owin.shape, BF16)
        for e in range(N_EXPERTS):
            window_copy(0, e, i).start()

    contract0 = (((0,), (0,)), ((), ()))

    for p in (0, 1):
        @pl.when(i % 2 == p)
        def _step(p=p):
            for e in range(N_EXPERTS):
                window_copy(p, e, i).wait()

            @pl.when(i < last)
            def _():
                for e in range(N_EXPERTS):
                    window_copy(1 - p, e, i + 1).start()

            one_hot = jnp.concatenate([_one_hot_rows(rank_ref, c, 0, gates_ref) for c in range(N_CHUNKS)], axis=0)
            acc_ref[...] = lax.dot_general(one_hot, win[p][...], contract0, preferred_element_type=F32)

    extra = _tile_windows(nwin_ref, i)

    @pl.when(extra > 1)
    def _long_segments():
        def body(w, carry):
            for c in range(N_CHUNKS):
                @pl.when(w < _chunk_windows(nwin_ref, i, c))
                def _(c=c):
                    for k in range(CHUNK_E):
                        idx = i * N_EXPERTS + c * CHUNK_E + k

                        @pl.when(w < nwin_ref[idx])
                        def _(k=k, idx=idx):
                            cp = pltpu.make_async_copy(_window(ys_hbm, base_ref, idx, w),
                                                       owin.at[pl.ds(k * CAP, CAP)], osem)
                            cp.start()
                            cp.wait()
                    acc_ref[...] += lax.dot_general(_one_hot_rows(rank_ref, c, w * CAP, gates_ref), owin[...],
                                                    contract0, preferred_element_type=F32)
            return carry

        lax.fori_loop(1, extra, body, 0)

    x2 = x1s_ref[...] + acc_ref[...]
    r_ = _rms(x2, gple_ref[...])
    gate = jax.nn.sigmoid(_dot(r_.astype(BF16), wgate_ref[...]))
    pt = jnp.where(i < split, pa_ref[...], pb_ref[...])
    x3 = x2 + _dot(pt.astype(BF16), wproj_ref[...]) * gate
    y = _rms(x3, gfin_ref[...])

    @pl.when(i < split)
    def _():
        ya_ref[...] = y

    @pl.when(i >= split)
    def _():
        yb_ref[...] = y


def _final(x1s, rank_t, gates_t, ys, pa, pb, gple, wgate, wproj, gfin, base, nwin, *, TM):
    N = x1s.shape[0]
    split = pa.shape[0] // TM
    tok = pl.BlockSpec((TM, D_MODEL), lambda i, *_: (i, 0))
    per_e = pl.BlockSpec((N_EXPERTS, TM), lambda i, *_: (0, i))
    consts = (gple, wgate, wproj, gfin)
    win = pltpu.VMEM((N_EXPERTS * CAP, D_MODEL), BF16)
    return pl.pallas_call(
        functools.partial(_final_kernel, split=split),
        grid_spec=pltpu.PrefetchScalarGridSpec(
            num_scalar_prefetch=2,
            grid=(N // TM,),
            in_specs=[tok, per_e, per_e, pl.BlockSpec(memory_space=pl.ANY)]
            + list(_group_specs((TM, PLE_DIM), split))
            + [pl.BlockSpec(c.shape, lambda i, *_: (0, 0)) for c in consts],
            out_specs=_group_specs((TM, D_MODEL), split),
            scratch_shapes=[win, win, pltpu.VMEM((CHUNK_E * CAP, D_MODEL), BF16), pltpu.VMEM((TM, D_MODEL), F32),
                            pltpu.SemaphoreType.DMA((2,)), pltpu.SemaphoreType.DMA(())],
        ),
        out_shape=(jax.ShapeDtypeStruct((pa.shape[0], D_MODEL), F32),
                   jax.ShapeDtypeStruct((pb.shape[0], D_MODEL), F32)),
        compiler_params=pltpu.CompilerParams(
            dimension_semantics=("arbitrary",), vmem_limit_bytes=VMEM_LIMIT),
        name="combine_ple_final",
    )(base, nwin, x1s, rank_t, gates_t, ys, pa, pb, *consts)


def kernel(x_prompt, x_sample, p_prompt, p_sample, state_C, state_n, state_m, state_pool, norm_mix_g, w_in, b_igate, b_fgate, head_norm_g, w_pool, pool_scale, w_out, norm_ffn_g, router_w, router_bias, ex_w1, ex_w3, ex_w2, sh_w1, sh_w3, sh_w2, norm_ple_g, w_ple_gate, w_ple_proj, final_norm_g):
    depth = norm_mix_g.shape[0]
    assert depth == 1
    l = 0
    B, T, _ = x_prompt.shape
    Bs, Ts, _ = x_sample.shape
    g0 = 4 * MLSTM_WIDTH
    w = w_in[l]
    lane_pad = jnp.zeros((D_MODEL, 128 - HEADS), F32)
    wcat = jnp.concatenate(
        [w[:, :g0], w[:, g0 + 2 * HEADS:], w[:, g0:g0 + HEADS], lane_pad,
         w[:, g0 + HEADS:g0 + 2 * HEADS], lane_pad], axis=1).astype(BF16)
    bias_pad = jnp.zeros((128 - HEADS,), F32)
    gbias = jnp.concatenate([b_igate[l], bias_pad, b_fgate[l], bias_pad])[None, :]
    mixer_w = (wcat, norm_mix_g[l][None, :], gbias, head_norm_g[l][None, :], w_pool[l].astype(BF16),
               pool_scale[l][None, :], w_out[l].astype(BF16))

    x1p, Cp, Np, Mp, Bp = _mixer(x_prompt, *mixer_w, None, BB=1, TT=512, L=128, start_pos=0)
    state = (state_C[l], state_n[l], state_m[l][..., None], state_pool[l])
    x1s_, Cs, Ns, Ms, Bs_ = _mixer(x_sample, *mixer_w, state, BB=16, TT=Ts, L=Ts, start_pos=PAST_LEN)

    N = B * T + Bs * Ts
    assert (B * T) % ROUTE_TILE == 0 and (Bs * Ts) % ROUTE_TILE == 0
    n_route_tiles = N // ROUTE_TILE

    xn, rank_t, gates_t, cnt, x1sh = _router(
        x1p.reshape(B * T, D_MODEL), x1s_.reshape(Bs * Ts, D_MODEL),
        norm_ffn_g[l][None, :], router_w[l].T, router_bias[l][:, None],
        sh_w1[l].astype(BF16), sh_w3[l].astype(BF16), sh_w2[l].astype(BF16), TM=ROUTE_TILE)

    max_rows = TOP_K * N + (SEG_ALIGN - 1) * n_route_tiles * N_EXPERTS + N_EXPERTS * (CAP + EXPERT_TILE)
    t_max = -(-max_rows // EXPERT_TILE)
    cnt = cnt[:, 0].reshape(n_route_tiles, N_EXPERTS).astype(jnp.int32)
    base, nwin, row_start, row_end, tile_e, n_tiles = _segment_plan(cnt, t_max)

    xs = _dispatch(xn, rank_t, base, nwin, row_start, row_end, n_rows=t_max * EXPERT_TILE, TM=ROUTE_TILE)
    ys = _experts(xs, tile_e, n_tiles, ex_w1[l], ex_w3[l], ex_w2[l])
    y_prompt, y_sample = _final(
        x1sh, rank_t, gates_t, ys, p_prompt[l].reshape(B * T, PLE_DIM), p_sample[l].reshape(Bs * Ts, PLE_DIM),
        norm_ple_g[l][None, :], w_ple_gate[l].astype(BF16), w_ple_proj[l].astype(BF16),
        final_norm_g[None, :], base, nwin, TM=ROUTE_TILE)
    return (y_prompt.reshape(B, T, D_MODEL), y_sample.reshape(Bs, Ts, D_MODEL),
            Cp[None], Np[None], Mp[..., 0][None], Bp[None],
            Cs[None], Ns[None], Ms[..., 0][None], Bs_[None])
```

```python
import functools

import jax
import jax.numpy as jnp
from jax import lax
from jax.experimental import pallas as pl
from jax.experimental.pallas import tpu as pltpu

D_MODEL = 1024
HEADS = 4
DH = 128
MLSTM_WIDTH = HEADS * DH
POOL_WIDTH = 512
POOL_WINDOWS = (2, 4, 8, 16)
POOL_GDIM = 128
POOL_BUF = 15
POOL_PAD = 16
N_EXPERTS = 64
TOP_K = 8
N_GROUPS = 8
GROUP_SIZE = N_EXPERTS // N_GROUPS
TOPK_GROUPS = 4
EXPERT_FF = 256
ROUTED_SCALE = 2.5
NORM_EPS = 1e-6
PLE_DIM = 256
PAST_LEN = 16384

COL_Q, COL_K, COL_V, COL_O, COL_U, COL_I, COL_F = 0, 512, 1024, 1536, 2048, 2560, 2688
IN_COLS_PAD = 2816
GATE_COLS = HEADS * 128

ROUTE_TILE = 256
EXPERT_TILE = 1024
SEG_ALIGN = 16
CAP = 48
CHUNK_E = 8
N_CHUNKS = N_EXPERTS // CHUNK_E

VMEM_LIMIT = 56 * 1024 * 1024
F32 = jnp.float32
BF16 = jnp.bfloat16
NEG_INF = float("-inf")


def _rms(x, g):
    return x * lax.rsqrt(jnp.mean(x * x, axis=-1, keepdims=True) + NORM_EPS) * g


def _log_sigmoid(x):
    return jnp.minimum(x, 0.0) - jnp.log1p(jnp.exp(-jnp.abs(x)))


def _dot(a, b):
    return jnp.dot(a, b, preferred_element_type=F32)


def _mlstm_chunk(q, k, v, b_col, i_col, C, n, m, causal, eye):
    L = q.shape[0]
    r_col = i_col - b_col
    r_row = jnp.sum(jnp.where(eye, r_col, 0.0), axis=0, keepdims=True)
    d = jnp.where(causal, b_col + r_row, NEG_INF)
    inter = b_col + m
    qb, kb, vb = q.astype(BF16), k.astype(BF16), v.astype(BF16)
    qk = lax.dot_general(qb, kb, (((1,), (1,)), ((), ())), preferred_element_type=F32)
    qc = _dot(qb, C.astype(BF16))
    qn = jnp.sum(q * n, axis=-1, keepdims=True)
    yield
    m_t = jnp.maximum(inter, jnp.max(d, axis=-1, keepdims=True))
    b_last = b_col[L - 1:L, :]
    m_new = jnp.maximum(b_last + m, jnp.max(b_last + r_row, axis=-1, keepdims=True))
    yield
    w_inter = jnp.exp(inter - m_t)
    s = qk * jnp.exp(d - m_t)
    fw = jnp.exp(b_last + m - m_new)
    iw_col = jnp.exp(b_last + r_col - m_new)
    kw = iw_col * k
    yield
    sv = _dot(s.astype(BF16), vb)
    kv = lax.dot_general(kw.astype(BF16), vb, (((0,), (0,)), ((), ())), preferred_element_type=F32)
    ssum = jnp.sum(s, axis=-1, keepdims=True)
    n_new = fw * n + jnp.sum(kw, axis=0, keepdims=True)
    yield
    num = w_inter * qc + sv
    nq = w_inter * qn + ssum
    h = num / jnp.maximum(jnp.abs(nq), jnp.exp(-m_t))
    C_new = fw * C + kv
    return h, C_new, n_new, m_new


def _interleave(gens):
    results = [None] * len(gens)
    live = list(enumerate(gens))
    while live:
        still = []
        for idx, g in live:
            try:
                next(g)
                still.append((idx, g))
            except StopIteration as stop:
                results[idx] = stop.value
        live = still
    return results


def _mlstm_chunk_square(q, k, v, r, b, mp, CN, m, causal, eye, ones):
    L = q.shape[0]
    r_row = jnp.sum(jnp.where(eye, r, 0.0), axis=0, keepdims=True)
    g = jnp.maximum(mp, m)
    qb, kb = q.astype(BF16), k.astype(BF16)
    v1 = jnp.concatenate([v.astype(BF16), ones], axis=1)
    qk = lax.dot_general(qb, kb, (((1,), (1,)), ((), ())), preferred_element_type=F32)
    qcn = _dot(qb, CN.astype(BF16))
    yield
    w_inter = jnp.exp(m - g)
    p = jnp.where(causal, jnp.exp(r_row - g), 0.0)
    g_last = g[L - 1:L, :]
    m_new = b[L - 1:L, :] + g_last
    fw = jnp.exp(m - g_last)
    kw = (jnp.exp(r - g_last) * k).astype(BF16)
    floor = jnp.exp(-(b + g))
    yield
    s = (qk * p).astype(BF16)
    sv = _dot(s, v1)
    kv = lax.dot_general(kw, v1, (((0,), (0,)), ((), ())), preferred_element_type=F32)
    yield
    num = w_inter * qcn[:, :DH] + sv[:, :DH]
    nq = w_inter * qcn[:, DH:] + sv[:, DH:]
    h = num / jnp.maximum(jnp.abs(nq), floor)
    CN_new = jnp.concatenate([fw, fw], axis=1) * CN + kv
    return h, CN_new, m_new[:, 0:1]


def _chunk_scan(x, L, op, fill):
    pos = lax.broadcasted_iota(jnp.int32, x.shape, 0) & (L - 1)
    k = 1
    while k < L:
        x = op(x, jnp.where(pos >= k, pltpu.roll(x, k, axis=0), fill))
        k *= 2
    return x


def _mixer_kernel(*refs, BB, TT, L, start_pos, zero_state):
    if zero_state:
        (x_ref, wcat_ref, gmix_ref, gbias_ref, hng_ref, wpool_ref, pscale_ref, wout_ref,
         x1_ref, c_ref, n_ref, m_ref, buf_ref,
         z_ref, a_ref, b_ref, mp_ref, cn_ref, mix_ref, ext_ref) = refs
    else:
        (x_ref, wcat_ref, gmix_ref, gbias_ref, hng_ref, wpool_ref, pscale_ref, wout_ref,
         c0_ref, n0_ref, m0_ref, buf0_ref,
         x1_ref, c_ref, n_ref, m_ref, buf_ref,
         z_ref, a_ref, b_ref, mp_ref, cn_ref, mix_ref, ext_ref) = refs
    t = pl.program_id(1)
    R = BB * TT
    n_chunks = TT // L
    square = L == DH
    assert not square or (zero_state and BB == 1)

    @pl.when(t == 0)
    def _init():
        ext_ref[:, 0:POOL_PAD, :] = jnp.zeros((BB, POOL_PAD, POOL_WIDTH), F32)
        if zero_state:
            cn_ref[...] = jnp.zeros(cn_ref.shape, F32)
            c_ref[...] = jnp.zeros(c_ref.shape, F32)
            n_ref[...] = jnp.zeros(n_ref.shape, F32)
            m_ref[...] = jnp.zeros(m_ref.shape, F32)
        else:
            c_ref[...] = c0_ref[...]
            n_ref[...] = n0_ref[...]
            m_ref[...] = m0_ref[...]
            ext_ref[:, 1:POOL_PAD, :] = buf0_ref[...]

    x = x_ref[...].reshape(R, D_MODEL)
    hn = _rms(x, gmix_ref[...])
    z_ref[...] = _dot(hn.astype(BF16), wcat_ref[...])
    gbias = gbias_ref[...]
    gi = z_ref[:, COL_I:COL_I + 128] + gbias[:, :128]
    b_c = _chunk_scan(_log_sigmoid(z_ref[:, COL_F:COL_F + 128] + gbias[:, 128:]), L, jnp.add, 0.0)
    if square:
        r_c = gi - b_c
        mp_c = _chunk_scan(r_c, L, jnp.maximum, NEG_INF)
        for hd in range(HEADS):
            lanes = slice(hd * DH, (hd + 1) * DH)
            a_ref[:, lanes] = jnp.broadcast_to(r_c[:, hd:hd + 1], (R, DH))
            b_ref[:, lanes] = jnp.broadcast_to(b_c[:, hd:hd + 1], (R, DH))
            mp_ref[:, lanes] = jnp.broadcast_to(mp_c[:, hd:hd + 1], (R, DH))
    else:
        a_ref[:, 0:128] = gi
        b_ref[:, 0:128] = b_c

    row = lax.broadcasted_iota(jnp.int32, (L, L), 0)
    col = lax.broadcasted_iota(jnp.int32, (L, L), 1)
    causal = row >= col
    eye = row == col
    ones = jnp.ones((DH, DH), BF16)
    hng = hng_ref[...]

    group = 4 if (not square and n_chunks == 1 and BB % 4 == 0) else 1

    def chunk_body(s, carry):
        work = []
        for j in range(group):
            sj = s * group + j
            rows = pl.ds(pl.multiple_of(sj * L, L), L)
            bb = 0 if BB == 1 else sj // n_chunks
            for hd in range(HEADS):
                lanes = slice(hd * DH, (hd + 1) * DH)
                q = z_ref[rows, COL_Q + hd * DH:COL_Q + (hd + 1) * DH] * (DH ** -0.5)
                k = z_ref[rows, COL_K + hd * DH:COL_K + (hd + 1) * DH]
                v = z_ref[rows, COL_V + hd * DH:COL_V + (hd + 1) * DH]
                o = z_ref[rows, COL_O + hd * DH:COL_O + (hd + 1) * DH]
                m = m_ref[bb, hd:hd + 1, :]
                if square:
                    state = (a_ref[rows, lanes], b_ref[rows, lanes], mp_ref[rows, lanes], cn_ref[hd])
                else:
                    state = (b_ref[rows, hd:hd + 1], a_ref[rows, hd:hd + 1], c_ref[bb, hd],
                             n_ref[bb, hd:hd + 1, :])
                work.append((rows, bb, hd, lanes, q, k, v, o, m, state))
        if square:
            results = _interleave([_mlstm_chunk_square(q, k, v, *state, m, causal, eye, ones)
                                   for _, _, _, _, q, k, v, _, m, state in work])
        else:
            results = _interleave([_mlstm_chunk(q, k, v, *state, m, causal, eye)
                                   for _, _, _, _, q, k, v, _, m, state in work])
        scales = [lax.rsqrt(jnp.mean(res[0] * res[0], axis=-1, keepdims=True) + NORM_EPS) for res in results]
        done = []
        for (rows, bb, hd, lanes, _, _, _, o, _, _), res, scale in zip(work, results, scales):
            out = res[0] * scale * hng[:, lanes] * jax.nn.sigmoid(o)
            done.append((rows, bb, hd, lanes, out, res[-1], res[1:-1]))
        for rows, bb, hd, lanes, out, m_new, new_state in done:
            mix_ref[rows, lanes] = out
            m_ref[bb, hd:hd + 1, :] = m_new
            if square:
                cn_ref[hd] = new_state[0]
            else:
                c_ref[bb, hd] = new_state[0]
                n_ref[bb, hd:hd + 1, :] = new_state[1]
        return carry

    lax.fori_loop(0, BB * n_chunks // group, chunk_body, 0)

    if square:
        @pl.when(t == pl.num_programs(1) - 1)
        def _emit_state():
            for hd in range(HEADS):
                cn = cn_ref[hd]
                c_ref[0, hd] = cn[:, :DH]
                n_ref[0, hd:hd + 1, :] = jnp.sum(jnp.where(eye, cn[:, DH:], 0.0), axis=0, keepdims=True)

    ext_ref[:, POOL_PAD:POOL_PAD + TT, :] = z_ref[:, COL_U:COL_U + POOL_WIDTH].reshape(BB, TT, POOL_WIDTH)
    pos = start_pos + t * TT + lax.broadcasted_iota(jnp.int32, (1, TT, 1), 1)
    pscale = pscale_ref[...]
    for gidx, w in enumerate(POOL_WINDOWS):
        lanes = slice(gidx * POOL_GDIM, (gidx + 1) * POOL_GDIM)
        u_g = ext_ref[:, POOL_PAD:POOL_PAD + TT, lanes]
        acc = u_g
        for j in range(1, w):
            acc = acc + ext_ref[:, POOL_PAD - j:POOL_PAD - j + TT, lanes]
        cnt = jnp.minimum(pos + 1, w).astype(F32)
        pooled = (acc / cnt - u_g).reshape(R, POOL_GDIM)
        mixed = _dot(pooled.astype(BF16), wpool_ref[gidx]) * pscale[:, lanes]
        mix_ref[:, MLSTM_WIDTH + gidx * POOL_GDIM:MLSTM_WIDTH + (gidx + 1) * POOL_GDIM] = mixed
    new_buf = ext_ref[:, TT + 1:TT + POOL_PAD, :]
    buf_ref[...] = new_buf
    ext_ref[:, 1:POOL_PAD, :] = new_buf

    out = x_ref[...].reshape(R, D_MODEL) + _dot(mix_ref[...].astype(BF16), wout_ref[...])
    x1_ref[...] = out.reshape(BB, TT, D_MODEL)


def _const_spec(shape):
    nd = len(shape)
    return pl.BlockSpec(shape, lambda b, t, _nd=nd: (0,) * _nd, pipeline_mode=pl.Buffered(1))


def _mixer(x, wcat, gmix, gbias, hng, wpool, pscale, wout, state, *, BB, TT, L, start_pos):
    B, T, _ = x.shape
    zero_state = state is None
    R = BB * TT
    grid = (B // BB, T // TT)
    x_spec = pl.BlockSpec((BB, TT, D_MODEL), lambda b, t: (b, t, 0))
    c_spec = pl.BlockSpec((BB, HEADS, DH, DH), lambda b, t: (b, 0, 0, 0))
    n_spec = pl.BlockSpec((BB, HEADS, DH), lambda b, t: (b, 0, 0))
    m_spec = pl.BlockSpec((BB, HEADS, 1), lambda b, t: (b, 0, 0))
    buf_spec = pl.BlockSpec((BB, POOL_BUF, POOL_WIDTH), lambda b, t: (b, 0, 0))
    weights = (wcat, gmix, gbias, hng, wpool, pscale, wout)
    in_specs = [x_spec] + [_const_spec(w.shape) for w in weights]
    args = [x, *weights]
    if not zero_state:
        in_specs += [c_spec, n_spec, m_spec, buf_spec]
        args += list(state)
    out_shape = (
        jax.ShapeDtypeStruct((B, T, D_MODEL), F32),
        jax.ShapeDtypeStruct((B, HEADS, DH, DH), F32),
        jax.ShapeDtypeStruct((B, HEADS, DH), F32),
        jax.ShapeDtypeStruct((B, HEADS, 1), F32),
        jax.ShapeDtypeStruct((B, POOL_BUF, POOL_WIDTH), F32),
    )
    kern = functools.partial(_mixer_kernel, BB=BB, TT=TT, L=L, start_pos=start_pos, zero_state=zero_state)
    return pl.pallas_call(
        kern,
        grid=grid,
        in_specs=in_specs,
        out_specs=(x_spec, c_spec, n_spec, m_spec, buf_spec),
        out_shape=out_shape,
        scratch_shapes=[
            pltpu.VMEM((R, IN_COLS_PAD), F32),
            pltpu.VMEM((R, GATE_COLS), F32),
            pltpu.VMEM((R, GATE_COLS), F32),
            pltpu.VMEM((R, GATE_COLS), F32),
            pltpu.VMEM((HEADS, DH, 2 * DH), F32),
            pltpu.VMEM((R, D_MODEL), F32),
            pltpu.VMEM((BB, POOL_PAD + TT, POOL_WIDTH), F32),
        ],
        compiler_params=pltpu.CompilerParams(
            dimension_semantics=("arbitrary", "arbitrary"), vmem_limit_bytes=VMEM_LIMIT),
        name="mixer_zero" if zero_state else "mixer_state",
    )(*args)


def _first_index_of_max(work, idx, n, axis):
    mx = jnp.max(work, axis=axis, keepdims=True)
    return jnp.min(jnp.where(work == mx, idx, float(n)), axis=axis, keepdims=True)


def _route(scores_t, bias_t):
    tm = scores_t.shape[1]
    biased = scores_t + bias_t
    b3 = biased.reshape(N_GROUPS, GROUP_SIZE, tm)
    sub = lax.broadcasted_iota(jnp.int32, b3.shape, 1).astype(F32)
    m1 = jnp.max(b3, axis=1, keepdims=True)
    first = jnp.min(jnp.where(b3 == m1, sub, float(GROUP_SIZE)), axis=1, keepdims=True)
    m2 = jnp.max(jnp.where(sub == first, NEG_INF, b3), axis=1, keepdims=True)
    gs = (m1 + m2).reshape(N_GROUPS, tm)
    gidx = lax.broadcasted_iota(jnp.int32, gs.shape, 0).astype(F32)
    gsel = jnp.zeros(gs.shape, F32)
    work = gs
    for _ in range(TOPK_GROUPS):
        pick = gidx == _first_index_of_max(work, gidx, N_GROUPS, 0)
        gsel = jnp.where(pick, 1.0, gsel)
        work = jnp.where(pick, NEG_INF, work)
    emask = jnp.broadcast_to(gsel.reshape(N_GROUPS, 1, tm), b3.shape).reshape(N_EXPERTS, tm)
    work = jnp.where(emask > 0, biased, NEG_INF)
    eidx = lax.broadcasted_iota(jnp.int32, work.shape, 0).astype(F32)
    mask = jnp.zeros(work.shape, F32)
    for _ in range(TOP_K):
        pick = eidx == _first_index_of_max(work, eidx, N_EXPERTS, 0)
        mask = jnp.where(pick, 1.0, mask)
        work = jnp.where(pick, NEG_INF, work)
    sel = mask * scores_t
    return mask, sel / jnp.sum(sel, axis=0, keepdims=True) * ROUTED_SCALE


def _group_specs(block, split):
    return (pl.BlockSpec(block, lambda i, *_: (jnp.minimum(i, split - 1), 0)),
            pl.BlockSpec(block, lambda i, *_: (jnp.maximum(i - split, 0), 0)))


def _router_kernel(x1a_ref, x1b_ref, gffn_ref, rwt_ref, rbias_ref, w1_ref, w3_ref, w2_ref,
                   xn_ref, rank_ref, gates_ref, cnt_ref, x1s_ref, *, split):
    x1 = jnp.where(pl.program_id(0) < split, x1a_ref[...], x1b_ref[...])
    tm = x1.shape[0]
    xn = _rms(x1, gffn_ref[...])
    xb = xn.astype(BF16)
    xn_ref[...] = xb
    logits_t = lax.dot_general(rwt_ref[...], xn, (((1,), (1,)), ((), ())),
                               preferred_element_type=F32, precision=lax.Precision.HIGHEST)
    mask, gates = _route(jax.nn.sigmoid(logits_t), rbias_ref[...])
    gates_ref[...] = gates
    rt = ROUTE_TILE
    before = (lax.broadcasted_iota(jnp.int32, (rt, rt), 0)
              < lax.broadcasted_iota(jnp.int32, (rt, rt), 1)).astype(BF16)
    for j in range(tm // rt):
        mj = mask[:, j * rt:(j + 1) * rt]
        rank_ref[:, j * rt:(j + 1) * rt] = jnp.where(mj > 0, _dot(mj.astype(BF16), before), -1.0)
        cnt_ref[j * N_EXPERTS:(j + 1) * N_EXPERTS, :] = jnp.broadcast_to(
            jnp.sum(mj, axis=1, keepdims=True), (N_EXPERTS, 128))
    a = _dot(xb, w1_ref[...])
    hsh = (a * jax.nn.sigmoid(a)) * _dot(xb, w3_ref[...])
    x1s_ref[...] = x1 + _dot(hsh.astype(BF16), w2_ref[...])


def _router(x1a, x1b, gffn, rwt, rbias, w1, w3, w2, *, TM):
    N = x1a.shape[0] + x1b.shape[0]
    split = x1a.shape[0] // TM
    tok = pl.BlockSpec((TM, D_MODEL), lambda i: (i, 0))
    per_e = pl.BlockSpec((N_EXPERTS, TM), lambda i: (0, i))
    consts = (gffn, rwt, rbias, w1, w3, w2)
    return pl.pallas_call(
        functools.partial(_router_kernel, split=split),
        grid=(N // TM,),
        in_specs=list(_group_specs((TM, D_MODEL), split)) + [pl.BlockSpec(c.shape, lambda i: (0, 0)) for c in consts],
        out_specs=(tok, per_e, per_e, pl.BlockSpec((TM // ROUTE_TILE * N_EXPERTS, 128), lambda i: (i, 0)), tok),
        out_shape=(jax.ShapeDtypeStruct((N, D_MODEL), BF16),
                   jax.ShapeDtypeStruct((N_EXPERTS, N), F32),
                   jax.ShapeDtypeStruct((N_EXPERTS, N), F32),
                   jax.ShapeDtypeStruct((N // ROUTE_TILE * N_EXPERTS, 128), F32),
                   jax.ShapeDtypeStruct((N, D_MODEL), F32)),
        compiler_params=pltpu.CompilerParams(
            dimension_semantics=("arbitrary",), vmem_limit_bytes=VMEM_LIMIT),
        name="router_shared",
    )(x1a, x1b, *consts)


def _segment_plan(cnt, n_tiles_max):
    seg = (cnt + SEG_ALIGN - 1) // SEG_ALIGN * SEG_ALIGN
    used = jnp.sum(seg, axis=0)
    size = (used + CAP + EXPERT_TILE - 1) // EXPERT_TILE * EXPERT_TILE
    row_end = jnp.cumsum(size)
    row_start = row_end - size
    base = row_start[None, :] + jnp.cumsum(seg, axis=0) - seg
    nwin = jnp.maximum((seg + CAP - 1) // CAP, 1)
    tile_end = row_end // EXPERT_TILE
    n_tiles = tile_end[-1]
    t_ids = jnp.arange(n_tiles_max, dtype=jnp.int32)
    tile_e = jnp.minimum(jnp.sum(t_ids[:, None] >= tile_end[None, :], axis=1), N_EXPERTS - 1)
    i32 = lambda a: a.astype(jnp.int32)
    return (i32(base).reshape(-1), i32(nwin).reshape(-1), i32(row_start), i32(row_end),
            i32(tile_e), i32(n_tiles).reshape(1))


def _one_hot_rows(rank_ref, chunk, first_row, values_ref=None):
    tm = rank_ref.shape[1]
    j = (lax.broadcasted_iota(jnp.int32, (CAP, tm), 0) + first_row).astype(F32)
    rows = []
    for k in range(CHUNK_E):
        e = chunk * CHUNK_E + k
        hit = j == rank_ref[e:e + 1, :]
        val = 1.0 if values_ref is None else values_ref[e:e + 1, :]
        rows.append(jnp.where(hit, val, 0.0).astype(BF16))
    return jnp.concatenate(rows, axis=0)


def _window(hbm, base_ref, idx, w):
    start = pl.multiple_of(base_ref[idx] + w * CAP, SEG_ALIGN)
    return hbm.at[pl.ds(start, CAP)]


def _chunk_windows(nwin_ref, tile_idx, c):
    extra = nwin_ref[tile_idx * N_EXPERTS + c * CHUNK_E]
    for k in range(1, CHUNK_E):
        extra = jnp.maximum(extra, nwin_ref[tile_idx * N_EXPERTS + c * CHUNK_E + k])
    return extra


def _tile_windows(nwin_ref, tile_idx):
    extra = _chunk_windows(nwin_ref, tile_idx, 0)
    for c in range(1, N_CHUNKS):
        extra = jnp.maximum(extra, _chunk_windows(nwin_ref, tile_idx, c))
    return extra


def _dispatch_kernel(base_ref, nwin_ref, rstart_ref, rend_ref, xn_ref, rank_ref, xs_hbm,
                     stage0, stage1, ostage, zbuf, sem, osem, zsem):
    stage = (stage0, stage1)
    i = pl.program_id(0)
    last = pl.num_programs(0) - 1

    def window_copy(p, e, tile_idx):
        return pltpu.make_async_copy(stage[p].at[pl.ds(e * CAP, CAP)],
                                     _window(xs_hbm, base_ref, tile_idx * N_EXPERTS + e, 0), sem.at[p])

    def wait_tile(p, tile_idx):
        for e in range(N_EXPERTS):
            window_copy(p, e, tile_idx).wait()

    @pl.when(i == 0)
    def _zero_tails():
        zbuf[...] = jnp.zeros(zbuf.shape, BF16)
        copies = []
        for e in range(N_EXPERTS):
            end = rend_ref[e]
            a0 = pl.multiple_of(end - EXPERT_TILE, SEG_ALIGN)
            b0 = pl.multiple_of(jnp.maximum(end - EXPERT_TILE - CAP, rstart_ref[e]), SEG_ALIGN)
            copies.append(pltpu.make_async_copy(zbuf, xs_hbm.at[pl.ds(a0, EXPERT_TILE)], zsem))
            copies.append(pltpu.make_async_copy(zbuf.at[pl.ds(0, CAP)], xs_hbm.at[pl.ds(b0, CAP)], zsem))
        for cp in copies:
            cp.start()
        for cp in copies:
            cp.wait()

    for p in (0, 1):
        @pl.when(i % 2 == p)
        def _step(p=p):
            xb = xn_ref[...]
            rows = CHUNK_E * CAP
            for c in range(N_CHUNKS):
                stage[p][pl.ds(c * rows, rows), :] = _dot(_one_hot_rows(rank_ref, c, 0), xb).astype(BF16)

            @pl.when(i >= 1)
            def _():
                wait_tile(1 - p, i - 1)

            for e in range(N_EXPERTS):
                window_copy(p, e, i).start()

    extra = _tile_windows(nwin_ref, i)

    @pl.when(extra > 1)
    def _long_segments():
        def body(w, carry):
            for c in range(N_CHUNKS):
                @pl.when(w < _chunk_windows(nwin_ref, i, c))
                def _(c=c):
                    ostage[...] = _dot(_one_hot_rows(rank_ref, c, w * CAP), xn_ref[...]).astype(BF16)
                    for k in range(CHUNK_E):
                        idx = i * N_EXPERTS + c * CHUNK_E + k

                        @pl.when(w < nwin_ref[idx])
                        def _(k=k, idx=idx):
                            cp = pltpu.make_async_copy(ostage.at[pl.ds(k * CAP, CAP)],
                                                       _window(xs_hbm, base_ref, idx, w), osem)
                            cp.start()
                            cp.wait()
            return carry

        lax.fori_loop(1, extra, body, 0)

    for p in (0, 1):
        @pl.when((i == last) & (i % 2 == p))
        def _drain(p=p):
            wait_tile(p, i)


def _dispatch(xn, rank_t, base, nwin, row_start, row_end, *, n_rows, TM):
    N = xn.shape[0]
    stage = pltpu.VMEM((N_EXPERTS * CAP, D_MODEL), BF16)
    return pl.pallas_call(
        _dispatch_kernel,
        grid_spec=pltpu.PrefetchScalarGridSpec(
            num_scalar_prefetch=4,
            grid=(N // TM,),
            in_specs=[pl.BlockSpec((TM, D_MODEL), lambda i, *_: (i, 0)),
                      pl.BlockSpec((N_EXPERTS, TM), lambda i, *_: (0, i))],
            out_specs=pl.BlockSpec(memory_space=pl.ANY),
            scratch_shapes=[
                stage, stage,
                pltpu.VMEM((CHUNK_E * CAP, D_MODEL), BF16),
                pltpu.VMEM((EXPERT_TILE, D_MODEL), BF16),
                pltpu.SemaphoreType.DMA((2,)),
                pltpu.SemaphoreType.DMA(()),
                pltpu.SemaphoreType.DMA(()),
            ],
        ),
        out_shape=jax.ShapeDtypeStruct((n_rows, D_MODEL), BF16),
        compiler_params=pltpu.CompilerParams(
            dimension_semantics=("arbitrary",), vmem_limit_bytes=VMEM_LIMIT),
        name="moe_dispatch",
    )(base, nwin, row_start, row_end, xn, rank_t)


def _expert_kernel(tile_e_ref, n_tiles_ref, xs_ref, w1_ref, w3_ref, w2_ref, ys_ref, w1b, w3b, w2b):
    t = pl.program_id(0)
    valid = t < n_tiles_ref[0]
    prev_e = tile_e_ref[jnp.maximum(t - 1, 0)]

    @pl.when(valid & ((t == 0) | (tile_e_ref[t] != prev_e)))
    def _cast_weights():
        w1b[...] = w1_ref[0].astype(BF16)
        w3b[...] = w3_ref[0].astype(BF16)
        w2b[...] = w2_ref[0].astype(BF16)

    @pl.when(valid)
    def _compute():
        xb = xs_ref[...]
        a = _dot(xb, w1b[...])
        hb = (a * jax.nn.sigmoid(a)) * _dot(xb, w3b[...])
        ys_ref[...] = _dot(hb.astype(BF16), w2b[...]).astype(BF16)

    @pl.when(jnp.logical_not(valid))
    def _spare():
        ys_ref[...] = jnp.zeros(ys_ref.shape, BF16)


def _experts(xs, tile_e, n_tiles, w1, w3, w2):
    t_max = xs.shape[0] // EXPERT_TILE
    clamp = lambda t, nt: jnp.minimum(t, nt[0] - 1)
    wspec_in = pl.BlockSpec((1, D_MODEL, EXPERT_FF), lambda t, te, nt: (te[clamp(t, nt)], 0, 0))
    wspec_out = pl.BlockSpec((1, EXPERT_FF, D_MODEL), lambda t, te, nt: (te[clamp(t, nt)], 0, 0))
    return pl.pallas_call(
        _expert_kernel,
        grid_spec=pltpu.PrefetchScalarGridSpec(
            num_scalar_prefetch=2,
            grid=(t_max,),
            in_specs=[pl.BlockSpec((EXPERT_TILE, D_MODEL), lambda t, te, nt: (clamp(t, nt), 0)),
                      wspec_in, wspec_in, wspec_out],
            out_specs=pl.BlockSpec((EXPERT_TILE, D_MODEL),
                                   lambda t, te, nt: (jnp.where(t < nt[0], t, t_max), 0)),
            scratch_shapes=[
                pltpu.VMEM((D_MODEL, EXPERT_FF), BF16),
                pltpu.VMEM((D_MODEL, EXPERT_FF), BF16),
                pltpu.VMEM((EXPERT_FF, D_MODEL), BF16),
            ],
        ),
        out_shape=jax.ShapeDtypeStruct(((t_max + 1) * EXPERT_TILE, D_MODEL), BF16),
        compiler_params=pltpu.CompilerParams(
            dimension_semantics=("arbitrary",), vmem_limit_bytes=VMEM_LIMIT),
        name="moe_experts",
    )(tile_e, n_tiles, xs, w1, w3, w2)


def _final_kernel(base_ref, nwin_ref, x1s_ref, rank_ref, gates_ref, ys_hbm, pa_ref, pb_ref, gple_ref,
                  wgate_ref, wproj_ref, gfin_ref, ya_ref, yb_ref, win0, win1, owin, acc_ref, sem, osem,
                  *, split):
    i = pl.program_id(0)
    last = pl.num_programs(0) - 1
    win = (win0, win1)

    def window_copy(p, e, tile_idx):
        return pltpu.make_async_copy(_window(ys_hbm, base_ref, tile_idx * N_EXPERTS + e, 0),
                                     win[p].at[pl.ds(e * CAP, CAP)], sem.at[p])

    @pl.when(i == 0)
    def _first():
        owin[...] = jnp.zeros(owin.shape, BF16)
        for e in range(N_EXPERTS):
            window_copy(0, e, i).start()

    contract0 = (((0,), (0,)), ((), ()))

    for p in (0, 1):
        @pl.when(i % 2 == p)
        def _step(p=p):
            for e in range(N_EXPERTS):
                window_copy(p, e, i).wait()

            @pl.when(i < last)
            def _():
                for e in range(N_EXPERTS):
                    window_copy(1 - p, e, i + 1).start()

            one_hot = jnp.concatenate([_one_hot_rows(rank_ref, c, 0, gates_ref) for c in range(N_CHUNKS)], axis=0)
            acc_ref[...] = lax.dot_general(one_hot, win[p][...], contract0, preferred_element_type=F32)

    extra = _tile_windows(nwin_ref, i)

    @pl.when(extra > 1)
    def _long_segments():
        def body(w, carry):
            for c in range(N_CHUNKS):
                @pl.when(w < _chunk_windows(nwin_ref, i, c))
                def _(c=c):
                    for k in range(CHUNK_E):
                        idx = i * N_EXPERTS + c * CHUNK_E + k

                        @pl.when(w < nwin_ref[idx])
                        def _(k=k, idx=idx):
                            cp = pltpu.make_async_copy(_window(ys_hbm, base_ref, idx, w),
                                                       owin.at[pl.ds(k * CAP, CAP)], osem)
                            cp.start()
                            cp.wait()
                    acc_ref[...] += lax.dot_general(_one_hot_rows(rank_ref, c, w * CAP, gates_ref), owin[...],
                                                    contract0, preferred_element_type=F32)
            return carry

        lax.fori_loop(1, extra, body, 0)

    x2 = x1s_ref[...] + acc_ref[...]
    r_ = _rms(x2, gple_ref[...])
    gate = jax.nn.sigmoid(_dot(r_.astype(BF16), wgate_ref[...]))
    pt = jnp.where(i < split, pa_ref[...], pb_ref[...])
    x3 = x2 + _dot(pt.astype(BF16), wproj_ref[...]) * gate
    y = _rms(x3, gfin_ref[...])

    @pl.when(i < split)
    def _():
        ya_ref[...] = y

    @pl.when(i >= split)
    def _():
        yb_ref[...] = y


def _final(x1s, rank_t, gates_t, ys, pa, pb, gple, wgate, wproj, gfin, base, nwin, *, TM):
    N = x1s.shape[0]
    split = pa.shape[0] // TM
    tok = pl.BlockSpec((TM, D_MODEL), lambda i, *_: (i, 0))
    per_e = pl.BlockSpec((N_EXPERTS, TM), lambda i, *_: (0, i))
    consts = (gple, wgate, wproj, gfin)
    win = pltpu.VMEM((N_EXPERTS * CAP, D_MODEL), BF16)
    return pl.pallas_call(
        functools.partial(_final_kernel, split=split),
        grid_spec=pltpu.PrefetchScalarGridSpec(
            num_scalar_prefetch=2,
            grid=(N // TM,),
            in_specs=[tok, per_e, per_e, pl.BlockSpec(memory_space=pl.ANY)]
            + list(_group_specs((TM, PLE_DIM), split))
            + [pl.BlockSpec(c.shape, lambda i, *_: (0, 0)) for c in consts],
            out_specs=_group_specs((TM, D_MODEL), split),
            scratch_shapes=[win, win, pltpu.VMEM((CHUNK_E * CAP, D_MODEL), BF16), pltpu.VMEM((TM, D_MODEL), F32),
                            pltpu.SemaphoreType.DMA((2,)), pltpu.SemaphoreType.DMA(())],
        ),
        out_shape=(jax.ShapeDtypeStruct((pa.shape[0], D_MODEL), F32),
                   jax.ShapeDtypeStruct((pb.shape[0], D_MODEL), F32)),
        compiler_params=pltpu.CompilerParams(
            dimension_semantics=("arbitrary",), vmem_limit_bytes=VMEM_LIMIT),
        name="combine_ple_final",
    )(base, nwin, x1s, rank_t, gates_t, ys, pa, pb, *consts)


def kernel(x_prompt, x_sample, p_prompt, p_sample, state_C, state_n, state_m, state_pool, norm_mix_g, w_in, b_igate, b_fgate, head_norm_g, w_pool, pool_scale, w_out, norm_ffn_g, router_w, router_bias, ex_w1, ex_w3, ex_w2, sh_w1, sh_w3, sh_w2, norm_ple_g, w_ple_gate, w_ple_proj, final_norm_g):
    depth = norm_mix_g.shape[0]
    assert depth == 1
    l = 0
    B, T, _ = x_prompt.shape
    Bs, Ts, _ = x_sample.shape
    g0 = 4 * MLSTM_WIDTH
    w = w_in[l]
    lane_pad = jnp.zeros((D_MODEL, 128 - HEADS), F32)
    wcat = jnp.concatenate(
        [w[:, :g0], w[:, g0 + 2 * HEADS:], w[:, g0:g0 + HEADS], lane_pad,
         w[:, g0 + HEADS:g0 + 2 * HEADS], lane_pad], axis=1).astype(BF16)
    bias_pad = jnp.zeros((128 - HEADS,), F32)
    gbias = jnp.concatenate([b_igate[l], bias_pad, b_fgate[l], bias_pad])[None, :]
    mixer_w = (wcat, norm_mix_g[l][None, :], gbias, head_norm_g[l][None, :], w_pool[l].astype(BF16),
               pool_scale[l][None, :], w_out[l].astype(BF16))

    x1p, Cp, Np, Mp, Bp = _mixer(x_prompt, *mixer_w, None, BB=1, TT=512, L=128, start_pos=0)
    state = (state_C[l], state_n[l], state_m[l][..., None], state_pool[l])
    x1s_, Cs, Ns, Ms, Bs_ = _mixer(x_sample, *mixer_w, state, BB=16, TT=Ts, L=Ts, start_pos=PAST_LEN)

    N = B * T + Bs * Ts
    assert (B * T) % ROUTE_TILE == 0 and (Bs * Ts) % ROUTE_TILE == 0
    n_route_tiles = N // ROUTE_TILE

    xn, rank_t, gates_t, cnt, x1sh = _router(
        x1p.reshape(B * T, D_MODEL), x1s_.reshape(Bs * Ts, D_MODEL),
        norm_ffn_g[l][None, :], router_w[l].T, router_bias[l][:, None],
        sh_w1[l].astype(BF16), sh_w3[l].astype(BF16), sh_w2[l].astype(BF16), TM=2 * ROUTE_TILE)

    max_rows = TOP_K * N + (SEG_ALIGN - 1) * n_route_tiles * N_EXPERTS + N_EXPERTS * (CAP + EXPERT_TILE)
    t_max = -(-max_rows // EXPERT_TILE)
    cnt = cnt[:, 0].reshape(n_route_tiles, N_EXPERTS).astype(jnp.int32)
    base, nwin, row_start, row_end, tile_e, n_tiles = _segment_plan(cnt, t_max)

    xs = _dispatch(xn, rank_t, base, nwin, row_start, row_end, n_rows=t_max * EXPERT_TILE, TM=ROUTE_TILE)
    ys = _experts(xs, tile_e, n_tiles, ex_w1[l], ex_w3[l], ex_w2[l])
    y_prompt, y_sample = _final(
        x1sh, rank_t, gates_t, ys, p_prompt[l].reshape(B * T, PLE_DIM), p_sample[l].reshape(Bs * Ts, PLE_DIM),
        norm_ple_g[l][None, :], w_ple_gate[l].astype(BF16), w_ple_proj[l].astype(BF16),
        final_norm_g[None, :], base, nwin, TM=ROUTE_TILE)
    return (y_prompt.reshape(B, T, D_MODEL), y_sample.reshape(Bs, Ts, D_MODEL),
            Cp[None], Np[None], Mp[..., 0][None], Bp[None],
            Cs[None], Ns[None], Ms[..., 0][None], Bs_[None])
```

```python
import functools

import jax
import jax.numpy as jnp
from jax import lax
from jax.experimental import pallas as pl
from jax.experimental.pallas import tpu as pltpu

D_MODEL = 1024
HEADS = 4
DH = 128
MLSTM_WIDTH = HEADS * DH
POOL_WIDTH = 512
POOL_WINDOWS = (2, 4, 8, 16)
POOL_GDIM = 128
POOL_BUF = 15
POOL_PAD = 16
N_EXPERTS = 64
TOP_K = 8
N_GROUPS = 8
GROUP_SIZE = N_EXPERTS // N_GROUPS
TOPK_GROUPS = 4
EXPERT_FF = 256
ROUTED_SCALE = 2.5
NORM_EPS = 1e-6
PLE_DIM = 256
PAST_LEN = 16384

COL_Q, COL_K, COL_V, COL_O, COL_U, COL_I, COL_F = 0, 512, 1024, 1536, 2048, 2560, 2688
IN_COLS_PAD = 2816
GATE_COLS = HEADS * 128

ROUTE_TILE = 256
EXPERT_TILE = 1024
SEG_ALIGN = 16
CAP = 48
CHUNK_E = 8
N_CHUNKS = N_EXPERTS // CHUNK_E

VMEM_LIMIT = 56 * 1024 * 1024
F32 = jnp.float32
BF16 = jnp.bfloat16
NEG_INF = float("-inf")


def _rms(x, g):
    return x * lax.rsqrt(jnp.mean(x * x, axis=-1, keepdims=True) + NORM_EPS) * g


def _log_sigmoid(x):
    return jnp.minimum(x, 0.0) - jnp.log1p(jnp.exp(-jnp.abs(x)))


def _dot(a, b):
    return jnp.dot(a, b, preferred_element_type=F32)


def _mlstm_chunk(q, k, v, b_col, i_col, C, n, m, causal, eye):
    L = q.shape[0]
    r_col = i_col - b_col
    r_row = jnp.sum(jnp.where(eye, r_col, 0.0), axis=0, keepdims=True)
    d = jnp.where(causal, b_col + r_row, NEG_INF)
    inter = b_col + m
    qb, kb, vb = q.astype(BF16), k.astype(BF16), v.astype(BF16)
    qk = lax.dot_general(qb, kb, (((1,), (1,)), ((), ())), preferred_element_type=F32)
    qc = _dot(qb, C.astype(BF16))
    qn = jnp.sum(q * n, axis=-1, keepdims=True)
    yield
    m_t = jnp.maximum(inter, jnp.max(d, axis=-1, keepdims=True))
    b_last = b_col[L - 1:L, :]
    m_new = jnp.maximum(b_last + m, jnp.max(b_last + r_row, axis=-1, keepdims=True))
    yield
    w_inter = jnp.exp(inter - m_t)
    s = qk * jnp.exp(d - m_t)
    fw = jnp.exp(b_last + m - m_new)
    iw_col = jnp.exp(b_last + r_col - m_new)
    kw = iw_col * k
    yield
    sv = _dot(s.astype(BF16), vb)
    kv = lax.dot_general(kw.astype(BF16), vb, (((0,), (0,)), ((), ())), preferred_element_type=F32)
    ssum = jnp.sum(s, axis=-1, keepdims=True)
    n_new = fw * n + jnp.sum(kw, axis=0, keepdims=True)
    yield
    num = w_inter * qc + sv
    nq = w_inter * qn + ssum
    h = num / jnp.maximum(jnp.abs(nq), jnp.exp(-m_t))
    C_new = fw * C + kv
    return h, C_new, n_new, m_new


def _interleave(gens):
    results = [None] * len(gens)
    live = list(enumerate(gens))
    while live:
        still = []
        for idx, g in live:
            try:
                next(g)
                still.append((idx, g))
            except StopIteration as stop:
                results[idx] = stop.value
        live = still
    return results


def _mlstm_chunk_square(q, k, v, r, b, mp, CN, m, causal, eye, ones):
    L = q.shape[0]
    r_row = jnp.sum(jnp.where(eye, r, 0.0), axis=0, keepdims=True)
    g = jnp.maximum(mp, m)
    qb, kb = q.astype(BF16), k.astype(BF16)
    v1 = jnp.concatenate([v.astype(BF16), ones], axis=1)
    qk = lax.dot_general(qb, kb, (((1,), (1,)), ((), ())), preferred_element_type=F32)
    qcn = _dot(qb, CN.astype(BF16))
    yield
    w_inter = jnp.exp(m - g)
    p = jnp.where(causal, jnp.exp(r_row - g), 0.0)
    g_last = g[L - 1:L, :]
    m_new = b[L - 1:L, :] + g_last
    fw = jnp.exp(m - g_last)
    kw = (jnp.exp(r - g_last) * k).astype(BF16)
    floor = jnp.exp(-(b + g))
    yield
    s = (qk * p).astype(BF16)
    sv = _dot(s, v1)
    kv = lax.dot_general(kw, v1, (((0,), (0,)), ((), ())), preferred_element_type=F32)
    yield
    num = w_inter * qcn[:, :DH] + sv[:, :DH]
    nq = w_inter * qcn[:, DH:] + sv[:, DH:]
    h = num / jnp.maximum(jnp.abs(nq), floor)
    CN_new = jnp.concatenate([fw, fw], axis=1) * CN + kv
    return h, CN_new, m_new[:, 0:1]


def _chunk_scan(x, L, op, fill):
    pos = lax.broadcasted_iota(jnp.int32, x.shape, 0) & (L - 1)
    k = 1
    while k < L:
        x = op(x, jnp.where(pos >= k, pltpu.roll(x, k, axis=0), fill))
        k *= 2
    return x


def _mixer_kernel(*refs, BB, TT, L, start_pos, zero_state):
    if zero_state:
        (x_ref, wcat_ref, gmix_ref, gbias_ref, hng_ref, wpool_ref, pscale_ref, wout_ref,
         x1_ref, c_ref, n_ref, m_ref, buf_ref,
         z_ref, a_ref, b_ref, mp_ref, cn_ref, mix_ref, ext_ref) = refs
    else:
        (x_ref, wcat_ref, gmix_ref, gbias_ref, hng_ref, wpool_ref, pscale_ref, wout_ref,
         c0_ref, n0_ref, m0_ref, buf0_ref,
         x1_ref, c_ref, n_ref, m_ref, buf_ref,
         z_ref, a_ref, b_ref, mp_ref, cn_ref, mix_ref, ext_ref) = refs
    t = pl.program_id(1)
    R = BB * TT
    n_chunks = TT // L
    square = L == DH
    assert not square or (zero_state and BB == 1)

    @pl.when(t == 0)
    def _init():
        ext_ref[:, 0:POOL_PAD, :] = jnp.zeros((BB, POOL_PAD, POOL_WIDTH), F32)
        if zero_state:
            cn_ref[...] = jnp.zeros(cn_ref.shape, F32)
            c_ref[...] = jnp.zeros(c_ref.shape, F32)
            n_ref[...] = jnp.zeros(n_ref.shape, F32)
            m_ref[...] = jnp.zeros(m_ref.shape, F32)
        else:
            c_ref[...] = c0_ref[...]
            n_ref[...] = n0_ref[...]
            m_ref[...] = m0_ref[...]
            ext_ref[:, 1:POOL_PAD, :] = buf0_ref[...]

    x = x_ref[...].reshape(R, D_MODEL)
    hn = _rms(x, gmix_ref[...])
    z_ref[...] = _dot(hn.astype(BF16), wcat_ref[...])
    gbias = gbias_ref[...]
    gi = z_ref[:, COL_I:COL_I + 128] + gbias[:, :128]
    b_c = _chunk_scan(_log_sigmoid(z_ref[:, COL_F:COL_F + 128] + gbias[:, 128:]), L, jnp.add, 0.0)
    if square:
        r_c = gi - b_c
        mp_c = _chunk_scan(r_c, L, jnp.maximum, NEG_INF)
        for hd in range(HEADS):
            lanes = slice(hd * DH, (hd + 1) * DH)
            a_ref[:, lanes] = jnp.broadcast_to(r_c[:, hd:hd + 1], (R, DH))
            b_ref[:, lanes] = jnp.broadcast_to(b_c[:, hd:hd + 1], (R, DH))
            mp_ref[:, lanes] = jnp.broadcast_to(mp_c[:, hd:hd + 1], (R, DH))
    else:
        a_ref[:, 0:128] = gi
        b_ref[:, 0:128] = b_c

    row = lax.broadcasted_iota(jnp.int32, (L, L), 0)
    col = lax.broadcasted_iota(jnp.int32, (L, L), 1)
    causal = row >= col
    eye = row == col
    ones = jnp.ones((DH, DH), BF16)
    hng = hng_ref[...]

    group = 4 if (not square and n_chunks == 1 and BB % 4 == 0) else 1

    def chunk_body(s, carry):
        work = []
        for j in range(group):
            sj = s * group + j
            rows = pl.ds(pl.multiple_of(sj * L, L), L)
            bb = 0 if BB == 1 else sj // n_chunks
            for hd in range(HEADS):
                lanes = slice(hd * DH, (hd + 1) * DH)
                q = z_ref[rows, COL_Q + hd * DH:COL_Q + (hd + 1) * DH] * (DH ** -0.5)
                k = z_ref[rows, COL_K + hd * DH:COL_K + (hd + 1) * DH]
                v = z_ref[rows, COL_V + hd * DH:COL_V + (hd + 1) * DH]
                o = z_ref[rows, COL_O + hd * DH:COL_O + (hd + 1) * DH]
                m = m_ref[bb, hd:hd + 1, :]
                if square:
                    state = (a_ref[rows, lanes], b_ref[rows, lanes], mp_ref[rows, lanes], cn_ref[hd])
                else:
                    state = (b_ref[rows, hd:hd + 1], a_ref[rows, hd:hd + 1], c_ref[bb, hd],
                             n_ref[bb, hd:hd + 1, :])
                work.append((rows, bb, hd, lanes, q, k, v, o, m, state))
        if square:
            results = _interleave([_mlstm_chunk_square(q, k, v, *state, m, causal, eye, ones)
                                   for _, _, _, _, q, k, v, _, m, state in work])
        else:
            results = _interleave([_mlstm_chunk(q, k, v, *state, m, causal, eye)
                                   for _, _, _, _, q, k, v, _, m, state in work])
        scales = [lax.rsqrt(jnp.mean(res[0] * res[0], axis=-1, keepdims=True) + NORM_EPS) for res in results]
        done = []
        for (rows, bb, hd, lanes, _, _, _, o, _, _), res, scale in zip(work, results, scales):
            out = res[0] * scale * hng[:, lanes] * jax.nn.sigmoid(o)
            done.append((rows, bb, hd, lanes, out, res[-1], res[1:-1]))
        for rows, bb, hd, lanes, out, m_new, new_state in done:
            mix_ref[rows, lanes] = out
            m_ref[bb, hd:hd + 1, :] = m_new
            if square:
                cn_ref[hd] = new_state[0]
            else:
                c_ref[bb, hd] = new_state[0]
                n_ref[bb, hd:hd + 1, :] = new_state[1]
        return carry

    lax.fori_loop(0, BB * n_chunks // group, chunk_body, 0)

    if square:
        @pl.when(t == pl.num_programs(1) - 1)
        def _emit_state():
            for hd in range(HEADS):
                cn = cn_ref[hd]
                c_ref[0, hd] = cn[:, :DH]
                n_ref[0, hd:hd + 1, :] = jnp.sum(jnp.where(eye, cn[:, DH:], 0.0), axis=0, keepdims=True)

    ext_ref[:, POOL_PAD:POOL_PAD + TT, :] = z_ref[:, COL_U:COL_U + POOL_WIDTH].reshape(BB, TT, POOL_WIDTH)
    pos = start_pos + t * TT + lax.broadcasted_iota(jnp.int32, (1, TT, 1), 1)
    pscale = pscale_ref[...]
    for gidx, w in enumerate(POOL_WINDOWS):
        lanes = slice(gidx * POOL_GDIM, (gidx + 1) * POOL_GDIM)
        u_g = ext_ref[:, POOL_PAD:POOL_PAD + TT, lanes]
        acc = u_g
        for j in range(1, w):
            acc = acc + ext_ref[:, POOL_PAD - j:POOL_PAD - j + TT, lanes]
        cnt = jnp.minimum(pos + 1, w).astype(F32)
        pooled = (acc / cnt - u_g).reshape(R, POOL_GDIM)
        mixed = _dot(pooled.astype(BF16), wpool_ref[gidx]) * pscale[:, lanes]
        mix_ref[:, MLSTM_WIDTH + gidx * POOL_GDIM:MLSTM_WIDTH + (gidx + 1) * POOL_GDIM] = mixed
    new_buf = ext_ref[:, TT + 1:TT + POOL_PAD, :]
    buf_ref[...] = new_buf
    ext_ref[:, 1:POOL_PAD, :] = new_buf

    out = x_ref[...].reshape(R, D_MODEL) + _dot(mix_ref[...].astype(BF16), wout_ref[...])
    x1_ref[...] = out.reshape(BB, TT, D_MODEL)


def _const_spec(shape):
    nd = len(shape)
    return pl.BlockSpec(shape, lambda b, t, _nd=nd: (0,) * _nd, pipeline_mode=pl.Buffered(1))


def _mixer(x, wcat, gmix, gbias, hng, wpool, pscale, wout, state, *, BB, TT, L, start_pos):
    B, T, _ = x.shape
    zero_state = state is None
    R = BB * TT
    grid = (B // BB, T // TT)
    x_spec = pl.BlockSpec((BB, TT, D_MODEL), lambda b, t: (b, t, 0))
    c_spec = pl.BlockSpec((BB, HEADS, DH, DH), lambda b, t: (b, 0, 0, 0))
    n_spec = pl.BlockSpec((BB, HEADS, DH), lambda b, t: (b, 0, 0))
    m_spec = pl.BlockSpec((BB, HEADS, 1), lambda b, t: (b, 0, 0))
    buf_spec = pl.BlockSpec((BB, POOL_BUF, POOL_WIDTH), lambda b, t: (b, 0, 0))
    weights = (wcat, gmix, gbias, hng, wpool, pscale, wout)
    in_specs = [x_spec] + [_const_spec(w.shape) for w in weights]
    args = [x, *weights]
    if not zero_state:
        in_specs += [c_spec, n_spec, m_spec, buf_spec]
        args += list(state)
    out_shape = (
        jax.ShapeDtypeStruct((B, T, D_MODEL), F32),
        jax.ShapeDtypeStruct((B, HEADS, DH, DH), F32),
        jax.ShapeDtypeStruct((B, HEADS, DH), F32),
        jax.ShapeDtypeStruct((B, HEADS, 1), F32),
        jax.ShapeDtypeStruct((B, POOL_BUF, POOL_WIDTH), F32),
    )
    kern = functools.partial(_mixer_kernel, BB=BB, TT=TT, L=L, start_pos=start_pos, zero_state=zero_state)
    return pl.pallas_call(
        kern,
        grid=grid,
        in_specs=in_specs,
        out_specs=(x_spec, c_spec, n_spec, m_spec, buf_spec),
        out_shape=out_shape,
        scratch_shapes=[
            pltpu.VMEM((R, IN_COLS_PAD), F32),
            pltpu.VMEM((R, GATE_COLS), F32),
            pltpu.VMEM((R, GATE_COLS), F32),
            pltpu.VMEM((R, GATE_COLS), F32),
            pltpu.VMEM((HEADS, DH, 2 * DH), F32),
            pltpu.VMEM((R, D_MODEL), F32),
            pltpu.VMEM((BB, POOL_PAD + TT, POOL_WIDTH), F32),
        ],
        compiler_params=pltpu.CompilerParams(
            dimension_semantics=("arbitrary", "arbitrary"), vmem_limit_bytes=VMEM_LIMIT),
        name="mixer_zero" if zero_state else "mixer_state",
    )(*args)


def _first_index_of_max(work, idx, n, axis):
    mx = jnp.max(work, axis=axis, keepdims=True)
    return jnp.min(jnp.where(work == mx, idx, float(n)), axis=axis, keepdims=True)


def _route(scores_t, bias_t):
    tm = scores_t.shape[1]
    biased = scores_t + bias_t
    b3 = biased.reshape(N_GROUPS, GROUP_SIZE, tm)
    sub = lax.broadcasted_iota(jnp.int32, b3.shape, 1).astype(F32)
    m1 = jnp.max(b3, axis=1, keepdims=True)
    first = jnp.min(jnp.where(b3 == m1, sub, float(GROUP_SIZE)), axis=1, keepdims=True)
    m2 = jnp.max(jnp.where(sub == first, NEG_INF, b3), axis=1, keepdims=True)
    gs = (m1 + m2).reshape(N_GROUPS, tm)
    gidx = lax.broadcasted_iota(jnp.int32, gs.shape, 0).astype(F32)
    gsel = jnp.zeros(gs.shape, F32)
    work = gs
    for _ in range(TOPK_GROUPS):
        pick = gidx == _first_index_of_max(work, gidx, N_GROUPS, 0)
        gsel = jnp.where(pick, 1.0, gsel)
        work = jnp.where(pick, NEG_INF, work)
    emask = jnp.broadcast_to(gsel.reshape(N_GROUPS, 1, tm), b3.shape).reshape(N_EXPERTS, tm)
    work = jnp.where(emask > 0, biased, NEG_INF)
    eidx = lax.broadcasted_iota(jnp.int32, work.shape, 0).astype(F32)
    mask = jnp.zeros(work.shape, F32)
    for _ in range(TOP_K):
        pick = eidx == _first_index_of_max(work, eidx, N_EXPERTS, 0)
        mask = jnp.where(pick, 1.0, mask)
        work = jnp.where(pick, NEG_INF, work)
    sel = mask * scores_t
    return mask, sel / jnp.sum(sel, axis=0, keepdims=True) * ROUTED_SCALE


def _group_specs(block, split):
    return (pl.BlockSpec(block, lambda i, *_: (jnp.minimum(i, split - 1), 0)),
            pl.BlockSpec(block, lambda i, *_: (jnp.maximum(i - split, 0), 0)))


def _router_kernel(x1a_ref, x1b_ref, gffn_ref, rwt_ref, rbias_ref, w1_ref, w3_ref, w2_ref,
                   xn_ref, rank_ref, gates_ref, cnt_ref, x1s_ref, *, split):
    x1 = jnp.where(pl.program_id(0) < split, x1a_ref[...], x1b_ref[...])
    tm = x1.shape[0]
    xn = _rms(x1, gffn_ref[...])
    xb = xn.astype(BF16)
    xn_ref[...] = xb
    logits_t = lax.dot_general(rwt_ref[...], xn, (((1,), (1,)), ((), ())),
                               preferred_element_type=F32, precision=lax.Precision.HIGHEST)
    mask, gates = _route(jax.nn.sigmoid(logits_t), rbias_ref[...])
    gates_ref[...] = gates
    rt = ROUTE_TILE
    before = (lax.broadcasted_iota(jnp.int32, (rt, rt), 0)
              < lax.broadcasted_iota(jnp.int32, (rt, rt), 1)).astype(BF16)
    for j in range(tm // rt):
        mj = mask[:, j * rt:(j + 1) * rt]
        rank_ref[:, j * rt:(j + 1) * rt] = jnp.where(mj > 0, _dot(mj.astype(BF16), before), -1.0)
        cnt_ref[j * N_EXPERTS:(j + 1) * N_EXPERTS, :] = jnp.broadcast_to(
            jnp.sum(mj, axis=1, keepdims=True), (N_EXPERTS, 128))
    a = _dot(xb, w1_ref[...])
    hsh = (a * jax.nn.sigmoid(a)) * _dot(xb, w3_ref[...])
    x1s_ref[...] = x1 + _dot(hsh.astype(BF16), w2_ref[...])


def _router(x1a, x1b, gffn, rwt, rbias, w1, w3, w2, *, TM):
    N = x1a.shape[0] + x1b.shape[0]
    split = x1a.shape[0] // TM
    tok = pl.BlockSpec((TM, D_MODEL), lambda i: (i, 0))
    per_e = pl.BlockSpec((N_EXPERTS, TM), lambda i: (0, i))
    consts = (gffn, rwt, rbias, w1, w3, w2)
    return pl.pallas_call(
        functools.partial(_router_kernel, split=split),
        grid=(N // TM,),
        in_specs=list(_group_specs((TM, D_MODEL), split)) + [pl.BlockSpec(c.shape, lambda i: (0, 0)) for c in consts],
        out_specs=(tok, per_e, per_e, pl.BlockSpec((TM // ROUTE_TILE * N_EXPERTS, 128), lambda i: (i, 0)), tok),
        out_shape=(jax.ShapeDtypeStruct((N, D_MODEL), BF16),
                   jax.ShapeDtypeStruct((N_EXPERTS, N), F32),
                   jax.ShapeDtypeStruct((N_EXPERTS, N), F32),
                   jax.ShapeDtypeStruct((N // ROUTE_TILE * N_EXPERTS, 128), F32),
                   jax.ShapeDtypeStruct((N, D_MODEL), F32)),
        compiler_params=pltpu.CompilerParams(
            dimension_semantics=("arbitrary",), vmem_limit_bytes=VMEM_LIMIT),
        name="router_shared",
    )(x1a, x1b, *consts)


def _segment_plan(cnt, n_tiles_max):
    seg = (cnt + SEG_ALIGN - 1) // SEG_ALIGN * SEG_ALIGN
    used = jnp.sum(seg, axis=0)
    size = (used + CAP + EXPERT_TILE - 1) // EXPERT_TILE * EXPERT_TILE
    row_end = jnp.cumsum(size)
    row_start = row_end - size
    base = row_start[None, :] + jnp.cumsum(seg, axis=0) - seg
    nwin = jnp.maximum((seg + CAP - 1) // CAP, 1)
    tile_end = row_end // EXPERT_TILE
    n_tiles = tile_end[-1]
    t_ids = jnp.arange(n_tiles_max, dtype=jnp.int32)
    tile_e = jnp.minimum(jnp.sum(t_ids[:, None] >= tile_end[None, :], axis=1), N_EXPERTS - 1)
    i32 = lambda a: a.astype(jnp.int32)
    return (i32(base).reshape(-1), i32(nwin).reshape(-1), i32(row_start + used), i32(row_end),
            i32(tile_e), i32(n_tiles).reshape(1))


def _one_hot_rows(rank_ref, chunk, first_row, values_ref=None):
    tm = rank_ref.shape[1]
    j = (lax.broadcasted_iota(jnp.int32, (CAP, tm), 0) + first_row).astype(F32)
    rows = []
    for k in range(CHUNK_E):
        e = chunk * CHUNK_E + k
        hit = j == rank_ref[e:e + 1, :]
        val = 1.0 if values_ref is None else values_ref[e:e + 1, :]
        rows.append(jnp.where(hit, val, 0.0).astype(BF16))
    return jnp.concatenate(rows, axis=0)


def _window(hbm, base_ref, idx, w):
    start = pl.multiple_of(base_ref[idx] + w * CAP, SEG_ALIGN)
    return hbm.at[pl.ds(start, CAP)]


def _chunk_windows(nwin_ref, tile_idx, c):
    extra = nwin_ref[tile_idx * N_EXPERTS + c * CHUNK_E]
    for k in range(1, CHUNK_E):
        extra = jnp.maximum(extra, nwin_ref[tile_idx * N_EXPERTS + c * CHUNK_E + k])
    return extra


def _tile_windows(nwin_ref, tile_idx):
    extra = _chunk_windows(nwin_ref, tile_idx, 0)
    for c in range(1, N_CHUNKS):
        extra = jnp.maximum(extra, _chunk_windows(nwin_ref, tile_idx, c))
    return extra


def _dispatch_kernel(base_ref, nwin_ref, uend_ref, rend_ref, xn_ref, rank_ref, xs_hbm,
                     stage0, stage1, ostage, zbuf, sem, osem, zsem):
    stage = (stage0, stage1)
    i = pl.program_id(0)
    last = pl.num_programs(0) - 1

    def window_copy(p, e, tile_idx):
        return pltpu.make_async_copy(stage[p].at[pl.ds(e * CAP, CAP)],
                                     _window(xs_hbm, base_ref, tile_idx * N_EXPERTS + e, 0), sem.at[p])

    def wait_tile(p, tile_idx):
        for e in range(N_EXPERTS):
            window_copy(p, e, tile_idx).wait()

    @pl.when(i == 0)
    def _zero_tails():
        zbuf[...] = jnp.zeros(zbuf.shape, BF16)
        sizes = []
        size = EXPERT_TILE
        while size >= SEG_ALIGN:
            sizes.append(size)
            size //= 2
        pieces = []
        for e in range(N_EXPERTS):
            pos = uend_ref[e]
            length = rend_ref[e] - pos
            for size in sizes:
                take = (length & size) != 0
                cp = pltpu.make_async_copy(zbuf.at[pl.ds(0, size)],
                                           xs_hbm.at[pl.ds(pl.multiple_of(pos, SEG_ALIGN), size)], zsem)
                pieces.append((take, cp))
                pos = pos + jnp.where(take, size, 0)
        for take, cp in pieces:
            pl.when(take)(cp.start)
        for take, cp in pieces:
            pl.when(take)(cp.wait)

    for p in (0, 1):
        @pl.when(i % 2 == p)
        def _step(p=p):
            xb = xn_ref[...]
            rows = CHUNK_E * CAP
            for c in range(N_CHUNKS):
                stage[p][pl.ds(c * rows, rows), :] = _dot(_one_hot_rows(rank_ref, c, 0), xb).astype(BF16)

            @pl.when(i >= 1)
            def _():
                wait_tile(1 - p, i - 1)

            for e in range(N_EXPERTS):
                window_copy(p, e, i).start()

    extra = _tile_windows(nwin_ref, i)

    @pl.when(extra > 1)
    def _long_segments():
        def body(w, carry):
            for c in range(N_CHUNKS):
                @pl.when(w < _chunk_windows(nwin_ref, i, c))
                def _(c=c):
                    ostage[...] = _dot(_one_hot_rows(rank_ref, c, w * CAP), xn_ref[...]).astype(BF16)
                    for k in range(CHUNK_E):
                        idx = i * N_EXPERTS + c * CHUNK_E + k

                        @pl.when(w < nwin_ref[idx])
                        def _(k=k, idx=idx):
                            cp = pltpu.make_async_copy(ostage.at[pl.ds(k * CAP, CAP)],
                                                       _window(xs_hbm, base_ref, idx, w), osem)
                            cp.start()
                            cp.wait()
            return carry

        lax.fori_loop(1, extra, body, 0)

    for p in (0, 1):
        @pl.when((i == last) & (i % 2 == p))
        def _drain(p=p):
            wait_tile(p, i)


def _dispatch(xn, rank_t, base, nwin, used_end, row_end, *, n_rows, TM):
    N = xn.shape[0]
    stage = pltpu.VMEM((N_EXPERTS * CAP, D_MODEL), BF16)
    return pl.pallas_call(
        _dispatch_kernel,
        grid_spec=pltpu.PrefetchScalarGridSpec(
            num_scalar_prefetch=4,
            grid=(N // TM,),
            in_specs=[pl.BlockSpec((TM, D_MODEL), lambda i, *_: (i, 0)),
                      pl.BlockSpec((N_EXPERTS, TM), lambda i, *_: (0, i))],
            out_specs=pl.BlockSpec(memory_space=pl.ANY),
            scratch_shapes=[
                stage, stage,
                pltpu.VMEM((CHUNK_E * CAP, D_MODEL), BF16),
                pltpu.VMEM((EXPERT_TILE, D_MODEL), BF16),
                pltpu.SemaphoreType.DMA((2,)),
                pltpu.SemaphoreType.DMA(()),
                pltpu.SemaphoreType.DMA(()),
            ],
        ),
        out_shape=jax.ShapeDtypeStruct((n_rows, D_MODEL), BF16),
        compiler_params=pltpu.CompilerParams(
            dimension_semantics=("arbitrary",), vmem_limit_bytes=VMEM_LIMIT),
        name="moe_dispatch",
    )(base, nwin, used_end, row_end, xn, rank_t)


def _expert_kernel(tile_e_ref, n_tiles_ref, xs_ref, w1_hbm, w3_hbm, w2_hbm, ys_ref,
                   w1f0, w1f1, w3f0, w3f1, w2f0, w2f1, w1b, w3b, w2b, wsem):
    t = pl.program_id(0)
    valid = t < n_tiles_ref[0]
    e = tile_e_ref[t]
    first_tile = valid & ((t == 0) | (e != tile_e_ref[jnp.maximum(t - 1, 0)]))
    f32_bufs = ((w1f0, w3f0, w2f0), (w1f1, w3f1, w2f1))

    def weight_copies(expert, p):
        return [pltpu.make_async_copy(hbm.at[expert], buf, wsem.at[p, j])
                for j, (hbm, buf) in enumerate(zip((w1_hbm, w3_hbm, w2_hbm), f32_bufs[p]))]

    for p in (0, 1):
        @pl.when(first_tile & (e % 2 == p))
        def _next_expert(p=p):
            @pl.when(t == 0)
            def _():
                for cp in weight_copies(e, p):
                    cp.start()

            for cp in weight_copies(e, p):
                cp.wait()

            @pl.when(e + 1 < N_EXPERTS)
            def _():
                for cp in weight_copies(e + 1, 1 - p):
                    cp.start()

            w1b[...] = f32_bufs[p][0][...].astype(BF16)
            w3b[...] = f32_bufs[p][1][...].astype(BF16)
            w2b[...] = f32_bufs[p][2][...].astype(BF16)

    @pl.when(valid)
    def _compute():
        xb = xs_ref[...]
        a = _dot(xb, w1b[...])
        hb = (a * jax.nn.sigmoid(a)) * _dot(xb, w3b[...])
        ys_ref[...] = _dot(hb.astype(BF16), w2b[...]).astype(BF16)

    @pl.when(t == n_tiles_ref[0])
    def _spare():
        ys_ref[...] = jnp.zeros(ys_ref.shape, BF16)


def _experts(xs, tile_e, n_tiles, w1, w3, w2):
    t_max = xs.shape[0] // EXPERT_TILE
    clamp = lambda t, nt: jnp.minimum(t, nt[0] - 1)
    any_spec = pl.BlockSpec(memory_space=pl.ANY)
    w_in = pltpu.VMEM((D_MODEL, EXPERT_FF), F32)
    w_out = pltpu.VMEM((EXPERT_FF, D_MODEL), F32)
    return pl.pallas_call(
        _expert_kernel,
        grid_spec=pltpu.PrefetchScalarGridSpec(
            num_scalar_prefetch=2,
            grid=(t_max,),
            in_specs=[pl.BlockSpec((EXPERT_TILE, D_MODEL), lambda t, te, nt: (clamp(t, nt), 0)),
                      any_spec, any_spec, any_spec],
            out_specs=pl.BlockSpec((EXPERT_TILE, D_MODEL),
                                   lambda t, te, nt: (jnp.where(t < nt[0], t, t_max), 0)),
            scratch_shapes=[
                w_in, w_in, w_in, w_in, w_out, w_out,
                pltpu.VMEM((D_MODEL, EXPERT_FF), BF16),
                pltpu.VMEM((D_MODEL, EXPERT_FF), BF16),
                pltpu.VMEM((EXPERT_FF, D_MODEL), BF16),
                pltpu.SemaphoreType.DMA((2, 3)),
            ],
        ),
        out_shape=jax.ShapeDtypeStruct(((t_max + 1) * EXPERT_TILE, D_MODEL), BF16),
        compiler_params=pltpu.CompilerParams(
            dimension_semantics=("arbitrary",), vmem_limit_bytes=VMEM_LIMIT),
        name="moe_experts",
    )(tile_e, n_tiles, xs, w1, w3, w2)


def _final_kernel(base_ref, nwin_ref, x1s_ref, rank_ref, gates_ref, ys_hbm, pa_ref, pb_ref, gple_ref,
                  wgate_ref, wproj_ref, gfin_ref, ya_ref, yb_ref, win0, win1, owin, acc_ref, sem, osem,
                  *, split):
    i = pl.program_id(0)
    last = pl.num_programs(0) - 1
    win = (win0, win1)

    def window_copy(p, e, tile_idx):
        return pltpu.make_async_copy(_window(ys_hbm, base_ref, tile_idx * N_EXPERTS + e, 0),
                                     win[p].at[pl.ds(e * CAP, CAP)], sem.at[p])

    @pl.when(i == 0)
    def _first():
        owin[...] = jnp.zeros(owin.shape, BF16)
        for e in range(N_EXPERTS):
            window_copy(0, e, i).start()

    contract0 = (((0,), (0,)), ((), ()))

    for p in (0, 1):
        @pl.when(i % 2 == p)
        def _step(p=p):
            for e in range(N_EXPERTS):
                window_copy(p, e, i).wait()

            @pl.when(i < last)
            def _():
                for e in range(N_EXPERTS):
                    window_copy(1 - p, e, i + 1).start()

            one_hot = jnp.concatenate([_one_hot_rows(rank_ref, c, 0, gates_ref) for c in range(N_CHUNKS)], axis=0)
            acc_ref[...] = lax.dot_general(one_hot, win[p][...], contract0, preferred_element_type=F32)

    extra = _tile_windows(nwin_ref, i)

    @pl.when(extra > 1)
    def _long_segments():
        def body(w, carry):
            for c in range(N_CHUNKS):
                @pl.when(w < _chunk_windows(nwin_ref, i, c))
                def _(c=c):
                    for k in range(CHUNK_E):
                        idx = i * N_EXPERTS + c * CHUNK_E + k

                        @pl.when(w < nwin_ref[idx])
                        def _(k=k, idx=idx):
                            cp = pltpu.make_async_copy(_window(ys_hbm, base_ref, idx, w),
                                                       owin.at[pl.ds(k * CAP, CAP)], osem)
                            cp.start()
                            cp.wait()
                    acc_ref[...] += lax.dot_general(_one_hot_rows(rank_ref, c, w * CAP, gates_ref), owin[...],
                                                    contract0, preferred_element_type=F32)
            return carry

        lax.fori_loop(1, extra, body, 0)

    x2 = x1s_ref[...] + acc_ref[...]
    r_ = _rms(x2, gple_ref[...])
    gate = jax.nn.sigmoid(_dot(r_.astype(BF16), wgate_ref[...]))
    pt = jnp.where(i < split, pa_ref[...], pb_ref[...])
    x3 = x2 + _dot(pt.astype(BF16), wproj_ref[...]) * gate
    y = _rms(x3, gfin_ref[...])

    @pl.when(i < split)
    def _():
        ya_ref[...] = y

    @pl.when(i >= split)
    def _():
        yb_ref[...] = y


def _final(x1s, rank_t, gates_t, ys, pa, pb, gple, wgate, wproj, gfin, base, nwin, *, TM):
    N = x1s.shape[0]
    split = pa.shape[0] // TM
    tok = pl.BlockSpec((TM, D_MODEL), lambda i, *_: (i, 0))
    per_e = pl.BlockSpec((N_EXPERTS, TM), lambda i, *_: (0, i))
    consts = (gple, wgate, wproj, gfin)
    win = pltpu.VMEM((N_EXPERTS * CAP, D_MODEL), BF16)
    return pl.pallas_call(
        functools.partial(_final_kernel, split=split),
        grid_spec=pltpu.PrefetchScalarGridSpec(
            num_scalar_prefetch=2,
            grid=(N // TM,),
            in_specs=[tok, per_e, per_e, pl.BlockSpec(memory_space=pl.ANY)]
            + list(_group_specs((TM, PLE_DIM), split))
            + [pl.BlockSpec(c.shape, lambda i, *_: (0, 0)) for c in consts],
            out_specs=_group_specs((TM, D_MODEL), split),
            scratch_shapes=[win, win, pltpu.VMEM((CHUNK_E * CAP, D_MODEL), BF16), pltpu.VMEM((TM, D_MODEL), F32),
                            pltpu.SemaphoreType.DMA((2,)), pltpu.SemaphoreType.DMA(())],
        ),
        out_shape=(jax.ShapeDtypeStruct((pa.shape[0], D_MODEL), F32),
                   jax.ShapeDtypeStruct((pb.shape[0], D_MODEL), F32)),
        compiler_params=pltpu.CompilerParams(
            dimension_semantics=("arbitrary",), vmem_limit_bytes=VMEM_LIMIT),
        name="combine_ple_final",
    )(base, nwin, x1s, rank_t, gates_t, ys, pa, pb, *consts)


def kernel(x_prompt, x_sample, p_prompt, p_sample, state_C, state_n, state_m, state_pool, norm_mix_g, w_in, b_igate, b_fgate, head_norm_g, w_pool, pool_scale, w_out, norm_ffn_g, router_w, router_bias, ex_w1, ex_w3, ex_w2, sh_w1, sh_w3, sh_w2, norm_ple_g, w_ple_gate, w_ple_proj, final_norm_g):
    depth = norm_mix_g.shape[0]
    assert depth == 1
    l = 0
    B, T, _ = x_prompt.shape
    Bs, Ts, _ = x_sample.shape
    g0 = 4 * MLSTM_WIDTH
    w = w_in[l]
    lane_pad = jnp.zeros((D_MODEL, 128 - HEADS), F32)
    wcat = jnp.concatenate(
        [w[:, :g0], w[:, g0 + 2 * HEADS:], w[:, g0:g0 + HEADS], lane_pad,
         w[:, g0 + HEADS:g0 + 2 * HEADS], lane_pad], axis=1).astype(BF16)
    bias_pad = jnp.zeros((128 - HEADS,), F32)
    gbias = jnp.concatenate([b_igate[l], bias_pad, b_fgate[l], bias_pad])[None, :]
    mixer_w = (wcat, norm_mix_g[l][None, :], gbias, head_norm_g[l][None, :], w_pool[l].astype(BF16),
               pool_scale[l][None, :], w_out[l].astype(BF16))

    x1p, Cp, Np, Mp, Bp = _mixer(x_prompt, *mixer_w, None, BB=1, TT=512, L=128, start_pos=0)
    state = (state_C[l], state_n[l], state_m[l][..., None], state_pool[l])
    x1s_, Cs, Ns, Ms, Bs_ = _mixer(x_sample, *mixer_w, state, BB=16, TT=Ts, L=Ts, start_pos=PAST_LEN)

    N = B * T + Bs * Ts
    assert (B * T) % ROUTE_TILE == 0 and (Bs * Ts) % ROUTE_TILE == 0
    n_route_tiles = N // ROUTE_TILE

    xn, rank_t, gates_t, cnt, x1sh = _router(
        x1p.reshape(B * T, D_MODEL), x1s_.reshape(Bs * Ts, D_MODEL),
        norm_ffn_g[l][None, :], router_w[l].T, router_bias[l][:, None],
        sh_w1[l].astype(BF16), sh_w3[l].astype(BF16), sh_w2[l].astype(BF16), TM=2 * ROUTE_TILE)

    max_rows = TOP_K * N + (SEG_ALIGN - 1) * n_route_tiles * N_EXPERTS + N_EXPERTS * (CAP + EXPERT_TILE)
    t_max = -(-max_rows // EXPERT_TILE)
    cnt = cnt[:, 0].reshape(n_route_tiles, N_EXPERTS).astype(jnp.int32)
    base, nwin, used_end, row_end, tile_e, n_tiles = _segment_plan(cnt, t_max)

    xs = _dispatch(xn, rank_t, base, nwin, used_end, row_end, n_rows=t_max * EXPERT_TILE, TM=ROUTE_TILE)
    ys = _experts(xs, tile_e, n_tiles, ex_w1[l], ex_w3[l], ex_w2[l])
    y_prompt, y_sample = _final(
        x1sh, rank_t, gates_t, ys, p_prompt[l].reshape(B * T, PLE_DIM), p_sample[l].reshape(Bs * Ts, PLE_DIM),
        norm_ple_g[l][None, :], w_ple_gate[l].astype(BF16), w_ple_proj[l].astype(BF16),
        final_norm_g[None, :], base, nwin, TM=ROUTE_TILE)
    return (y_prompt.reshape(B, T, D_MODEL), y_sample.reshape(Bs, Ts, D_MODEL),
            Cp[None], Np[None], Mp[..., 0][None], Bp[None],
            Cs[None], Ns[None], Ms[..., 0][None], Bs_[None])
```

```python
import functools

import jax
import jax.numpy as jnp
from jax import lax
from jax.experimental import pallas as pl
from jax.experimental.pallas import tpu as pltpu

D_MODEL = 1024
HEADS = 4
DH = 128
MLSTM_WIDTH = HEADS * DH
POOL_WIDTH = 512
POOL_WINDOWS = (2, 4, 8, 16)
POOL_GDIM = 128
POOL_BUF = 15
POOL_PAD = 16
N_EXPERTS = 64
TOP_K = 8
N_GROUPS = 8
GROUP_SIZE = N_EXPERTS // N_GROUPS
TOPK_GROUPS = 4
EXPERT_FF = 256
ROUTED_SCALE = 2.5
NORM_EPS = 1e-6
PLE_DIM = 256
PAST_LEN = 16384

COL_Q, COL_K, COL_V, COL_O, COL_U, COL_I, COL_F = 0, 512, 1024, 1536, 2048, 2560, 2688
IN_COLS_PAD = 2816
GATE_COLS = HEADS * 128

ROUTE_TILE = 256
EXPERT_TILE = 1024
SEG_ALIGN = 16
CAP = 48
CHUNK_E = 8
N_CHUNKS = N_EXPERTS // CHUNK_E

VMEM_LIMIT = 56 * 1024 * 1024
F32 = jnp.float32
BF16 = jnp.bfloat16
NEG_INF = float("-inf")


def _rms(x, g):
    return x * lax.rsqrt(jnp.mean(x * x, axis=-1, keepdims=True) + NORM_EPS) * g


def _log_sigmoid(x):
    return jnp.minimum(x, 0.0) - jnp.log1p(jnp.exp(-jnp.abs(x)))


def _dot(a, b):
    return jnp.dot(a, b, preferred_element_type=F32)


def _mlstm_chunk(q, k, v, b_col, i_col, C, n, m, causal, eye):
    L = q.shape[0]
    r_col = i_col - b_col
    r_row = jnp.sum(jnp.where(eye, r_col, 0.0), axis=0, keepdims=True)
    d = jnp.where(causal, b_col + r_row, NEG_INF)
    inter = b_col + m
    qb, kb, vb = q.astype(BF16), k.astype(BF16), v.astype(BF16)
    qk = lax.dot_general(qb, kb, (((1,), (1,)), ((), ())), preferred_element_type=F32)
    qc = _dot(qb, C.astype(BF16))
    qn = jnp.sum(q * n, axis=-1, keepdims=True)
    yield
    m_t = jnp.maximum(inter, jnp.max(d, axis=-1, keepdims=True))
    b_last = b_col[L - 1:L, :]
    m_new = jnp.maximum(b_last + m, jnp.max(b_last + r_row, axis=-1, keepdims=True))
    yield
    w_inter = jnp.exp(inter - m_t)
    s = qk * jnp.exp(d - m_t)
    fw = jnp.exp(b_last + m - m_new)
    iw_col = jnp.exp(b_last + r_col - m_new)
    kw = iw_col * k
    yield
    sv = _dot(s.astype(BF16), vb)
    kv = lax.dot_general(kw.astype(BF16), vb, (((0,), (0,)), ((), ())), preferred_element_type=F32)
    ssum = jnp.sum(s, axis=-1, keepdims=True)
    n_new = fw * n + jnp.sum(kw, axis=0, keepdims=True)
    yield
    num = w_inter * qc + sv
    nq = w_inter * qn + ssum
    h = num / jnp.maximum(jnp.abs(nq), jnp.exp(-m_t))
    C_new = fw * C + kv
    return h, C_new, n_new, m_new


def _interleave(gens):
    results = [None] * len(gens)
    live = list(enumerate(gens))
    while live:
        still = []
        for idx, g in live:
            try:
                next(g)
                still.append((idx, g))
            except StopIteration as stop:
                results[idx] = stop.value
        live = still
    return results


def _mlstm_chunk_square(q, k, v, r, b, mp, CN, m, causal, eye, ones):
    L = q.shape[0]
    r_row = jnp.sum(jnp.where(eye, r, 0.0), axis=0, keepdims=True)
    g = jnp.maximum(mp, m)
    qb, kb = q.astype(BF16), k.astype(BF16)
    v1 = jnp.concatenate([v.astype(BF16), ones], axis=1)
    qk = lax.dot_general(qb, kb, (((1,), (1,)), ((), ())), preferred_element_type=F32)
    qcn = _dot(qb, CN.astype(BF16))
    yield
    w_inter = jnp.exp(m - g)
    p = jnp.where(causal, jnp.exp(r_row - g), 0.0)
    g_last = g[L - 1:L, :]
    m_new = b[L - 1:L, :] + g_last
    fw = jnp.exp(m - g_last)
    kw = (jnp.exp(r - g_last) * k).astype(BF16)
    floor = jnp.exp(-(b + g))
    yield
    s = (qk * p).astype(BF16)
    sv = _dot(s, v1)
    kv = lax.dot_general(kw, v1, (((0,), (0,)), ((), ())), preferred_element_type=F32)
    yield
    num = w_inter * qcn[:, :DH] + sv[:, :DH]
    nq = w_inter * qcn[:, DH:] + sv[:, DH:]
    h = num / jnp.maximum(jnp.abs(nq), floor)
    CN_new = jnp.concatenate([fw, fw], axis=1) * CN + kv
    return h, CN_new, m_new[:, 0:1]


def _chunk_scan(x, L, op, fill):
    pos = lax.broadcasted_iota(jnp.int32, x.shape, 0) & (L - 1)
    k = 1
    while k < L:
        x = op(x, jnp.where(pos >= k, pltpu.roll(x, k, axis=0), fill))
        k *= 2
        yield
    return x


def _advance(gens, steps):
    for g in list(gens):
        for _ in range(steps):
            try:
                next(g)
            except StopIteration:
                gens.remove(g)
                break


def _mixer_kernel(*refs, BB, TT, L, start_pos, zero_state):
    if zero_state:
        (x_ref, wcat_ref, gmix_ref, gbias_ref, hng_ref, wpool_ref, pscale_ref, wout_ref,
         x1_ref, c_ref, n_ref, m_ref, buf_ref,
         z_ref, a_ref, b_ref, mp_ref, cn_ref, mix_ref, ext_ref) = refs
    else:
        (x_ref, wcat_ref, gmix_ref, gbias_ref, hng_ref, wpool_ref, pscale_ref, wout_ref,
         c0_ref, n0_ref, m0_ref, buf0_ref,
         x1_ref, c_ref, n_ref, m_ref, buf_ref,
         z_ref, a_ref, b_ref, mp_ref, cn_ref, mix_ref, ext_ref) = refs
    t = pl.program_id(1)
    R = BB * TT
    n_chunks = TT // L
    square = L == DH
    assert not square or (zero_state and BB == 1)

    @pl.when(t == 0)
    def _init():
        ext_ref[:, 0:POOL_PAD, :] = jnp.zeros((BB, POOL_PAD, POOL_WIDTH), F32)
        if zero_state:
            cn_ref[...] = jnp.zeros(cn_ref.shape, F32)
            c_ref[...] = jnp.zeros(c_ref.shape, F32)
            n_ref[...] = jnp.zeros(n_ref.shape, F32)
            m_ref[...] = jnp.zeros(m_ref.shape, F32)
        else:
            c_ref[...] = c0_ref[...]
            n_ref[...] = n0_ref[...]
            m_ref[...] = m0_ref[...]
            ext_ref[:, 1:POOL_PAD, :] = buf0_ref[...]

    hb = _rms(x_ref[...].reshape(R, D_MODEL), gmix_ref[...]).astype(BF16)

    def project(lo, hi):
        z_ref[:, lo:hi] = _dot(hb, wcat_ref[:, lo:hi])

    def gate_stages():
        gbias = gbias_ref[...]
        gi = z_ref[:, COL_I:COL_I + 128] + gbias[:, :128]
        lf = _log_sigmoid(z_ref[:, COL_F:COL_F + 128] + gbias[:, 128:])
        yield
        b_c = yield from _chunk_scan(lf, L, jnp.add, 0.0)
        if square:
            r_c = gi - b_c
            mp_c = yield from _chunk_scan(r_c, L, jnp.maximum, NEG_INF)
            for hd in range(HEADS):
                lanes = slice(hd * DH, (hd + 1) * DH)
                a_ref[:, lanes] = jnp.broadcast_to(r_c[:, hd:hd + 1], (R, DH))
                b_ref[:, lanes] = jnp.broadcast_to(b_c[:, hd:hd + 1], (R, DH))
                mp_ref[:, lanes] = jnp.broadcast_to(mp_c[:, hd:hd + 1], (R, DH))
                yield
        else:
            a_ref[:, 0:128] = gi
            b_ref[:, 0:128] = b_c

    def pool_stages():
        ext_ref[:, POOL_PAD:POOL_PAD + TT, :] = z_ref[:, COL_U:COL_U + POOL_WIDTH].reshape(BB, TT, POOL_WIDTH)
        pos = start_pos + t * TT + lax.broadcasted_iota(jnp.int32, (1, TT, 1), 1)
        pscale = pscale_ref[...]
        yield
        for gidx, w in enumerate(POOL_WINDOWS):
            lanes = slice(gidx * POOL_GDIM, (gidx + 1) * POOL_GDIM)
            u_g = ext_ref[:, POOL_PAD:POOL_PAD + TT, lanes]
            acc = u_g
            for j in range(1, w):
                acc = acc + ext_ref[:, POOL_PAD - j:POOL_PAD - j + TT, lanes]
                if j % 4 == 3:
                    yield
            cnt = jnp.minimum(pos + 1, w).astype(F32)
            pooled = (acc / cnt - u_g).reshape(R, POOL_GDIM)
            mixed = _dot(pooled.astype(BF16), wpool_ref[gidx]) * pscale[:, lanes]
            mix_ref[:, MLSTM_WIDTH + gidx * POOL_GDIM:MLSTM_WIDTH + (gidx + 1) * POOL_GDIM] = mixed
            yield
        new_buf = ext_ref[:, TT + 1:TT + POOL_PAD, :]
        buf_ref[...] = new_buf
        ext_ref[:, 1:POOL_PAD, :] = new_buf

    project(COL_I, IN_COLS_PAD)
    project(COL_U, COL_U + 256)
    project(COL_U + 256, COL_I)
    side = [gate_stages(), pool_stages()]
    for lo in range(0, COL_U, 256):
        project(lo, lo + 256)
        _advance(side, 3)
    while side:
        _advance(side, 1)

    row = lax.broadcasted_iota(jnp.int32, (L, L), 0)
    col = lax.broadcasted_iota(jnp.int32, (L, L), 1)
    causal = row >= col
    eye = row == col
    ones = jnp.ones((DH, DH), BF16)
    hng = hng_ref[...]

    group = 4 if (not square and n_chunks == 1 and BB % 4 == 0) else 1

    def chunk_body(s, carry):
        work = []
        for j in range(group):
            sj = s * group + j
            rows = pl.ds(pl.multiple_of(sj * L, L), L)
            bb = 0 if BB == 1 else sj // n_chunks
            for hd in range(HEADS):
                lanes = slice(hd * DH, (hd + 1) * DH)
                q = z_ref[rows, COL_Q + hd * DH:COL_Q + (hd + 1) * DH] * (DH ** -0.5)
                k = z_ref[rows, COL_K + hd * DH:COL_K + (hd + 1) * DH]
                v = z_ref[rows, COL_V + hd * DH:COL_V + (hd + 1) * DH]
                o = z_ref[rows, COL_O + hd * DH:COL_O + (hd + 1) * DH]
                m = m_ref[bb, hd:hd + 1, :]
                if square:
                    state = (a_ref[rows, lanes], b_ref[rows, lanes], mp_ref[rows, lanes], cn_ref[hd])
                else:
                    state = (b_ref[rows, hd:hd + 1], a_ref[rows, hd:hd + 1], c_ref[bb, hd],
                             n_ref[bb, hd:hd + 1, :])
                work.append((rows, bb, hd, lanes, q, k, v, o, m, state))
        if square:
            results = _interleave([_mlstm_chunk_square(q, k, v, *state, m, causal, eye, ones)
                                   for _, _, _, _, q, k, v, _, m, state in work])
        else:
            results = _interleave([_mlstm_chunk(q, k, v, *state, m, causal, eye)
                                   for _, _, _, _, q, k, v, _, m, state in work])
        scales = [lax.rsqrt(jnp.mean(res[0] * res[0], axis=-1, keepdims=True) + NORM_EPS) for res in results]
        done = []
        for (rows, bb, hd, lanes, _, _, _, o, _, _), res, scale in zip(work, results, scales):
            out = res[0] * scale * hng[:, lanes] * jax.nn.sigmoid(o)
            done.append((rows, bb, hd, lanes, out, res[-1], res[1:-1]))
        for rows, bb, hd, lanes, out, m_new, new_state in done:
            mix_ref[rows, lanes] = out
            m_ref[bb, hd:hd + 1, :] = m_new
            if square:
                cn_ref[hd] = new_state[0]
            else:
                c_ref[bb, hd] = new_state[0]
                n_ref[bb, hd:hd + 1, :] = new_state[1]
        return carry

    lax.fori_loop(0, BB * n_chunks // group, chunk_body, 0)

    if square:
        @pl.when(t == pl.num_programs(1) - 1)
        def _emit_state():
            for hd in range(HEADS):
                cn = cn_ref[hd]
                c_ref[0, hd] = cn[:, :DH]
                n_ref[0, hd:hd + 1, :] = jnp.sum(jnp.where(eye, cn[:, DH:], 0.0), axis=0, keepdims=True)

    out = x_ref[...].reshape(R, D_MODEL) + _dot(mix_ref[...].astype(BF16), wout_ref[...])
    x1_ref[...] = out.reshape(BB, TT, D_MODEL)


def _const_spec(shape):
    nd = len(shape)
    return pl.BlockSpec(shape, lambda b, t, _nd=nd: (0,) * _nd, pipeline_mode=pl.Buffered(1))


def _mixer(x, wcat, gmix, gbias, hng, wpool, pscale, wout, state, *, BB, TT, L, start_pos):
    B, T, _ = x.shape
    zero_state = state is None
    R = BB * TT
    grid = (B // BB, T // TT)
    x_spec = pl.BlockSpec((BB, TT, D_MODEL), lambda b, t: (b, t, 0))
    c_spec = pl.BlockSpec((BB, HEADS, DH, DH), lambda b, t: (b, 0, 0, 0))
    n_spec = pl.BlockSpec((BB, HEADS, DH), lambda b, t: (b, 0, 0))
    m_spec = pl.BlockSpec((BB, HEADS, 1), lambda b, t: (b, 0, 0))
    buf_spec = pl.BlockSpec((BB, POOL_BUF, POOL_WIDTH), lambda b, t: (b, 0, 0))
    weights = (wcat, gmix, gbias, hng, wpool, pscale, wout)
    in_specs = [x_spec] + [_const_spec(w.shape) for w in weights]
    args = [x, *weights]
    if not zero_state:
        in_specs += [c_spec, n_spec, m_spec, buf_spec]
        args += list(state)
    out_shape = (
        jax.ShapeDtypeStruct((B, T, D_MODEL), F32),
        jax.ShapeDtypeStruct((B, HEADS, DH, DH), F32),
        jax.ShapeDtypeStruct((B, HEADS, DH), F32),
        jax.ShapeDtypeStruct((B, HEADS, 1), F32),
        jax.ShapeDtypeStruct((B, POOL_BUF, POOL_WIDTH), F32),
    )
    kern = functools.partial(_mixer_kernel, BB=BB, TT=TT, L=L, start_pos=start_pos, zero_state=zero_state)
    return pl.pallas_call(
        kern,
        grid=grid,
        in_specs=in_specs,
        out_specs=(x_spec, c_spec, n_spec, m_spec, buf_spec),
        out_shape=out_shape,
        scratch_shapes=[
            pltpu.VMEM((R, IN_COLS_PAD), F32),
            pltpu.VMEM((R, GATE_COLS), F32),
            pltpu.VMEM((R, GATE_COLS), F32),
            pltpu.VMEM((R, GATE_COLS), F32),
            pltpu.VMEM((HEADS, DH, 2 * DH), F32),
            pltpu.VMEM((R, D_MODEL), F32),
            pltpu.VMEM((BB, POOL_PAD + TT, POOL_WIDTH), F32),
        ],
        compiler_params=pltpu.CompilerParams(
            dimension_semantics=("arbitrary", "arbitrary"), vmem_limit_bytes=VMEM_LIMIT),
        name="mixer_zero" if zero_state else "mixer_state",
    )(*args)


def _first_index_of_max(work, idx, n, axis):
    mx = jnp.max(work, axis=axis, keepdims=True)
    return jnp.min(jnp.where(work == mx, idx, float(n)), axis=axis, keepdims=True)


def _route(scores_t, bias_t):
    tm = scores_t.shape[1]
    biased = scores_t + bias_t
    b3 = biased.reshape(N_GROUPS, GROUP_SIZE, tm)
    sub = lax.broadcasted_iota(jnp.int32, b3.shape, 1).astype(F32)
    m1 = jnp.max(b3, axis=1, keepdims=True)
    first = jnp.min(jnp.where(b3 == m1, sub, float(GROUP_SIZE)), axis=1, keepdims=True)
    m2 = jnp.max(jnp.where(sub == first, NEG_INF, b3), axis=1, keepdims=True)
    gs = (m1 + m2).reshape(N_GROUPS, tm)
    gidx = lax.broadcasted_iota(jnp.int32, gs.shape, 0).astype(F32)
    gsel = jnp.zeros(gs.shape, F32)
    work = gs
    for _ in range(TOPK_GROUPS):
        pick = gidx == _first_index_of_max(work, gidx, N_GROUPS, 0)
        gsel = jnp.where(pick, 1.0, gsel)
        work = jnp.where(pick, NEG_INF, work)
    emask = jnp.broadcast_to(gsel.reshape(N_GROUPS, 1, tm), b3.shape).reshape(N_EXPERTS, tm)
    work = jnp.where(emask > 0, biased, NEG_INF)
    eidx = lax.broadcasted_iota(jnp.int32, work.shape, 0).astype(F32)
    mask = jnp.zeros(work.shape, F32)
    for _ in range(TOP_K):
        pick = eidx == _first_index_of_max(work, eidx, N_EXPERTS, 0)
        mask = jnp.where(pick, 1.0, mask)
        work = jnp.where(pick, NEG_INF, work)
    sel = mask * scores_t
    return mask, sel / jnp.sum(sel, axis=0, keepdims=True) * ROUTED_SCALE


def _group_specs(block, split):
    return (pl.BlockSpec(block, lambda i, *_: (jnp.minimum(i, split - 1), 0)),
            pl.BlockSpec(block, lambda i, *_: (jnp.maximum(i - split, 0), 0)))


def _router_kernel(x1a_ref, x1b_ref, gffn_ref, rwt_ref, rbias_ref, w1_ref, w3_ref, w2_ref,
                   xn_ref, rank_ref, gates_ref, cnt_ref, x1s_ref, *, split):
    x1 = jnp.where(pl.program_id(0) < split, x1a_ref[...], x1b_ref[...])
    tm = x1.shape[0]
    xn = _rms(x1, gffn_ref[...])
    xb = xn.astype(BF16)
    xn_ref[...] = xb
    logits_t = lax.dot_general(rwt_ref[...], xn, (((1,), (1,)), ((), ())),
                               preferred_element_type=F32, precision=lax.Precision.HIGHEST)
    mask, gates = _route(jax.nn.sigmoid(logits_t), rbias_ref[...])
    gates_ref[...] = gates
    rt = ROUTE_TILE
    before = (lax.broadcasted_iota(jnp.int32, (rt, rt), 0)
              < lax.broadcasted_iota(jnp.int32, (rt, rt), 1)).astype(BF16)
    for j in range(tm // rt):
        mj = mask[:, j * rt:(j + 1) * rt]
        rank_ref[:, j * rt:(j + 1) * rt] = jnp.where(mj > 0, _dot(mj.astype(BF16), before), -1.0)
        cnt_ref[j * N_EXPERTS:(j + 1) * N_EXPERTS, :] = jnp.broadcast_to(
            jnp.sum(mj, axis=1, keepdims=True), (N_EXPERTS, 128))
    a = _dot(xb, w1_ref[...])
    hsh = (a * jax.nn.sigmoid(a)) * _dot(xb, w3_ref[...])
    x1s_ref[...] = x1 + _dot(hsh.astype(BF16), w2_ref[...])


def _router(x1a, x1b, gffn, rwt, rbias, w1, w3, w2, *, TM):
    N = x1a.shape[0] + x1b.shape[0]
    split = x1a.shape[0] // TM
    tok = pl.BlockSpec((TM, D_MODEL), lambda i: (i, 0))
    per_e = pl.BlockSpec((N_EXPERTS, TM), lambda i: (0, i))
    consts = (gffn, rwt, rbias, w1, w3, w2)
    return pl.pallas_call(
        functools.partial(_router_kernel, split=split),
        grid=(N // TM,),
        in_specs=list(_group_specs((TM, D_MODEL), split)) + [pl.BlockSpec(c.shape, lambda i: (0, 0)) for c in consts],
        out_specs=(tok, per_e, per_e, pl.BlockSpec((TM // ROUTE_TILE * N_EXPERTS, 128), lambda i: (i, 0)), tok),
        out_shape=(jax.ShapeDtypeStruct((N, D_MODEL), BF16),
                   jax.ShapeDtypeStruct((N_EXPERTS, N), F32),
                   jax.ShapeDtypeStruct((N_EXPERTS, N), F32),
                   jax.ShapeDtypeStruct((N // ROUTE_TILE * N_EXPERTS, 128), F32),
                   jax.ShapeDtypeStruct((N, D_MODEL), F32)),
        compiler_params=pltpu.CompilerParams(
            dimension_semantics=("arbitrary",), vmem_limit_bytes=VMEM_LIMIT),
        name="router_shared",
    )(x1a, x1b, *consts)


def _segment_plan(cnt, n_tiles_max):
    seg = (cnt + SEG_ALIGN - 1) // SEG_ALIGN * SEG_ALIGN
    used = jnp.sum(seg, axis=0)
    size = (used + CAP + EXPERT_TILE - 1) // EXPERT_TILE * EXPERT_TILE
    row_end = jnp.cumsum(size)
    row_start = row_end - size
    base = row_start[None, :] + jnp.cumsum(seg, axis=0) - seg
    nwin = jnp.maximum((seg + CAP - 1) // CAP, 1)
    tile_end = row_end // EXPERT_TILE
    n_tiles = tile_end[-1]
    t_ids = jnp.arange(n_tiles_max, dtype=jnp.int32)
    tile_e = jnp.minimum(jnp.sum(t_ids[:, None] >= tile_end[None, :], axis=1), N_EXPERTS - 1)
    i32 = lambda a: a.astype(jnp.int32)
    return (i32(base).reshape(-1), i32(nwin).reshape(-1), i32(row_start + used), i32(row_end),
            i32(tile_e), i32(n_tiles).reshape(1))


def _one_hot_rows(rank_ref, chunk, first_row, values_ref=None):
    tm = rank_ref.shape[1]
    j = (lax.broadcasted_iota(jnp.int32, (CAP, tm), 0) + first_row).astype(F32)
    rows = []
    for k in range(CHUNK_E):
        e = chunk * CHUNK_E + k
        hit = j == rank_ref[e:e + 1, :]
        val = 1.0 if values_ref is None else values_ref[e:e + 1, :]
        rows.append(jnp.where(hit, val, 0.0).astype(BF16))
    return jnp.concatenate(rows, axis=0)


def _window(hbm, base_ref, idx, w):
    start = pl.multiple_of(base_ref[idx] + w * CAP, SEG_ALIGN)
    return hbm.at[pl.ds(start, CAP)]


def _chunk_windows(nwin_ref, tile_idx, c):
    extra = nwin_ref[tile_idx * N_EXPERTS + c * CHUNK_E]
    for k in range(1, CHUNK_E):
        extra = jnp.maximum(extra, nwin_ref[tile_idx * N_EXPERTS + c * CHUNK_E + k])
    return extra


def _tile_windows(nwin_ref, tile_idx):
    extra = _chunk_windows(nwin_ref, tile_idx, 0)
    for c in range(1, N_CHUNKS):
        extra = jnp.maximum(extra, _chunk_windows(nwin_ref, tile_idx, c))
    return extra


def _dispatch_kernel(base_ref, nwin_ref, uend_ref, rend_ref, xn_ref, rank_ref, xs_hbm,
                     stage0, stage1, ostage, zbuf, sem, osem, zsem):
    stage = (stage0, stage1)
    i = pl.program_id(0)
    last = pl.num_programs(0) - 1

    def window_copy(p, e, tile_idx):
        return pltpu.make_async_copy(stage[p].at[pl.ds(e * CAP, CAP)],
                                     _window(xs_hbm, base_ref, tile_idx * N_EXPERTS + e, 0), sem.at[p])

    def wait_tile(p, tile_idx):
        for e in range(N_EXPERTS):
            window_copy(p, e, tile_idx).wait()

    @pl.when(i == 0)
    def _zero_tails():
        zbuf[...] = jnp.zeros(zbuf.shape, BF16)
        sizes = []
        size = EXPERT_TILE
        while size >= SEG_ALIGN:
            sizes.append(size)
            size //= 2
        pieces = []
        for e in range(N_EXPERTS):
            pos = uend_ref[e]
            length = rend_ref[e] - pos
            for size in sizes:
                take = (length & size) != 0
                cp = pltpu.make_async_copy(zbuf.at[pl.ds(0, size)],
                                           xs_hbm.at[pl.ds(pl.multiple_of(pos, SEG_ALIGN), size)], zsem)
                pieces.append((take, cp))
                pos = pos + jnp.where(take, size, 0)
        for take, cp in pieces:
            pl.when(take)(cp.start)
        for take, cp in pieces:
            pl.when(take)(cp.wait)

    for p in (0, 1):
        @pl.when(i % 2 == p)
        def _step(p=p):
            xb = xn_ref[...]
            rows = CHUNK_E * CAP
            for c in range(N_CHUNKS):
                stage[p][pl.ds(c * rows, rows), :] = _dot(_one_hot_rows(rank_ref, c, 0), xb).astype(BF16)

            @pl.when(i >= 1)
            def _():
                wait_tile(1 - p, i - 1)

            for e in range(N_EXPERTS):
                window_copy(p, e, i).start()

    extra = _tile_windows(nwin_ref, i)

    @pl.when(extra > 1)
    def _long_segments():
        def body(w, carry):
            for c in range(N_CHUNKS):
                @pl.when(w < _chunk_windows(nwin_ref, i, c))
                def _(c=c):
                    ostage[...] = _dot(_one_hot_rows(rank_ref, c, w * CAP), xn_ref[...]).astype(BF16)
                    for k in range(CHUNK_E):
                        idx = i * N_EXPERTS + c * CHUNK_E + k

                        @pl.when(w < nwin_ref[idx])
                        def _(k=k, idx=idx):
                            cp = pltpu.make_async_copy(ostage.at[pl.ds(k * CAP, CAP)],
                                                       _window(xs_hbm, base_ref, idx, w), osem)
                            cp.start()
                            cp.wait()
            return carry

        lax.fori_loop(1, extra, body, 0)

    for p in (0, 1):
        @pl.when((i == last) & (i % 2 == p))
        def _drain(p=p):
            wait_tile(p, i)


def _dispatch(xn, rank_t, base, nwin, used_end, row_end, *, n_rows, TM):
    N = xn.shape[0]
    stage = pltpu.VMEM((N_EXPERTS * CAP, D_MODEL), BF16)
    return pl.pallas_call(
        _dispatch_kernel,
        grid_spec=pltpu.PrefetchScalarGridSpec(
            num_scalar_prefetch=4,
            grid=(N // TM,),
            in_specs=[pl.BlockSpec((TM, D_MODEL), lambda i, *_: (i, 0)),
                      pl.BlockSpec((N_EXPERTS, TM), lambda i, *_: (0, i))],
            out_specs=pl.BlockSpec(memory_space=pl.ANY),
            scratch_shapes=[
                stage, stage,
                pltpu.VMEM((CHUNK_E * CAP, D_MODEL), BF16),
                pltpu.VMEM((EXPERT_TILE, D_MODEL), BF16),
                pltpu.SemaphoreType.DMA((2,)),
                pltpu.SemaphoreType.DMA(()),
                pltpu.SemaphoreType.DMA(()),
            ],
        ),
        out_shape=jax.ShapeDtypeStruct((n_rows, D_MODEL), BF16),
        compiler_params=pltpu.CompilerParams(
            dimension_semantics=("arbitrary",), vmem_limit_bytes=VMEM_LIMIT),
        name="moe_dispatch",
    )(base, nwin, used_end, row_end, xn, rank_t)


def _expert_kernel(tile_e_ref, n_tiles_ref, xs_ref, w1_hbm, w3_hbm, w2_hbm, ys_ref,
                   w1f0, w1f1, w3f0, w3f1, w2f0, w2f1, w1b, w3b, w2b, wsem):
    t = pl.program_id(0)
    valid = t < n_tiles_ref[0]
    e = tile_e_ref[t]
    first_tile = valid & ((t == 0) | (e != tile_e_ref[jnp.maximum(t - 1, 0)]))
    f32_bufs = ((w1f0, w3f0, w2f0), (w1f1, w3f1, w2f1))

    def weight_copies(expert, p):
        return [pltpu.make_async_copy(hbm.at[expert], buf, wsem.at[p, j])
                for j, (hbm, buf) in enumerate(zip((w1_hbm, w3_hbm, w2_hbm), f32_bufs[p]))]

    for p in (0, 1):
        @pl.when(first_tile & (e % 2 == p))
        def _next_expert(p=p):
            @pl.when(t == 0)
            def _():
                for cp in weight_copies(e, p):
                    cp.start()

            for cp in weight_copies(e, p):
                cp.wait()

            @pl.when(e + 1 < N_EXPERTS)
            def _():
                for cp in weight_copies(e + 1, 1 - p):
                    cp.start()

            w1b[...] = f32_bufs[p][0][...].astype(BF16)
            w3b[...] = f32_bufs[p][1][...].astype(BF16)
            w2b[...] = f32_bufs[p][2][...].astype(BF16)

    @pl.when(valid)
    def _compute():
        xb = xs_ref[...]
        a = _dot(xb, w1b[...])
        hb = (a * jax.nn.sigmoid(a)) * _dot(xb, w3b[...])
        ys_ref[...] = _dot(hb.astype(BF16), w2b[...]).astype(BF16)

    @pl.when(t == n_tiles_ref[0])
    def _spare():
        ys_ref[...] = jnp.zeros(ys_ref.shape, BF16)


def _experts(xs, tile_e, n_tiles, w1, w3, w2):
    t_max = xs.shape[0] // EXPERT_TILE
    clamp = lambda t, nt: jnp.minimum(t, nt[0] - 1)
    any_spec = pl.BlockSpec(memory_space=pl.ANY)
    w_in = pltpu.VMEM((D_MODEL, EXPERT_FF), F32)
    w_out = pltpu.VMEM((EXPERT_FF, D_MODEL), F32)
    return pl.pallas_call(
        _expert_kernel,
        grid_spec=pltpu.PrefetchScalarGridSpec(
            num_scalar_prefetch=2,
            grid=(t_max,),
            in_specs=[pl.BlockSpec((EXPERT_TILE, D_MODEL), lambda t, te, nt: (clamp(t, nt), 0)),
                      any_spec, any_spec, any_spec],
            out_specs=pl.BlockSpec((EXPERT_TILE, D_MODEL),
                                   lambda t, te, nt: (jnp.where(t < nt[0], t, t_max), 0)),
            scratch_shapes=[
                w_in, w_in, w_in, w_in, w_out, w_out,
                pltpu.VMEM((D_MODEL, EXPERT_FF), BF16),
                pltpu.VMEM((D_MODEL, EXPERT_FF), BF16),
                pltpu.VMEM((EXPERT_FF, D_MODEL), BF16),
                pltpu.SemaphoreType.DMA((2, 3)),
            ],
        ),
        out_shape=jax.ShapeDtypeStruct(((t_max + 1) * EXPERT_TILE, D_MODEL), BF16),
        compiler_params=pltpu.CompilerParams(
            dimension_semantics=("arbitrary",), vmem_limit_bytes=VMEM_LIMIT),
        name="moe_experts",
    )(tile_e, n_tiles, xs, w1, w3, w2)


def _final_kernel(base_ref, nwin_ref, x1s_ref, rank_ref, gates_ref, ys_hbm, pa_ref, pb_ref, gple_ref,
                  wgate_ref, wproj_ref, gfin_ref, ya_ref, yb_ref, win0, win1, owin, acc_ref, sem, osem,
                  *, split):
    i = pl.program_id(0)
    last = pl.num_programs(0) - 1
    win = (win0, win1)

    def window_copy(p, e, tile_idx):
        return pltpu.make_async_copy(_window(ys_hbm, base_ref, tile_idx * N_EXPERTS + e, 0),
                                     win[p].at[pl.ds(e * CAP, CAP)], sem.at[p])

    @pl.when(i == 0)
    def _first():
        owin[...] = jnp.zeros(owin.shape, BF16)
        for e in range(N_EXPERTS):
            window_copy(0, e, i).start()

    contract0 = (((0,), (0,)), ((), ()))

    for p in (0, 1):
        @pl.when(i % 2 == p)
        def _step(p=p):
            for e in range(N_EXPERTS):
                window_copy(p, e, i).wait()

            @pl.when(i < last)
            def _():
                for e in range(N_EXPERTS):
                    window_copy(1 - p, e, i + 1).start()

            one_hot = jnp.concatenate([_one_hot_rows(rank_ref, c, 0, gates_ref) for c in range(N_CHUNKS)], axis=0)
            acc_ref[...] = lax.dot_general(one_hot, win[p][...], contract0, preferred_element_type=F32)

    extra = _tile_windows(nwin_ref, i)

    @pl.when(extra > 1)
    def _long_segments():
        def body(w, carry):
            for c in range(N_CHUNKS):
                @pl.when(w < _chunk_windows(nwin_ref, i, c))
                def _(c=c):
                    for k in range(CHUNK_E):
                        idx = i * N_EXPERTS + c * CHUNK_E + k

                        @pl.when(w < nwin_ref[idx])
                        def _(k=k, idx=idx):
                            cp = pltpu.make_async_copy(_window(ys_hbm, base_ref, idx, w),
                                                       owin.at[pl.ds(k * CAP, CAP)], osem)
                            cp.start()
                            cp.wait()
                    acc_ref[...] += lax.dot_general(_one_hot_rows(rank_ref, c, w * CAP, gates_ref), owin[...],
                                                    contract0, preferred_element_type=F32)
            return carry

        lax.fori_loop(1, extra, body, 0)

    x2 = x1s_ref[...] + acc_ref[...]
    r_ = _rms(x2, gple_ref[...])
    gate = jax.nn.sigmoid(_dot(r_.astype(BF16), wgate_ref[...]))
    pt = jnp.where(i < split, pa_ref[...], pb_ref[...])
    x3 = x2 + _dot(pt.astype(BF16), wproj_ref[...]) * gate
    y = _rms(x3, gfin_ref[...])

    @pl.when(i < split)
    def _():
        ya_ref[...] = y

    @pl.when(i >= split)
    def _():
        yb_ref[...] = y


def _final(x1s, rank_t, gates_t, ys, pa, pb, gple, wgate, wproj, gfin, base, nwin, *, TM):
    N = x1s.shape[0]
    split = pa.shape[0] // TM
    tok = pl.BlockSpec((TM, D_MODEL), lambda i, *_: (i, 0))
    per_e = pl.BlockSpec((N_EXPERTS, TM), lambda i, *_: (0, i))
    consts = (gple, wgate, wproj, gfin)
    win = pltpu.VMEM((N_EXPERTS * CAP, D_MODEL), BF16)
    return pl.pallas_call(
        functools.partial(_final_kernel, split=split),
        grid_spec=pltpu.PrefetchScalarGridSpec(
            num_scalar_prefetch=2,
            grid=(N // TM,),
            in_specs=[tok, per_e, per_e, pl.BlockSpec(memory_space=pl.ANY)]
            + list(_group_specs((TM, PLE_DIM), split))
            + [pl.BlockSpec(c.shape, lambda i, *_: (0, 0)) for c in consts],
            out_specs=_group_specs((TM, D_MODEL), split),
            scratch_shapes=[win, win, pltpu.VMEM((CHUNK_E * CAP, D_MODEL), BF16), pltpu.VMEM((TM, D_MODEL), F32),
                            pltpu.SemaphoreType.DMA((2,)), pltpu.SemaphoreType.DMA(())],
        ),
        out_shape=(jax.ShapeDtypeStruct((pa.shape[0], D_MODEL), F32),
                   jax.ShapeDtypeStruct((pb.shape[0], D_MODEL), F32)),
        compiler_params=pltpu.CompilerParams(
            dimension_semantics=("arbitrary",), vmem_limit_bytes=VMEM_LIMIT),
        name="combine_ple_final",
    )(base, nwin, x1s, rank_t, gates_t, ys, pa, pb, *consts)


def kernel(x_prompt, x_sample, p_prompt, p_sample, state_C, state_n, state_m, state_pool, norm_mix_g, w_in, b_igate, b_fgate, head_norm_g, w_pool, pool_scale, w_out, norm_ffn_g, router_w, router_bias, ex_w1, ex_w3, ex_w2, sh_w1, sh_w3, sh_w2, norm_ple_g, w_ple_gate, w_ple_proj, final_norm_g):
    depth = norm_mix_g.shape[0]
    assert depth == 1
    l = 0
    B, T, _ = x_prompt.shape
    Bs, Ts, _ = x_sample.shape
    g0 = 4 * MLSTM_WIDTH
    w = w_in[l]
    lane_pad = jnp.zeros((D_MODEL, 128 - HEADS), F32)
    wcat = jnp.concatenate(
        [w[:, :g0], w[:, g0 + 2 * HEADS:], w[:, g0:g0 + HEADS], lane_pad,
         w[:, g0 + HEADS:g0 + 2 * HEADS], lane_pad], axis=1).astype(BF16)
    bias_pad = jnp.zeros((128 - HEADS,), F32)
    gbias = jnp.concatenate([b_igate[l], bias_pad, b_fgate[l], bias_pad])[None, :]
    mixer_w = (wcat, norm_mix_g[l][None, :], gbias, head_norm_g[l][None, :], w_pool[l].astype(BF16),
               pool_scale[l][None, :], w_out[l].astype(BF16))

    x1p, Cp, Np, Mp, Bp = _mixer(x_prompt, *mixer_w, None, BB=1, TT=512, L=128, start_pos=0)
    state = (state_C[l], state_n[l], state_m[l][..., None], state_pool[l])
    x1s_, Cs, Ns, Ms, Bs_ = _mixer(x_sample, *mixer_w, state, BB=16, TT=Ts, L=Ts, start_pos=PAST_LEN)

    N = B * T + Bs * Ts
    assert (B * T) % ROUTE_TILE == 0 and (Bs * Ts) % ROUTE_TILE == 0
    n_route_tiles = N // ROUTE_TILE

    xn, rank_t, gates_t, cnt, x1sh = _router(
        x1p.reshape(B * T, D_MODEL), x1s_.reshape(Bs * Ts, D_MODEL),
        norm_ffn_g[l][None, :], router_w[l].T, router_bias[l][:, None],
        sh_w1[l].astype(BF16), sh_w3[l].astype(BF16), sh_w2[l].astype(BF16), TM=2 * ROUTE_TILE)

    max_rows = TOP_K * N + (SEG_ALIGN - 1) * n_route_tiles * N_EXPERTS + N_EXPERTS * (CAP + EXPERT_TILE)
    t_max = -(-max_rows // EXPERT_TILE)
    cnt = cnt[:, 0].reshape(n_route_tiles, N_EXPERTS).astype(jnp.int32)
    base, nwin, used_end, row_end, tile_e, n_tiles = _segment_plan(cnt, t_max)

    xs = _dispatch(xn, rank_t, base, nwin, used_end, row_end, n_rows=t_max * EXPERT_TILE, TM=ROUTE_TILE)
    ys = _experts(xs, tile_e, n_tiles, ex_w1[l], ex_w3[l], ex_w2[l])
    y_prompt, y_sample = _final(
        x1sh, rank_t, gates_t, ys, p_prompt[l].reshape(B * T, PLE_DIM), p_sample[l].reshape(Bs * Ts, PLE_DIM),
        norm_ple_g[l][None, :], w_ple_gate[l].astype(BF16), w_ple_proj[l].astype(BF16),
        final_norm_g[None, :], base, nwin, TM=ROUTE_TILE)
    return (y_prompt.reshape(B, T, D_MODEL), y_sample.reshape(Bs, Ts, D_MODEL),
            Cp[None], Np[None], Mp[..., 0][None], Bp[None],
            Cs[None], Ns[None], Ms[..., 0][None], Bs_[None])
```

```python
import functools

import jax
import jax.numpy as jnp
from jax import lax
from jax.experimental import pallas as pl
from jax.experimental.pallas import tpu as pltpu

D_MODEL = 1024
HEADS = 4
DH = 128
MLSTM_WIDTH = HEADS * DH
POOL_WIDTH = 512
POOL_WINDOWS = (2, 4, 8, 16)
POOL_GDIM = 128
POOL_BUF = 15
POOL_PAD = 16
N_EXPERTS = 64
TOP_K = 8
N_GROUPS = 8
GROUP_SIZE = N_EXPERTS // N_GROUPS
TOPK_GROUPS = 4
EXPERT_FF = 256
ROUTED_SCALE = 2.5
NORM_EPS = 1e-6
PLE_DIM = 256
PAST_LEN = 16384

COL_Q, COL_K, COL_V, COL_O, COL_U, COL_I, COL_F = 0, 512, 1024, 1536, 2048, 2560, 2688
IN_COLS_PAD = 2816
GATE_COLS = HEADS * 128

ROUTE_TILE = 256
EXPERT_TILE = 1024
SEG_ALIGN = 16
CAP = 48
CHUNK_E = 8
N_CHUNKS = N_EXPERTS // CHUNK_E

VMEM_LIMIT = 56 * 1024 * 1024
F32 = jnp.float32
BF16 = jnp.bfloat16
NEG_INF = float("-inf")


def _rms(x, g):
    return x * lax.rsqrt(jnp.mean(x * x, axis=-1, keepdims=True) + NORM_EPS) * g


def _log_sigmoid(x):
    return jnp.minimum(x, 0.0) - jnp.log1p(jnp.exp(-jnp.abs(x)))


def _dot(a, b):
    return jnp.dot(a, b, preferred_element_type=F32)


def _mlstm_chunk(q, k, v, b_col, i_col, C, n, m, causal, eye):
    L = q.shape[0]
    r_col = i_col - b_col
    r_row = jnp.sum(jnp.where(eye, r_col, 0.0), axis=0, keepdims=True)
    d = jnp.where(causal, b_col + r_row, NEG_INF)
    inter = b_col + m
    qb, kb, vb = q.astype(BF16), k.astype(BF16), v.astype(BF16)
    qk = lax.dot_general(qb, kb, (((1,), (1,)), ((), ())), preferred_element_type=F32)
    qc = _dot(qb, C.astype(BF16))
    qn = jnp.sum(q * n, axis=-1, keepdims=True)
    yield
    m_t = jnp.maximum(inter, jnp.max(d, axis=-1, keepdims=True))
    b_last = b_col[L - 1:L, :]
    m_new = jnp.maximum(b_last + m, jnp.max(b_last + r_row, axis=-1, keepdims=True))
    yield
    w_inter = jnp.exp(inter - m_t)
    s = qk * jnp.exp(d - m_t)
    fw = jnp.exp(b_last + m - m_new)
    iw_col = jnp.exp(b_last + r_col - m_new)
    kw = iw_col * k
    yield
    sv = _dot(s.astype(BF16), vb)
    kv = lax.dot_general(kw.astype(BF16), vb, (((0,), (0,)), ((), ())), preferred_element_type=F32)
    ssum = jnp.sum(s, axis=-1, keepdims=True)
    n_new = fw * n + jnp.sum(kw, axis=0, keepdims=True)
    yield
    num = w_inter * qc + sv
    nq = w_inter * qn + ssum
    h = num / jnp.maximum(jnp.abs(nq), jnp.exp(-m_t))
    C_new = fw * C + kv
    return h, C_new, n_new, m_new


def _interleave(gens):
    results = [None] * len(gens)
    live = list(enumerate(gens))
    while live:
        still = []
        for idx, g in live:
            try:
                next(g)
                still.append((idx, g))
            except StopIteration as stop:
                results[idx] = stop.value
        live = still
    return results


def _mlstm_chunk_square(q, k, v, r, b, mp, CN, m, causal, eye, ones):
    L = q.shape[0]
    r_row = jnp.sum(jnp.where(eye, r, 0.0), axis=0, keepdims=True)
    g = jnp.maximum(mp, m)
    qb, kb = q.astype(BF16), k.astype(BF16)
    v1 = jnp.concatenate([v.astype(BF16), ones], axis=1)
    qk = lax.dot_general(qb, kb, (((1,), (1,)), ((), ())), preferred_element_type=F32)
    qcn = _dot(qb, CN.astype(BF16))
    yield
    w_inter = jnp.exp(m - g)
    p = jnp.where(causal, jnp.exp(r_row - g), 0.0)
    g_last = g[L - 1:L, :]
    m_new = b[L - 1:L, :] + g_last
    fw = jnp.exp(m - g_last)
    kw = (jnp.exp(r - g_last) * k).astype(BF16)
    floor = jnp.exp(-(b + g))
    yield
    s = (qk * p).astype(BF16)
    sv = _dot(s, v1)
    kv = lax.dot_general(kw, v1, (((0,), (0,)), ((), ())), preferred_element_type=F32)
    yield
    num = w_inter * qcn[:, :DH] + sv[:, :DH]
    nq = w_inter * qcn[:, DH:] + sv[:, DH:]
    h = num / jnp.maximum(jnp.abs(nq), floor)
    CN_new = jnp.concatenate([fw, fw], axis=1) * CN + kv
    return h, CN_new, m_new[:, 0:1]


def _chunk_scan(x, L, op, fill):
    pos = lax.broadcasted_iota(jnp.int32, x.shape, 0) & (L - 1)
    k = 1
    while k < L:
        x = op(x, jnp.where(pos >= k, pltpu.roll(x, k, axis=0), fill))
        k *= 2
        yield
    return x


def _advance(gens, steps):
    for g in list(gens):
        for _ in range(steps):
            try:
                next(g)
            except StopIteration:
                gens.remove(g)
                break


def _mixer_kernel(*refs, BB, TT, L, start_pos, zero_state):
    if zero_state:
        (x_ref, wcat_ref, gmix_ref, gbias_ref, hng_ref, wpool_ref, pscale_ref, wout_ref,
         x1_ref, c_ref, n_ref, m_ref, buf_ref,
         z_ref, a_ref, b_ref, mp_ref, cn_ref, mix_ref, ext_ref) = refs
    else:
        (x_ref, wcat_ref, gmix_ref, gbias_ref, hng_ref, wpool_ref, pscale_ref, wout_ref,
         c0_ref, n0_ref, m0_ref, buf0_ref,
         x1_ref, c_ref, n_ref, m_ref, buf_ref,
         z_ref, a_ref, b_ref, mp_ref, cn_ref, mix_ref, ext_ref) = refs
    t = pl.program_id(1)
    R = BB * TT
    n_chunks = TT // L
    square = L == DH
    assert not square or (zero_state and BB == 1)

    @pl.when(t == 0)
    def _init():
        ext_ref[:, 0:POOL_PAD, :] = jnp.zeros((BB, POOL_PAD, POOL_WIDTH), F32)
        if zero_state:
            cn_ref[...] = jnp.zeros(cn_ref.shape, F32)
            c_ref[...] = jnp.zeros(c_ref.shape, F32)
            n_ref[...] = jnp.zeros(n_ref.shape, F32)
            m_ref[...] = jnp.zeros(m_ref.shape, F32)
        else:
            c_ref[...] = c0_ref[...]
            n_ref[...] = n0_ref[...]
            m_ref[...] = m0_ref[...]
            ext_ref[:, 1:POOL_PAD, :] = buf0_ref[...]

    hb = _rms(x_ref[...].reshape(R, D_MODEL), gmix_ref[...]).astype(BF16)

    def project(lo, hi):
        z_ref[:, lo:hi] = _dot(hb, wcat_ref[:, lo:hi])

    def gate_stages():
        gbias = gbias_ref[...]
        gi = z_ref[:, COL_I:COL_I + 128] + gbias[:, :128]
        lf = _log_sigmoid(z_ref[:, COL_F:COL_F + 128] + gbias[:, 128:])
        yield
        b_c = yield from _chunk_scan(lf, L, jnp.add, 0.0)
        if square:
            r_c = gi - b_c
            mp_c = yield from _chunk_scan(r_c, L, jnp.maximum, NEG_INF)
            for hd in range(HEADS):
                lanes = slice(hd * DH, (hd + 1) * DH)
                a_ref[:, lanes] = jnp.broadcast_to(r_c[:, hd:hd + 1], (R, DH))
                b_ref[:, lanes] = jnp.broadcast_to(b_c[:, hd:hd + 1], (R, DH))
                mp_ref[:, lanes] = jnp.broadcast_to(mp_c[:, hd:hd + 1], (R, DH))
                yield
        else:
            a_ref[:, 0:128] = gi
            b_ref[:, 0:128] = b_c

    def pool_stages():
        ext_ref[:, POOL_PAD:POOL_PAD + TT, :] = z_ref[:, COL_U:COL_U + POOL_WIDTH].reshape(BB, TT, POOL_WIDTH)
        pos = start_pos + t * TT + lax.broadcasted_iota(jnp.int32, (1, TT, 1), 1)
        pscale = pscale_ref[...]
        yield
        for gidx, w in enumerate(POOL_WINDOWS):
            lanes = slice(gidx * POOL_GDIM, (gidx + 1) * POOL_GDIM)
            u_g = ext_ref[:, POOL_PAD:POOL_PAD + TT, lanes]
            acc = u_g
            for j in range(1, w):
                acc = acc + ext_ref[:, POOL_PAD - j:POOL_PAD - j + TT, lanes]
                if j % 4 == 3:
                    yield
            cnt = jnp.minimum(pos + 1, w).astype(F32)
            pooled = (acc / cnt - u_g).reshape(R, POOL_GDIM)
            mixed = _dot(pooled.astype(BF16), wpool_ref[gidx]) * pscale[:, lanes]
            mix_ref[:, MLSTM_WIDTH + gidx * POOL_GDIM:MLSTM_WIDTH + (gidx + 1) * POOL_GDIM] = mixed
            yield
        new_buf = ext_ref[:, TT + 1:TT + POOL_PAD, :]
        buf_ref[...] = new_buf
        ext_ref[:, 1:POOL_PAD, :] = new_buf

    project(COL_I, IN_COLS_PAD)
    project(COL_U, COL_U + 256)
    project(COL_U + 256, COL_I)
    side = [gate_stages(), pool_stages()]
    for lo in range(0, COL_U, 256):
        project(lo, lo + 256)
        _advance(side, 3)
    while side:
        _advance(side, 1)

    row = lax.broadcasted_iota(jnp.int32, (L, L), 0)
    col = lax.broadcasted_iota(jnp.int32, (L, L), 1)
    causal = row >= col
    eye = row == col
    ones = jnp.ones((DH, DH), BF16)
    hng = hng_ref[...]

    group = 4 if (not square and n_chunks == 1 and BB % 4 == 0) else 1

    def chunk_body(s, carry):
        work = []
        for j in range(group):
            sj = s * group + j
            rows = pl.ds(pl.multiple_of(sj * L, L), L)
            bb = 0 if BB == 1 else sj // n_chunks
            for hd in range(HEADS):
                lanes = slice(hd * DH, (hd + 1) * DH)
                q = z_ref[rows, COL_Q + hd * DH:COL_Q + (hd + 1) * DH] * (DH ** -0.5)
                k = z_ref[rows, COL_K + hd * DH:COL_K + (hd + 1) * DH]
                v = z_ref[rows, COL_V + hd * DH:COL_V + (hd + 1) * DH]
                o = z_ref[rows, COL_O + hd * DH:COL_O + (hd + 1) * DH]
                m = m_ref[bb, hd:hd + 1, :]
                if square:
                    state = (a_ref[rows, lanes], b_ref[rows, lanes], mp_ref[rows, lanes], cn_ref[hd])
                else:
                    state = (b_ref[rows, hd:hd + 1], a_ref[rows, hd:hd + 1], c_ref[bb, hd],
                             n_ref[bb, hd:hd + 1, :])
                work.append((rows, bb, hd, lanes, q, k, v, o, m, state))
        if square:
            results = _interleave([_mlstm_chunk_square(q, k, v, *state, m, causal, eye, ones)
                                   for _, _, _, _, q, k, v, _, m, state in work])
        else:
            results = _interleave([_mlstm_chunk(q, k, v, *state, m, causal, eye)
                                   for _, _, _, _, q, k, v, _, m, state in work])
        scales = [lax.rsqrt(jnp.mean(res[0] * res[0], axis=-1, keepdims=True) + NORM_EPS) for res in results]
        done = []
        for (rows, bb, hd, lanes, _, _, _, o, _, _), res, scale in zip(work, results, scales):
            out = res[0] * scale * hng[:, lanes] * jax.nn.sigmoid(o)
            done.append((rows, bb, hd, lanes, out, res[-1], res[1:-1]))
        for rows, bb, hd, lanes, out, m_new, new_state in done:
            mix_ref[rows, lanes] = out
            m_ref[bb, hd:hd + 1, :] = m_new
            if square:
                cn_ref[hd] = new_state[0]
            else:
                c_ref[bb, hd] = new_state[0]
                n_ref[bb, hd:hd + 1, :] = new_state[1]
        return carry

    lax.fori_loop(0, BB * n_chunks // group, chunk_body, 0)

    if square:
        @pl.when(t == pl.num_programs(1) - 1)
        def _emit_state():
            for hd in range(HEADS):
                cn = cn_ref[hd]
                c_ref[0, hd] = cn[:, :DH]
                n_ref[0, hd:hd + 1, :] = jnp.sum(jnp.where(eye, cn[:, DH:], 0.0), axis=0, keepdims=True)

    out = x_ref[...].reshape(R, D_MODEL) + _dot(mix_ref[...].astype(BF16), wout_ref[...])
    x1_ref[...] = out.reshape(BB, TT, D_MODEL)


def _const_spec(shape):
    nd = len(shape)
    return pl.BlockSpec(shape, lambda b, t, _nd=nd: (0,) * _nd, pipeline_mode=pl.Buffered(1))


def _mixer(x, wcat, gmix, gbias, hng, wpool, pscale, wout, state, *, BB, TT, L, start_pos):
    B, T, _ = x.shape
    zero_state = state is None
    R = BB * TT
    grid = (B // BB, T // TT)
    x_spec = pl.BlockSpec((BB, TT, D_MODEL), lambda b, t: (b, t, 0))
    c_spec = pl.BlockSpec((BB, HEADS, DH, DH), lambda b, t: (b, 0, 0, 0))
    n_spec = pl.BlockSpec((BB, HEADS, DH), lambda b, t: (b, 0, 0))
    m_spec = pl.BlockSpec((BB, HEADS, 1), lambda b, t: (b, 0, 0))
    buf_spec = pl.BlockSpec((BB, POOL_BUF, POOL_WIDTH), lambda b, t: (b, 0, 0))
    weights = (wcat, gmix, gbias, hng, wpool, pscale, wout)
    in_specs = [x_spec] + [_const_spec(w.shape) for w in weights]
    args = [x, *weights]
    if not zero_state:
        in_specs += [c_spec, n_spec, m_spec, buf_spec]
        args += list(state)
    out_shape = (
        jax.ShapeDtypeStruct((B, T, D_MODEL), F32),
        jax.ShapeDtypeStruct((B, HEADS, DH, DH), F32),
        jax.ShapeDtypeStruct((B, HEADS, DH), F32),
        jax.ShapeDtypeStruct((B, HEADS, 1), F32),
        jax.ShapeDtypeStruct((B, POOL_BUF, POOL_WIDTH), F32),
    )
    kern = functools.partial(_mixer_kernel, BB=BB, TT=TT, L=L, start_pos=start_pos, zero_state=zero_state)
    return pl.pallas_call(
        kern,
        grid=grid,
        in_specs=in_specs,
        out_specs=(x_spec, c_spec, n_spec, m_spec, buf_spec),
        out_shape=out_shape,
        scratch_shapes=[
            pltpu.VMEM((R, IN_COLS_PAD), F32),
            pltpu.VMEM((R, GATE_COLS), F32),
            pltpu.VMEM((R, GATE_COLS), F32),
            pltpu.VMEM((R, GATE_COLS), F32),
            pltpu.VMEM((HEADS, DH, 2 * DH), F32),
            pltpu.VMEM((R, D_MODEL), F32),
            pltpu.VMEM((BB, POOL_PAD + TT, POOL_WIDTH), F32),
        ],
        compiler_params=pltpu.CompilerParams(
            dimension_semantics=("arbitrary", "arbitrary"), vmem_limit_bytes=VMEM_LIMIT),
        name="mixer_zero" if zero_state else "mixer_state",
    )(*args)


def _first_index_of_max(work, idx, n, axis):
    mx = jnp.max(work, axis=axis, keepdims=True)
    return jnp.min(jnp.where(work == mx, idx, float(n)), axis=axis, keepdims=True)


def _route(scores_t, bias_t):
    tm = scores_t.shape[1]
    biased = scores_t + bias_t
    b3 = biased.reshape(N_GROUPS, GROUP_SIZE, tm)
    sub = lax.broadcasted_iota(jnp.int32, b3.shape, 1).astype(F32)
    m1 = jnp.max(b3, axis=1, keepdims=True)
    first = jnp.min(jnp.where(b3 == m1, sub, float(GROUP_SIZE)), axis=1, keepdims=True)
    m2 = jnp.max(jnp.where(sub == first, NEG_INF, b3), axis=1, keepdims=True)
    gs = (m1 + m2).reshape(N_GROUPS, tm)
    gidx = lax.broadcasted_iota(jnp.int32, gs.shape, 0).astype(F32)
    gsel = jnp.zeros(gs.shape, F32)
    work = gs
    yield
    for _ in range(TOPK_GROUPS):
        pick = gidx == _first_index_of_max(work, gidx, N_GROUPS, 0)
        gsel = jnp.where(pick, 1.0, gsel)
        work = jnp.where(pick, NEG_INF, work)
    emask = jnp.broadcast_to(gsel.reshape(N_GROUPS, 1, tm), b3.shape).reshape(N_EXPERTS, tm)
    work = jnp.where(emask > 0, biased, NEG_INF)
    eidx = lax.broadcasted_iota(jnp.int32, work.shape, 0).astype(F32)
    mask = jnp.zeros(work.shape, F32)
    yield
    for _ in range(TOP_K):
        pick = eidx == _first_index_of_max(work, eidx, N_EXPERTS, 0)
        mask = jnp.where(pick, 1.0, mask)
        work = jnp.where(pick, NEG_INF, work)
        yield
    sel = mask * scores_t
    return mask, sel / jnp.sum(sel, axis=0, keepdims=True) * ROUTED_SCALE


def _group_specs(block, split):
    return (pl.BlockSpec(block, lambda i, *_: (jnp.minimum(i, split - 1), 0)),
            pl.BlockSpec(block, lambda i, *_: (jnp.maximum(i - split, 0), 0)))


def _router_kernel(x1a_ref, x1b_ref, gffn_ref, rwt_ref, rbias_ref, w1_ref, w3_ref, w2_ref,
                   xn_ref, rank_ref, gates_ref, cnt_ref, x1s_ref, *, split):
    x1 = jnp.where(pl.program_id(0) < split, x1a_ref[...], x1b_ref[...])
    tm = x1.shape[0]
    xn = _rms(x1, gffn_ref[...])
    xb = xn.astype(BF16)
    xn_ref[...] = xb
    logits_t = lax.dot_general(rwt_ref[...], xn, (((1,), (1,)), ((), ())),
                               preferred_element_type=F32, precision=lax.Precision.HIGHEST)
    def routing():
        mask, gates = yield from _route(jax.nn.sigmoid(logits_t), rbias_ref[...])
        gates_ref[...] = gates
        rt = ROUTE_TILE
        before = (lax.broadcasted_iota(jnp.int32, (rt, rt), 0)
                  < lax.broadcasted_iota(jnp.int32, (rt, rt), 1)).astype(BF16)
        for j in range(tm // rt):
            mj = mask[:, j * rt:(j + 1) * rt]
            rank_ref[:, j * rt:(j + 1) * rt] = jnp.where(mj > 0, _dot(mj.astype(BF16), before), -1.0)
            cnt_ref[j * N_EXPERTS:(j + 1) * N_EXPERTS, :] = jnp.broadcast_to(
                jnp.sum(mj, axis=1, keepdims=True), (N_EXPERTS, 128))
            yield

    def shared_expert():
        half = D_MODEL // 2
        a = _dot(xb, w1_ref[...])
        yield
        g = _dot(xb, w3_ref[...])
        yield
        hsh = ((a * jax.nn.sigmoid(a)) * g).astype(BF16)
        yield
        x1s_ref[:, :half] = x1[:, :half] + _dot(hsh, w2_ref[:, :half])
        yield
        x1s_ref[:, half:] = x1[:, half:] + _dot(hsh, w2_ref[:, half:])

    side = [routing(), shared_expert()]
    while side:
        _advance(side, 1)


def _router(x1a, x1b, gffn, rwt, rbias, w1, w3, w2, *, TM):
    N = x1a.shape[0] + x1b.shape[0]
    split = x1a.shape[0] // TM
    tok = pl.BlockSpec((TM, D_MODEL), lambda i: (i, 0))
    per_e = pl.BlockSpec((N_EXPERTS, TM), lambda i: (0, i))
    consts = (gffn, rwt, rbias, w1, w3, w2)
    return pl.pallas_call(
        functools.partial(_router_kernel, split=split),
        grid=(N // TM,),
        in_specs=list(_group_specs((TM, D_MODEL), split)) + [pl.BlockSpec(c.shape, lambda i: (0, 0)) for c in consts],
        out_specs=(tok, per_e, per_e, pl.BlockSpec((TM // ROUTE_TILE * N_EXPERTS, 128), lambda i: (i, 0)), tok),
        out_shape=(jax.ShapeDtypeStruct((N, D_MODEL), BF16),
                   jax.ShapeDtypeStruct((N_EXPERTS, N), F32),
                   jax.ShapeDtypeStruct((N_EXPERTS, N), F32),
                   jax.ShapeDtypeStruct((N // ROUTE_TILE * N_EXPERTS, 128), F32),
                   jax.ShapeDtypeStruct((N, D_MODEL), F32)),
        compiler_params=pltpu.CompilerParams(
            dimension_semantics=("arbitrary",), vmem_limit_bytes=VMEM_LIMIT),
        name="router_shared",
    )(x1a, x1b, *consts)


def _segment_plan(cnt, n_tiles_max):
    seg = (cnt + SEG_ALIGN - 1) // SEG_ALIGN * SEG_ALIGN
    used = jnp.sum(seg, axis=0)
    size = (used + CAP + EXPERT_TILE - 1) // EXPERT_TILE * EXPERT_TILE
    row_end = jnp.cumsum(size)
    row_start = row_end - size
    base = row_start[None, :] + jnp.cumsum(seg, axis=0) - seg
    nwin = jnp.maximum((seg + CAP - 1) // CAP, 1)
    tile_end = row_end // EXPERT_TILE
    n_tiles = tile_end[-1]
    t_ids = jnp.arange(n_tiles_max, dtype=jnp.int32)
    tile_e = jnp.minimum(jnp.sum(t_ids[:, None] >= tile_end[None, :], axis=1), N_EXPERTS - 1)
    i32 = lambda a: a.astype(jnp.int32)
    return (i32(base).reshape(-1), i32(nwin).reshape(-1), i32(row_start + used), i32(row_end),
            i32(tile_e), i32(n_tiles).reshape(1))


def _one_hot_rows(rank_ref, chunk, first_row, values_ref=None):
    tm = rank_ref.shape[1]
    j = (lax.broadcasted_iota(jnp.int32, (CAP, tm), 0) + first_row).astype(F32)
    rows = []
    for k in range(CHUNK_E):
        e = chunk * CHUNK_E + k
        hit = j == rank_ref[e:e + 1, :]
        val = 1.0 if values_ref is None else values_ref[e:e + 1, :]
        rows.append(jnp.where(hit, val, 0.0).astype(BF16))
    return jnp.concatenate(rows, axis=0)


def _window(hbm, base_ref, idx, w):
    start = pl.multiple_of(base_ref[idx] + w * CAP, SEG_ALIGN)
    return hbm.at[pl.ds(start, CAP)]


def _chunk_windows(nwin_ref, tile_idx, c):
    extra = nwin_ref[tile_idx * N_EXPERTS + c * CHUNK_E]
    for k in range(1, CHUNK_E):
        extra = jnp.maximum(extra, nwin_ref[tile_idx * N_EXPERTS + c * CHUNK_E + k])
    return extra


def _tile_windows(nwin_ref, tile_idx):
    extra = _chunk_windows(nwin_ref, tile_idx, 0)
    for c in range(1, N_CHUNKS):
        extra = jnp.maximum(extra, _chunk_windows(nwin_ref, tile_idx, c))
    return extra


def _dispatch_kernel(base_ref, nwin_ref, uend_ref, rend_ref, xn_ref, rank_ref, xs_hbm,
                     stage0, stage1, ostage, zbuf, sem, osem, zsem):
    stage = (stage0, stage1)
    i = pl.program_id(0)
    last = pl.num_programs(0) - 1

    def window_copy(p, e, tile_idx):
        return pltpu.make_async_copy(stage[p].at[pl.ds(e * CAP, CAP)],
                                     _window(xs_hbm, base_ref, tile_idx * N_EXPERTS + e, 0), sem.at[p])

    def wait_tile(p, tile_idx):
        for e in range(N_EXPERTS):
            window_copy(p, e, tile_idx).wait()

    @pl.when(i == 0)
    def _zero_tails():
        zbuf[...] = jnp.zeros(zbuf.shape, BF16)
        sizes = []
        size = EXPERT_TILE
        while size >= SEG_ALIGN:
            sizes.append(size)
            size //= 2
        pieces = []
        for e in range(N_EXPERTS):
            pos = uend_ref[e]
            length = rend_ref[e] - pos
            for size in sizes:
                take = (length & size) != 0
                cp = pltpu.make_async_copy(zbuf.at[pl.ds(0, size)],
                                           xs_hbm.at[pl.ds(pl.multiple_of(pos, SEG_ALIGN), size)], zsem)
                pieces.append((take, cp))
                pos = pos + jnp.where(take, size, 0)
        for take, cp in pieces:
            pl.when(take)(cp.start)
        for take, cp in pieces:
            pl.when(take)(cp.wait)

    for p in (0, 1):
        @pl.when(i % 2 == p)
        def _step(p=p):
            xb = xn_ref[...]
            rows = CHUNK_E * CAP
            for c in range(N_CHUNKS):
                stage[p][pl.ds(c * rows, rows), :] = _dot(_one_hot_rows(rank_ref, c, 0), xb).astype(BF16)

            @pl.when(i >= 1)
            def _():
                wait_tile(1 - p, i - 1)

            for e in range(N_EXPERTS):
                window_copy(p, e, i).start()

    extra = _tile_windows(nwin_ref, i)

    @pl.when(extra > 1)
    def _long_segments():
        def body(w, carry):
            for c in range(N_CHUNKS):
                @pl.when(w < _chunk_windows(nwin_ref, i, c))
                def _(c=c):
                    ostage[...] = _dot(_one_hot_rows(rank_ref, c, w * CAP), xn_ref[...]).astype(BF16)
                    for k in range(CHUNK_E):
                        idx = i * N_EXPERTS + c * CHUNK_E + k

                        @pl.when(w < nwin_ref[idx])
                        def _(k=k, idx=idx):
                            cp = pltpu.make_async_copy(ostage.at[pl.ds(k * CAP, CAP)],
                                                       _window(xs_hbm, base_ref, idx, w), osem)
                            cp.start()
                            cp.wait()
            return carry

        lax.fori_loop(1, extra, body, 0)

    for p in (0, 1):
        @pl.when((i == last) & (i % 2 == p))
        def _drain(p=p):
            wait_tile(p, i)


def _dispatch(xn, rank_t, base, nwin, used_end, row_end, *, n_rows, TM):
    N = xn.shape[0]
    stage = pltpu.VMEM((N_EXPERTS * CAP, D_MODEL), BF16)
    return pl.pallas_call(
        _dispatch_kernel,
        grid_spec=pltpu.PrefetchScalarGridSpec(
            num_scalar_prefetch=4,
            grid=(N // TM,),
            in_specs=[pl.BlockSpec((TM, D_MODEL), lambda i, *_: (i, 0)),
                      pl.BlockSpec((N_EXPERTS, TM), lambda i, *_: (0, i))],
            out_specs=pl.BlockSpec(memory_space=pl.ANY),
            scratch_shapes=[
                stage, stage,
                pltpu.VMEM((CHUNK_E * CAP, D_MODEL), BF16),
                pltpu.VMEM((EXPERT_TILE, D_MODEL), BF16),
                pltpu.SemaphoreType.DMA((2,)),
                pltpu.SemaphoreType.DMA(()),
                pltpu.SemaphoreType.DMA(()),
            ],
        ),
        out_shape=jax.ShapeDtypeStruct((n_rows, D_MODEL), BF16),
        compiler_params=pltpu.CompilerParams(
            dimension_semantics=("arbitrary",), vmem_limit_bytes=VMEM_LIMIT),
        name="moe_dispatch",
    )(base, nwin, used_end, row_end, xn, rank_t)


def _expert_kernel(tile_e_ref, n_tiles_ref, xs_ref, w1_hbm, w3_hbm, w2_hbm, ys_ref,
                   w1f0, w1f1, w3f0, w3f1, w2f0, w2f1, w1b, w3b, w2b, wsem):
    t = pl.program_id(0)
    valid = t < n_tiles_ref[0]
    e = tile_e_ref[t]
    first_tile = valid & ((t == 0) | (e != tile_e_ref[jnp.maximum(t - 1, 0)]))
    f32_bufs = ((w1f0, w3f0, w2f0), (w1f1, w3f1, w2f1))

    def weight_copies(expert, p):
        return [pltpu.make_async_copy(hbm.at[expert], buf, wsem.at[p, j])
                for j, (hbm, buf) in enumerate(zip((w1_hbm, w3_hbm, w2_hbm), f32_bufs[p]))]

    for p in (0, 1):
        @pl.when(first_tile & (e % 2 == p))
        def _next_expert(p=p):
            @pl.when(t == 0)
            def _():
                for cp in weight_copies(e, p):
                    cp.start()

            for cp in weight_copies(e, p):
                cp.wait()

            @pl.when(e + 1 < N_EXPERTS)
            def _():
                for cp in weight_copies(e + 1, 1 - p):
                    cp.start()

            w1b[...] = f32_bufs[p][0][...].astype(BF16)
            w3b[...] = f32_bufs[p][1][...].astype(BF16)
            w2b[...] = f32_bufs[p][2][...].astype(BF16)

    @pl.when(valid)
    def _compute():
        xb = xs_ref[...]
        a = _dot(xb, w1b[...])
        hb = (a * jax.nn.sigmoid(a)) * _dot(xb, w3b[...])
        ys_ref[...] = _dot(hb.astype(BF16), w2b[...]).astype(BF16)

    @pl.when(t == n_tiles_ref[0])
    def _spare():
        ys_ref[...] = jnp.zeros(ys_ref.shape, BF16)


def _experts(xs, tile_e, n_tiles, w1, w3, w2):
    t_max = xs.shape[0] // EXPERT_TILE
    clamp = lambda t, nt: jnp.minimum(t, nt[0] - 1)
    any_spec = pl.BlockSpec(memory_space=pl.ANY)
    w_in = pltpu.VMEM((D_MODEL, EXPERT_FF), F32)
    w_out = pltpu.VMEM((EXPERT_FF, D_MODEL), F32)
    return pl.pallas_call(
        _expert_kernel,
        grid_spec=pltpu.PrefetchScalarGridSpec(
            num_scalar_prefetch=2,
            grid=(t_max,),
            in_specs=[pl.BlockSpec((EXPERT_TILE, D_MODEL), lambda t, te, nt: (clamp(t, nt), 0)),
                      any_spec, any_spec, any_spec],
            out_specs=pl.BlockSpec((EXPERT_TILE, D_MODEL),
                                   lambda t, te, nt: (jnp.where(t < nt[0], t, t_max), 0)),
            scratch_shapes=[
                w_in, w_in, w_in, w_in, w_out, w_out,
                pltpu.VMEM((D_MODEL, EXPERT_FF), BF16),
                pltpu.VMEM((D_MODEL, EXPERT_FF), BF16),
                pltpu.VMEM((EXPERT_FF, D_MODEL), BF16),
                pltpu.SemaphoreType.DMA((2, 3)),
            ],
        ),
        out_shape=jax.ShapeDtypeStruct(((t_max + 1) * EXPERT_TILE, D_MODEL), BF16),
        compiler_params=pltpu.CompilerParams(
            dimension_semantics=("arbitrary",), vmem_limit_bytes=VMEM_LIMIT),
        name="moe_experts",
    )(tile_e, n_tiles, xs, w1, w3, w2)


def _final_kernel(base_ref, nwin_ref, x1s_ref, rank_ref, gates_ref, ys_hbm, pa_ref, pb_ref, gple_ref,
                  wgate_ref, wproj_ref, gfin_ref, ya_ref, yb_ref, win0, win1, owin, acc_ref, sem, osem,
                  *, split, n_tiles):
    i = pl.program_id(0)
    j = i - 1
    win = (win0, win1)
    parts = 4
    part_chunks = N_CHUNKS // parts
    part_rows = part_chunks * CHUNK_E * CAP
    contract0 = (((0,), (0,)), ((), ()))

    def window_copy(p, e, tile_idx):
        return pltpu.make_async_copy(_window(ys_hbm, base_ref, tile_idx * N_EXPERTS + e, 0),
                                     win[p].at[pl.ds(e * CAP, CAP)], sem.at[p])

    @pl.when(i == 0)
    def _first():
        owin[...] = jnp.zeros(owin.shape, BF16)
        acc_ref[1] = jnp.zeros(acc_ref.shape[1:], F32)
        for e in range(N_EXPERTS):
            window_copy(0, e, i).start()

    def combine_stages(p):
        total = None
        for part in range(parts):
            one_hot = jnp.concatenate(
                [_one_hot_rows(rank_ref, part * part_chunks + c, 0, gates_ref) for c in range(part_chunks)], axis=0)
            d = lax.dot_general(one_hot, win[p][pl.ds(part * part_rows, part_rows), :], contract0,
                                preferred_element_type=F32)
            total = d if total is None else total + d
            yield
        acc_ref[p] = total

    def ple_stages(q):
        half = D_MODEL // 2
        x2 = x1s_ref[...] + acc_ref[q]
        rb = _rms(x2, gple_ref[...]).astype(BF16)
        yield
        g0 = jax.nn.sigmoid(_dot(rb, wgate_ref[:, :half]))
        yield
        g1 = jax.nn.sigmoid(_dot(rb, wgate_ref[:, half:]))
        yield
        pt = jnp.where(j < split, pa_ref[...], pb_ref[...])
        pp = _dot(pt.astype(BF16), wproj_ref[...])
        yield
        y = _rms(x2 + pp * jnp.concatenate([g0, g1], axis=1), gfin_ref[...])

        @pl.when(j < split)
        def _():
            ya_ref[...] = y

        @pl.when(j >= split)
        def _():
            yb_ref[...] = y

    for p in (0, 1):
        @pl.when((i < n_tiles) & (i % 2 == p))
        def _step(p=p):
            for e in range(N_EXPERTS):
                window_copy(p, e, i).wait()

            @pl.when(i + 1 < n_tiles)
            def _():
                for e in range(N_EXPERTS):
                    window_copy(1 - p, e, i + 1).start()

            side = [combine_stages(p), ple_stages(1 - p)]
            while side:
                _advance(side, 1)

    @pl.when(i == n_tiles)
    def _last_tile():
        for _ in ple_stages((n_tiles - 1) % 2):
            pass

    extra = _tile_windows(nwin_ref, jnp.minimum(i, n_tiles - 1))

    @pl.when((i < n_tiles) & (extra > 1))
    def _long_segments():
        def body(w, carry):
            for c in range(N_CHUNKS):
                @pl.when(w < _chunk_windows(nwin_ref, i, c))
                def _(c=c):
                    for k in range(CHUNK_E):
                        idx = i * N_EXPERTS + c * CHUNK_E + k

                        @pl.when(w < nwin_ref[idx])
                        def _(k=k, idx=idx):
                            cp = pltpu.make_async_copy(_window(ys_hbm, base_ref, idx, w),
                                                       owin.at[pl.ds(k * CAP, CAP)], osem)
                            cp.start()
                            cp.wait()
                    acc_ref[i % 2] += lax.dot_general(_one_hot_rows(rank_ref, c, w * CAP, gates_ref), owin[...],
                                                      contract0, preferred_element_type=F32)
            return carry

        lax.fori_loop(1, extra, body, 0)


def _final(x1s, rank_t, gates_t, ys, pa, pb, gple, wgate, wproj, gfin, base, nwin, *, TM):
    N = x1s.shape[0]
    n_tiles = N // TM
    split = pa.shape[0] // TM
    prev = lambda i: jnp.maximum(i - 1, 0)
    tok = pl.BlockSpec((TM, D_MODEL), lambda i, *_: (prev(i), 0))
    per_e = pl.BlockSpec((N_EXPERTS, TM), lambda i, *_: (0, jnp.minimum(i, n_tiles - 1)))
    consts = (gple, wgate, wproj, gfin)
    win = pltpu.VMEM((N_EXPERTS * CAP, D_MODEL), BF16)

    def group_specs(block):
        a, b = _group_specs(block, split)
        return (pl.BlockSpec(block, lambda i, *_: a.index_map(prev(i))),
                pl.BlockSpec(block, lambda i, *_: b.index_map(prev(i))))

    return pl.pallas_call(
        functools.partial(_final_kernel, split=split, n_tiles=n_tiles),
        grid_spec=pltpu.PrefetchScalarGridSpec(
            num_scalar_prefetch=2,
            grid=(n_tiles + 1,),
            in_specs=[tok, per_e, per_e, pl.BlockSpec(memory_space=pl.ANY)]
            + list(group_specs((TM, PLE_DIM)))
            + [pl.BlockSpec(c.shape, lambda i, *_: (0, 0)) for c in consts],
            out_specs=group_specs((TM, D_MODEL)),
            scratch_shapes=[win, win, pltpu.VMEM((CHUNK_E * CAP, D_MODEL), BF16),
                            pltpu.VMEM((2, TM, D_MODEL), F32),
                            pltpu.SemaphoreType.DMA((2,)), pltpu.SemaphoreType.DMA(())],
        ),
        out_shape=(jax.ShapeDtypeStruct((pa.shape[0], D_MODEL), F32),
                   jax.ShapeDtypeStruct((pb.shape[0], D_MODEL), F32)),
        compiler_params=pltpu.CompilerParams(
            dimension_semantics=("arbitrary",), vmem_limit_bytes=VMEM_LIMIT),
        name="combine_ple_final",
    )(base, nwin, x1s, rank_t, gates_t, ys, pa, pb, *consts)


def kernel(x_prompt, x_sample, p_prompt, p_sample, state_C, state_n, state_m, state_pool, norm_mix_g, w_in, b_igate, b_fgate, head_norm_g, w_pool, pool_scale, w_out, norm_ffn_g, router_w, router_bias, ex_w1, ex_w3, ex_w2, sh_w1, sh_w3, sh_w2, norm_ple_g, w_ple_gate, w_ple_proj, final_norm_g):
    depth = norm_mix_g.shape[0]
    assert depth == 1
    l = 0
    B, T, _ = x_prompt.shape
    Bs, Ts, _ = x_sample.shape
    g0 = 4 * MLSTM_WIDTH
    w = w_in[l]
    lane_pad = jnp.zeros((D_MODEL, 128 - HEADS), F32)
    wcat = jnp.concatenate(
        [w[:, :g0], w[:, g0 + 2 * HEADS:], w[:, g0:g0 + HEADS], lane_pad,
         w[:, g0 + HEADS:g0 + 2 * HEADS], lane_pad], axis=1).astype(BF16)
    bias_pad = jnp.zeros((128 - HEADS,), F32)
    gbias = jnp.concatenate([b_igate[l], bias_pad, b_fgate[l], bias_pad])[None, :]
    mixer_w = (wcat, norm_mix_g[l][None, :], gbias, head_norm_g[l][None, :], w_pool[l].astype(BF16),
               pool_scale[l][None, :], w_out[l].astype(BF16))

    x1p, Cp, Np, Mp, Bp = _mixer(x_prompt, *mixer_w, None, BB=1, TT=512, L=128, start_pos=0)
    state = (state_C[l], state_n[l], state_m[l][..., None], state_pool[l])
    x1s_, Cs, Ns, Ms, Bs_ = _mixer(x_sample, *mixer_w, state, BB=16, TT=Ts, L=Ts, start_pos=PAST_LEN)

    N = B * T + Bs * Ts
    assert (B * T) % ROUTE_TILE == 0 and (Bs * Ts) % ROUTE_TILE == 0
    n_route_tiles = N // ROUTE_TILE

    xn, rank_t, gates_t, cnt, x1sh = _router(
        x1p.reshape(B * T, D_MODEL), x1s_.reshape(Bs * Ts, D_MODEL),
        norm_ffn_g[l][None, :], router_w[l].T, router_bias[l][:, None],
        sh_w1[l].astype(BF16), sh_w3[l].astype(BF16), sh_w2[l].astype(BF16), TM=2 * ROUTE_TILE)

    max_rows = TOP_K * N + (SEG_ALIGN - 1) * n_route_tiles * N_EXPERTS + N_EXPERTS * (CAP + EXPERT_TILE)
    t_max = -(-max_rows // EXPERT_TILE)
    cnt = cnt[:, 0].reshape(n_route_tiles, N_EXPERTS).astype(jnp.int32)
    base, nwin, used_end, row_end, tile_e, n_tiles = _segment_plan(cnt, t_max)

    xs = _dispatch(xn, rank_t, base, nwin, used_end, row_end, n_rows=t_max * EXPERT_TILE, TM=ROUTE_TILE)
    ys = _experts(xs, tile_e, n_tiles, ex_w1[l], ex_w3[l], ex_w2[l])
    y_prompt, y_sample = _final(
        x1sh, rank_t, gates_t, ys, p_prompt[l].reshape(B * T, PLE_DIM), p_sample[l].reshape(Bs * Ts, PLE_DIM),
        norm_ple_g[l][None, :], w_ple_gate[l].astype(BF16), w_ple_proj[l].astype(BF16),
        final_norm_g[None, :], base, nwin, TM=ROUTE_TILE)
    return (y_prompt.reshape(B, T, D_MODEL), y_sample.reshape(Bs, Ts, D_MODEL),
            Cp[None], Np[None], Mp[..., 0][None], Bp[None],
            Cs[None], Ns[None], Ms[..., 0][None], Bs_[None])
```

```python
import functools

import jax
import jax.numpy as jnp
from jax import lax
from jax.experimental import pallas as pl
from jax.experimental.pallas import tpu as pltpu

D_MODEL = 1024
HEADS = 4
DH = 128
MLSTM_WIDTH = HEADS * DH
POOL_WIDTH = 512
POOL_WINDOWS = (2, 4, 8, 16)
POOL_GDIM = 128
POOL_BUF = 15
POOL_PAD = 16
N_EXPERTS = 64
TOP_K = 8
N_GROUPS = 8
GROUP_SIZE = N_EXPERTS // N_GROUPS
TOPK_GROUPS = 4
EXPERT_FF = 256
ROUTED_SCALE = 2.5
NORM_EPS = 1e-6
PLE_DIM = 256
PAST_LEN = 16384

COL_Q, COL_K, COL_V, COL_O, COL_U, COL_I, COL_F = 0, 512, 1024, 1536, 2048, 2560, 2688
IN_COLS_PAD = 2816
GATE_COLS = HEADS * 128

ROUTE_TILE = 256
EXPERT_TILE = 1440
ZERO_ROWS = 1024
SEG_ALIGN = 16
CAP = 48
CHUNK_E = 8
N_CHUNKS = N_EXPERTS // CHUNK_E

VMEM_LIMIT = 56 * 1024 * 1024
F32 = jnp.float32
BF16 = jnp.bfloat16
NEG_INF = float("-inf")


def _rms(x, g):
    return x * lax.rsqrt(jnp.mean(x * x, axis=-1, keepdims=True) + NORM_EPS) * g


def _log_sigmoid(x):
    return jnp.minimum(x, 0.0) - jnp.log1p(jnp.exp(-jnp.abs(x)))


def _dot(a, b):
    return jnp.dot(a, b, preferred_element_type=F32)


def _mlstm_chunk(q, k, v, b_col, i_col, C, n, m, causal, eye):
    L = q.shape[0]
    r_col = i_col - b_col
    r_row = jnp.sum(jnp.where(eye, r_col, 0.0), axis=0, keepdims=True)
    d = jnp.where(causal, b_col + r_row, NEG_INF)
    inter = b_col + m
    qb, kb, vb = q.astype(BF16), k.astype(BF16), v.astype(BF16)
    qk = lax.dot_general(qb, kb, (((1,), (1,)), ((), ())), preferred_element_type=F32)
    qc = _dot(qb, C.astype(BF16))
    qn = jnp.sum(q * n, axis=-1, keepdims=True)
    yield
    m_t = jnp.maximum(inter, jnp.max(d, axis=-1, keepdims=True))
    b_last = b_col[L - 1:L, :]
    m_new = jnp.maximum(b_last + m, jnp.max(b_last + r_row, axis=-1, keepdims=True))
    yield
    w_inter = jnp.exp(inter - m_t)
    s = qk * jnp.exp(d - m_t)
    fw = jnp.exp(b_last + m - m_new)
    iw_col = jnp.exp(b_last + r_col - m_new)
    kw = iw_col * k
    yield
    sv = _dot(s.astype(BF16), vb)
    kv = lax.dot_general(kw.astype(BF16), vb, (((0,), (0,)), ((), ())), preferred_element_type=F32)
    ssum = jnp.sum(s, axis=-1, keepdims=True)
    n_new = fw * n + jnp.sum(kw, axis=0, keepdims=True)
    yield
    num = w_inter * qc + sv
    nq = w_inter * qn + ssum
    h = num / jnp.maximum(jnp.abs(nq), jnp.exp(-m_t))
    C_new = fw * C + kv
    return h, C_new, n_new, m_new


def _interleave(gens):
    results = [None] * len(gens)
    live = list(enumerate(gens))
    while live:
        still = []
        for idx, g in live:
            try:
                next(g)
                still.append((idx, g))
            except StopIteration as stop:
                results[idx] = stop.value
        live = still
    return results


def _mlstm_chunk_square(q, k, v, r, b, mp, CN, m, causal, eye, ones):
    L = q.shape[0]
    r_row = jnp.sum(jnp.where(eye, r, 0.0), axis=0, keepdims=True)
    g = jnp.maximum(mp, m)
    qb, kb = q.astype(BF16), k.astype(BF16)
    v1 = jnp.concatenate([v.astype(BF16), ones], axis=1)
    qk = lax.dot_general(qb, kb, (((1,), (1,)), ((), ())), preferred_element_type=F32)
    qcn = _dot(qb, CN.astype(BF16))
    yield
    w_inter = jnp.exp(m - g)
    p = jnp.where(causal, jnp.exp(r_row - g), 0.0)
    g_last = g[L - 1:L, :]
    m_new = b[L - 1:L, :] + g_last
    fw = jnp.exp(m - g_last)
    kw = (jnp.exp(r - g_last) * k).astype(BF16)
    floor = jnp.exp(-(b + g))
    yield
    s = (qk * p).astype(BF16)
    sv = _dot(s, v1)
    kv = lax.dot_general(kw, v1, (((0,), (0,)), ((), ())), preferred_element_type=F32)
    yield
    num = w_inter * qcn[:, :DH] + sv[:, :DH]
    nq = w_inter * qcn[:, DH:] + sv[:, DH:]
    h = num / jnp.maximum(jnp.abs(nq), floor)
    CN_new = jnp.concatenate([fw, fw], axis=1) * CN + kv
    return h, CN_new, m_new[:, 0:1]


def _chunk_scan(x, L, op, fill):
    pos = lax.broadcasted_iota(jnp.int32, x.shape, 0) & (L - 1)
    k = 1
    while k < L:
        x = op(x, jnp.where(pos >= k, pltpu.roll(x, k, axis=0), fill))
        k *= 2
        yield
    return x


def _advance(gens, steps):
    for g in list(gens):
        for _ in range(steps):
            try:
                next(g)
            except StopIteration:
                gens.remove(g)
                break


def _mixer_kernel(*refs, BB, TT, L, start_pos, zero_state):
    if zero_state:
        (x_ref, wcat_ref, gmix_ref, gbias_ref, hng_ref, wpool_ref, pscale_ref, wout_ref,
         x1_ref, c_ref, n_ref, m_ref, buf_ref,
         z_ref, a_ref, b_ref, mp_ref, cn_ref, mix_ref, ext_ref) = refs
    else:
        (x_ref, wcat_ref, gmix_ref, gbias_ref, hng_ref, wpool_ref, pscale_ref, wout_ref,
         c0_ref, n0_ref, m0_ref, buf0_ref,
         x1_ref, c_ref, n_ref, m_ref, buf_ref,
         z_ref, a_ref, b_ref, mp_ref, cn_ref, mix_ref, ext_ref) = refs
    t = pl.program_id(1)
    R = BB * TT
    n_chunks = TT // L
    square = L == DH
    assert not square or (zero_state and BB == 1)

    @pl.when(t == 0)
    def _init():
        ext_ref[:, 0:POOL_PAD, :] = jnp.zeros((BB, POOL_PAD, POOL_WIDTH), F32)
        if zero_state:
            cn_ref[...] = jnp.zeros(cn_ref.shape, F32)
            c_ref[...] = jnp.zeros(c_ref.shape, F32)
            n_ref[...] = jnp.zeros(n_ref.shape, F32)
            m_ref[...] = jnp.zeros(m_ref.shape, F32)
        else:
            c_ref[...] = c0_ref[...]
            n_ref[...] = n0_ref[...]
            m_ref[...] = m0_ref[...]
            ext_ref[:, 1:POOL_PAD, :] = buf0_ref[...]

    hb = _rms(x_ref[...].reshape(R, D_MODEL), gmix_ref[...]).astype(BF16)

    def project(lo, hi):
        z_ref[:, lo:hi] = _dot(hb, wcat_ref[:, lo:hi])

    def gate_stages():
        gbias = gbias_ref[...]
        gi = z_ref[:, COL_I:COL_I + 128] + gbias[:, :128]
        lf = _log_sigmoid(z_ref[:, COL_F:COL_F + 128] + gbias[:, 128:])
        yield
        b_c = yield from _chunk_scan(lf, L, jnp.add, 0.0)
        if square:
            r_c = gi - b_c
            mp_c = yield from _chunk_scan(r_c, L, jnp.maximum, NEG_INF)
            for hd in range(HEADS):
                lanes = slice(hd * DH, (hd + 1) * DH)
                a_ref[:, lanes] = jnp.broadcast_to(r_c[:, hd:hd + 1], (R, DH))
                b_ref[:, lanes] = jnp.broadcast_to(b_c[:, hd:hd + 1], (R, DH))
                mp_ref[:, lanes] = jnp.broadcast_to(mp_c[:, hd:hd + 1], (R, DH))
                yield
        else:
            a_ref[:, 0:128] = gi
            b_ref[:, 0:128] = b_c

    def pool_stages():
        ext_ref[:, POOL_PAD:POOL_PAD + TT, :] = z_ref[:, COL_U:COL_U + POOL_WIDTH].reshape(BB, TT, POOL_WIDTH)
        pos = start_pos + t * TT + lax.broadcasted_iota(jnp.int32, (1, TT, 1), 1)
        pscale = pscale_ref[...]
        yield
        for gidx, w in enumerate(POOL_WINDOWS):
            lanes = slice(gidx * POOL_GDIM, (gidx + 1) * POOL_GDIM)
            u_g = ext_ref[:, POOL_PAD:POOL_PAD + TT, lanes]
            acc = u_g
            for j in range(1, w):
                acc = acc + ext_ref[:, POOL_PAD - j:POOL_PAD - j + TT, lanes]
                if j % 4 == 3:
                    yield
            cnt = jnp.minimum(pos + 1, w).astype(F32)
            pooled = (acc / cnt - u_g).reshape(R, POOL_GDIM)
            mixed = _dot(pooled.astype(BF16), wpool_ref[gidx]) * pscale[:, lanes]
            mix_ref[:, MLSTM_WIDTH + gidx * POOL_GDIM:MLSTM_WIDTH + (gidx + 1) * POOL_GDIM] = mixed
            yield
        new_buf = ext_ref[:, TT + 1:TT + POOL_PAD, :]
        buf_ref[...] = new_buf
        ext_ref[:, 1:POOL_PAD, :] = new_buf

    project(COL_I, IN_COLS_PAD)
    project(COL_U, COL_U + 256)
    project(COL_U + 256, COL_I)
    side = [gate_stages(), pool_stages()]
    for lo in range(0, COL_U, 256):
        project(lo, lo + 256)
        _advance(side, 3)
    while side:
        _advance(side, 1)

    row = lax.broadcasted_iota(jnp.int32, (L, L), 0)
    col = lax.broadcasted_iota(jnp.int32, (L, L), 1)
    causal = row >= col
    eye = row == col
    ones = jnp.ones((DH, DH), BF16)
    hng = hng_ref[...]

    group = 4 if (not square and n_chunks == 1 and BB % 4 == 0) else 1

    def chunk_body(s, carry):
        work = []
        for j in range(group):
            sj = s * group + j
            rows = pl.ds(pl.multiple_of(sj * L, L), L)
            bb = 0 if BB == 1 else sj // n_chunks
            for hd in range(HEADS):
                lanes = slice(hd * DH, (hd + 1) * DH)
                q = z_ref[rows, COL_Q + hd * DH:COL_Q + (hd + 1) * DH] * (DH ** -0.5)
                k = z_ref[rows, COL_K + hd * DH:COL_K + (hd + 1) * DH]
                v = z_ref[rows, COL_V + hd * DH:COL_V + (hd + 1) * DH]
                o = z_ref[rows, COL_O + hd * DH:COL_O + (hd + 1) * DH]
                m = m_ref[bb, hd:hd + 1, :]
                if square:
                    state = (a_ref[rows, lanes], b_ref[rows, lanes], mp_ref[rows, lanes], cn_ref[hd])
                else:
                    state = (b_ref[rows, hd:hd + 1], a_ref[rows, hd:hd + 1], c_ref[bb, hd],
                             n_ref[bb, hd:hd + 1, :])
                work.append((rows, bb, hd, lanes, q, k, v, o, m, state))
        if square:
            results = _interleave([_mlstm_chunk_square(q, k, v, *state, m, causal, eye, ones)
                                   for _, _, _, _, q, k, v, _, m, state in work])
        else:
            results = _interleave([_mlstm_chunk(q, k, v, *state, m, causal, eye)
                                   for _, _, _, _, q, k, v, _, m, state in work])
        scales = [lax.rsqrt(jnp.mean(res[0] * res[0], axis=-1, keepdims=True) + NORM_EPS) for res in results]
        done = []
        for (rows, bb, hd, lanes, _, _, _, o, _, _), res, scale in zip(work, results, scales):
            out = res[0] * scale * hng[:, lanes] * jax.nn.sigmoid(o)
            done.append((rows, bb, hd, lanes, out, res[-1], res[1:-1]))
        for rows, bb, hd, lanes, out, m_new, new_state in done:
            mix_ref[rows, lanes] = out
            m_ref[bb, hd:hd + 1, :] = m_new
            if square:
                cn_ref[hd] = new_state[0]
            else:
                c_ref[bb, hd] = new_state[0]
                n_ref[bb, hd:hd + 1, :] = new_state[1]
        return carry

    lax.fori_loop(0, BB * n_chunks // group, chunk_body, 0)

    if square:
        @pl.when(t == pl.num_programs(1) - 1)
        def _emit_state():
            for hd in range(HEADS):
                cn = cn_ref[hd]
                c_ref[0, hd] = cn[:, :DH]
                n_ref[0, hd:hd + 1, :] = jnp.sum(jnp.where(eye, cn[:, DH:], 0.0), axis=0, keepdims=True)

    out = x_ref[...].reshape(R, D_MODEL) + _dot(mix_ref[...].astype(BF16), wout_ref[...])
    x1_ref[...] = out.reshape(BB, TT, D_MODEL)


def _const_spec(shape):
    nd = len(shape)
    return pl.BlockSpec(shape, lambda b, t, _nd=nd: (0,) * _nd, pipeline_mode=pl.Buffered(1))


def _mixer(x, wcat, gmix, gbias, hng, wpool, pscale, wout, state, *, BB, TT, L, start_pos):
    B, T, _ = x.shape
    zero_state = state is None
    R = BB * TT
    grid = (B // BB, T // TT)
    x_spec = pl.BlockSpec((BB, TT, D_MODEL), lambda b, t: (b, t, 0))
    c_spec = pl.BlockSpec((BB, HEADS, DH, DH), lambda b, t: (b, 0, 0, 0))
    n_spec = pl.BlockSpec((BB, HEADS, DH), lambda b, t: (b, 0, 0))
    m_spec = pl.BlockSpec((BB, HEADS, 1), lambda b, t: (b, 0, 0))
    buf_spec = pl.BlockSpec((BB, POOL_BUF, POOL_WIDTH), lambda b, t: (b, 0, 0))
    weights = (wcat, gmix, gbias, hng, wpool, pscale, wout)
    in_specs = [x_spec] + [_const_spec(w.shape) for w in weights]
    args = [x, *weights]
    if not zero_state:
        in_specs += [c_spec, n_spec, m_spec, buf_spec]
        args += list(state)
    out_shape = (
        jax.ShapeDtypeStruct((B, T, D_MODEL), F32),
        jax.ShapeDtypeStruct((B, HEADS, DH, DH), F32),
        jax.ShapeDtypeStruct((B, HEADS, DH), F32),
        jax.ShapeDtypeStruct((B, HEADS, 1), F32),
        jax.ShapeDtypeStruct((B, POOL_BUF, POOL_WIDTH), F32),
    )
    kern = functools.partial(_mixer_kernel, BB=BB, TT=TT, L=L, start_pos=start_pos, zero_state=zero_state)
    return pl.pallas_call(
        kern,
        grid=grid,
        in_specs=in_specs,
        out_specs=(x_spec, c_spec, n_spec, m_spec, buf_spec),
        out_shape=out_shape,
        scratch_shapes=[
            pltpu.VMEM((R, IN_COLS_PAD), F32),
            pltpu.VMEM((R, GATE_COLS), F32),
            pltpu.VMEM((R, GATE_COLS), F32),
            pltpu.VMEM((R, GATE_COLS), F32),
            pltpu.VMEM((HEADS, DH, 2 * DH), F32),
            pltpu.VMEM((R, D_MODEL), F32),
            pltpu.VMEM((BB, POOL_PAD + TT, POOL_WIDTH), F32),
        ],
        compiler_params=pltpu.CompilerParams(
            dimension_semantics=("arbitrary", "arbitrary"), vmem_limit_bytes=VMEM_LIMIT),
        name="mixer_zero" if zero_state else "mixer_state",
    )(*args)


def _first_index_of_max(work, idx, n, axis):
    mx = jnp.max(work, axis=axis, keepdims=True)
    return jnp.min(jnp.where(work == mx, idx, float(n)), axis=axis, keepdims=True)


def _route(scores_t, bias_t):
    tm = scores_t.shape[1]
    biased = scores_t + bias_t
    b3 = biased.reshape(N_GROUPS, GROUP_SIZE, tm)
    sub = lax.broadcasted_iota(jnp.int32, b3.shape, 1).astype(F32)
    m1 = jnp.max(b3, axis=1, keepdims=True)
    first = jnp.min(jnp.where(b3 == m1, sub, float(GROUP_SIZE)), axis=1, keepdims=True)
    m2 = jnp.max(jnp.where(sub == first, NEG_INF, b3), axis=1, keepdims=True)
    gs = (m1 + m2).reshape(N_GROUPS, tm)
    gidx = lax.broadcasted_iota(jnp.int32, gs.shape, 0).astype(F32)
    gsel = jnp.zeros(gs.shape, F32)
    work = gs
    yield
    for _ in range(TOPK_GROUPS):
        pick = gidx == _first_index_of_max(work, gidx, N_GROUPS, 0)
        gsel = jnp.where(pick, 1.0, gsel)
        work = jnp.where(pick, NEG_INF, work)
    emask = jnp.broadcast_to(gsel.reshape(N_GROUPS, 1, tm), b3.shape).reshape(N_EXPERTS, tm)
    work = jnp.where(emask > 0, biased, NEG_INF)
    eidx = lax.broadcasted_iota(jnp.int32, work.shape, 0).astype(F32)
    mask = jnp.zeros(work.shape, F32)
    yield
    for _ in range(TOP_K):
        pick = eidx == _first_index_of_max(work, eidx, N_EXPERTS, 0)
        mask = jnp.where(pick, 1.0, mask)
        work = jnp.where(pick, NEG_INF, work)
        yield
    sel = mask * scores_t
    return mask, sel / jnp.sum(sel, axis=0, keepdims=True) * ROUTED_SCALE


def _group_specs(block, split):
    return (pl.BlockSpec(block, lambda i, *_: (jnp.minimum(i, split - 1), 0)),
            pl.BlockSpec(block, lambda i, *_: (jnp.maximum(i - split, 0), 0)))


def _router_kernel(x1a_ref, x1b_ref, gffn_ref, rwt_ref, rbias_ref, w1_ref, w3_ref, w2_ref,
                   xn_ref, rank_ref, gates_ref, cnt_ref, x1s_ref, *, split):
    x1 = jnp.where(pl.program_id(0) < split, x1a_ref[...], x1b_ref[...])
    tm = x1.shape[0]
    xn = _rms(x1, gffn_ref[...])
    xb = xn.astype(BF16)
    xn_ref[...] = xb
    logits_t = lax.dot_general(rwt_ref[...], xn, (((1,), (1,)), ((), ())),
                               preferred_element_type=F32, precision=lax.Precision.HIGHEST)
    def routing():
        mask, gates = yield from _route(jax.nn.sigmoid(logits_t), rbias_ref[...])
        gates_ref[...] = gates
        rt = ROUTE_TILE
        before = (lax.broadcasted_iota(jnp.int32, (rt, rt), 0)
                  < lax.broadcasted_iota(jnp.int32, (rt, rt), 1)).astype(BF16)
        for j in range(tm // rt):
            mj = mask[:, j * rt:(j + 1) * rt]
            rank_ref[:, j * rt:(j + 1) * rt] = jnp.where(mj > 0, _dot(mj.astype(BF16), before), -1.0)
            cnt_ref[j * N_EXPERTS:(j + 1) * N_EXPERTS, :] = jnp.broadcast_to(
                jnp.sum(mj, axis=1, keepdims=True), (N_EXPERTS, 128))
            yield

    def shared_expert():
        half = D_MODEL // 2
        a = _dot(xb, w1_ref[...])
        yield
        g = _dot(xb, w3_ref[...])
        yield
        hsh = ((a * jax.nn.sigmoid(a)) * g).astype(BF16)
        yield
        x1s_ref[:, :half] = x1[:, :half] + _dot(hsh, w2_ref[:, :half])
        yield
        x1s_ref[:, half:] = x1[:, half:] + _dot(hsh, w2_ref[:, half:])

    side = [routing(), shared_expert()]
    while side:
        _advance(side, 1)


def _router(x1a, x1b, gffn, rwt, rbias, w1, w3, w2, *, TM):
    N = x1a.shape[0] + x1b.shape[0]
    split = x1a.shape[0] // TM
    tok = pl.BlockSpec((TM, D_MODEL), lambda i: (i, 0))
    per_e = pl.BlockSpec((N_EXPERTS, TM), lambda i: (0, i))
    consts = (gffn, rwt, rbias, w1, w3, w2)
    return pl.pallas_call(
        functools.partial(_router_kernel, split=split),
        grid=(N // TM,),
        in_specs=list(_group_specs((TM, D_MODEL), split)) + [pl.BlockSpec(c.shape, lambda i: (0, 0)) for c in consts],
        out_specs=(tok, per_e, per_e, pl.BlockSpec((TM // ROUTE_TILE * N_EXPERTS, 128), lambda i: (i, 0)), tok),
        out_shape=(jax.ShapeDtypeStruct((N, D_MODEL), BF16),
                   jax.ShapeDtypeStruct((N_EXPERTS, N), F32),
                   jax.ShapeDtypeStruct((N_EXPERTS, N), F32),
                   jax.ShapeDtypeStruct((N // ROUTE_TILE * N_EXPERTS, 128), F32),
                   jax.ShapeDtypeStruct((N, D_MODEL), F32)),
        compiler_params=pltpu.CompilerParams(
            dimension_semantics=("arbitrary",), vmem_limit_bytes=VMEM_LIMIT),
        name="router_shared",
    )(x1a, x1b, *consts)


def _segment_plan(cnt, n_tiles_max):
    seg = (cnt + SEG_ALIGN - 1) // SEG_ALIGN * SEG_ALIGN
    used = jnp.sum(seg, axis=0)
    size = (used + CAP + EXPERT_TILE - 1) // EXPERT_TILE * EXPERT_TILE
    row_end = jnp.cumsum(size)
    row_start = row_end - size
    base = row_start[None, :] + jnp.cumsum(seg, axis=0) - seg
    nwin = jnp.maximum((seg + CAP - 1) // CAP, 1)
    tile_end = row_end // EXPERT_TILE
    n_tiles = tile_end[-1]
    t_ids = jnp.arange(n_tiles_max, dtype=jnp.int32)
    tile_e = jnp.minimum(jnp.sum(t_ids[:, None] >= tile_end[None, :], axis=1), N_EXPERTS - 1)
    i32 = lambda a: a.astype(jnp.int32)
    return (i32(base).reshape(-1), i32(nwin).reshape(-1), i32(row_start + used), i32(row_end),
            i32(tile_e), i32(n_tiles).reshape(1))


def _one_hot_rows(rank_ref, chunk, first_row, values_ref=None):
    tm = rank_ref.shape[1]
    j = (lax.broadcasted_iota(jnp.int32, (CAP, tm), 0) + first_row).astype(F32)
    rows = []
    for k in range(CHUNK_E):
        e = chunk * CHUNK_E + k
        hit = j == rank_ref[e:e + 1, :]
        val = 1.0 if values_ref is None else values_ref[e:e + 1, :]
        rows.append(jnp.where(hit, val, 0.0).astype(BF16))
    return jnp.concatenate(rows, axis=0)


def _window(hbm, base_ref, idx, w):
    start = pl.multiple_of(base_ref[idx] + w * CAP, SEG_ALIGN)
    return hbm.at[pl.ds(start, CAP)]


def _chunk_windows(nwin_ref, tile_idx, c):
    extra = nwin_ref[tile_idx * N_EXPERTS + c * CHUNK_E]
    for k in range(1, CHUNK_E):
        extra = jnp.maximum(extra, nwin_ref[tile_idx * N_EXPERTS + c * CHUNK_E + k])
    return extra


def _tile_windows(nwin_ref, tile_idx):
    extra = _chunk_windows(nwin_ref, tile_idx, 0)
    for c in range(1, N_CHUNKS):
        extra = jnp.maximum(extra, _chunk_windows(nwin_ref, tile_idx, c))
    return extra


def _dispatch_kernel(base_ref, nwin_ref, uend_ref, rend_ref, xn_ref, rank_ref, xs_hbm,
                     stage0, stage1, ostage, zbuf, sem, osem, zsem):
    stage = (stage0, stage1)
    i = pl.program_id(0)
    last = pl.num_programs(0) - 1

    def window_copy(p, e, tile_idx):
        return pltpu.make_async_copy(stage[p].at[pl.ds(e * CAP, CAP)],
                                     _window(xs_hbm, base_ref, tile_idx * N_EXPERTS + e, 0), sem.at[p])

    def wait_tile(p, tile_idx):
        for e in range(N_EXPERTS):
            window_copy(p, e, tile_idx).wait()

    @pl.when(i == 0)
    def _zero_tails():
        zbuf[...] = jnp.zeros(zbuf.shape, BF16)
        sizes = []
        assert 2 * ZERO_ROWS > EXPERT_TILE + CAP
        size = ZERO_ROWS
        while size >= SEG_ALIGN:
            sizes.append(size)
            size //= 2
        pieces = []
        for e in range(N_EXPERTS):
            pos = uend_ref[e]
            length = rend_ref[e] - pos
            for size in sizes:
                take = (length & size) != 0
                cp = pltpu.make_async_copy(zbuf.at[pl.ds(0, size)],
                                           xs_hbm.at[pl.ds(pl.multiple_of(pos, SEG_ALIGN), size)], zsem)
                pieces.append((take, cp))
                pos = pos + jnp.where(take, size, 0)
        for take, cp in pieces:
            pl.when(take)(cp.start)
        for take, cp in pieces:
            pl.when(take)(cp.wait)

    for p in (0, 1):
        @pl.when(i % 2 == p)
        def _step(p=p):
            xb = xn_ref[...]
            rows = CHUNK_E * CAP
            for c in range(N_CHUNKS):
                stage[p][pl.ds(c * rows, rows), :] = _dot(_one_hot_rows(rank_ref, c, 0), xb).astype(BF16)

            @pl.when(i >= 1)
            def _():
                wait_tile(1 - p, i - 1)

            for e in range(N_EXPERTS):
                window_copy(p, e, i).start()

    extra = _tile_windows(nwin_ref, i)

    @pl.when(extra > 1)
    def _long_segments():
        def body(w, carry):
            for c in range(N_CHUNKS):
                @pl.when(w < _chunk_windows(nwin_ref, i, c))
                def _(c=c):
                    ostage[...] = _dot(_one_hot_rows(rank_ref, c, w * CAP), xn_ref[...]).astype(BF16)
                    for k in range(CHUNK_E):
                        idx = i * N_EXPERTS + c * CHUNK_E + k

                        @pl.when(w < nwin_ref[idx])
                        def _(k=k, idx=idx):
                            cp = pltpu.make_async_copy(ostage.at[pl.ds(k * CAP, CAP)],
                                                       _window(xs_hbm, base_ref, idx, w), osem)
                            cp.start()
                            cp.wait()
            return carry

        lax.fori_loop(1, extra, body, 0)

    for p in (0, 1):
        @pl.when((i == last) & (i % 2 == p))
        def _drain(p=p):
            wait_tile(p, i)


def _dispatch(xn, rank_t, base, nwin, used_end, row_end, *, n_rows, TM):
    N = xn.shape[0]
    stage = pltpu.VMEM((N_EXPERTS * CAP, D_MODEL), BF16)
    return pl.pallas_call(
        _dispatch_kernel,
        grid_spec=pltpu.PrefetchScalarGridSpec(
            num_scalar_prefetch=4,
            grid=(N // TM,),
            in_specs=[pl.BlockSpec((TM, D_MODEL), lambda i, *_: (i, 0)),
                      pl.BlockSpec((N_EXPERTS, TM), lambda i, *_: (0, i))],
            out_specs=pl.BlockSpec(memory_space=pl.ANY),
            scratch_shapes=[
                stage, stage,
                pltpu.VMEM((CHUNK_E * CAP, D_MODEL), BF16),
                pltpu.VMEM((ZERO_ROWS, D_MODEL), BF16),
                pltpu.SemaphoreType.DMA((2,)),
                pltpu.SemaphoreType.DMA(()),
                pltpu.SemaphoreType.DMA(()),
            ],
        ),
        out_shape=jax.ShapeDtypeStruct((n_rows, D_MODEL), BF16),
        compiler_params=pltpu.CompilerParams(
            dimension_semantics=("arbitrary",), vmem_limit_bytes=VMEM_LIMIT),
        name="moe_dispatch",
    )(base, nwin, used_end, row_end, xn, rank_t)


def _expert_kernel(tile_e_ref, n_tiles_ref, xs_ref, w1_hbm, w3_hbm, w2_hbm, ys_ref,
                   w1f0, w1f1, w3f0, w3f1, w2f0, w2f1, w1b, w3b, w2b, wsem):
    t = pl.program_id(0)
    valid = t < n_tiles_ref[0]
    e = tile_e_ref[t]
    first_tile = valid & ((t == 0) | (e != tile_e_ref[jnp.maximum(t - 1, 0)]))
    f32_bufs = ((w1f0, w3f0, w2f0), (w1f1, w3f1, w2f1))

    def weight_copies(expert, p):
        return [pltpu.make_async_copy(hbm.at[expert], buf, wsem.at[p, j])
                for j, (hbm, buf) in enumerate(zip((w1_hbm, w3_hbm, w2_hbm), f32_bufs[p]))]

    for p in (0, 1):
        @pl.when(first_tile & (e % 2 == p))
        def _next_expert(p=p):
            @pl.when(t == 0)
            def _():
                for cp in weight_copies(e, p):
                    cp.start()

            for cp in weight_copies(e, p):
                cp.wait()

            @pl.when(e + 1 < N_EXPERTS)
            def _():
                for cp in weight_copies(e + 1, 1 - p):
                    cp.start()

            w1b[...] = f32_bufs[p][0][...].astype(BF16)
            w3b[...] = f32_bufs[p][1][...].astype(BF16)
            w2b[...] = f32_bufs[p][2][...].astype(BF16)

    @pl.when(valid)
    def _compute():
        xb = xs_ref[...]
        a = _dot(xb, w1b[...])
        hb = (a * jax.nn.sigmoid(a)) * _dot(xb, w3b[...])
        ys_ref[...] = _dot(hb.astype(BF16), w2b[...]).astype(BF16)

    @pl.when(t == n_tiles_ref[0])
    def _spare():
        ys_ref[...] = jnp.zeros(ys_ref.shape, BF16)


def _experts(xs, tile_e, n_tiles, w1, w3, w2):
    t_max = xs.shape[0] // EXPERT_TILE
    clamp = lambda t, nt: jnp.minimum(t, nt[0] - 1)
    any_spec = pl.BlockSpec(memory_space=pl.ANY)
    w_in = pltpu.VMEM((D_MODEL, EXPERT_FF), F32)
    w_out = pltpu.VMEM((EXPERT_FF, D_MODEL), F32)
    return pl.pallas_call(
        _expert_kernel,
        grid_spec=pltpu.PrefetchScalarGridSpec(
            num_scalar_prefetch=2,
            grid=(t_max,),
            in_specs=[pl.BlockSpec((EXPERT_TILE, D_MODEL), lambda t, te, nt: (clamp(t, nt), 0)),
                      any_spec, any_spec, any_spec],
            out_specs=pl.BlockSpec((EXPERT_TILE, D_MODEL),
                                   lambda t, te, nt: (jnp.where(t < nt[0], t, t_max), 0)),
            scratch_shapes=[
                w_in, w_in, w_in, w_in, w_out, w_out,
                pltpu.VMEM((D_MODEL, EXPERT_FF), BF16),
                pltpu.VMEM((D_MODEL, EXPERT_FF), BF16),
                pltpu.VMEM((EXPERT_FF, D_MODEL), BF16),
                pltpu.SemaphoreType.DMA((2, 3)),
            ],
        ),
        out_shape=jax.ShapeDtypeStruct(((t_max + 1) * EXPERT_TILE, D_MODEL), BF16),
        compiler_params=pltpu.CompilerParams(
            dimension_semantics=("arbitrary",), vmem_limit_bytes=VMEM_LIMIT),
        name="moe_experts",
    )(tile_e, n_tiles, xs, w1, w3, w2)


def _final_kernel(base_ref, nwin_ref, x1s_ref, rank_ref, gates_ref, ys_hbm, pa_ref, pb_ref, gple_ref,
                  wgate_ref, wproj_ref, gfin_ref, ya_ref, yb_ref, win0, win1, owin, acc_ref, sem, osem,
                  *, split, n_tiles):
    i = pl.program_id(0)
    j = i - 1
    win = (win0, win1)
    parts = 4
    part_chunks = N_CHUNKS // parts
    part_rows = part_chunks * CHUNK_E * CAP
    contract0 = (((0,), (0,)), ((), ()))

    def window_copy(p, e, tile_idx):
        return pltpu.make_async_copy(_window(ys_hbm, base_ref, tile_idx * N_EXPERTS + e, 0),
                                     win[p].at[pl.ds(e * CAP, CAP)], sem.at[p])

    @pl.when(i == 0)
    def _first():
        owin[...] = jnp.zeros(owin.shape, BF16)
        acc_ref[1] = jnp.zeros(acc_ref.shape[1:], F32)
        for e in range(N_EXPERTS):
            window_copy(0, e, i).start()

    def combine_stages(p):
        total = None
        for part in range(parts):
            one_hot = jnp.concatenate(
                [_one_hot_rows(rank_ref, part * part_chunks + c, 0, gates_ref) for c in range(part_chunks)], axis=0)
            d = lax.dot_general(one_hot, win[p][pl.ds(part * part_rows, part_rows), :], contract0,
                                preferred_element_type=F32)
            total = d if total is None else total + d
            yield
        acc_ref[p] = total

    def ple_stages(q):
        half = D_MODEL // 2
        x2 = x1s_ref[...] + acc_ref[q]
        rb = _rms(x2, gple_ref[...]).astype(BF16)
        yield
        g0 = jax.nn.sigmoid(_dot(rb, wgate_ref[:, :half]))
        yield
        g1 = jax.nn.sigmoid(_dot(rb, wgate_ref[:, half:]))
        yield
        pt = jnp.where(j < split, pa_ref[...], pb_ref[...])
        pp = _dot(pt.astype(BF16), wproj_ref[...])
        yield
        y = _rms(x2 + pp * jnp.concatenate([g0, g1], axis=1), gfin_ref[...])

        @pl.when(j < split)
        def _():
            ya_ref[...] = y

        @pl.when(j >= split)
        def _():
            yb_ref[...] = y

    for p in (0, 1):
        @pl.when((i < n_tiles) & (i % 2 == p))
        def _step(p=p):
            for e in range(N_EXPERTS):
                window_copy(p, e, i).wait()

            @pl.when(i + 1 < n_tiles)
            def _():
                for e in range(N_EXPERTS):
                    window_copy(1 - p, e, i + 1).start()

            side = [combine_stages(p), ple_stages(1 - p)]
            while side:
                _advance(side, 1)

    @pl.when(i == n_tiles)
    def _last_tile():
        for _ in ple_stages((n_tiles - 1) % 2):
            pass

    extra = _tile_windows(nwin_ref, jnp.minimum(i, n_tiles - 1))

    @pl.when((i < n_tiles) & (extra > 1))
    def _long_segments():
        def body(w, carry):
            for c in range(N_CHUNKS):
                @pl.when(w < _chunk_windows(nwin_ref, i, c))
                def _(c=c):
                    for k in range(CHUNK_E):
                        idx = i * N_EXPERTS + c * CHUNK_E + k

                        @pl.when(w < nwin_ref[idx])
                        def _(k=k, idx=idx):
                            cp = pltpu.make_async_copy(_window(ys_hbm, base_ref, idx, w),
                                                       owin.at[pl.ds(k * CAP, CAP)], osem)
                            cp.start()
                            cp.wait()
                    acc_ref[i % 2] += lax.dot_general(_one_hot_rows(rank_ref, c, w * CAP, gates_ref), owin[...],
                                                      contract0, preferred_element_type=F32)
            return carry

        lax.fori_loop(1, extra, body, 0)


def _final(x1s, rank_t, gates_t, ys, pa, pb, gple, wgate, wproj, gfin, base, nwin, *, TM):
    N = x1s.shape[0]
    n_tiles = N // TM
    split = pa.shape[0] // TM
    prev = lambda i: jnp.maximum(i - 1, 0)
    tok = pl.BlockSpec((TM, D_MODEL), lambda i, *_: (prev(i), 0))
    per_e = pl.BlockSpec((N_EXPERTS, TM), lambda i, *_: (0, jnp.minimum(i, n_tiles - 1)))
    consts = (gple, wgate, wproj, gfin)
    win = pltpu.VMEM((N_EXPERTS * CAP, D_MODEL), BF16)

    def group_specs(block):
        a, b = _group_specs(block, split)
        return (pl.BlockSpec(block, lambda i, *_: a.index_map(prev(i))),
                pl.BlockSpec(block, lambda i, *_: b.index_map(prev(i))))

    return pl.pallas_call(
        functools.partial(_final_kernel, split=split, n_tiles=n_tiles),
        grid_spec=pltpu.PrefetchScalarGridSpec(
            num_scalar_prefetch=2,
            grid=(n_tiles + 1,),
            in_specs=[tok, per_e, per_e, pl.BlockSpec(memory_space=pl.ANY)]
            + list(group_specs((TM, PLE_DIM)))
            + [pl.BlockSpec(c.shape, lambda i, *_: (0, 0)) for c in consts],
            out_specs=group_specs((TM, D_MODEL)),
            scratch_shapes=[win, win, pltpu.VMEM((CHUNK_E * CAP, D_MODEL), BF16),
                            pltpu.VMEM((2, TM, D_MODEL), F32),
                            pltpu.SemaphoreType.DMA((2,)), pltpu.SemaphoreType.DMA(())],
        ),
        out_shape=(jax.ShapeDtypeStruct((pa.shape[0], D_MODEL), F32),
                   jax.ShapeDtypeStruct((pb.shape[0], D_MODEL), F32)),
        compiler_params=pltpu.CompilerParams(
            dimension_semantics=("arbitrary",), vmem_limit_bytes=VMEM_LIMIT),
        name="combine_ple_final",
    )(base, nwin, x1s, rank_t, gates_t, ys, pa, pb, *consts)


def kernel(x_prompt, x_sample, p_prompt, p_sample, state_C, state_n, state_m, state_pool, norm_mix_g, w_in, b_igate, b_fgate, head_norm_g, w_pool, pool_scale, w_out, norm_ffn_g, router_w, router_bias, ex_w1, ex_w3, ex_w2, sh_w1, sh_w3, sh_w2, norm_ple_g, w_ple_gate, w_ple_proj, final_norm_g):
    depth = norm_mix_g.shape[0]
    assert depth == 1
    l = 0
    B, T, _ = x_prompt.shape
    Bs, Ts, _ = x_sample.shape
    g0 = 4 * MLSTM_WIDTH
    w = w_in[l]
    lane_pad = jnp.zeros((D_MODEL, 128 - HEADS), F32)
    wcat = jnp.concatenate(
        [w[:, :g0], w[:, g0 + 2 * HEADS:], w[:, g0:g0 + HEADS], lane_pad,
         w[:, g0 + HEADS:g0 + 2 * HEADS], lane_pad], axis=1).astype(BF16)
    bias_pad = jnp.zeros((128 - HEADS,), F32)
    gbias = jnp.concatenate([b_igate[l], bias_pad, b_fgate[l], bias_pad])[None, :]
    mixer_w = (wcat, norm_mix_g[l][None, :], gbias, head_norm_g[l][None, :], w_pool[l].astype(BF16),
               pool_scale[l][None, :], w_out[l].astype(BF16))

    x1p, Cp, Np, Mp, Bp = _mixer(x_prompt, *mixer_w, None, BB=1, TT=512, L=128, start_pos=0)
    state = (state_C[l], state_n[l], state_m[l][..., None], state_pool[l])
    x1s_, Cs, Ns, Ms, Bs_ = _mixer(x_sample, *mixer_w, state, BB=16, TT=Ts, L=Ts, start_pos=PAST_LEN)

    N = B * T + Bs * Ts
    assert (B * T) % ROUTE_TILE == 0 and (Bs * Ts) % ROUTE_TILE == 0
    n_route_tiles = N // ROUTE_TILE

    xn, rank_t, gates_t, cnt, x1sh = _router(
        x1p.reshape(B * T, D_MODEL), x1s_.reshape(Bs * Ts, D_MODEL),
        norm_ffn_g[l][None, :], router_w[l].T, router_bias[l][:, None],
        sh_w1[l].astype(BF16), sh_w3[l].astype(BF16), sh_w2[l].astype(BF16), TM=2 * ROUTE_TILE)

    max_rows = TOP_K * N + (SEG_ALIGN - 1) * n_route_tiles * N_EXPERTS + N_EXPERTS * (CAP + EXPERT_TILE)
    t_max = -(-max_rows // EXPERT_TILE)
    cnt = cnt[:, 0].reshape(n_route_tiles, N_EXPERTS).astype(jnp.int32)
    base, nwin, used_end, row_end, tile_e, n_tiles = _segment_plan(cnt, t_max)

    xs = _dispatch(xn, rank_t, base, nwin, used_end, row_end, n_rows=t_max * EXPERT_TILE, TM=ROUTE_TILE)
    ys = _experts(xs, tile_e, n_tiles, ex_w1[l], ex_w3[l], ex_w2[l])
    y_prompt, y_sample = _final(
        x1sh, rank_t, gates_t, ys, p_prompt[l].reshape(B * T, PLE_DIM), p_sample[l].reshape(Bs * Ts, PLE_DIM),
        norm_ple_g[l][None, :], w_ple_gate[l].astype(BF16), w_ple_proj[l].astype(BF16),
        final_norm_g[None, :], base, nwin, TM=ROUTE_TILE)
    return (y_prompt.reshape(B, T, D_MODEL), y_sample.reshape(Bs, Ts, D_MODEL),
            Cp[None], Np[None], Mp[..., 0][None], Bp[None],
            Cs[None], Ns[None], Ms[..., 0][None], Bs_[None])
```

```python
import functools

import jax
import jax.numpy as jnp
from jax import lax
from jax.experimental import pallas as pl
from jax.experimental.pallas import tpu as pltpu

D_MODEL = 1024
HEADS = 4
DH = 128
MLSTM_WIDTH = HEADS * DH
POOL_WIDTH = 512
POOL_WINDOWS = (2, 4, 8, 16)
POOL_GDIM = 128
POOL_BUF = 15
POOL_PAD = 16
N_EXPERTS = 64
TOP_K = 8
N_GROUPS = 8
GROUP_SIZE = N_EXPERTS // N_GROUPS
TOPK_GROUPS = 4
EXPERT_FF = 256
ROUTED_SCALE = 2.5
NORM_EPS = 1e-6
PLE_DIM = 256
PAST_LEN = 16384

COL_Q, COL_K, COL_V, COL_O, COL_U, COL_I, COL_F = 0, 512, 1024, 1536, 2048, 2560, 2688
IN_COLS_PAD = 2816
GATE_COLS = HEADS * 128

ROUTE_TILE = 256
EXPERT_TILE = 1440
ZERO_ROWS = 1024
SEG_ALIGN = 16
CAP = 48
CHUNK_E = 8
N_CHUNKS = N_EXPERTS // CHUNK_E

VMEM_LIMIT = 56 * 1024 * 1024
F32 = jnp.float32
BF16 = jnp.bfloat16
NEG_INF = float("-inf")


def _rms(x, g):
    return x * lax.rsqrt(jnp.mean(x * x, axis=-1, keepdims=True) + NORM_EPS) * g


def _log_sigmoid(x):
    return jnp.minimum(x, 0.0) - jnp.log1p(jnp.exp(-jnp.abs(x)))


def _dot(a, b):
    return jnp.dot(a, b, preferred_element_type=F32)


def _mlstm_chunk(q, k, v, b_col, i_col, C, n, m, causal, eye):
    L = q.shape[0]
    r_col = i_col - b_col
    r_row = jnp.sum(jnp.where(eye, r_col, 0.0), axis=0, keepdims=True)
    d = jnp.where(causal, b_col + r_row, NEG_INF)
    inter = b_col + m
    qb, kb, vb = q.astype(BF16), k.astype(BF16), v.astype(BF16)
    qk = lax.dot_general(qb, kb, (((1,), (1,)), ((), ())), preferred_element_type=F32)
    qc = _dot(qb, C.astype(BF16))
    qn = jnp.sum(q * n, axis=-1, keepdims=True)
    yield
    m_t = jnp.maximum(inter, jnp.max(d, axis=-1, keepdims=True))
    b_last = b_col[L - 1:L, :]
    m_new = jnp.maximum(b_last + m, jnp.max(b_last + r_row, axis=-1, keepdims=True))
    yield
    w_inter = jnp.exp(inter - m_t)
    s = qk * jnp.exp(d - m_t)
    fw = jnp.exp(b_last + m - m_new)
    iw_col = jnp.exp(b_last + r_col - m_new)
    kw = iw_col * k
    yield
    sv = _dot(s.astype(BF16), vb)
    kv = lax.dot_general(kw.astype(BF16), vb, (((0,), (0,)), ((), ())), preferred_element_type=F32)
    ssum = jnp.sum(s, axis=-1, keepdims=True)
    n_new = fw * n + jnp.sum(kw, axis=0, keepdims=True)
    yield
    num = w_inter * qc + sv
    nq = w_inter * qn + ssum
    h = num / jnp.maximum(jnp.abs(nq), jnp.exp(-m_t))
    C_new = fw * C + kv
    return h, C_new, n_new, m_new


def _interleave(gens):
    results = [None] * len(gens)
    live = list(enumerate(gens))
    while live:
        still = []
        for idx, g in live:
            try:
                next(g)
                still.append((idx, g))
            except StopIteration as stop:
                results[idx] = stop.value
        live = still
    return results


def _mlstm_chunk_square(q, k, v, r, b, mp, CN, m, causal, eye, ones):
    L = q.shape[0]
    r_row = jnp.sum(jnp.where(eye, r, 0.0), axis=0, keepdims=True)
    g = jnp.maximum(mp, m)
    qb, kb = q.astype(BF16), k.astype(BF16)
    v1 = jnp.concatenate([v.astype(BF16), ones], axis=1)
    qk = lax.dot_general(qb, kb, (((1,), (1,)), ((), ())), preferred_element_type=F32)
    qcn = _dot(qb, CN.astype(BF16))
    yield
    w_inter = jnp.exp(m - g)
    p = jnp.where(causal, jnp.exp(r_row - g), 0.0)
    g_last = g[L - 1:L, :]
    m_new = b[L - 1:L, :] + g_last
    fw = jnp.exp(m - g_last)
    kw = (jnp.exp(r - g_last) * k).astype(BF16)
    floor = jnp.exp(-(b + g))
    yield
    s = (qk * p).astype(BF16)
    sv = _dot(s, v1)
    kv = lax.dot_general(kw, v1, (((0,), (0,)), ((), ())), preferred_element_type=F32)
    yield
    num = w_inter * qcn[:, :DH] + sv[:, :DH]
    nq = w_inter * qcn[:, DH:] + sv[:, DH:]
    h = num / jnp.maximum(jnp.abs(nq), floor)
    CN_new = jnp.concatenate([fw, fw], axis=1) * CN + kv
    return h, CN_new, m_new[:, 0:1]


def _chunk_scan(x, L, op, fill):
    pos = lax.broadcasted_iota(jnp.int32, x.shape, 0) & (L - 1)
    k = 1
    while k < L:
        x = op(x, jnp.where(pos >= k, pltpu.roll(x, k, axis=0), fill))
        k *= 2
        yield
    return x


def _advance(gens, steps):
    for g in list(gens):
        for _ in range(steps):
            try:
                next(g)
            except StopIteration:
                gens.remove(g)
                break


def _mixer_kernel(*refs, BB, TT, L, start_pos, zero_state):
    if zero_state:
        (x_ref, wcat_ref, gmix_ref, gbias_ref, hng_ref, wpool_ref, pscale_ref, wout_ref,
         x1_ref, c_ref, n_ref, m_ref, buf_ref,
         z_ref, a_ref, b_ref, mp_ref, cn_ref, mix_ref, ext_ref) = refs
    else:
        (x_ref, wcat_ref, gmix_ref, gbias_ref, hng_ref, wpool_ref, pscale_ref, wout_ref,
         c0_ref, n0_ref, m0_ref, buf0_ref,
         x1_ref, c_ref, n_ref, m_ref, buf_ref,
         z_ref, a_ref, b_ref, mp_ref, cn_ref, mix_ref, ext_ref) = refs
    t = pl.program_id(1)
    R = BB * TT
    n_chunks = TT // L
    square = L == DH
    assert not square or (zero_state and BB == 1)

    @pl.when(t == 0)
    def _init():
        ext_ref[:, 0:POOL_PAD, :] = jnp.zeros((BB, POOL_PAD, POOL_WIDTH), F32)
        if zero_state:
            cn_ref[...] = jnp.zeros(cn_ref.shape, F32)
            c_ref[...] = jnp.zeros(c_ref.shape, F32)
            n_ref[...] = jnp.zeros(n_ref.shape, F32)
            m_ref[...] = jnp.zeros(m_ref.shape, F32)
        else:
            c_ref[...] = c0_ref[...]
            n_ref[...] = n0_ref[...]
            m_ref[...] = m0_ref[...]
            ext_ref[:, 1:POOL_PAD, :] = buf0_ref[...]

    hb = _rms(x_ref[...].reshape(R, D_MODEL), gmix_ref[...]).astype(BF16)

    def project(lo, hi):
        z_ref[:, lo:hi] = _dot(hb, wcat_ref[:, lo:hi])

    def gate_stages():
        gbias = gbias_ref[...]
        gi = z_ref[:, COL_I:COL_I + 128] + gbias[:, :128]
        lf = _log_sigmoid(z_ref[:, COL_F:COL_F + 128] + gbias[:, 128:])
        yield
        b_c = yield from _chunk_scan(lf, L, jnp.add, 0.0)
        if square:
            r_c = gi - b_c
            mp_c = yield from _chunk_scan(r_c, L, jnp.maximum, NEG_INF)
            for hd in range(HEADS):
                lanes = slice(hd * DH, (hd + 1) * DH)
                a_ref[:, lanes] = jnp.broadcast_to(r_c[:, hd:hd + 1], (R, DH))
                b_ref[:, lanes] = jnp.broadcast_to(b_c[:, hd:hd + 1], (R, DH))
                mp_ref[:, lanes] = jnp.broadcast_to(mp_c[:, hd:hd + 1], (R, DH))
                yield
        else:
            a_ref[:, 0:128] = gi
            b_ref[:, 0:128] = b_c

    def pool_stages():
        ext_ref[:, POOL_PAD:POOL_PAD + TT, :] = z_ref[:, COL_U:COL_U + POOL_WIDTH].reshape(BB, TT, POOL_WIDTH)
        pos = start_pos + t * TT + lax.broadcasted_iota(jnp.int32, (1, TT, 1), 1)
        pscale = pscale_ref[...]
        yield
        for gidx, w in enumerate(POOL_WINDOWS):
            lanes = slice(gidx * POOL_GDIM, (gidx + 1) * POOL_GDIM)
            u_g = ext_ref[:, POOL_PAD:POOL_PAD + TT, lanes]
            acc = u_g
            for j in range(1, w):
                acc = acc + ext_ref[:, POOL_PAD - j:POOL_PAD - j + TT, lanes]
                if j % 4 == 3:
                    yield
            cnt = jnp.minimum(pos + 1, w).astype(F32)
            pooled = (acc / cnt - u_g).reshape(R, POOL_GDIM)
            mixed = _dot(pooled.astype(BF16), wpool_ref[gidx]) * pscale[:, lanes]
            mix_ref[:, MLSTM_WIDTH + gidx * POOL_GDIM:MLSTM_WIDTH + (gidx + 1) * POOL_GDIM] = mixed
            yield
        new_buf = ext_ref[:, TT + 1:TT + POOL_PAD, :]
        buf_ref[...] = new_buf
        ext_ref[:, 1:POOL_PAD, :] = new_buf

    project(COL_I, IN_COLS_PAD)
    project(COL_U, COL_U + 256)
    project(COL_U + 256, COL_I)
    side = [gate_stages(), pool_stages()]
    for lo in range(0, COL_U, 256):
        project(lo, lo + 256)
        _advance(side, 3)
    while side:
        _advance(side, 1)

    row = lax.broadcasted_iota(jnp.int32, (L, L), 0)
    col = lax.broadcasted_iota(jnp.int32, (L, L), 1)
    causal = row >= col
    eye = row == col
    ones = jnp.ones((DH, DH), BF16)
    hng = hng_ref[...]

    group = 4 if (not square and n_chunks == 1 and BB % 4 == 0) else 1

    def chunk_body(s, carry):
        work = []
        for j in range(group):
            sj = s * group + j
            rows = pl.ds(pl.multiple_of(sj * L, L), L)
            bb = 0 if BB == 1 else sj // n_chunks
            for hd in range(HEADS):
                lanes = slice(hd * DH, (hd + 1) * DH)
                q = z_ref[rows, COL_Q + hd * DH:COL_Q + (hd + 1) * DH] * (DH ** -0.5)
                k = z_ref[rows, COL_K + hd * DH:COL_K + (hd + 1) * DH]
                v = z_ref[rows, COL_V + hd * DH:COL_V + (hd + 1) * DH]
                o = z_ref[rows, COL_O + hd * DH:COL_O + (hd + 1) * DH]
                m = m_ref[bb, hd:hd + 1, :]
                if square:
                    state = (a_ref[rows, lanes], b_ref[rows, lanes], mp_ref[rows, lanes], cn_ref[hd])
                else:
                    state = (b_ref[rows, hd:hd + 1], a_ref[rows, hd:hd + 1], c_ref[bb, hd],
                             n_ref[bb, hd:hd + 1, :])
                work.append((rows, bb, hd, lanes, q, k, v, o, m, state))
        if square:
            results = _interleave([_mlstm_chunk_square(q, k, v, *state, m, causal, eye, ones)
                                   for _, _, _, _, q, k, v, _, m, state in work])
        else:
            results = _interleave([_mlstm_chunk(q, k, v, *state, m, causal, eye)
                                   for _, _, _, _, q, k, v, _, m, state in work])
        scales = [lax.rsqrt(jnp.mean(res[0] * res[0], axis=-1, keepdims=True) + NORM_EPS) for res in results]
        done = []
        for (rows, bb, hd, lanes, _, _, _, o, _, _), res, scale in zip(work, results, scales):
            out = res[0] * scale * hng[:, lanes] * jax.nn.sigmoid(o)
            done.append((rows, bb, hd, lanes, out, res[-1], res[1:-1]))
        for rows, bb, hd, lanes, out, m_new, new_state in done:
            mix_ref[rows, lanes] = out
            m_ref[bb, hd:hd + 1, :] = m_new
            if square:
                cn_ref[hd] = new_state[0]
            else:
                c_ref[bb, hd] = new_state[0]
                n_ref[bb, hd:hd + 1, :] = new_state[1]
        return carry

    lax.fori_loop(0, BB * n_chunks // group, chunk_body, 0)

    if square:
        @pl.when(t == pl.num_programs(1) - 1)
        def _emit_state():
            for hd in range(HEADS):
                cn = cn_ref[hd]
                c_ref[0, hd] = cn[:, :DH]
                n_ref[0, hd:hd + 1, :] = jnp.sum(jnp.where(eye, cn[:, DH:], 0.0), axis=0, keepdims=True)

    out = x_ref[...].reshape(R, D_MODEL) + _dot(mix_ref[...].astype(BF16), wout_ref[...])
    x1_ref[...] = out.reshape(BB, TT, D_MODEL)


def _const_spec(shape):
    nd = len(shape)
    return pl.BlockSpec(shape, lambda b, t, _nd=nd: (0,) * _nd, pipeline_mode=pl.Buffered(1))


def _mixer(x, wcat, gmix, gbias, hng, wpool, pscale, wout, state, *, BB, TT, L, start_pos):
    B, T, _ = x.shape
    zero_state = state is None
    R = BB * TT
    grid = (B // BB, T // TT)
    x_spec = pl.BlockSpec((BB, TT, D_MODEL), lambda b, t: (b, t, 0))
    c_spec = pl.BlockSpec((BB, HEADS, DH, DH), lambda b, t: (b, 0, 0, 0))
    n_spec = pl.BlockSpec((BB, HEADS, DH), lambda b, t: (b, 0, 0))
    m_spec = pl.BlockSpec((BB, HEADS, 1), lambda b, t: (b, 0, 0))
    buf_spec = pl.BlockSpec((BB, POOL_BUF, POOL_WIDTH), lambda b, t: (b, 0, 0))
    weights = (wcat, gmix, gbias, hng, wpool, pscale, wout)
    in_specs = [x_spec] + [_const_spec(w.shape) for w in weights]
    args = [x, *weights]
    if not zero_state:
        in_specs += [c_spec, n_spec, m_spec, buf_spec]
        args += list(state)
    out_shape = (
        jax.ShapeDtypeStruct((B, T, D_MODEL), F32),
        jax.ShapeDtypeStruct((B, HEADS, DH, DH), F32),
        jax.ShapeDtypeStruct((B, HEADS, DH), F32),
        jax.ShapeDtypeStruct((B, HEADS, 1), F32),
        jax.ShapeDtypeStruct((B, POOL_BUF, POOL_WIDTH), F32),
    )
    kern = functools.partial(_mixer_kernel, BB=BB, TT=TT, L=L, start_pos=start_pos, zero_state=zero_state)
    return pl.pallas_call(
        kern,
        grid=grid,
        in_specs=in_specs,
        out_specs=(x_spec, c_spec, n_spec, m_spec, buf_spec),
        out_shape=out_shape,
        scratch_shapes=[
            pltpu.VMEM((R, IN_COLS_PAD), F32),
            pltpu.VMEM((R, GATE_COLS), F32),
            pltpu.VMEM((R, GATE_COLS), F32),
            pltpu.VMEM((R, GATE_COLS), F32),
            pltpu.VMEM((HEADS, DH, 2 * DH), F32),
            pltpu.VMEM((R, D_MODEL), F32),
            pltpu.VMEM((BB, POOL_PAD + TT, POOL_WIDTH), F32),
        ],
        compiler_params=pltpu.CompilerParams(
            dimension_semantics=("arbitrary", "arbitrary"), vmem_limit_bytes=VMEM_LIMIT),
        name="mixer_zero" if zero_state else "mixer_state",
    )(*args)


def _first_index_of_max(work, idx, n, axis):
    mx = jnp.max(work, axis=axis, keepdims=True)
    return jnp.min(jnp.where(work == mx, idx, float(n)), axis=axis, keepdims=True)


def _route(scores_t, bias_t):
    tm = scores_t.shape[1]
    biased = scores_t + bias_t
    b3 = biased.reshape(N_GROUPS, GROUP_SIZE, tm)
    sub = lax.broadcasted_iota(jnp.int32, b3.shape, 1).astype(F32)
    m1 = jnp.max(b3, axis=1, keepdims=True)
    first = jnp.min(jnp.where(b3 == m1, sub, float(GROUP_SIZE)), axis=1, keepdims=True)
    m2 = jnp.max(jnp.where(sub == first, NEG_INF, b3), axis=1, keepdims=True)
    gs = (m1 + m2).reshape(N_GROUPS, tm)
    gidx = lax.broadcasted_iota(jnp.int32, gs.shape, 0).astype(F32)
    gsel = jnp.zeros(gs.shape, F32)
    work = gs
    yield
    for _ in range(TOPK_GROUPS):
        pick = gidx == _first_index_of_max(work, gidx, N_GROUPS, 0)
        gsel = jnp.where(pick, 1.0, gsel)
        work = jnp.where(pick, NEG_INF, work)
    emask = jnp.broadcast_to(gsel.reshape(N_GROUPS, 1, tm), b3.shape).reshape(N_EXPERTS, tm)
    work = jnp.where(emask > 0, biased, NEG_INF)
    eidx = lax.broadcasted_iota(jnp.int32, work.shape, 0).astype(F32)
    mask = jnp.zeros(work.shape, F32)
    yield
    for _ in range(TOP_K):
        pick = eidx == _first_index_of_max(work, eidx, N_EXPERTS, 0)
        mask = jnp.where(pick, 1.0, mask)
        work = jnp.where(pick, NEG_INF, work)
        yield
    sel = mask * scores_t
    return mask, sel / jnp.sum(sel, axis=0, keepdims=True) * ROUTED_SCALE


def _group_specs(block, split):
    return (pl.BlockSpec(block, lambda i, *_: (jnp.minimum(i, split - 1), 0)),
            pl.BlockSpec(block, lambda i, *_: (jnp.maximum(i - split, 0), 0)))


def _router_kernel(x1a_ref, x1b_ref, gffn_ref, rwt_ref, rbias_ref, w1_ref, w3_ref, w2_ref,
                   xn_ref, rank_ref, gates_ref, cnt_ref, x1s_ref, *, split):
    x1 = jnp.where(pl.program_id(0) < split, x1a_ref[...], x1b_ref[...])
    tm = x1.shape[0]
    xn = _rms(x1, gffn_ref[...])
    xb = xn.astype(BF16)
    xn_ref[...] = xb
    logits_t = lax.dot_general(rwt_ref[...], xn, (((1,), (1,)), ((), ())),
                               preferred_element_type=F32, precision=lax.Precision.HIGHEST)
    def routing():
        mask, gates = yield from _route(jax.nn.sigmoid(logits_t), rbias_ref[...])
        gates_ref[...] = gates
        rt = ROUTE_TILE
        before = (lax.broadcasted_iota(jnp.int32, (rt, rt), 0)
                  < lax.broadcasted_iota(jnp.int32, (rt, rt), 1)).astype(BF16)
        for j in range(tm // rt):
            mj = mask[:, j * rt:(j + 1) * rt]
            rank_ref[:, j * rt:(j + 1) * rt] = jnp.where(mj > 0, _dot(mj.astype(BF16), before), -1.0)
            cnt_ref[j * N_EXPERTS:(j + 1) * N_EXPERTS, :] = jnp.broadcast_to(
                jnp.sum(mj, axis=1, keepdims=True), (N_EXPERTS, 128))
            yield

    def shared_expert():
        half = D_MODEL // 2
        a = _dot(xb, w1_ref[...])
        yield
        g = _dot(xb, w3_ref[...])
        yield
        hsh = ((a * jax.nn.sigmoid(a)) * g).astype(BF16)
        yield
        x1s_ref[:, :half] = x1[:, :half] + _dot(hsh, w2_ref[:, :half])
        yield
        x1s_ref[:, half:] = x1[:, half:] + _dot(hsh, w2_ref[:, half:])

    side = [routing(), shared_expert()]
    while side:
        _advance(side, 1)


def _router(x1a, x1b, gffn, rwt, rbias, w1, w3, w2, *, TM):
    N = x1a.shape[0] + x1b.shape[0]
    split = x1a.shape[0] // TM
    tok = pl.BlockSpec((TM, D_MODEL), lambda i: (i, 0))
    per_e = pl.BlockSpec((N_EXPERTS, TM), lambda i: (0, i))
    consts = (gffn, rwt, rbias, w1, w3, w2)
    return pl.pallas_call(
        functools.partial(_router_kernel, split=split),
        grid=(N // TM,),
        in_specs=list(_group_specs((TM, D_MODEL), split)) + [pl.BlockSpec(c.shape, lambda i: (0, 0)) for c in consts],
        out_specs=(tok, per_e, per_e, pl.BlockSpec((TM // ROUTE_TILE * N_EXPERTS, 128), lambda i: (i, 0)), tok),
        out_shape=(jax.ShapeDtypeStruct((N, D_MODEL), BF16),
                   jax.ShapeDtypeStruct((N_EXPERTS, N), F32),
                   jax.ShapeDtypeStruct((N_EXPERTS, N), F32),
                   jax.ShapeDtypeStruct((N // ROUTE_TILE * N_EXPERTS, 128), F32),
                   jax.ShapeDtypeStruct((N, D_MODEL), F32)),
        compiler_params=pltpu.CompilerParams(
            dimension_semantics=("arbitrary",), vmem_limit_bytes=VMEM_LIMIT),
        name="router_shared",
    )(x1a, x1b, *consts)


def _segment_plan(cnt, n_tiles_max):
    seg = (cnt + SEG_ALIGN - 1) // SEG_ALIGN * SEG_ALIGN
    used = jnp.sum(seg, axis=0)
    size = (used + CAP + EXPERT_TILE - 1) // EXPERT_TILE * EXPERT_TILE
    row_end = jnp.cumsum(size)
    row_start = row_end - size
    base = row_start[None, :] + jnp.cumsum(seg, axis=0) - seg
    nwin = jnp.maximum((seg + CAP - 1) // CAP, 1)
    tile_end = row_end // EXPERT_TILE
    n_tiles = tile_end[-1]
    t_ids = jnp.arange(n_tiles_max, dtype=jnp.int32)
    tile_e = jnp.minimum(jnp.sum(t_ids[:, None] >= tile_end[None, :], axis=1), N_EXPERTS - 1)
    i32 = lambda a: a.astype(jnp.int32)
    return (i32(base).reshape(-1), i32(nwin).reshape(-1), i32(row_start + used), i32(row_end),
            i32(tile_e), i32(n_tiles).reshape(1))


def _one_hot_rows(rank_ref, chunk, first_row, values_ref=None):
    tm = rank_ref.shape[1]
    j = (lax.broadcasted_iota(jnp.int32, (CAP, tm), 0) + first_row).astype(F32)
    rows = []
    for k in range(CHUNK_E):
        e = chunk * CHUNK_E + k
        hit = j == rank_ref[e:e + 1, :]
        val = 1.0 if values_ref is None else values_ref[e:e + 1, :]
        rows.append(jnp.where(hit, val, 0.0).astype(BF16))
    return jnp.concatenate(rows, axis=0)


def _window(hbm, base_ref, idx, w):
    start = pl.multiple_of(base_ref[idx] + w * CAP, SEG_ALIGN)
    return hbm.at[pl.ds(start, CAP)]


def _chunk_windows(nwin_ref, tile_idx, c):
    extra = nwin_ref[tile_idx * N_EXPERTS + c * CHUNK_E]
    for k in range(1, CHUNK_E):
        extra = jnp.maximum(extra, nwin_ref[tile_idx * N_EXPERTS + c * CHUNK_E + k])
    return extra


def _tile_windows(nwin_ref, tile_idx):
    extra = _chunk_windows(nwin_ref, tile_idx, 0)
    for c in range(1, N_CHUNKS):
        extra = jnp.maximum(extra, _chunk_windows(nwin_ref, tile_idx, c))
    return extra


def _dispatch_kernel(base_ref, nwin_ref, uend_ref, rend_ref, xn_ref, rank_ref, xs_hbm,
                     stage0, stage1, ostage, zbuf, sem, osem, zsem):
    stage = (stage0, stage1)
    i = pl.program_id(0)
    last = pl.num_programs(0) - 1

    def window_copy(p, e, tile_idx):
        return pltpu.make_async_copy(stage[p].at[pl.ds(e * CAP, CAP)],
                                     _window(xs_hbm, base_ref, tile_idx * N_EXPERTS + e, 0), sem.at[p])

    def wait_tile(p, tile_idx):
        for e in range(N_EXPERTS):
            window_copy(p, e, tile_idx).wait()

    @pl.when(i == 0)
    def _zero_tails():
        zbuf[...] = jnp.zeros(zbuf.shape, BF16)
        sizes = []
        assert 2 * ZERO_ROWS > EXPERT_TILE + CAP
        size = ZERO_ROWS
        while size >= SEG_ALIGN:
            sizes.append(size)
            size //= 2
        pieces = []
        for e in range(N_EXPERTS):
            pos = uend_ref[e]
            length = rend_ref[e] - pos
            for size in sizes:
                take = (length & size) != 0
                cp = pltpu.make_async_copy(zbuf.at[pl.ds(0, size)],
                                           xs_hbm.at[pl.ds(pl.multiple_of(pos, SEG_ALIGN), size)], zsem)
                pieces.append((take, cp))
                pos = pos + jnp.where(take, size, 0)
        for take, cp in pieces:
            pl.when(take)(cp.start)
        for take, cp in pieces:
            pl.when(take)(cp.wait)

    for p in (0, 1):
        @pl.when(i % 2 == p)
        def _step(p=p):
            xb = xn_ref[...]
            rows = CHUNK_E * CAP
            for c in range(N_CHUNKS):
                stage[p][pl.ds(c * rows, rows), :] = _dot(_one_hot_rows(rank_ref, c, 0), xb).astype(BF16)

            @pl.when(i >= 1)
            def _():
                wait_tile(1 - p, i - 1)

            for e in range(N_EXPERTS):
                window_copy(p, e, i).start()

    extra = _tile_windows(nwin_ref, i)

    @pl.when(extra > 1)
    def _long_segments():
        def body(w, carry):
            for c in range(N_CHUNKS):
                @pl.when(w < _chunk_windows(nwin_ref, i, c))
                def _(c=c):
                    ostage[...] = _dot(_one_hot_rows(rank_ref, c, w * CAP), xn_ref[...]).astype(BF16)
                    for k in range(CHUNK_E):
                        idx = i * N_EXPERTS + c * CHUNK_E + k

                        @pl.when(w < nwin_ref[idx])
                        def _(k=k, idx=idx):
                            cp = pltpu.make_async_copy(ostage.at[pl.ds(k * CAP, CAP)],
                                                       _window(xs_hbm, base_ref, idx, w), osem)
                            cp.start()
                            cp.wait()
            return carry

        lax.fori_loop(1, extra, body, 0)

    for p in (0, 1):
        @pl.when((i == last) & (i % 2 == p))
        def _drain(p=p):
            wait_tile(p, i)


def _dispatch(xn, rank_t, base, nwin, used_end, row_end, *, n_rows, TM):
    N = xn.shape[0]
    stage = pltpu.VMEM((N_EXPERTS * CAP, D_MODEL), BF16)
    return pl.pallas_call(
        _dispatch_kernel,
        grid_spec=pltpu.PrefetchScalarGridSpec(
            num_scalar_prefetch=4,
            grid=(N // TM,),
            in_specs=[pl.BlockSpec((TM, D_MODEL), lambda i, *_: (i, 0)),
                      pl.BlockSpec((N_EXPERTS, TM), lambda i, *_: (0, i))],
            out_specs=pl.BlockSpec(memory_space=pl.ANY),
            scratch_shapes=[
                stage, stage,
                pltpu.VMEM((CHUNK_E * CAP, D_MODEL), BF16),
                pltpu.VMEM((ZERO_ROWS, D_MODEL), BF16),
                pltpu.SemaphoreType.DMA((2,)),
                pltpu.SemaphoreType.DMA(()),
                pltpu.SemaphoreType.DMA(()),
            ],
        ),
        out_shape=jax.ShapeDtypeStruct((n_rows, D_MODEL), BF16),
        compiler_params=pltpu.CompilerParams(
            dimension_semantics=("arbitrary",), vmem_limit_bytes=VMEM_LIMIT),
        name="moe_dispatch",
    )(base, nwin, used_end, row_end, xn, rank_t)


def _expert_kernel(tile_e_ref, n_tiles_ref, xs_ref, w1_hbm, w3_hbm, w2_hbm, ys_ref,
                   w1f0, w1f1, w3f0, w3f1, w2f0, w2f1, w1b, w3b, w2b, wsem):
    t = pl.program_id(0)
    valid = t < n_tiles_ref[0]
    e = tile_e_ref[t]
    first_tile = valid & ((t == 0) | (e != tile_e_ref[jnp.maximum(t - 1, 0)]))
    f32_bufs = ((w1f0, w3f0, w2f0), (w1f1, w3f1, w2f1))

    def weight_copies(expert, p):
        return [pltpu.make_async_copy(hbm.at[expert], buf, wsem.at[p, j])
                for j, (hbm, buf) in enumerate(zip((w1_hbm, w3_hbm, w2_hbm), f32_bufs[p]))]

    for p in (0, 1):
        @pl.when(first_tile & (e % 2 == p))
        def _next_expert(p=p):
            @pl.when(t == 0)
            def _():
                for cp in weight_copies(e, p):
                    cp.start()

            for cp in weight_copies(e, p):
                cp.wait()

            @pl.when(e + 1 < N_EXPERTS)
            def _():
                for cp in weight_copies(e + 1, 1 - p):
                    cp.start()

            w1b[...] = f32_bufs[p][0][...].astype(BF16)
            w3b[...] = f32_bufs[p][1][...].astype(BF16)
            w2b[...] = f32_bufs[p][2][...].astype(BF16)

    @pl.when(valid)
    def _compute():
        xb = xs_ref[...]
        a = _dot(xb, w1b[...])
        hb = (a * jax.nn.sigmoid(a)) * _dot(xb, w3b[...])
        ys_ref[...] = _dot(hb.astype(BF16), w2b[...]).astype(BF16)

    @pl.when(t == n_tiles_ref[0])
    def _spare():
        ys_ref[...] = jnp.zeros(ys_ref.shape, BF16)


def _experts(xs, tile_e, n_tiles, w1, w3, w2):
    t_max = xs.shape[0] // EXPERT_TILE
    clamp = lambda t, nt: jnp.minimum(t, nt[0] - 1)
    any_spec = pl.BlockSpec(memory_space=pl.ANY)
    w_in = pltpu.VMEM((D_MODEL, EXPERT_FF), F32)
    w_out = pltpu.VMEM((EXPERT_FF, D_MODEL), F32)
    return pl.pallas_call(
        _expert_kernel,
        grid_spec=pltpu.PrefetchScalarGridSpec(
            num_scalar_prefetch=2,
            grid=(t_max,),
            in_specs=[pl.BlockSpec((EXPERT_TILE, D_MODEL), lambda t, te, nt: (clamp(t, nt), 0)),
                      any_spec, any_spec, any_spec],
            out_specs=pl.BlockSpec((EXPERT_TILE, D_MODEL),
                                   lambda t, te, nt: (jnp.where(t < nt[0], t, t_max), 0)),
            scratch_shapes=[
                w_in, w_in, w_in, w_in, w_out, w_out,
                pltpu.VMEM((D_MODEL, EXPERT_FF), BF16),
                pltpu.VMEM((D_MODEL, EXPERT_FF), BF16),
                pltpu.VMEM((EXPERT_FF, D_MODEL), BF16),
                pltpu.SemaphoreType.DMA((2, 3)),
            ],
        ),
        out_shape=jax.ShapeDtypeStruct(((t_max + 1) * EXPERT_TILE, D_MODEL), BF16),
        compiler_params=pltpu.CompilerParams(
            dimension_semantics=("arbitrary",), vmem_limit_bytes=VMEM_LIMIT),
        name="moe_experts",
    )(tile_e, n_tiles, xs, w1, w3, w2)


def _final_kernel(base_ref, nwin_ref, x1s_ref, rank_ref, gates_ref, ys_hbm, pa_ref, pb_ref, gple_ref,
                  wgate_ref, wproj_ref, gfin_ref, ya_ref, yb_ref, win0, win1, owin, acc_ref, sem, osem,
                  *, split, n_tiles):
    i = pl.program_id(0)
    j = i - 1
    win = (win0, win1)
    parts = 4
    part_chunks = N_CHUNKS // parts
    part_rows = part_chunks * CHUNK_E * CAP
    contract0 = (((0,), (0,)), ((), ()))

    def window_copy(p, e, tile_idx):
        return pltpu.make_async_copy(_window(ys_hbm, base_ref, tile_idx * N_EXPERTS + e, 0),
                                     win[p].at[pl.ds(e * CAP, CAP)], sem.at[p])

    @pl.when(i == 0)
    def _first():
        owin[...] = jnp.zeros(owin.shape, BF16)
        acc_ref[1] = jnp.zeros(acc_ref.shape[1:], F32)
        for e in range(N_EXPERTS):
            window_copy(0, e, i).start()

    def combine_stages(p):
        total = None
        for part in range(parts):
            one_hot = jnp.concatenate(
                [_one_hot_rows(rank_ref, part * part_chunks + c, 0, gates_ref) for c in range(part_chunks)], axis=0)
            d = lax.dot_general(one_hot, win[p][pl.ds(part * part_rows, part_rows), :], contract0,
                                preferred_element_type=F32)
            total = d if total is None else total + d
            yield
        acc_ref[p] = total

    def ple_stages(q):
        half = D_MODEL // 2
        x2 = x1s_ref[...] + acc_ref[q]
        rb = _rms(x2, gple_ref[...]).astype(BF16)
        yield
        g0 = jax.nn.sigmoid(_dot(rb, wgate_ref[:, :half]))
        yield
        g1 = jax.nn.sigmoid(_dot(rb, wgate_ref[:, half:]))
        yield
        pt = jnp.where(j < split, pa_ref[...], pb_ref[...])
        pp = _dot(pt.astype(BF16), wproj_ref[...])
        yield
        y = _rms(x2 + pp * jnp.concatenate([g0, g1], axis=1), gfin_ref[...])

        @pl.when(j < split)
        def _():
            ya_ref[...] = y

        @pl.when(j >= split)
        def _():
            yb_ref[...] = y

    for p in (0, 1):
        @pl.when((i < n_tiles) & (i % 2 == p))
        def _step(p=p):
            for e in range(N_EXPERTS):
                window_copy(p, e, i).wait()

            @pl.when(i + 1 < n_tiles)
            def _():
                for e in range(N_EXPERTS):
                    window_copy(1 - p, e, i + 1).start()

            side = [combine_stages(p), ple_stages(1 - p)]
            while side:
                _advance(side, 1)

    @pl.when(i == n_tiles)
    def _last_tile():
        for _ in ple_stages((n_tiles - 1) % 2):
            pass

    extra = _tile_windows(nwin_ref, jnp.minimum(i, n_tiles - 1))

    @pl.when((i < n_tiles) & (extra > 1))
    def _long_segments():
        def body(w, carry):
            for c in range(N_CHUNKS):
                @pl.when(w < _chunk_windows(nwin_ref, i, c))
                def _(c=c):
                    for k in range(CHUNK_E):
                        idx = i * N_EXPERTS + c * CHUNK_E + k

                        @pl.when(w < nwin_ref[idx])
                        def _(k=k, idx=idx):
                            cp = pltpu.make_async_copy(_window(ys_hbm, base_ref, idx, w),
                                                       owin.at[pl.ds(k * CAP, CAP)], osem)
                            cp.start()
                            cp.wait()
                    acc_ref[i % 2] += lax.dot_general(_one_hot_rows(rank_ref, c, w * CAP, gates_ref), owin[...],
                                                      contract0, preferred_element_type=F32)
            return carry

        lax.fori_loop(1, extra, body, 0)


def _final(x1s, rank_t, gates_t, ys, pa, pb, gple, wgate, wproj, gfin, base, nwin, *, TM):
    N = x1s.shape[0]
    n_tiles = N // TM
    split = pa.shape[0] // TM
    prev = lambda i: jnp.maximum(i - 1, 0)
    tok = pl.BlockSpec((TM, D_MODEL), lambda i, *_: (prev(i), 0))
    per_e = pl.BlockSpec((N_EXPERTS, TM), lambda i, *_: (0, jnp.minimum(i, n_tiles - 1)))
    consts = (gple, wgate, wproj, gfin)
    win = pltpu.VMEM((N_EXPERTS * CAP, D_MODEL), BF16)

    def group_specs(block):
        a, b = _group_specs(block, split)
        return (pl.BlockSpec(block, lambda i, *_: a.index_map(prev(i))),
                pl.BlockSpec(block, lambda i, *_: b.index_map(prev(i))))

    return pl.pallas_call(
        functools.partial(_final_kernel, split=split, n_tiles=n_tiles),
        grid_spec=pltpu.PrefetchScalarGridSpec(
            num_scalar_prefetch=2,
            grid=(n_tiles + 1,),
            in_specs=[tok, per_e, per_e, pl.BlockSpec(memory_space=pl.ANY)]
            + list(group_specs((TM, PLE_DIM)))
            + [pl.BlockSpec(c.shape, lambda i, *_: (0, 0)) for c in consts],
            out_specs=group_specs((TM, D_MODEL)),
            scratch_shapes=[win, win, pltpu.VMEM((CHUNK_E * CAP, D_MODEL), BF16),
                            pltpu.VMEM((2, TM, D_MODEL), F32),
                            pltpu.SemaphoreType.DMA((2,)), pltpu.SemaphoreType.DMA(())],
        ),
        out_shape=(jax.ShapeDtypeStruct((pa.shape[0], D_MODEL), F32),
                   jax.ShapeDtypeStruct((pb.shape[0], D_MODEL), F32)),
        compiler_params=pltpu.CompilerParams(
            dimension_semantics=("arbitrary",), vmem_limit_bytes=VMEM_LIMIT),
        name="combine_ple_final",
    )(base, nwin, x1s, rank_t, gates_t, ys, pa, pb, *consts)


def kernel(x_prompt, x_sample, p_prompt, p_sample, state_C, state_n, state_m, state_pool, norm_mix_g, w_in, b_igate, b_fgate, head_norm_g, w_pool, pool_scale, w_out, norm_ffn_g, router_w, router_bias, ex_w1, ex_w3, ex_w2, sh_w1, sh_w3, sh_w2, norm_ple_g, w_ple_gate, w_ple_proj, final_norm_g):
    depth = norm_mix_g.shape[0]
    assert depth == 1
    l = 0
    B, T, _ = x_prompt.shape
    Bs, Ts, _ = x_sample.shape
    g0 = 4 * MLSTM_WIDTH
    w = w_in[l]
    lane_pad = jnp.zeros((D_MODEL, 128 - HEADS), F32)
    wcat = jnp.concatenate(
        [w[:, :g0], w[:, g0 + 2 * HEADS:], w[:, g0:g0 + HEADS], lane_pad,
         w[:, g0 + HEADS:g0 + 2 * HEADS], lane_pad], axis=1).astype(BF16)
    bias_pad = jnp.zeros((128 - HEADS,), F32)
    gbias = jnp.concatenate([b_igate[l], bias_pad, b_fgate[l], bias_pad])[None, :]
    mixer_w = (wcat, norm_mix_g[l][None, :], gbias, head_norm_g[l][None, :], w_pool[l].astype(BF16),
               pool_scale[l][None, :], w_out[l].astype(BF16))

    x1p, Cp, Np, Mp, Bp = _mixer(x_prompt, *mixer_w, None, BB=1, TT=1024, L=128, start_pos=0)
    state = (state_C[l], state_n[l], state_m[l][..., None], state_pool[l])
    x1s_, Cs, Ns, Ms, Bs_ = _mixer(x_sample, *mixer_w, state, BB=16, TT=Ts, L=Ts, start_pos=PAST_LEN)

    N = B * T + Bs * Ts
    assert (B * T) % ROUTE_TILE == 0 and (Bs * Ts) % ROUTE_TILE == 0
    n_route_tiles = N // ROUTE_TILE

    xn, rank_t, gates_t, cnt, x1sh = _router(
        x1p.reshape(B * T, D_MODEL), x1s_.reshape(Bs * Ts, D_MODEL),
        norm_ffn_g[l][None, :], router_w[l].T, router_bias[l][:, None],
        sh_w1[l].astype(BF16), sh_w3[l].astype(BF16), sh_w2[l].astype(BF16), TM=2 * ROUTE_TILE)

    max_rows = TOP_K * N + (SEG_ALIGN - 1) * n_route_tiles * N_EXPERTS + N_EXPERTS * (CAP + EXPERT_TILE)
    t_max = -(-max_rows // EXPERT_TILE)
    cnt = cnt[:, 0].reshape(n_route_tiles, N_EXPERTS).astype(jnp.int32)
    base, nwin, used_end, row_end, tile_e, n_tiles = _segment_plan(cnt, t_max)

    xs = _dispatch(xn, rank_t, base, nwin, used_end, row_end, n_rows=t_max * EXPERT_TILE, TM=ROUTE_TILE)
    ys = _experts(xs, tile_e, n_tiles, ex_w1[l], ex_w3[l], ex_w2[l])
    y_prompt, y_sample = _final(
        x1sh, rank_t, gates_t, ys, p_prompt[l].reshape(B * T, PLE_DIM), p_sample[l].reshape(Bs * Ts, PLE_DIM),
        norm_ple_g[l][None, :], w_ple_gate[l].astype(BF16), w_ple_proj[l].astype(BF16),
        final_norm_g[None, :], base, nwin, TM=ROUTE_TILE)
    return (y_prompt.reshape(B, T, D_MODEL), y_sample.reshape(Bs, Ts, D_MODEL),
            Cp[None], Np[None], Mp[..., 0][None], Bp[None],
            Cs[None], Ns[None], Ms[..., 0][None], Bs_[None])
```

```python
import functools

import jax
import jax.numpy as jnp
from jax import lax
from jax.experimental import pallas as pl
from jax.experimental.pallas import tpu as pltpu

D_MODEL = 1024
HEADS = 4
DH = 128
MLSTM_WIDTH = HEADS * DH
POOL_WIDTH = 512
POOL_WINDOWS = (2, 4, 8, 16)
POOL_GDIM = 128
POOL_BUF = 15
POOL_PAD = 16
N_EXPERTS = 64
TOP_K = 8
N_GROUPS = 8
GROUP_SIZE = N_EXPERTS // N_GROUPS
TOPK_GROUPS = 4
EXPERT_FF = 256
ROUTED_SCALE = 2.5
NORM_EPS = 1e-6
PLE_DIM = 256
PAST_LEN = 16384

COL_Q, COL_K, COL_V, COL_O, COL_U, COL_I, COL_F = 0, 512, 1024, 1536, 2048, 2560, 2688
IN_COLS_PAD = 2816
GATE_COLS = HEADS * 128

ROUTE_TILE = 256
EXPERT_TILE = 1536
TILE_STEP = 512
TILE_HEIGHTS = tuple(range(TILE_STEP, EXPERT_TILE + 1, TILE_STEP))
ZERO_ROWS = 1024
SEG_ALIGN = 16
CAP = 48
CHUNK_E = 8
N_CHUNKS = N_EXPERTS // CHUNK_E

VMEM_LIMIT = 56 * 1024 * 1024
F32 = jnp.float32
BF16 = jnp.bfloat16
NEG_INF = float("-inf")


def _rms(x, g):
    return x * lax.rsqrt(jnp.mean(x * x, axis=-1, keepdims=True) + NORM_EPS) * g


def _log_sigmoid(x):
    return jnp.minimum(x, 0.0) - jnp.log1p(jnp.exp(-jnp.abs(x)))


def _dot(a, b):
    return jnp.dot(a, b, preferred_element_type=F32)


def _mlstm_chunk(q, k, v, b_col, i_col, C, n, m, causal, eye):
    L = q.shape[0]
    r_col = i_col - b_col
    r_row = jnp.sum(jnp.where(eye, r_col, 0.0), axis=0, keepdims=True)
    d = jnp.where(causal, b_col + r_row, NEG_INF)
    inter = b_col + m
    qb, kb, vb = q.astype(BF16), k.astype(BF16), v.astype(BF16)
    qk = lax.dot_general(qb, kb, (((1,), (1,)), ((), ())), preferred_element_type=F32)
    qc = _dot(qb, C.astype(BF16))
    qn = jnp.sum(q * n, axis=-1, keepdims=True)
    yield
    m_t = jnp.maximum(inter, jnp.max(d, axis=-1, keepdims=True))
    b_last = b_col[L - 1:L, :]
    m_new = jnp.maximum(b_last + m, jnp.max(b_last + r_row, axis=-1, keepdims=True))
    yield
    w_inter = jnp.exp(inter - m_t)
    s = qk * jnp.exp(d - m_t)
    fw = jnp.exp(b_last + m - m_new)
    iw_col = jnp.exp(b_last + r_col - m_new)
    kw = iw_col * k
    yield
    sv = _dot(s.astype(BF16), vb)
    kv = lax.dot_general(kw.astype(BF16), vb, (((0,), (0,)), ((), ())), preferred_element_type=F32)
    ssum = jnp.sum(s, axis=-1, keepdims=True)
    n_new = fw * n + jnp.sum(kw, axis=0, keepdims=True)
    yield
    num = w_inter * qc + sv
    nq = w_inter * qn + ssum
    h = num / jnp.maximum(jnp.abs(nq), jnp.exp(-m_t))
    C_new = fw * C + kv
    return h, C_new, n_new, m_new


def _interleave(gens):
    results = [None] * len(gens)
    live = list(enumerate(gens))
    while live:
        still = []
        for idx, g in live:
            try:
                next(g)
                still.append((idx, g))
            except StopIteration as stop:
                results[idx] = stop.value
        live = still
    return results


def _mlstm_chunk_square(q, k, v, r, b, mp, CN, m, causal, eye, ones):
    L = q.shape[0]
    r_row = jnp.sum(jnp.where(eye, r, 0.0), axis=0, keepdims=True)
    g = jnp.maximum(mp, m)
    qb, kb = q.astype(BF16), k.astype(BF16)
    v1 = jnp.concatenate([v.astype(BF16), ones], axis=1)
    qk = lax.dot_general(qb, kb, (((1,), (1,)), ((), ())), preferred_element_type=F32)
    qcn = _dot(qb, CN.astype(BF16))
    yield
    w_inter = jnp.exp(m - g)
    p = jnp.where(causal, jnp.exp(r_row - g), 0.0)
    g_last = g[L - 1:L, :]
    m_new = b[L - 1:L, :] + g_last
    fw = jnp.exp(m - g_last)
    kw = (jnp.exp(r - g_last) * k).astype(BF16)
    floor = jnp.exp(-(b + g))
    yield
    s = (qk * p).astype(BF16)
    sv = _dot(s, v1)
    kv = lax.dot_general(kw, v1, (((0,), (0,)), ((), ())), preferred_element_type=F32)
    yield
    num = w_inter * qcn[:, :DH] + sv[:, :DH]
    nq = w_inter * qcn[:, DH:] + sv[:, DH:]
    h = num / jnp.maximum(jnp.abs(nq), floor)
    CN_new = jnp.concatenate([fw, fw], axis=1) * CN + kv
    return h, CN_new, m_new[:, 0:1]


def _chunk_scan(x, L, op, fill):
    pos = lax.broadcasted_iota(jnp.int32, x.shape, 0) & (L - 1)
    k = 1
    while k < L:
        x = op(x, jnp.where(pos >= k, pltpu.roll(x, k, axis=0), fill))
        k *= 2
        yield
    return x


def _advance(gens, steps):
    for g in list(gens):
        for _ in range(steps):
            try:
                next(g)
            except StopIteration:
                gens.remove(g)
                break


def _mixer_kernel(*refs, BB, TT, L, start_pos, zero_state):
    if zero_state:
        (x_ref, wcat_ref, gmix_ref, gbias_ref, hng_ref, wpool_ref, pscale_ref, wout_ref,
         x1_ref, c_ref, n_ref, m_ref, buf_ref,
         z_ref, a_ref, b_ref, mp_ref, cn_ref, mix_ref, ext_ref) = refs
    else:
        (x_ref, wcat_ref, gmix_ref, gbias_ref, hng_ref, wpool_ref, pscale_ref, wout_ref,
         c0_ref, n0_ref, m0_ref, buf0_ref,
         x1_ref, c_ref, n_ref, m_ref, buf_ref,
         z_ref, a_ref, b_ref, mp_ref, cn_ref, mix_ref, ext_ref) = refs
    t = pl.program_id(1)
    R = BB * TT
    n_chunks = TT // L
    square = L == DH
    assert not square or (zero_state and BB == 1)

    @pl.when(t == 0)
    def _init():
        ext_ref[:, 0:POOL_PAD, :] = jnp.zeros((BB, POOL_PAD, POOL_WIDTH), F32)
        if zero_state:
            cn_ref[...] = jnp.zeros(cn_ref.shape, F32)
            c_ref[...] = jnp.zeros(c_ref.shape, F32)
            n_ref[...] = jnp.zeros(n_ref.shape, F32)
            m_ref[...] = jnp.zeros(m_ref.shape, F32)
        else:
            c_ref[...] = c0_ref[...]
            n_ref[...] = n0_ref[...]
            m_ref[...] = m0_ref[...]
            ext_ref[:, 1:POOL_PAD, :] = buf0_ref[...]

    hb = _rms(x_ref[...].reshape(R, D_MODEL), gmix_ref[...]).astype(BF16)

    def project(lo, hi):
        z_ref[:, lo:hi] = _dot(hb, wcat_ref[:, lo:hi])

    def gate_stages():
        gbias = gbias_ref[...]
        gi = z_ref[:, COL_I:COL_I + 128] + gbias[:, :128]
        lf = _log_sigmoid(z_ref[:, COL_F:COL_F + 128] + gbias[:, 128:])
        yield
        b_c = yield from _chunk_scan(lf, L, jnp.add, 0.0)
        if square:
            r_c = gi - b_c
            mp_c = yield from _chunk_scan(r_c, L, jnp.maximum, NEG_INF)
            for hd in range(HEADS):
                lanes = slice(hd * DH, (hd + 1) * DH)
                a_ref[:, lanes] = jnp.broadcast_to(r_c[:, hd:hd + 1], (R, DH))
                b_ref[:, lanes] = jnp.broadcast_to(b_c[:, hd:hd + 1], (R, DH))
                mp_ref[:, lanes] = jnp.broadcast_to(mp_c[:, hd:hd + 1], (R, DH))
                yield
        else:
            a_ref[:, 0:128] = gi
            b_ref[:, 0:128] = b_c

    def pool_stages():
        ext_ref[:, POOL_PAD:POOL_PAD + TT, :] = z_ref[:, COL_U:COL_U + POOL_WIDTH].reshape(BB, TT, POOL_WIDTH)
        pos = start_pos + t * TT + lax.broadcasted_iota(jnp.int32, (1, TT, 1), 1)
        pscale = pscale_ref[...]
        yield
        for gidx, w in enumerate(POOL_WINDOWS):
            lanes = slice(gidx * POOL_GDIM, (gidx + 1) * POOL_GDIM)
            u_g = ext_ref[:, POOL_PAD:POOL_PAD + TT, lanes]
            acc = u_g
            for j in range(1, w):
                acc = acc + ext_ref[:, POOL_PAD - j:POOL_PAD - j + TT, lanes]
                if j % 4 == 3:
                    yield
            cnt = jnp.minimum(pos + 1, w).astype(F32)
            pooled = (acc / cnt - u_g).reshape(R, POOL_GDIM)
            mixed = _dot(pooled.astype(BF16), wpool_ref[gidx]) * pscale[:, lanes]
            mix_ref[:, MLSTM_WIDTH + gidx * POOL_GDIM:MLSTM_WIDTH + (gidx + 1) * POOL_GDIM] = mixed
            yield
        new_buf = ext_ref[:, TT + 1:TT + POOL_PAD, :]
        buf_ref[...] = new_buf
        ext_ref[:, 1:POOL_PAD, :] = new_buf

    project(COL_I, IN_COLS_PAD)
    project(COL_U, COL_U + 256)
    project(COL_U + 256, COL_I)
    side = [gate_stages(), pool_stages()]
    for lo in range(0, COL_U, 256):
        project(lo, lo + 256)
        _advance(side, 3)
    while side:
        _advance(side, 1)

    row = lax.broadcasted_iota(jnp.int32, (L, L), 0)
    col = lax.broadcasted_iota(jnp.int32, (L, L), 1)
    causal = row >= col
    eye = row == col
    ones = jnp.ones((DH, DH), BF16)
    hng = hng_ref[...]

    group = 4 if (not square and n_chunks == 1 and BB % 4 == 0) else 1

    def chunk_body(s, carry):
        work = []
        for j in range(group):
            sj = s * group + j
            rows = pl.ds(pl.multiple_of(sj * L, L), L)
            bb = 0 if BB == 1 else sj // n_chunks
            for hd in range(HEADS):
                lanes = slice(hd * DH, (hd + 1) * DH)
                q = z_ref[rows, COL_Q + hd * DH:COL_Q + (hd + 1) * DH] * (DH ** -0.5)
                k = z_ref[rows, COL_K + hd * DH:COL_K + (hd + 1) * DH]
                v = z_ref[rows, COL_V + hd * DH:COL_V + (hd + 1) * DH]
                o = z_ref[rows, COL_O + hd * DH:COL_O + (hd + 1) * DH]
                m = m_ref[bb, hd:hd + 1, :]
                if square:
                    state = (a_ref[rows, lanes], b_ref[rows, lanes], mp_ref[rows, lanes], cn_ref[hd])
                else:
                    state = (b_ref[rows, hd:hd + 1], a_ref[rows, hd:hd + 1], c_ref[bb, hd],
                             n_ref[bb, hd:hd + 1, :])
                work.append((rows, bb, hd, lanes, q, k, v, o, m, state))
        if square:
            results = _interleave([_mlstm_chunk_square(q, k, v, *state, m, causal, eye, ones)
                                   for _, _, _, _, q, k, v, _, m, state in work])
        else:
            results = _interleave([_mlstm_chunk(q, k, v, *state, m, causal, eye)
                                   for _, _, _, _, q, k, v, _, m, state in work])
        scales = [lax.rsqrt(jnp.mean(res[0] * res[0], axis=-1, keepdims=True) + NORM_EPS) for res in results]
        done = []
        for (rows, bb, hd, lanes, _, _, _, o, _, _), res, scale in zip(work, results, scales):
            out = res[0] * scale * hng[:, lanes] * jax.nn.sigmoid(o)
            done.append((rows, bb, hd, lanes, out, res[-1], res[1:-1]))
        for rows, bb, hd, lanes, out, m_new, new_state in done:
            mix_ref[rows, lanes] = out
            m_ref[bb, hd:hd + 1, :] = m_new
            if square:
                cn_ref[hd] = new_state[0]
            else:
                c_ref[bb, hd] = new_state[0]
                n_ref[bb, hd:hd + 1, :] = new_state[1]
        return carry

    lax.fori_loop(0, BB * n_chunks // group, chunk_body, 0)

    if square:
        @pl.when(t == pl.num_programs(1) - 1)
        def _emit_state():
            for hd in range(HEADS):
                cn = cn_ref[hd]
                c_ref[0, hd] = cn[:, :DH]
                n_ref[0, hd:hd + 1, :] = jnp.sum(jnp.where(eye, cn[:, DH:], 0.0), axis=0, keepdims=True)

    out = x_ref[...].reshape(R, D_MODEL) + _dot(mix_ref[...].astype(BF16), wout_ref[...])
    x1_ref[...] = out.reshape(BB, TT, D_MODEL)


def _const_spec(shape):
    nd = len(shape)
    return pl.BlockSpec(shape, lambda b, t, _nd=nd: (0,) * _nd, pipeline_mode=pl.Buffered(1))


def _mixer(x, wcat, gmix, gbias, hng, wpool, pscale, wout, state, *, BB, TT, L, start_pos):
    B, T, _ = x.shape
    zero_state = state is None
    R = BB * TT
    grid = (B // BB, T // TT)
    x_spec = pl.BlockSpec((BB, TT, D_MODEL), lambda b, t: (b, t, 0))
    c_spec = pl.BlockSpec((BB, HEADS, DH, DH), lambda b, t: (b, 0, 0, 0))
    n_spec = pl.BlockSpec((BB, HEADS, DH), lambda b, t: (b, 0, 0))
    m_spec = pl.BlockSpec((BB, HEADS, 1), lambda b, t: (b, 0, 0))
    buf_spec = pl.BlockSpec((BB, POOL_BUF, POOL_WIDTH), lambda b, t: (b, 0, 0))
    weights = (wcat, gmix, gbias, hng, wpool, pscale, wout)
    in_specs = [x_spec] + [_const_spec(w.shape) for w in weights]
    args = [x, *weights]
    if not zero_state:
        in_specs += [c_spec, n_spec, m_spec, buf_spec]
        args += list(state)
    out_shape = (
        jax.ShapeDtypeStruct((B, T, D_MODEL), F32),
        jax.ShapeDtypeStruct((B, HEADS, DH, DH), F32),
        jax.ShapeDtypeStruct((B, HEADS, DH), F32),
        jax.ShapeDtypeStruct((B, HEADS, 1), F32),
        jax.ShapeDtypeStruct((B, POOL_BUF, POOL_WIDTH), F32),
    )
    kern = functools.partial(_mixer_kernel, BB=BB, TT=TT, L=L, start_pos=start_pos, zero_state=zero_state)
    return pl.pallas_call(
        kern,
        grid=grid,
        in_specs=in_specs,
        out_specs=(x_spec, c_spec, n_spec, m_spec, buf_spec),
        out_shape=out_shape,
        scratch_shapes=[
            pltpu.VMEM((R, IN_COLS_PAD), F32),
            pltpu.VMEM((R, GATE_COLS), F32),
            pltpu.VMEM((R, GATE_COLS), F32),
            pltpu.VMEM((R, GATE_COLS), F32),
            pltpu.VMEM((HEADS, DH, 2 * DH), F32),
            pltpu.VMEM((R, D_MODEL), F32),
            pltpu.VMEM((BB, POOL_PAD + TT, POOL_WIDTH), F32),
        ],
        compiler_params=pltpu.CompilerParams(
            dimension_semantics=("arbitrary", "arbitrary"), vmem_limit_bytes=VMEM_LIMIT),
        name="mixer_zero" if zero_state else "mixer_state",
    )(*args)


def _first_index_of_max(work, idx, n, axis):
    mx = jnp.max(work, axis=axis, keepdims=True)
    return jnp.min(jnp.where(work == mx, idx, float(n)), axis=axis, keepdims=True)


def _route(scores_t, bias_t):
    tm = scores_t.shape[1]
    biased = scores_t + bias_t
    b3 = biased.reshape(N_GROUPS, GROUP_SIZE, tm)
    sub = lax.broadcasted_iota(jnp.int32, b3.shape, 1).astype(F32)
    m1 = jnp.max(b3, axis=1, keepdims=True)
    first = jnp.min(jnp.where(b3 == m1, sub, float(GROUP_SIZE)), axis=1, keepdims=True)
    m2 = jnp.max(jnp.where(sub == first, NEG_INF, b3), axis=1, keepdims=True)
    gs = (m1 + m2).reshape(N_GROUPS, tm)
    gidx = lax.broadcasted_iota(jnp.int32, gs.shape, 0).astype(F32)
    gsel = jnp.zeros(gs.shape, F32)
    work = gs
    yield
    for _ in range(TOPK_GROUPS):
        pick = gidx == _first_index_of_max(work, gidx, N_GROUPS, 0)
        gsel = jnp.where(pick, 1.0, gsel)
        work = jnp.where(pick, NEG_INF, work)
    emask = jnp.broadcast_to(gsel.reshape(N_GROUPS, 1, tm), b3.shape).reshape(N_EXPERTS, tm)
    work = jnp.where(emask > 0, biased, NEG_INF)
    eidx = lax.broadcasted_iota(jnp.int32, work.shape, 0).astype(F32)
    mask = jnp.zeros(work.shape, F32)
    yield
    for _ in range(TOP_K):
        pick = eidx == _first_index_of_max(work, eidx, N_EXPERTS, 0)
        mask = jnp.where(pick, 1.0, mask)
        work = jnp.where(pick, NEG_INF, work)
        yield
    sel = mask * scores_t
    return mask, sel / jnp.sum(sel, axis=0, keepdims=True) * ROUTED_SCALE


def _group_specs(block, split):
    return (pl.BlockSpec(block, lambda i, *_: (jnp.minimum(i, split - 1), 0)),
            pl.BlockSpec(block, lambda i, *_: (jnp.maximum(i - split, 0), 0)))


def _router_kernel(x1a_ref, x1b_ref, gffn_ref, rwt_ref, rbias_ref, w1_ref, w3_ref, w2_ref,
                   xn_ref, rank_ref, gates_ref, cnt_ref, x1s_ref, *, split):
    x1 = jnp.where(pl.program_id(0) < split, x1a_ref[...], x1b_ref[...])
    tm = x1.shape[0]
    xn = _rms(x1, gffn_ref[...])
    xb = xn.astype(BF16)
    xn_ref[...] = xb
    logits_t = lax.dot_general(rwt_ref[...], xn, (((1,), (1,)), ((), ())),
                               preferred_element_type=F32, precision=lax.Precision.HIGHEST)
    def routing():
        mask, gates = yield from _route(jax.nn.sigmoid(logits_t), rbias_ref[...])
        gates_ref[...] = gates
        rt = ROUTE_TILE
        before = (lax.broadcasted_iota(jnp.int32, (rt, rt), 0)
                  < lax.broadcasted_iota(jnp.int32, (rt, rt), 1)).astype(BF16)
        for j in range(tm // rt):
            mj = mask[:, j * rt:(j + 1) * rt]
            rank_ref[:, j * rt:(j + 1) * rt] = jnp.where(mj > 0, _dot(mj.astype(BF16), before), -1.0)
            cnt_ref[j * N_EXPERTS:(j + 1) * N_EXPERTS, :] = jnp.broadcast_to(
                jnp.sum(mj, axis=1, keepdims=True), (N_EXPERTS, 128))
            yield

    def shared_expert():
        half = D_MODEL // 2
        a = _dot(xb, w1_ref[...])
        yield
        g = _dot(xb, w3_ref[...])
        yield
        hsh = ((a * jax.nn.sigmoid(a)) * g).astype(BF16)
        yield
        x1s_ref[:, :half] = x1[:, :half] + _dot(hsh, w2_ref[:, :half])
        yield
        x1s_ref[:, half:] = x1[:, half:] + _dot(hsh, w2_ref[:, half:])

    side = [routing(), shared_expert()]
    while side:
        _advance(side, 1)


def _router(x1a, x1b, gffn, rwt, rbias, w1, w3, w2, *, TM):
    N = x1a.shape[0] + x1b.shape[0]
    split = x1a.shape[0] // TM
    tok = pl.BlockSpec((TM, D_MODEL), lambda i: (i, 0))
    per_e = pl.BlockSpec((N_EXPERTS, TM), lambda i: (0, i))
    consts = (gffn, rwt, rbias, w1, w3, w2)
    return pl.pallas_call(
        functools.partial(_router_kernel, split=split),
        grid=(N // TM,),
        in_specs=list(_group_specs((TM, D_MODEL), split)) + [pl.BlockSpec(c.shape, lambda i: (0, 0)) for c in consts],
        out_specs=(tok, per_e, per_e, pl.BlockSpec((TM // ROUTE_TILE * N_EXPERTS, 128), lambda i: (i, 0)), tok),
        out_shape=(jax.ShapeDtypeStruct((N, D_MODEL), BF16),
                   jax.ShapeDtypeStruct((N_EXPERTS, N), F32),
                   jax.ShapeDtypeStruct((N_EXPERTS, N), F32),
                   jax.ShapeDtypeStruct((N // ROUTE_TILE * N_EXPERTS, 128), F32),
                   jax.ShapeDtypeStruct((N, D_MODEL), F32)),
        compiler_params=pltpu.CompilerParams(
            dimension_semantics=("arbitrary",), vmem_limit_bytes=VMEM_LIMIT),
        name="router_shared",
    )(x1a, x1b, *consts)


def _segment_plan(cnt, n_tiles_max):
    seg = (cnt + SEG_ALIGN - 1) // SEG_ALIGN * SEG_ALIGN
    used = jnp.sum(seg, axis=0)
    size = (used + CAP + TILE_STEP - 1) // TILE_STEP * TILE_STEP
    row_end = jnp.cumsum(size)
    row_start = row_end - size
    base = row_start[None, :] + jnp.cumsum(seg, axis=0) - seg
    nwin = jnp.maximum((seg + CAP - 1) // CAP, 1)
    n_full = size // EXPERT_TILE
    rest = size % EXPERT_TILE
    tile_end = jnp.cumsum(n_full + (rest > 0))
    tile_begin = tile_end - n_full - (rest > 0)
    n_tiles = tile_end[-1]
    t_ids = jnp.arange(n_tiles_max, dtype=jnp.int32)
    tile_e = jnp.minimum(jnp.sum(t_ids[:, None] >= tile_end[None, :], axis=1), N_EXPERTS - 1)
    k = t_ids - tile_begin[tile_e]
    tile_row = row_start[tile_e] + k * EXPERT_TILE
    tile_h = jnp.where(t_ids >= n_tiles, 0, jnp.where(k < n_full[tile_e], EXPERT_TILE, rest[tile_e]))
    tile_row = jnp.where(t_ids >= n_tiles, 0, tile_row)
    i32 = lambda a: a.astype(jnp.int32)
    return (i32(base).reshape(-1), i32(nwin).reshape(-1), i32(row_start + used), i32(row_end),
            i32(tile_e), i32(tile_row), i32(tile_h), i32(n_tiles).reshape(1))


def _one_hot_rows(rank_ref, chunk, first_row, values_ref=None):
    tm = rank_ref.shape[1]
    j = (lax.broadcasted_iota(jnp.int32, (CAP, tm), 0) + first_row).astype(F32)
    rows = []
    for k in range(CHUNK_E):
        e = chunk * CHUNK_E + k
        hit = j == rank_ref[e:e + 1, :]
        val = 1.0 if values_ref is None else values_ref[e:e + 1, :]
        rows.append(jnp.where(hit, val, 0.0).astype(BF16))
    return jnp.concatenate(rows, axis=0)


def _window(hbm, base_ref, idx, w):
    start = pl.multiple_of(base_ref[idx] + w * CAP, SEG_ALIGN)
    return hbm.at[pl.ds(start, CAP)]


def _chunk_windows(nwin_ref, tile_idx, c):
    extra = nwin_ref[tile_idx * N_EXPERTS + c * CHUNK_E]
    for k in range(1, CHUNK_E):
        extra = jnp.maximum(extra, nwin_ref[tile_idx * N_EXPERTS + c * CHUNK_E + k])
    return extra


def _tile_windows(nwin_ref, tile_idx):
    extra = _chunk_windows(nwin_ref, tile_idx, 0)
    for c in range(1, N_CHUNKS):
        extra = jnp.maximum(extra, _chunk_windows(nwin_ref, tile_idx, c))
    return extra


def _dispatch_kernel(base_ref, nwin_ref, uend_ref, rend_ref, xn_ref, rank_ref, xs_hbm,
                     stage0, stage1, ostage, zbuf, sem, osem, zsem):
    stage = (stage0, stage1)
    i = pl.program_id(0)
    last = pl.num_programs(0) - 1

    def window_copy(p, e, tile_idx):
        return pltpu.make_async_copy(stage[p].at[pl.ds(e * CAP, CAP)],
                                     _window(xs_hbm, base_ref, tile_idx * N_EXPERTS + e, 0), sem.at[p])

    def wait_tile(p, tile_idx):
        for e in range(N_EXPERTS):
            window_copy(p, e, tile_idx).wait()

    @pl.when(i == 0)
    def _zero_tails():
        zbuf[...] = jnp.zeros(zbuf.shape, BF16)
        sizes = []
        assert 2 * ZERO_ROWS > EXPERT_TILE + CAP
        size = ZERO_ROWS
        while size >= SEG_ALIGN:
            sizes.append(size)
            size //= 2
        pieces = []
        for e in range(N_EXPERTS):
            pos = uend_ref[e]
            length = rend_ref[e] - pos
            for size in sizes:
                take = (length & size) != 0
                cp = pltpu.make_async_copy(zbuf.at[pl.ds(0, size)],
                                           xs_hbm.at[pl.ds(pl.multiple_of(pos, SEG_ALIGN), size)], zsem)
                pieces.append((take, cp))
                pos = pos + jnp.where(take, size, 0)
        pos = rend_ref[N_EXPERTS - 1]
        for size in (ZERO_ROWS, EXPERT_TILE - ZERO_ROWS):
            pieces.append((None, pltpu.make_async_copy(
                zbuf.at[pl.ds(0, size)], xs_hbm.at[pl.ds(pl.multiple_of(pos, SEG_ALIGN), size)], zsem)))
            pos = pos + size
        for take, cp in pieces:
            cp.start() if take is None else pl.when(take)(cp.start)
        for take, cp in pieces:
            cp.wait() if take is None else pl.when(take)(cp.wait)

    for p in (0, 1):
        @pl.when(i % 2 == p)
        def _step(p=p):
            xb = xn_ref[...]
            rows = CHUNK_E * CAP
            for c in range(N_CHUNKS):
                stage[p][pl.ds(c * rows, rows), :] = _dot(_one_hot_rows(rank_ref, c, 0), xb).astype(BF16)

            @pl.when(i >= 1)
            def _():
                wait_tile(1 - p, i - 1)

            for e in range(N_EXPERTS):
                window_copy(p, e, i).start()

    extra = _tile_windows(nwin_ref, i)

    @pl.when(extra > 1)
    def _long_segments():
        def body(w, carry):
            for c in range(N_CHUNKS):
                @pl.when(w < _chunk_windows(nwin_ref, i, c))
                def _(c=c):
                    ostage[...] = _dot(_one_hot_rows(rank_ref, c, w * CAP), xn_ref[...]).astype(BF16)
                    for k in range(CHUNK_E):
                        idx = i * N_EXPERTS + c * CHUNK_E + k

                        @pl.when(w < nwin_ref[idx])
                        def _(k=k, idx=idx):
                            cp = pltpu.make_async_copy(ostage.at[pl.ds(k * CAP, CAP)],
                                                       _window(xs_hbm, base_ref, idx, w), osem)
                            cp.start()
                            cp.wait()
            return carry

        lax.fori_loop(1, extra, body, 0)

    for p in (0, 1):
        @pl.when((i == last) & (i % 2 == p))
        def _drain(p=p):
            wait_tile(p, i)


def _dispatch(xn, rank_t, base, nwin, used_end, row_end, *, n_rows, TM):
    N = xn.shape[0]
    stage = pltpu.VMEM((N_EXPERTS * CAP, D_MODEL), BF16)
    return pl.pallas_call(
        _dispatch_kernel,
        grid_spec=pltpu.PrefetchScalarGridSpec(
            num_scalar_prefetch=4,
            grid=(N // TM,),
            in_specs=[pl.BlockSpec((TM, D_MODEL), lambda i, *_: (i, 0)),
                      pl.BlockSpec((N_EXPERTS, TM), lambda i, *_: (0, i))],
            out_specs=pl.BlockSpec(memory_space=pl.ANY),
            scratch_shapes=[
                stage, stage,
                pltpu.VMEM((CHUNK_E * CAP, D_MODEL), BF16),
                pltpu.VMEM((ZERO_ROWS, D_MODEL), BF16),
                pltpu.SemaphoreType.DMA((2,)),
                pltpu.SemaphoreType.DMA(()),
                pltpu.SemaphoreType.DMA(()),
            ],
        ),
        out_shape=jax.ShapeDtypeStruct((n_rows, D_MODEL), BF16),
        compiler_params=pltpu.CompilerParams(
            dimension_semantics=("arbitrary",), vmem_limit_bytes=VMEM_LIMIT),
        name="moe_dispatch",
    )(base, nwin, used_end, row_end, xn, rank_t)


def _expert_kernel(tile_e_ref, tile_row_ref, tile_h_ref, n_tiles_ref, xs_hbm, w1_hbm, w3_hbm, w2_hbm, ys_hbm,
                   xbuf, obuf, w1f0, w1f1, w3f0, w3f1, w2f0, w2f1, w1b, w3b, w2b, xsem, osem, wsem):
    t = pl.program_id(0)
    n_tiles = n_tiles_ref[0]
    valid = t < n_tiles
    slot = t % 2
    e = tile_e_ref[t]
    first_tile = valid & ((t == 0) | (e != tile_e_ref[jnp.maximum(t - 1, 0)]))
    f32_bufs = ((w1f0, w3f0, w2f0), (w1f1, w3f1, w2f1))

    def rows_in(tile, s):
        row = pl.multiple_of(tile_row_ref[tile], SEG_ALIGN)
        return pltpu.make_async_copy(xs_hbm.at[pl.ds(row, EXPERT_TILE)], xbuf.at[s], xsem.at[s])

    def rows_out(tile, s, h):
        row = pl.multiple_of(tile_row_ref[tile], SEG_ALIGN)
        return pltpu.make_async_copy(obuf.at[s, pl.ds(0, h)], ys_hbm.at[pl.ds(row, h)], osem.at[s])

    def wait_rows_out(tile, s):
        for h in TILE_HEIGHTS:
            pl.when(tile_h_ref[tile] == h)(rows_out(tile, s, h).wait)

    def weight_copies(expert, p):
        return [pltpu.make_async_copy(hbm.at[expert], buf, wsem.at[p, j])
                for j, (hbm, buf) in enumerate(zip((w1_hbm, w3_hbm, w2_hbm), f32_bufs[p]))]

    @pl.when(t == 0)
    def _first():
        rows_in(0, 0).start()

    @pl.when(valid)
    def _arrivals():
        rows_in(t, slot).wait()

        @pl.when(t + 1 < n_tiles)
        def _():
            rows_in(t + 1, 1 - slot).start()

        @pl.when(t >= 2)
        def _():
            wait_rows_out(t - 2, slot)

    for p in (0, 1):
        @pl.when(first_tile & (e % 2 == p))
        def _next_expert(p=p):
            @pl.when(t == 0)
            def _():
                for cp in weight_copies(e, p):
                    cp.start()

            for cp in weight_copies(e, p):
                cp.wait()

            @pl.when(e + 1 < N_EXPERTS)
            def _():
                for cp in weight_copies(e + 1, 1 - p):
                    cp.start()

            w1b[...] = f32_bufs[p][0][...].astype(BF16)
            w3b[...] = f32_bufs[p][1][...].astype(BF16)
            w2b[...] = f32_bufs[p][2][...].astype(BF16)

    for h in TILE_HEIGHTS:
        @pl.when(valid & (tile_h_ref[t] == h))
        def _compute(h=h):
            xb = xbuf[slot, pl.ds(0, h), :]
            a = _dot(xb, w1b[...])
            hb = (a * jax.nn.sigmoid(a)) * _dot(xb, w3b[...])
            obuf[slot, pl.ds(0, h), :] = _dot(hb.astype(BF16), w2b[...]).astype(BF16)
            rows_out(t, slot, h).start()

    @pl.when(t == n_tiles - 1)
    def _drain():
        wait_rows_out(t, slot)

        @pl.when(t >= 1)
        def _():
            wait_rows_out(t - 1, 1 - slot)


def _experts(xs, tile_e, tile_row, tile_h, n_tiles, w1, w3, w2):
    t_max = tile_e.shape[0]
    any_spec = pl.BlockSpec(memory_space=pl.ANY)
    w_in = pltpu.VMEM((D_MODEL, EXPERT_FF), F32)
    w_out = pltpu.VMEM((EXPERT_FF, D_MODEL), F32)
    return pl.pallas_call(
        _expert_kernel,
        grid_spec=pltpu.PrefetchScalarGridSpec(
            num_scalar_prefetch=4,
            grid=(t_max,),
            in_specs=[any_spec, any_spec, any_spec, any_spec],
            out_specs=any_spec,
            scratch_shapes=[
                pltpu.VMEM((2, EXPERT_TILE, D_MODEL), BF16),
                pltpu.VMEM((2, EXPERT_TILE, D_MODEL), BF16),
                w_in, w_in, w_in, w_in, w_out, w_out,
                pltpu.VMEM((D_MODEL, EXPERT_FF), BF16),
                pltpu.VMEM((D_MODEL, EXPERT_FF), BF16),
                pltpu.VMEM((EXPERT_FF, D_MODEL), BF16),
                pltpu.SemaphoreType.DMA((2,)),
                pltpu.SemaphoreType.DMA((2,)),
                pltpu.SemaphoreType.DMA((2, 3)),
            ],
        ),
        out_shape=jax.ShapeDtypeStruct(xs.shape, BF16),
        compiler_params=pltpu.CompilerParams(
            dimension_semantics=("arbitrary",), vmem_limit_bytes=VMEM_LIMIT),
        name="moe_experts",
    )(tile_e, tile_row, tile_h, n_tiles, xs, w1, w3, w2)


def _final_kernel(base_ref, nwin_ref, x1s_ref, rank_ref, gates_ref, ys_hbm, pa_ref, pb_ref, gple_ref,
                  wgate_ref, wproj_ref, gfin_ref, ya_ref, yb_ref, win0, win1, owin, acc_ref, sem, osem,
                  *, split, n_tiles):
    i = pl.program_id(0)
    j = i - 1
    win = (win0, win1)
    parts = 4
    part_chunks = N_CHUNKS // parts
    part_rows = part_chunks * CHUNK_E * CAP
    contract0 = (((0,), (0,)), ((), ()))

    def window_copy(p, e, tile_idx):
        return pltpu.make_async_copy(_window(ys_hbm, base_ref, tile_idx * N_EXPERTS + e, 0),
                                     win[p].at[pl.ds(e * CAP, CAP)], sem.at[p])

    @pl.when(i == 0)
    def _first():
        owin[...] = jnp.zeros(owin.shape, BF16)
        acc_ref[1] = jnp.zeros(acc_ref.shape[1:], F32)
        for e in range(N_EXPERTS):
            window_copy(0, e, i).start()

    def combine_stages(p):
        total = None
        for part in range(parts):
            one_hot = jnp.concatenate(
                [_one_hot_rows(rank_ref, part * part_chunks + c, 0, gates_ref) for c in range(part_chunks)], axis=0)
            d = lax.dot_general(one_hot, win[p][pl.ds(part * part_rows, part_rows), :], contract0,
                                preferred_element_type=F32)
            total = d if total is None else total + d
            yield
        acc_ref[p] = total

    def ple_stages(q):
        half = D_MODEL // 2
        x2 = x1s_ref[...] + acc_ref[q]
        rb = _rms(x2, gple_ref[...]).astype(BF16)
        yield
        g0 = jax.nn.sigmoid(_dot(rb, wgate_ref[:, :half]))
        yield
        g1 = jax.nn.sigmoid(_dot(rb, wgate_ref[:, half:]))
        yield
        pt = jnp.where(j < split, pa_ref[...], pb_ref[...])
        pp = _dot(pt.astype(BF16), wproj_ref[...])
        yield
        y = _rms(x2 + pp * jnp.concatenate([g0, g1], axis=1), gfin_ref[...])

        @pl.when(j < split)
        def _():
            ya_ref[...] = y

        @pl.when(j >= split)
        def _():
            yb_ref[...] = y

    for p in (0, 1):
        @pl.when((i < n_tiles) & (i % 2 == p))
        def _step(p=p):
            for e in range(N_EXPERTS):
                window_copy(p, e, i).wait()

            @pl.when(i + 1 < n_tiles)
            def _():
                for e in range(N_EXPERTS):
                    window_copy(1 - p, e, i + 1).start()

            side = [combine_stages(p), ple_stages(1 - p)]
            while side:
                _advance(side, 1)

    @pl.when(i == n_tiles)
    def _last_tile():
        for _ in ple_stages((n_tiles - 1) % 2):
            pass

    extra = _tile_windows(nwin_ref, jnp.minimum(i, n_tiles - 1))

    @pl.when((i < n_tiles) & (extra > 1))
    def _long_segments():
        def body(w, carry):
            for c in range(N_CHUNKS):
                @pl.when(w < _chunk_windows(nwin_ref, i, c))
                def _(c=c):
                    for k in range(CHUNK_E):
                        idx = i * N_EXPERTS + c * CHUNK_E + k

                        @pl.when(w < nwin_ref[idx])
                        def _(k=k, idx=idx):
                            cp = pltpu.make_async_copy(_window(ys_hbm, base_ref, idx, w),
                                                       owin.at[pl.ds(k * CAP, CAP)], osem)
                            cp.start()
                            cp.wait()
                    acc_ref[i % 2] += lax.dot_general(_one_hot_rows(rank_ref, c, w * CAP, gates_ref), owin[...],
                                                      contract0, preferred_element_type=F32)
            return carry

        lax.fori_loop(1, extra, body, 0)


def _final(x1s, rank_t, gates_t, ys, pa, pb, gple, wgate, wproj, gfin, base, nwin, *, TM):
    N = x1s.shape[0]
    n_tiles = N // TM
    split = pa.shape[0] // TM
    prev = lambda i: jnp.maximum(i - 1, 0)
    tok = pl.BlockSpec((TM, D_MODEL), lambda i, *_: (prev(i), 0))
    per_e = pl.BlockSpec((N_EXPERTS, TM), lambda i, *_: (0, jnp.minimum(i, n_tiles - 1)))
    consts = (gple, wgate, wproj, gfin)
    win = pltpu.VMEM((N_EXPERTS * CAP, D_MODEL), BF16)

    def group_specs(block):
        a, b = _group_specs(block, split)
        return (pl.BlockSpec(block, lambda i, *_: a.index_map(prev(i))),
                pl.BlockSpec(block, lambda i, *_: b.index_map(prev(i))))

    return pl.pallas_call(
        functools.partial(_final_kernel, split=split, n_tiles=n_tiles),
        grid_spec=pltpu.PrefetchScalarGridSpec(
            num_scalar_prefetch=2,
            grid=(n_tiles + 1,),
            in_specs=[tok, per_e, per_e, pl.BlockSpec(memory_space=pl.ANY)]
            + list(group_specs((TM, PLE_DIM)))
            + [pl.BlockSpec(c.shape, lambda i, *_: (0, 0)) for c in consts],
            out_specs=group_specs((TM, D_MODEL)),
            scratch_shapes=[win, win, pltpu.VMEM((CHUNK_E * CAP, D_MODEL), BF16),
                            pltpu.VMEM((2, TM, D_MODEL), F32),
                            pltpu.SemaphoreType.DMA((2,)), pltpu.SemaphoreType.DMA(())],
        ),
        out_shape=(jax.ShapeDtypeStruct((pa.shape[0], D_MODEL), F32),
                   jax.ShapeDtypeStruct((pb.shape[0], D_MODEL), F32)),
        compiler_params=pltpu.CompilerParams(
            dimension_semantics=("arbitrary",), vmem_limit_bytes=VMEM_LIMIT),
        name="combine_ple_final",
    )(base, nwin, x1s, rank_t, gates_t, ys, pa, pb, *consts)


def kernel(x_prompt, x_sample, p_prompt, p_sample, state_C, state_n, state_m, state_pool, norm_mix_g, w_in, b_igate, b_fgate, head_norm_g, w_pool, pool_scale, w_out, norm_ffn_g, router_w, router_bias, ex_w1, ex_w3, ex_w2, sh_w1, sh_w3, sh_w2, norm_ple_g, w_ple_gate, w_ple_proj, final_norm_g):
    depth = norm_mix_g.shape[0]
    assert depth == 1
    l = 0
    B, T, _ = x_prompt.shape
    Bs, Ts, _ = x_sample.shape
    g0 = 4 * MLSTM_WIDTH
    w = w_in[l]
    lane_pad = jnp.zeros((D_MODEL, 128 - HEADS), F32)
    wcat = jnp.concatenate(
        [w[:, :g0], w[:, g0 + 2 * HEADS:], w[:, g0:g0 + HEADS], lane_pad,
         w[:, g0 + HEADS:g0 + 2 * HEADS], lane_pad], axis=1).astype(BF16)
    bias_pad = jnp.zeros((128 - HEADS,), F32)
    gbias = jnp.concatenate([b_igate[l], bias_pad, b_fgate[l], bias_pad])[None, :]
    mixer_w = (wcat, norm_mix_g[l][None, :], gbias, head_norm_g[l][None, :], w_pool[l].astype(BF16),
               pool_scale[l][None, :], w_out[l].astype(BF16))

    x1p, Cp, Np, Mp, Bp = _mixer(x_prompt, *mixer_w, None, BB=1, TT=1024, L=128, start_pos=0)
    state = (state_C[l], state_n[l], state_m[l][..., None], state_pool[l])
    x1s_, Cs, Ns, Ms, Bs_ = _mixer(x_sample, *mixer_w, state, BB=16, TT=Ts, L=Ts, start_pos=PAST_LEN)

    N = B * T + Bs * Ts
    assert (B * T) % ROUTE_TILE == 0 and (Bs * Ts) % ROUTE_TILE == 0
    n_route_tiles = N // ROUTE_TILE

    xn, rank_t, gates_t, cnt, x1sh = _router(
        x1p.reshape(B * T, D_MODEL), x1s_.reshape(Bs * Ts, D_MODEL),
        norm_ffn_g[l][None, :], router_w[l].T, router_bias[l][:, None],
        sh_w1[l].astype(BF16), sh_w3[l].astype(BF16), sh_w2[l].astype(BF16), TM=2 * ROUTE_TILE)

    max_rows = TOP_K * N + (SEG_ALIGN - 1) * n_route_tiles * N_EXPERTS + N_EXPERTS * (CAP + TILE_STEP)
    t_max = max_rows // EXPERT_TILE + N_EXPERTS
    cnt = cnt[:, 0].reshape(n_route_tiles, N_EXPERTS).astype(jnp.int32)
    base, nwin, used_end, row_end, tile_e, tile_row, tile_h, n_tiles = _segment_plan(cnt, t_max)

    xs = _dispatch(xn, rank_t, base, nwin, used_end, row_end, n_rows=max_rows + EXPERT_TILE, TM=ROUTE_TILE)
    ys = _experts(xs, tile_e, tile_row, tile_h, n_tiles, ex_w1[l], ex_w3[l], ex_w2[l])
    y_prompt, y_sample = _final(
        x1sh, rank_t, gates_t, ys, p_prompt[l].reshape(B * T, PLE_DIM), p_sample[l].reshape(Bs * Ts, PLE_DIM),
        norm_ple_g[l][None, :], w_ple_gate[l].astype(BF16), w_ple_proj[l].astype(BF16),
        final_norm_g[None, :], base, nwin, TM=ROUTE_TILE)
    return (y_prompt.reshape(B, T, D_MODEL), y_sample.reshape(Bs, Ts, D_MODEL),
            Cp[None], Np[None], Mp[..., 0][None], Bp[None],
            Cs[None], Ns[None], Ms[..., 0][None], Bs_[None])
```

```python
import functools

import jax
import jax.numpy as jnp
from jax import lax
from jax.experimental import pallas as pl
from jax.experimental.pallas import tpu as pltpu

D_MODEL = 1024
HEADS = 4
DH = 128
MLSTM_WIDTH = HEADS * DH
POOL_WIDTH = 512
POOL_WINDOWS = (2, 4, 8, 16)
POOL_GDIM = 128
POOL_BUF = 15
POOL_PAD = 16
N_EXPERTS = 64
TOP_K = 8
N_GROUPS = 8
GROUP_SIZE = N_EXPERTS // N_GROUPS
TOPK_GROUPS = 4
EXPERT_FF = 256
ROUTED_SCALE = 2.5
NORM_EPS = 1e-6
PLE_DIM = 256
PAST_LEN = 16384

COL_Q, COL_K, COL_V, COL_O, COL_U, COL_I, COL_F = 0, 512, 1024, 1536, 2048, 2560, 2688
IN_COLS_PAD = 2816
GATE_COLS = HEADS * 128

ROUTE_TILE = 256
EXPERT_TILE = 1536
TILE_STEP = 512
TILE_HEIGHTS = tuple(range(TILE_STEP, EXPERT_TILE + 1, TILE_STEP))
ZERO_ROWS = 1024
SEG_ALIGN = 16
CAP = 48
CHUNK_E = 8
N_CHUNKS = N_EXPERTS // CHUNK_E

VMEM_LIMIT = 56 * 1024 * 1024
F32 = jnp.float32
BF16 = jnp.bfloat16
NEG_INF = float("-inf")


def _rms(x, g):
    return x * lax.rsqrt(jnp.mean(x * x, axis=-1, keepdims=True) + NORM_EPS) * g


def _log_sigmoid(x):
    return jnp.minimum(x, 0.0) - jnp.log1p(jnp.exp(-jnp.abs(x)))


def _dot(a, b):
    return jnp.dot(a, b, preferred_element_type=F32)


def _mlstm_chunk(q, k, v, b_col, i_col, C, n, m, causal, eye):
    L = q.shape[0]
    r_col = i_col - b_col
    r_row = jnp.sum(jnp.where(eye, r_col, 0.0), axis=0, keepdims=True)
    d = jnp.where(causal, b_col + r_row, NEG_INF)
    inter = b_col + m
    qb, kb, vb = q.astype(BF16), k.astype(BF16), v.astype(BF16)
    qk = lax.dot_general(qb, kb, (((1,), (1,)), ((), ())), preferred_element_type=F32)
    qc = _dot(qb, C.astype(BF16))
    qn = jnp.sum(q * n, axis=-1, keepdims=True)
    yield
    m_t = jnp.maximum(inter, jnp.max(d, axis=-1, keepdims=True))
    b_last = b_col[L - 1:L, :]
    m_new = jnp.maximum(b_last + m, jnp.max(b_last + r_row, axis=-1, keepdims=True))
    yield
    w_inter = jnp.exp(inter - m_t)
    s = qk * jnp.exp(d - m_t)
    fw = jnp.exp(b_last + m - m_new)
    iw_col = jnp.exp(b_last + r_col - m_new)
    kw = iw_col * k
    yield
    sv = _dot(s.astype(BF16), vb)
    kv = lax.dot_general(kw.astype(BF16), vb, (((0,), (0,)), ((), ())), preferred_element_type=F32)
    ssum = jnp.sum(s, axis=-1, keepdims=True)
    n_new = fw * n + jnp.sum(kw, axis=0, keepdims=True)
    yield
    num = w_inter * qc + sv
    nq = w_inter * qn + ssum
    h = num / jnp.maximum(jnp.abs(nq), jnp.exp(-m_t))
    C_new = fw * C + kv
    return h, C_new, n_new, m_new


def _interleave(gens):
    results = [None] * len(gens)
    live = list(enumerate(gens))
    while live:
        still = []
        for idx, g in live:
            try:
                next(g)
                still.append((idx, g))
            except StopIteration as stop:
                results[idx] = stop.value
        live = still
    return results


def _mlstm_chunk_square(q, k, v, r, b, mp, CN, m, causal, eye, ones):
    L = q.shape[0]
    r_row = jnp.sum(jnp.where(eye, r, 0.0), axis=0, keepdims=True)
    g = jnp.maximum(mp, m)
    qb, kb = q.astype(BF16), k.astype(BF16)
    v1 = jnp.concatenate([v.astype(BF16), ones], axis=1)
    qk = lax.dot_general(qb, kb, (((1,), (1,)), ((), ())), preferred_element_type=F32)
    qcn = _dot(qb, CN.astype(BF16))
    yield
    w_inter = jnp.exp(m - g)
    p = jnp.where(causal, jnp.exp(r_row - g), 0.0)
    g_last = g[L - 1:L, :]
    m_new = b[L - 1:L, :] + g_last
    fw = jnp.exp(m - g_last)
    kw = (jnp.exp(r - g_last) * k).astype(BF16)
    floor = jnp.exp(-(b + g))
    yield
    s = (qk * p).astype(BF16)
    sv = _dot(s, v1)
    kv = lax.dot_general(kw, v1, (((0,), (0,)), ((), ())), preferred_element_type=F32)
    yield
    num = w_inter * qcn[:, :DH] + sv[:, :DH]
    nq = w_inter * qcn[:, DH:] + sv[:, DH:]
    h = num / jnp.maximum(jnp.abs(nq), floor)
    CN_new = jnp.concatenate([fw, fw], axis=1) * CN + kv
    return h, CN_new, m_new[:, 0:1]


def _chunk_scan(x, L, op, fill):
    pos = lax.broadcasted_iota(jnp.int32, x.shape, 0) & (L - 1)
    k = 1
    while k < L:
        x = op(x, jnp.where(pos >= k, pltpu.roll(x, k, axis=0), fill))
        k *= 2
        yield
    return x


def _advance(gens, steps):
    for g in list(gens):
        for _ in range(steps):
            try:
                next(g)
            except StopIteration:
                gens.remove(g)
                break


def _mixer_kernel(*refs, BB, TT, L, start_pos, zero_state):
    if zero_state:
        (x_ref, wcat_ref, gmix_ref, gbias_ref, hng_ref, wpool_ref, pscale_ref, wout_ref,
         x1_ref, c_ref, n_ref, m_ref, buf_ref,
         z_ref, a_ref, b_ref, mp_ref, cn_ref, mix_ref, ext_ref) = refs
    else:
        (x_ref, wcat_ref, gmix_ref, gbias_ref, hng_ref, wpool_ref, pscale_ref, wout_ref,
         c0_ref, n0_ref, m0_ref, buf0_ref,
         x1_ref, c_ref, n_ref, m_ref, buf_ref,
         z_ref, a_ref, b_ref, mp_ref, cn_ref, mix_ref, ext_ref) = refs
    t = pl.program_id(1)
    R = BB * TT
    n_chunks = TT // L
    square = L == DH
    assert not square or (zero_state and BB == 1)

    @pl.when(t == 0)
    def _init():
        ext_ref[:, 0:POOL_PAD, :] = jnp.zeros((BB, POOL_PAD, POOL_WIDTH), F32)
        if zero_state:
            cn_ref[...] = jnp.zeros(cn_ref.shape, F32)
            c_ref[...] = jnp.zeros(c_ref.shape, F32)
            n_ref[...] = jnp.zeros(n_ref.shape, F32)
            m_ref[...] = jnp.zeros(m_ref.shape, F32)
        else:
            c_ref[...] = c0_ref[...]
            n_ref[...] = n0_ref[...]
            m_ref[...] = m0_ref[...]
            ext_ref[:, 1:POOL_PAD, :] = buf0_ref[...]

    hb = _rms(x_ref[...].reshape(R, D_MODEL), gmix_ref[...]).astype(BF16)

    def project(lo, hi):
        z_ref[:, lo:hi] = _dot(hb, wcat_ref[:, lo:hi])

    def gate_stages():
        gbias = gbias_ref[...]
        gi = z_ref[:, COL_I:COL_I + 128] + gbias[:, :128]
        lf = _log_sigmoid(z_ref[:, COL_F:COL_F + 128] + gbias[:, 128:])
        yield
        b_c = yield from _chunk_scan(lf, L, jnp.add, 0.0)
        if square:
            r_c = gi - b_c
            mp_c = yield from _chunk_scan(r_c, L, jnp.maximum, NEG_INF)
            for hd in range(HEADS):
                lanes = slice(hd * DH, (hd + 1) * DH)
                a_ref[:, lanes] = jnp.broadcast_to(r_c[:, hd:hd + 1], (R, DH))
                b_ref[:, lanes] = jnp.broadcast_to(b_c[:, hd:hd + 1], (R, DH))
                mp_ref[:, lanes] = jnp.broadcast_to(mp_c[:, hd:hd + 1], (R, DH))
                yield
        else:
            a_ref[:, 0:128] = gi
            b_ref[:, 0:128] = b_c

    def pool_stages():
        ext_ref[:, POOL_PAD:POOL_PAD + TT, :] = z_ref[:, COL_U:COL_U + POOL_WIDTH].reshape(BB, TT, POOL_WIDTH)
        pos = start_pos + t * TT + lax.broadcasted_iota(jnp.int32, (1, TT, 1), 1)
        pscale = pscale_ref[...]
        yield
        for gidx, w in enumerate(POOL_WINDOWS):
            lanes = slice(gidx * POOL_GDIM, (gidx + 1) * POOL_GDIM)
            u_g = ext_ref[:, POOL_PAD:POOL_PAD + TT, lanes]
            acc = u_g
            for j in range(1, w):
                acc = acc + ext_ref[:, POOL_PAD - j:POOL_PAD - j + TT, lanes]
                if j % 4 == 3:
                    yield
            cnt = jnp.minimum(pos + 1, w).astype(F32)
            pooled = (acc / cnt - u_g).reshape(R, POOL_GDIM)
            mixed = _dot(pooled.astype(BF16), wpool_ref[gidx]) * pscale[:, lanes]
            mix_ref[:, MLSTM_WIDTH + gidx * POOL_GDIM:MLSTM_WIDTH + (gidx + 1) * POOL_GDIM] = mixed
            yield
        new_buf = ext_ref[:, TT + 1:TT + POOL_PAD, :]
        buf_ref[...] = new_buf
        ext_ref[:, 1:POOL_PAD, :] = new_buf

    project(COL_I, IN_COLS_PAD)
    project(COL_U, COL_U + 256)
    project(COL_U + 256, COL_I)
    side = [gate_stages(), pool_stages()]
    for lo in range(0, COL_U, 256):
        project(lo, lo + 256)
        _advance(side, 3)
    while side:
        _advance(side, 1)

    row = lax.broadcasted_iota(jnp.int32, (L, L), 0)
    col = lax.broadcasted_iota(jnp.int32, (L, L), 1)
    causal = row >= col
    eye = row == col
    ones = jnp.ones((DH, DH), BF16)
    hng = hng_ref[...]

    group = 4 if (not square and n_chunks == 1 and BB % 4 == 0) else 1

    def chunk_body(s, carry):
        work = []
        for j in range(group):
            sj = s * group + j
            rows = pl.ds(pl.multiple_of(sj * L, L), L)
            bb = 0 if BB == 1 else sj // n_chunks
            for hd in range(HEADS):
                lanes = slice(hd * DH, (hd + 1) * DH)
                q = z_ref[rows, COL_Q + hd * DH:COL_Q + (hd + 1) * DH] * (DH ** -0.5)
                k = z_ref[rows, COL_K + hd * DH:COL_K + (hd + 1) * DH]
                v = z_ref[rows, COL_V + hd * DH:COL_V + (hd + 1) * DH]
                o = z_ref[rows, COL_O + hd * DH:COL_O + (hd + 1) * DH]
                m = m_ref[bb, hd:hd + 1, :]
                if square:
                    state = (a_ref[rows, lanes], b_ref[rows, lanes], mp_ref[rows, lanes], cn_ref[hd])
                else:
                    state = (b_ref[rows, hd:hd + 1], a_ref[rows, hd:hd + 1], c_ref[bb, hd],
                             n_ref[bb, hd:hd + 1, :])
                work.append((rows, bb, hd, lanes, q, k, v, o, m, state))
        if square:
            results = _interleave([_mlstm_chunk_square(q, k, v, *state, m, causal, eye, ones)
                                   for _, _, _, _, q, k, v, _, m, state in work])
        else:
            results = _interleave([_mlstm_chunk(q, k, v, *state, m, causal, eye)
                                   for _, _, _, _, q, k, v, _, m, state in work])
        scales = [lax.rsqrt(jnp.mean(res[0] * res[0], axis=-1, keepdims=True) + NORM_EPS) for res in results]
        done = []
        for (rows, bb, hd, lanes, _, _, _, o, _, _), res, scale in zip(work, results, scales):
            out = res[0] * scale * hng[:, lanes] * jax.nn.sigmoid(o)
            done.append((rows, bb, hd, lanes, out, res[-1], res[1:-1]))
        for rows, bb, hd, lanes, out, m_new, new_state in done:
            mix_ref[rows, lanes] = out
            m_ref[bb, hd:hd + 1, :] = m_new
            if square:
                cn_ref[hd] = new_state[0]
            else:
                c_ref[bb, hd] = new_state[0]
                n_ref[bb, hd:hd + 1, :] = new_state[1]
        return carry

    lax.fori_loop(0, BB * n_chunks // group, chunk_body, 0)

    if square:
        @pl.when(t == pl.num_programs(1) - 1)
        def _emit_state():
            for hd in range(HEADS):
                cn = cn_ref[hd]
                c_ref[0, hd] = cn[:, :DH]
                n_ref[0, hd:hd + 1, :] = jnp.sum(jnp.where(eye, cn[:, DH:], 0.0), axis=0, keepdims=True)

    out = x_ref[...].reshape(R, D_MODEL) + _dot(mix_ref[...].astype(BF16), wout_ref[...])
    x1_ref[...] = out.reshape(BB, TT, D_MODEL)


def _const_spec(shape):
    nd = len(shape)
    return pl.BlockSpec(shape, lambda b, t, _nd=nd: (0,) * _nd, pipeline_mode=pl.Buffered(1))


def _mixer(x, wcat, gmix, gbias, hng, wpool, pscale, wout, state, *, BB, TT, L, start_pos):
    B, T, _ = x.shape
    zero_state = state is None
    R = BB * TT
    grid = (B // BB, T // TT)
    x_spec = pl.BlockSpec((BB, TT, D_MODEL), lambda b, t: (b, t, 0))
    c_spec = pl.BlockSpec((BB, HEADS, DH, DH), lambda b, t: (b, 0, 0, 0))
    n_spec = pl.BlockSpec((BB, HEADS, DH), lambda b, t: (b, 0, 0))
    m_spec = pl.BlockSpec((BB, HEADS, 1), lambda b, t: (b, 0, 0))
    buf_spec = pl.BlockSpec((BB, POOL_BUF, POOL_WIDTH), lambda b, t: (b, 0, 0))
    weights = (wcat, gmix, gbias, hng, wpool, pscale, wout)
    in_specs = [x_spec] + [_const_spec(w.shape) for w in weights]
    args = [x, *weights]
    if not zero_state:
        in_specs += [c_spec, n_spec, m_spec, buf_spec]
        args += list(state)
    out_shape = (
        jax.ShapeDtypeStruct((B, T, D_MODEL), F32),
        jax.ShapeDtypeStruct((B, HEADS, DH, DH), F32),
        jax.ShapeDtypeStruct((B, HEADS, DH), F32),
        jax.ShapeDtypeStruct((B, HEADS, 1), F32),
        jax.ShapeDtypeStruct((B, POOL_BUF, POOL_WIDTH), F32),
    )
    kern = functools.partial(_mixer_kernel, BB=BB, TT=TT, L=L, start_pos=start_pos, zero_state=zero_state)
    return pl.pallas_call(
        kern,
        grid=grid,
        in_specs=in_specs,
        out_specs=(x_spec, c_spec, n_spec, m_spec, buf_spec),
        out_shape=out_shape,
        scratch_shapes=[
            pltpu.VMEM((R, IN_COLS_PAD), F32),
            pltpu.VMEM((R, GATE_COLS), F32),
            pltpu.VMEM((R, GATE_COLS), F32),
            pltpu.VMEM((R, GATE_COLS), F32),
            pltpu.VMEM((HEADS, DH, 2 * DH), F32),
            pltpu.VMEM((R, D_MODEL), F32),
            pltpu.VMEM((BB, POOL_PAD + TT, POOL_WIDTH), F32),
        ],
        compiler_params=pltpu.CompilerParams(
            dimension_semantics=("arbitrary", "arbitrary"), vmem_limit_bytes=VMEM_LIMIT),
        name="mixer_zero" if zero_state else "mixer_state",
    )(*args)


def _first_index_of_max(work, idx, n, axis):
    mx = jnp.max(work, axis=axis, keepdims=True)
    return jnp.min(jnp.where(work == mx, idx, float(n)), axis=axis, keepdims=True)


def _route(scores_t, bias_t):
    tm = scores_t.shape[1]
    biased = scores_t + bias_t
    b3 = biased.reshape(N_GROUPS, GROUP_SIZE, tm)
    sub = lax.broadcasted_iota(jnp.int32, b3.shape, 1).astype(F32)
    m1 = jnp.max(b3, axis=1, keepdims=True)
    first = jnp.min(jnp.where(b3 == m1, sub, float(GROUP_SIZE)), axis=1, keepdims=True)
    m2 = jnp.max(jnp.where(sub == first, NEG_INF, b3), axis=1, keepdims=True)
    gs = (m1 + m2).reshape(N_GROUPS, tm)
    gidx = lax.broadcasted_iota(jnp.int32, gs.shape, 0).astype(F32)
    gsel = jnp.zeros(gs.shape, F32)
    work = gs
    yield
    for _ in range(TOPK_GROUPS):
        pick = gidx == _first_index_of_max(work, gidx, N_GROUPS, 0)
        gsel = jnp.where(pick, 1.0, gsel)
        work = jnp.where(pick, NEG_INF, work)
    emask = jnp.broadcast_to(gsel.reshape(N_GROUPS, 1, tm), b3.shape).reshape(N_EXPERTS, tm)
    work = jnp.where(emask > 0, biased, NEG_INF)
    eidx = lax.broadcasted_iota(jnp.int32, work.shape, 0).astype(F32)
    mask = jnp.zeros(work.shape, F32)
    yield
    for _ in range(TOP_K):
        pick = eidx == _first_index_of_max(work, eidx, N_EXPERTS, 0)
        mask = jnp.where(pick, 1.0, mask)
        work = jnp.where(pick, NEG_INF, work)
        yield
    sel = mask * scores_t
    return mask, sel / jnp.sum(sel, axis=0, keepdims=True) * ROUTED_SCALE


def _group_specs(block, split):
    return (pl.BlockSpec(block, lambda i, *_: (jnp.minimum(i, split - 1), 0)),
            pl.BlockSpec(block, lambda i, *_: (jnp.maximum(i - split, 0), 0)))


def _router_kernel(x1a_ref, x1b_ref, gffn_ref, rwt_ref, rbias_ref, w1_ref, w3_ref, w2_ref,
                   xn_ref, rank_ref, gates_ref, cnt_ref, x1s_ref, *, split):
    x1 = jnp.where(pl.program_id(0) < split, x1a_ref[...], x1b_ref[...])
    tm = x1.shape[0]
    xn = _rms(x1, gffn_ref[...])
    xb = xn.astype(BF16)
    xn_ref[...] = xb
    logits_t = lax.dot_general(rwt_ref[...], xn, (((1,), (1,)), ((), ())),
                               preferred_element_type=F32, precision=lax.Precision.HIGHEST)
    def routing():
        mask, gates = yield from _route(jax.nn.sigmoid(logits_t), rbias_ref[...])
        gates_ref[...] = gates
        rt = ROUTE_TILE
        before = (lax.broadcasted_iota(jnp.int32, (rt, rt), 0)
                  < lax.broadcasted_iota(jnp.int32, (rt, rt), 1)).astype(BF16)
        for j in range(tm // rt):
            mj = mask[:, j * rt:(j + 1) * rt]
            rank_ref[:, j * rt:(j + 1) * rt] = jnp.where(mj > 0, _dot(mj.astype(BF16), before), -1.0)
            cnt_ref[j * N_EXPERTS:(j + 1) * N_EXPERTS, :] = jnp.broadcast_to(
                jnp.sum(mj, axis=1, keepdims=True), (N_EXPERTS, 128))
            yield

    def shared_expert():
        half = D_MODEL // 2
        a = _dot(xb, w1_ref[...])
        yield
        g = _dot(xb, w3_ref[...])
        yield
        hsh = ((a * jax.nn.sigmoid(a)) * g).astype(BF16)
        yield
        x1s_ref[:, :half] = x1[:, :half] + _dot(hsh, w2_ref[:, :half])
        yield
        x1s_ref[:, half:] = x1[:, half:] + _dot(hsh, w2_ref[:, half:])

    side = [routing(), shared_expert()]
    while side:
        _advance(side, 1)


def _router(x1a, x1b, gffn, rwt, rbias, w1, w3, w2, *, TM):
    N = x1a.shape[0] + x1b.shape[0]
    split = x1a.shape[0] // TM
    tok = pl.BlockSpec((TM, D_MODEL), lambda i: (i, 0))
    per_e = pl.BlockSpec((N_EXPERTS, TM), lambda i: (0, i))
    consts = (gffn, rwt, rbias, w1, w3, w2)
    return pl.pallas_call(
        functools.partial(_router_kernel, split=split),
        grid=(N // TM,),
        in_specs=list(_group_specs((TM, D_MODEL), split)) + [pl.BlockSpec(c.shape, lambda i: (0, 0)) for c in consts],
        out_specs=(tok, per_e, per_e, pl.BlockSpec((TM // ROUTE_TILE * N_EXPERTS, 128), lambda i: (i, 0)), tok),
        out_shape=(jax.ShapeDtypeStruct((N, D_MODEL), BF16),
                   jax.ShapeDtypeStruct((N_EXPERTS, N), F32),
                   jax.ShapeDtypeStruct((N_EXPERTS, N), F32),
                   jax.ShapeDtypeStruct((N // ROUTE_TILE * N_EXPERTS, 128), F32),
                   jax.ShapeDtypeStruct((N, D_MODEL), F32)),
        compiler_params=pltpu.CompilerParams(
            dimension_semantics=("arbitrary",), vmem_limit_bytes=VMEM_LIMIT),
        name="router_shared",
    )(x1a, x1b, *consts)


def _segment_plan(cnt, n_tiles_max):
    seg = (cnt + SEG_ALIGN - 1) // SEG_ALIGN * SEG_ALIGN
    used = jnp.sum(seg, axis=0)
    size = (used + CAP + TILE_STEP - 1) // TILE_STEP * TILE_STEP
    row_end = jnp.cumsum(size)
    row_start = row_end - size
    base = row_start[None, :] + jnp.cumsum(seg, axis=0) - seg
    nwin = jnp.maximum((seg + CAP - 1) // CAP, 1)
    n_full = size // EXPERT_TILE
    rest = size % EXPERT_TILE
    tile_end = jnp.cumsum(n_full + (rest > 0))
    tile_begin = tile_end - n_full - (rest > 0)
    n_tiles = tile_end[-1]
    t_ids = jnp.arange(n_tiles_max, dtype=jnp.int32)
    tile_e = jnp.minimum(jnp.sum(t_ids[:, None] >= tile_end[None, :], axis=1), N_EXPERTS - 1)
    k = t_ids - tile_begin[tile_e]
    tile_row = row_start[tile_e] + k * EXPERT_TILE
    tile_h = jnp.where(t_ids >= n_tiles, 0, jnp.where(k < n_full[tile_e], EXPERT_TILE, rest[tile_e]))
    tile_row = jnp.where(t_ids >= n_tiles, 0, tile_row)
    i32 = lambda a: a.astype(jnp.int32)
    return (i32(base).reshape(-1), i32(nwin).reshape(-1), i32(row_start + used), i32(row_end),
            i32(tile_e), i32(tile_row), i32(tile_h), i32(n_tiles).reshape(1))


def _one_hot_rows(rank_ref, chunk, first_row, values_ref=None):
    tm = rank_ref.shape[1]
    j = (lax.broadcasted_iota(jnp.int32, (CAP, tm), 0) + first_row).astype(F32)
    rows = []
    for k in range(CHUNK_E):
        e = chunk * CHUNK_E + k
        hit = j == rank_ref[e:e + 1, :]
        val = 1.0 if values_ref is None else values_ref[e:e + 1, :]
        rows.append(jnp.where(hit, val, 0.0).astype(BF16))
    return jnp.concatenate(rows, axis=0)


def _window(hbm, base_ref, idx, w):
    start = pl.multiple_of(base_ref[idx] + w * CAP, SEG_ALIGN)
    return hbm.at[pl.ds(start, CAP)]


def _chunk_windows(nwin_ref, tile_idx, c):
    extra = nwin_ref[tile_idx * N_EXPERTS + c * CHUNK_E]
    for k in range(1, CHUNK_E):
        extra = jnp.maximum(extra, nwin_ref[tile_idx * N_EXPERTS + c * CHUNK_E + k])
    return extra


def _tile_windows(nwin_ref, tile_idx):
    extra = _chunk_windows(nwin_ref, tile_idx, 0)
    for c in range(1, N_CHUNKS):
        extra = jnp.maximum(extra, _chunk_windows(nwin_ref, tile_idx, c))
    return extra


def _dispatch_kernel(base_ref, nwin_ref, uend_ref, rend_ref, xn_ref, rank_ref, xs_hbm,
                     stage0, stage1, ostage, zbuf, sem, osem, zsem):
    stage = (stage0, stage1)
    i = pl.program_id(0)
    last = pl.num_programs(0) - 1

    def window_copy(p, e, tile_idx):
        return pltpu.make_async_copy(stage[p].at[pl.ds(e * CAP, CAP)],
                                     _window(xs_hbm, base_ref, tile_idx * N_EXPERTS + e, 0), sem.at[p])

    def wait_tile(p, tile_idx):
        for e in range(N_EXPERTS):
            window_copy(p, e, tile_idx).wait()

    @pl.when(i == 0)
    def _zero_tails():
        zbuf[...] = jnp.zeros(zbuf.shape, BF16)
        sizes = []
        assert 2 * ZERO_ROWS > EXPERT_TILE + CAP
        size = ZERO_ROWS
        while size >= SEG_ALIGN:
            sizes.append(size)
            size //= 2
        pieces = []
        for e in range(N_EXPERTS):
            pos = uend_ref[e]
            length = rend_ref[e] - pos
            for size in sizes:
                take = (length & size) != 0
                cp = pltpu.make_async_copy(zbuf.at[pl.ds(0, size)],
                                           xs_hbm.at[pl.ds(pl.multiple_of(pos, SEG_ALIGN), size)], zsem)
                pieces.append((take, cp))
                pos = pos + jnp.where(take, size, 0)
        for take, cp in pieces:
            pl.when(take)(cp.start)
        for take, cp in pieces:
            pl.when(take)(cp.wait)

    for p in (0, 1):
        @pl.when(i % 2 == p)
        def _step(p=p):
            xb = xn_ref[...]
            rows = CHUNK_E * CAP
            for c in range(N_CHUNKS):
                stage[p][pl.ds(c * rows, rows), :] = _dot(_one_hot_rows(rank_ref, c, 0), xb).astype(BF16)

            @pl.when(i >= 1)
            def _():
                wait_tile(1 - p, i - 1)

            for e in range(N_EXPERTS):
                window_copy(p, e, i).start()

    extra = _tile_windows(nwin_ref, i)

    @pl.when(extra > 1)
    def _long_segments():
        def body(w, carry):
            for c in range(N_CHUNKS):
                @pl.when(w < _chunk_windows(nwin_ref, i, c))
                def _(c=c):
                    ostage[...] = _dot(_one_hot_rows(rank_ref, c, w * CAP), xn_ref[...]).astype(BF16)
                    for k in range(CHUNK_E):
                        idx = i * N_EXPERTS + c * CHUNK_E + k

                        @pl.when(w < nwin_ref[idx])
                        def _(k=k, idx=idx):
                            cp = pltpu.make_async_copy(ostage.at[pl.ds(k * CAP, CAP)],
                                                       _window(xs_hbm, base_ref, idx, w), osem)
                            cp.start()
                            cp.wait()
            return carry

        lax.fori_loop(1, extra, body, 0)

    for p in (0, 1):
        @pl.when((i == last) & (i % 2 == p))
        def _drain(p=p):
            wait_tile(p, i)


def _dispatch(xn, rank_t, base, nwin, used_end, row_end, *, n_rows, TM):
    N = xn.shape[0]
    stage = pltpu.VMEM((N_EXPERTS * CAP, D_MODEL), BF16)
    return pl.pallas_call(
        _dispatch_kernel,
        grid_spec=pltpu.PrefetchScalarGridSpec(
            num_scalar_prefetch=4,
            grid=(N // TM,),
            in_specs=[pl.BlockSpec((TM, D_MODEL), lambda i, *_: (i, 0)),
                      pl.BlockSpec((N_EXPERTS, TM), lambda i, *_: (0, i))],
            out_specs=pl.BlockSpec(memory_space=pl.ANY),
            scratch_shapes=[
                stage, stage,
                pltpu.VMEM((CHUNK_E * CAP, D_MODEL), BF16),
                pltpu.VMEM((ZERO_ROWS, D_MODEL), BF16),
                pltpu.SemaphoreType.DMA((2,)),
                pltpu.SemaphoreType.DMA(()),
                pltpu.SemaphoreType.DMA(()),
            ],
        ),
        out_shape=jax.ShapeDtypeStruct((n_rows, D_MODEL), BF16),
        compiler_params=pltpu.CompilerParams(
            dimension_semantics=("arbitrary",), vmem_limit_bytes=VMEM_LIMIT),
        name="moe_dispatch",
    )(base, nwin, used_end, row_end, xn, rank_t)


def _expert_kernel(tile_e_ref, tile_row_ref, tile_h_ref, n_tiles_ref, xs_hbm, w1_hbm, w3_hbm, w2_hbm, ys_hbm,
                   xbuf, obuf, w1f0, w1f1, w3f0, w3f1, w2f0, w2f1, w1b, w3b, w2b, xsem, osem, wsem):
    t = pl.program_id(0)
    n_tiles = n_tiles_ref[0]
    valid = t < n_tiles
    slot = t % 2
    e = tile_e_ref[t]
    first_tile = valid & ((t == 0) | (e != tile_e_ref[jnp.maximum(t - 1, 0)]))
    f32_bufs = ((w1f0, w3f0, w2f0), (w1f1, w3f1, w2f1))

    def rows_in(tile, s, h):
        row = pl.multiple_of(tile_row_ref[tile], SEG_ALIGN)
        return pltpu.make_async_copy(xs_hbm.at[pl.ds(row, h)], xbuf.at[s, pl.ds(0, h)], xsem.at[s])

    def rows_out(tile, s, h):
        row = pl.multiple_of(tile_row_ref[tile], SEG_ALIGN)
        return pltpu.make_async_copy(obuf.at[s, pl.ds(0, h)], ys_hbm.at[pl.ds(row, h)], osem.at[s])

    def by_height(tile, make, action):
        for h in TILE_HEIGHTS:
            pl.when(tile_h_ref[tile] == h)(getattr(make(h), action))

    def wait_rows_out(tile, s):
        by_height(tile, lambda h: rows_out(tile, s, h), "wait")

    def weight_copies(expert, p):
        return [pltpu.make_async_copy(hbm.at[expert], buf, wsem.at[p, j])
                for j, (hbm, buf) in enumerate(zip((w1_hbm, w3_hbm, w2_hbm), f32_bufs[p]))]

    @pl.when(t == 0)
    def _first():
        by_height(0, lambda h: rows_in(0, 0, h), "start")

    @pl.when(valid)
    def _arrivals():
        by_height(t, lambda h: rows_in(t, slot, h), "wait")

        @pl.when(t + 1 < n_tiles)
        def _():
            by_height(t + 1, lambda h: rows_in(t + 1, 1 - slot, h), "start")

        @pl.when(t >= 2)
        def _():
            wait_rows_out(t - 2, slot)

    for p in (0, 1):
        @pl.when(first_tile & (e % 2 == p))
        def _next_expert(p=p):
            @pl.when(t == 0)
            def _():
                for cp in weight_copies(e, p):
                    cp.start()

            for cp in weight_copies(e, p):
                cp.wait()

            @pl.when(e + 1 < N_EXPERTS)
            def _():
                for cp in weight_copies(e + 1, 1 - p):
                    cp.start()

            w1b[...] = f32_bufs[p][0][...].astype(BF16)
            w3b[...] = f32_bufs[p][1][...].astype(BF16)
            w2b[...] = f32_bufs[p][2][...].astype(BF16)

    for h in TILE_HEIGHTS:
        @pl.when(valid & (tile_h_ref[t] == h))
        def _compute(h=h):
            xb = xbuf[slot, pl.ds(0, h), :]
            a = _dot(xb, w1b[...])
            hb = (a * jax.nn.sigmoid(a)) * _dot(xb, w3b[...])
            obuf[slot, pl.ds(0, h), :] = _dot(hb.astype(BF16), w2b[...]).astype(BF16)
            rows_out(t, slot, h).start()

    @pl.when(t == n_tiles - 1)
    def _drain():
        wait_rows_out(t, slot)

        @pl.when(t >= 1)
        def _():
            wait_rows_out(t - 1, 1 - slot)


def _experts(xs, tile_e, tile_row, tile_h, n_tiles, w1, w3, w2):
    t_max = tile_e.shape[0]
    any_spec = pl.BlockSpec(memory_space=pl.ANY)
    w_in = pltpu.VMEM((D_MODEL, EXPERT_FF), F32)
    w_out = pltpu.VMEM((EXPERT_FF, D_MODEL), F32)
    return pl.pallas_call(
        _expert_kernel,
        grid_spec=pltpu.PrefetchScalarGridSpec(
            num_scalar_prefetch=4,
            grid=(t_max,),
            in_specs=[any_spec, any_spec, any_spec, any_spec],
            out_specs=any_spec,
            scratch_shapes=[
                pltpu.VMEM((2, EXPERT_TILE, D_MODEL), BF16),
                pltpu.VMEM((2, EXPERT_TILE, D_MODEL), BF16),
                w_in, w_in, w_in, w_in, w_out, w_out,
                pltpu.VMEM((D_MODEL, EXPERT_FF), BF16),
                pltpu.VMEM((D_MODEL, EXPERT_FF), BF16),
                pltpu.VMEM((EXPERT_FF, D_MODEL), BF16),
                pltpu.SemaphoreType.DMA((2,)),
                pltpu.SemaphoreType.DMA((2,)),
                pltpu.SemaphoreType.DMA((2, 3)),
            ],
        ),
        out_shape=jax.ShapeDtypeStruct(xs.shape, BF16),
        compiler_params=pltpu.CompilerParams(
            dimension_semantics=("arbitrary",), vmem_limit_bytes=VMEM_LIMIT),
        name="moe_experts",
    )(tile_e, tile_row, tile_h, n_tiles, xs, w1, w3, w2)


def _final_kernel(base_ref, nwin_ref, x1s_ref, rank_ref, gates_ref, ys_hbm, pa_ref, pb_ref, gple_ref,
                  wgate_ref, wproj_ref, gfin_ref, ya_ref, yb_ref, win0, win1, owin, acc_ref, sem, osem,
                  *, split, n_tiles):
    i = pl.program_id(0)
    j = i - 1
    win = (win0, win1)
    parts = 4
    part_chunks = N_CHUNKS // parts
    part_rows = part_chunks * CHUNK_E * CAP
    contract0 = (((0,), (0,)), ((), ()))

    def window_copy(p, e, tile_idx):
        return pltpu.make_async_copy(_window(ys_hbm, base_ref, tile_idx * N_EXPERTS + e, 0),
                                     win[p].at[pl.ds(e * CAP, CAP)], sem.at[p])

    @pl.when(i == 0)
    def _first():
        owin[...] = jnp.zeros(owin.shape, BF16)
        acc_ref[1] = jnp.zeros(acc_ref.shape[1:], F32)
        for e in range(N_EXPERTS):
            window_copy(0, e, i).start()

    def combine_stages(p):
        total = None
        for part in range(parts):
            one_hot = jnp.concatenate(
                [_one_hot_rows(rank_ref, part * part_chunks + c, 0, gates_ref) for c in range(part_chunks)], axis=0)
            d = lax.dot_general(one_hot, win[p][pl.ds(part * part_rows, part_rows), :], contract0,
                                preferred_element_type=F32)
            total = d if total is None else total + d
            yield
        acc_ref[p] = total

    def ple_stages(q):
        half = D_MODEL // 2
        x2 = x1s_ref[...] + acc_ref[q]
        rb = _rms(x2, gple_ref[...]).astype(BF16)
        yield
        g0 = jax.nn.sigmoid(_dot(rb, wgate_ref[:, :half]))
        yield
        g1 = jax.nn.sigmoid(_dot(rb, wgate_ref[:, half:]))
        yield
        pt = jnp.where(j < split, pa_ref[...], pb_ref[...])
        pp = _dot(pt.astype(BF16), wproj_ref[...])
        yield
        y = _rms(x2 + pp * jnp.concatenate([g0, g1], axis=1), gfin_ref[...])

        @pl.when(j < split)
        def _():
            ya_ref[...] = y

        @pl.when(j >= split)
        def _():
            yb_ref[...] = y

    for p in (0, 1):
        @pl.when((i < n_tiles) & (i % 2 == p))
        def _step(p=p):
            for e in range(N_EXPERTS):
                window_copy(p, e, i).wait()

            @pl.when(i + 1 < n_tiles)
            def _():
                for e in range(N_EXPERTS):
                    window_copy(1 - p, e, i + 1).start()

            side = [combine_stages(p), ple_stages(1 - p)]
            while side:
                _advance(side, 1)

    @pl.when(i == n_tiles)
    def _last_tile():
        for _ in ple_stages((n_tiles - 1) % 2):
            pass

    extra = _tile_windows(nwin_ref, jnp.minimum(i, n_tiles - 1))

    @pl.when((i < n_tiles) & (extra > 1))
    def _long_segments():
        def body(w, carry):
            for c in range(N_CHUNKS):
                @pl.when(w < _chunk_windows(nwin_ref, i, c))
                def _(c=c):
                    for k in range(CHUNK_E):
                        idx = i * N_EXPERTS + c * CHUNK_E + k

                        @pl.when(w < nwin_ref[idx])
                        def _(k=k, idx=idx):
                            cp = pltpu.make_async_copy(_window(ys_hbm, base_ref, idx, w),
                                                       owin.at[pl.ds(k * CAP, CAP)], osem)
                            cp.start()
                            cp.wait()
                    acc_ref[i % 2] += lax.dot_general(_one_hot_rows(rank_ref, c, w * CAP, gates_ref), owin[...],
                                                      contract0, preferred_element_type=F32)
            return carry

        lax.fori_loop(1, extra, body, 0)


def _final(x1s, rank_t, gates_t, ys, pa, pb, gple, wgate, wproj, gfin, base, nwin, *, TM):
    N = x1s.shape[0]
    n_tiles = N // TM
    split = pa.shape[0] // TM
    prev = lambda i: jnp.maximum(i - 1, 0)
    tok = pl.BlockSpec((TM, D_MODEL), lambda i, *_: (prev(i), 0))
    per_e = pl.BlockSpec((N_EXPERTS, TM), lambda i, *_: (0, jnp.minimum(i, n_tiles - 1)))
    consts = (gple, wgate, wproj, gfin)
    win = pltpu.VMEM((N_EXPERTS * CAP, D_MODEL), BF16)

    def group_specs(block):
        a, b = _group_specs(block, split)
        return (pl.BlockSpec(block, lambda i, *_: a.index_map(prev(i))),
                pl.BlockSpec(block, lambda i, *_: b.index_map(prev(i))))

    return pl.pallas_call(
        functools.partial(_final_kernel, split=split, n_tiles=n_tiles),
        grid_spec=pltpu.PrefetchScalarGridSpec(
            num_scalar_prefetch=2,
            grid=(n_tiles + 1,),
            in_specs=[tok, per_e, per_e, pl.BlockSpec(memory_space=pl.ANY)]
            + list(group_specs((TM, PLE_DIM)))
            + [pl.BlockSpec(c.shape, lambda i, *_: (0, 0)) for c in consts],
            out_specs=group_specs((TM, D_MODEL)),
            scratch_shapes=[win, win, pltpu.VMEM((CHUNK_E * CAP, D_MODEL), BF16),
                            pltpu.VMEM((2, TM, D_MODEL), F32),
                            pltpu.SemaphoreType.DMA((2,)), pltpu.SemaphoreType.DMA(())],
        ),
        out_shape=(jax.ShapeDtypeStruct((pa.shape[0], D_MODEL), F32),
                   jax.ShapeDtypeStruct((pb.shape[0], D_MODEL), F32)),
        compiler_params=pltpu.CompilerParams(
            dimension_semantics=("arbitrary",), vmem_limit_bytes=VMEM_LIMIT),
        name="combine_ple_final",
    )(base, nwin, x1s, rank_t, gates_t, ys, pa, pb, *consts)


def kernel(x_prompt, x_sample, p_prompt, p_sample, state_C, state_n, state_m, state_pool, norm_mix_g, w_in, b_igate, b_fgate, head_norm_g, w_pool, pool_scale, w_out, norm_ffn_g, router_w, router_bias, ex_w1, ex_w3, ex_w2, sh_w1, sh_w3, sh_w2, norm_ple_g, w_ple_gate, w_ple_proj, final_norm_g):
    depth = norm_mix_g.shape[0]
    assert depth == 1
    l = 0
    B, T, _ = x_prompt.shape
    Bs, Ts, _ = x_sample.shape
    g0 = 4 * MLSTM_WIDTH
    w = w_in[l]
    lane_pad = jnp.zeros((D_MODEL, 128 - HEADS), F32)
    wcat = jnp.concatenate(
        [w[:, :g0], w[:, g0 + 2 * HEADS:], w[:, g0:g0 + HEADS], lane_pad,
         w[:, g0 + HEADS:g0 + 2 * HEADS], lane_pad], axis=1).astype(BF16)
    bias_pad = jnp.zeros((128 - HEADS,), F32)
    gbias = jnp.concatenate([b_igate[l], bias_pad, b_fgate[l], bias_pad])[None, :]
    mixer_w = (wcat, norm_mix_g[l][None, :], gbias, head_norm_g[l][None, :], w_pool[l].astype(BF16),
               pool_scale[l][None, :], w_out[l].astype(BF16))

    x1p, Cp, Np, Mp, Bp = _mixer(x_prompt, *mixer_w, None, BB=1, TT=1024, L=128, start_pos=0)
    state = (state_C[l], state_n[l], state_m[l][..., None], state_pool[l])
    x1s_, Cs, Ns, Ms, Bs_ = _mixer(x_sample, *mixer_w, state, BB=16, TT=Ts, L=Ts, start_pos=PAST_LEN)

    N = B * T + Bs * Ts
    assert (B * T) % ROUTE_TILE == 0 and (Bs * Ts) % ROUTE_TILE == 0
    n_route_tiles = N // ROUTE_TILE

    xn, rank_t, gates_t, cnt, x1sh = _router(
        x1p.reshape(B * T, D_MODEL), x1s_.reshape(Bs * Ts, D_MODEL),
        norm_ffn_g[l][None, :], router_w[l].T, router_bias[l][:, None],
        sh_w1[l].astype(BF16), sh_w3[l].astype(BF16), sh_w2[l].astype(BF16), TM=2 * ROUTE_TILE)

    max_rows = TOP_K * N + (SEG_ALIGN - 1) * n_route_tiles * N_EXPERTS + N_EXPERTS * (CAP + TILE_STEP)
    t_max = max_rows // EXPERT_TILE + N_EXPERTS
    cnt = cnt[:, 0].reshape(n_route_tiles, N_EXPERTS).astype(jnp.int32)
    base, nwin, used_end, row_end, tile_e, tile_row, tile_h, n_tiles = _segment_plan(cnt, t_max)

    xs = _dispatch(xn, rank_t, base, nwin, used_end, row_end, n_rows=max_rows, TM=ROUTE_TILE)
    ys = _experts(xs, tile_e, tile_row, tile_h, n_tiles, ex_w1[l], ex_w3[l], ex_w2[l])
    y_prompt, y_sample = _final(
        x1sh, rank_t, gates_t, ys, p_prompt[l].reshape(B * T, PLE_DIM), p_sample[l].reshape(Bs * Ts, PLE_DIM),
        norm_ple_g[l][None, :], w_ple_gate[l].astype(BF16), w_ple_proj[l].astype(BF16),
        final_norm_g[None, :], base, nwin, TM=ROUTE_TILE)
    return (y_prompt.reshape(B, T, D_MODEL), y_sample.reshape(Bs, Ts, D_MODEL),
            Cp[None], Np[None], Mp[..., 0][None], Bp[None],
            Cs[None], Ns[None], Ms[..., 0][None], Bs_[None])
```

```python
import functools

import jax
import jax.numpy as jnp
from jax import lax
from jax.experimental import pallas as pl
from jax.experimental.pallas import tpu as pltpu

D_MODEL = 1024
HEADS = 4
DH = 128
MLSTM_WIDTH = HEADS * DH
POOL_WIDTH = 512
POOL_WINDOWS = (2, 4, 8, 16)
POOL_GDIM = 128
POOL_BUF = 15
POOL_PAD = 16
N_EXPERTS = 64
TOP_K = 8
N_GROUPS = 8
GROUP_SIZE = N_EXPERTS // N_GROUPS
TOPK_GROUPS = 4
EXPERT_FF = 256
ROUTED_SCALE = 2.5
NORM_EPS = 1e-6
PLE_DIM = 256
PAST_LEN = 16384

COL_Q, COL_K, COL_V, COL_O, COL_U, COL_I, COL_F = 0, 512, 1024, 1536, 2048, 2560, 2688
IN_COLS_PAD = 2816
GATE_COLS = HEADS * 128

ROUTE_TILE = 256
EXPERT_TILE = 3168
ZERO_ROWS = 2048
SEG_ALIGN = 16
CAP = 48
CHUNK_E = 8
N_CHUNKS = N_EXPERTS // CHUNK_E

VMEM_LIMIT = 56 * 1024 * 1024
F32 = jnp.float32
BF16 = jnp.bfloat16
NEG_INF = float("-inf")


def _rms(x, g):
    return x * lax.rsqrt(jnp.mean(x * x, axis=-1, keepdims=True) + NORM_EPS) * g


def _log_sigmoid(x):
    return jnp.minimum(x, 0.0) - jnp.log1p(jnp.exp(-jnp.abs(x)))


def _dot(a, b):
    return jnp.dot(a, b, preferred_element_type=F32)


def _mlstm_chunk(q, k, v, b_col, i_col, C, n, m, causal, eye):
    L = q.shape[0]
    r_col = i_col - b_col
    r_row = jnp.sum(jnp.where(eye, r_col, 0.0), axis=0, keepdims=True)
    d = jnp.where(causal, b_col + r_row, NEG_INF)
    inter = b_col + m
    qb, kb, vb = q.astype(BF16), k.astype(BF16), v.astype(BF16)
    qk = lax.dot_general(qb, kb, (((1,), (1,)), ((), ())), preferred_element_type=F32)
    qc = _dot(qb, C.astype(BF16))
    qn = jnp.sum(q * n, axis=-1, keepdims=True)
    yield
    m_t = jnp.maximum(inter, jnp.max(d, axis=-1, keepdims=True))
    b_last = b_col[L - 1:L, :]
    m_new = jnp.maximum(b_last + m, jnp.max(b_last + r_row, axis=-1, keepdims=True))
    yield
    w_inter = jnp.exp(inter - m_t)
    s = qk * jnp.exp(d - m_t)
    fw = jnp.exp(b_last + m - m_new)
    iw_col = jnp.exp(b_last + r_col - m_new)
    kw = iw_col * k
    yield
    sv = _dot(s.astype(BF16), vb)
    kv = lax.dot_general(kw.astype(BF16), vb, (((0,), (0,)), ((), ())), preferred_element_type=F32)
    ssum = jnp.sum(s, axis=-1, keepdims=True)
    n_new = fw * n + jnp.sum(kw, axis=0, keepdims=True)
    yield
    num = w_inter * qc + sv
    nq = w_inter * qn + ssum
    h = num / jnp.maximum(jnp.abs(nq), jnp.exp(-m_t))
    C_new = fw * C + kv
    return h, C_new, n_new, m_new


def _interleave(gens):
    results = [None] * len(gens)
    live = list(enumerate(gens))
    while live:
        still = []
        for idx, g in live:
            try:
                next(g)
                still.append((idx, g))
            except StopIteration as stop:
                results[idx] = stop.value
        live = still
    return results


def _mlstm_chunk_square(q, k, v, r, b, mp, CN, m, causal, eye, ones):
    L = q.shape[0]
    r_row = jnp.sum(jnp.where(eye, r, 0.0), axis=0, keepdims=True)
    g = jnp.maximum(mp, m)
    qb, kb = q.astype(BF16), k.astype(BF16)
    v1 = jnp.concatenate([v.astype(BF16), ones], axis=1)
    qk = lax.dot_general(qb, kb, (((1,), (1,)), ((), ())), preferred_element_type=F32)
    qcn = _dot(qb, CN.astype(BF16))
    yield
    w_inter = jnp.exp(m - g)
    p = jnp.where(causal, jnp.exp(r_row - g), 0.0)
    g_last = g[L - 1:L, :]
    m_new = b[L - 1:L, :] + g_last
    fw = jnp.exp(m - g_last)
    kw = (jnp.exp(r - g_last) * k).astype(BF16)
    floor = jnp.exp(-(b + g))
    yield
    s = (qk * p).astype(BF16)
    sv = _dot(s, v1)
    kv = lax.dot_general(kw, v1, (((0,), (0,)), ((), ())), preferred_element_type=F32)
    yield
    num = w_inter * qcn[:, :DH] + sv[:, :DH]
    nq = w_inter * qcn[:, DH:] + sv[:, DH:]
    h = num / jnp.maximum(jnp.abs(nq), floor)
    CN_new = jnp.concatenate([fw, fw], axis=1) * CN + kv
    return h, CN_new, m_new[:, 0:1]


def _chunk_scan(x, L, op, fill):
    pos = lax.broadcasted_iota(jnp.int32, x.shape, 0) & (L - 1)
    k = 1
    while k < L:
        x = op(x, jnp.where(pos >= k, pltpu.roll(x, k, axis=0), fill))
        k *= 2
        yield
    return x


def _advance(gens, steps):
    for g in list(gens):
        for _ in range(steps):
            try:
                next(g)
            except StopIteration:
                gens.remove(g)
                break


def _mixer_kernel(*refs, BB, TT, L, start_pos, zero_state):
    if zero_state:
        (x_ref, wcat_ref, gmix_ref, gbias_ref, hng_ref, wpool_ref, pscale_ref, wout_ref,
         x1_ref, c_ref, n_ref, m_ref, buf_ref,
         z_ref, a_ref, b_ref, mp_ref, cn_ref, mix_ref, ext_ref) = refs
    else:
        (x_ref, wcat_ref, gmix_ref, gbias_ref, hng_ref, wpool_ref, pscale_ref, wout_ref,
         c0_ref, n0_ref, m0_ref, buf0_ref,
         x1_ref, c_ref, n_ref, m_ref, buf_ref,
         z_ref, a_ref, b_ref, mp_ref, cn_ref, mix_ref, ext_ref) = refs
    t = pl.program_id(1)
    R = BB * TT
    n_chunks = TT // L
    square = L == DH
    assert not square or (zero_state and BB == 1)

    @pl.when(t == 0)
    def _init():
        ext_ref[:, 0:POOL_PAD, :] = jnp.zeros((BB, POOL_PAD, POOL_WIDTH), F32)
        if zero_state:
            cn_ref[...] = jnp.zeros(cn_ref.shape, F32)
            c_ref[...] = jnp.zeros(c_ref.shape, F32)
            n_ref[...] = jnp.zeros(n_ref.shape, F32)
            m_ref[...] = jnp.zeros(m_ref.shape, F32)
        else:
            c_ref[...] = c0_ref[...]
            n_ref[...] = n0_ref[...]
            m_ref[...] = m0_ref[...]
            ext_ref[:, 1:POOL_PAD, :] = buf0_ref[...]

    hb = _rms(x_ref[...].reshape(R, D_MODEL), gmix_ref[...]).astype(BF16)

    def project(lo, hi):
        z_ref[:, lo:hi] = _dot(hb, wcat_ref[:, lo:hi])

    def gate_stages():
        gbias = gbias_ref[...]
        gi = z_ref[:, COL_I:COL_I + 128] + gbias[:, :128]
        lf = _log_sigmoid(z_ref[:, COL_F:COL_F + 128] + gbias[:, 128:])
        yield
        b_c = yield from _chunk_scan(lf, L, jnp.add, 0.0)
        if square:
            r_c = gi - b_c
            mp_c = yield from _chunk_scan(r_c, L, jnp.maximum, NEG_INF)
            for hd in range(HEADS):
                lanes = slice(hd * DH, (hd + 1) * DH)
                a_ref[:, lanes] = jnp.broadcast_to(r_c[:, hd:hd + 1], (R, DH))
                b_ref[:, lanes] = jnp.broadcast_to(b_c[:, hd:hd + 1], (R, DH))
                mp_ref[:, lanes] = jnp.broadcast_to(mp_c[:, hd:hd + 1], (R, DH))
                yield
        else:
            a_ref[:, 0:128] = gi
            b_ref[:, 0:128] = b_c

    def pool_stages():
        ext_ref[:, POOL_PAD:POOL_PAD + TT, :] = z_ref[:, COL_U:COL_U + POOL_WIDTH].reshape(BB, TT, POOL_WIDTH)
        pos = start_pos + t * TT + lax.broadcasted_iota(jnp.int32, (1, TT, 1), 1)
        pscale = pscale_ref[...]
        yield
        for gidx, w in enumerate(POOL_WINDOWS):
            lanes = slice(gidx * POOL_GDIM, (gidx + 1) * POOL_GDIM)
            u_g = ext_ref[:, POOL_PAD:POOL_PAD + TT, lanes]
            acc = u_g
            for j in range(1, w):
                acc = acc + ext_ref[:, POOL_PAD - j:POOL_PAD - j + TT, lanes]
                if j % 4 == 3:
                    yield
            cnt = jnp.minimum(pos + 1, w).astype(F32)
            pooled = (acc / cnt - u_g).reshape(R, POOL_GDIM)
            mixed = _dot(pooled.astype(BF16), wpool_ref[gidx]) * pscale[:, lanes]
            mix_ref[:, MLSTM_WIDTH + gidx * POOL_GDIM:MLSTM_WIDTH + (gidx + 1) * POOL_GDIM] = mixed
            yield
        new_buf = ext_ref[:, TT + 1:TT + POOL_PAD, :]
        buf_ref[...] = new_buf
        ext_ref[:, 1:POOL_PAD, :] = new_buf

    project(COL_I, IN_COLS_PAD)
    project(COL_U, COL_U + 256)
    project(COL_U + 256, COL_I)
    side = [gate_stages(), pool_stages()]
    for lo in range(0, COL_U, 256):
        project(lo, lo + 256)
        _advance(side, 3)
    while side:
        _advance(side, 1)

    row = lax.broadcasted_iota(jnp.int32, (L, L), 0)
    col = lax.broadcasted_iota(jnp.int32, (L, L), 1)
    causal = row >= col
    eye = row == col
    ones = jnp.ones((DH, DH), BF16)
    hng = hng_ref[...]

    group = 4 if (not square and n_chunks == 1 and BB % 4 == 0) else 1

    def chunk_body(s, carry):
        work = []
        for j in range(group):
            sj = s * group + j
            rows = pl.ds(pl.multiple_of(sj * L, L), L)
            bb = 0 if BB == 1 else sj // n_chunks
            for hd in range(HEADS):
                lanes = slice(hd * DH, (hd + 1) * DH)
                q = z_ref[rows, COL_Q + hd * DH:COL_Q + (hd + 1) * DH] * (DH ** -0.5)
                k = z_ref[rows, COL_K + hd * DH:COL_K + (hd + 1) * DH]
                v = z_ref[rows, COL_V + hd * DH:COL_V + (hd + 1) * DH]
                o = z_ref[rows, COL_O + hd * DH:COL_O + (hd + 1) * DH]
                m = m_ref[bb, hd:hd + 1, :]
                if square:
                    state = (a_ref[rows, lanes], b_ref[rows, lanes], mp_ref[rows, lanes], cn_ref[hd])
                else:
                    state = (b_ref[rows, hd:hd + 1], a_ref[rows, hd:hd + 1], c_ref[bb, hd],
                             n_ref[bb, hd:hd + 1, :])
                work.append((rows, bb, hd, lanes, q, k, v, o, m, state))
        if square:
            results = _interleave([_mlstm_chunk_square(q, k, v, *state, m, causal, eye, ones)
                                   for _, _, _, _, q, k, v, _, m, state in work])
        else:
            results = _interleave([_mlstm_chunk(q, k, v, *state, m, causal, eye)
                                   for _, _, _, _, q, k, v, _, m, state in work])
        scales = [lax.rsqrt(jnp.mean(res[0] * res[0], axis=-1, keepdims=True) + NORM_EPS) for res in results]
        done = []
        for (rows, bb, hd, lanes, _, _, _, o, _, _), res, scale in zip(work, results, scales):
            out = res[0] * scale * hng[:, lanes] * jax.nn.sigmoid(o)
            done.append((rows, bb, hd, lanes, out, res[-1], res[1:-1]))
        for rows, bb, hd, lanes, out, m_new, new_state in done:
            mix_ref[rows, lanes] = out
            m_ref[bb, hd:hd + 1, :] = m_new
            if square:
                cn_ref[hd] = new_state[0]
            else:
                c_ref[bb, hd] = new_state[0]
                n_ref[bb, hd:hd + 1, :] = new_state[1]
        return carry

    lax.fori_loop(0, BB * n_chunks // group, chunk_body, 0)

    if square:
        @pl.when(t == pl.num_programs(1) - 1)
        def _emit_state():
            for hd in range(HEADS):
                cn = cn_ref[hd]
                c_ref[0, hd] = cn[:, :DH]
                n_ref[0, hd:hd + 1, :] = jnp.sum(jnp.where(eye, cn[:, DH:], 0.0), axis=0, keepdims=True)

    out = x_ref[...].reshape(R, D_MODEL) + _dot(mix_ref[...].astype(BF16), wout_ref[...])
    x1_ref[...] = out.reshape(BB, TT, D_MODEL)


def _const_spec(shape):
    nd = len(shape)
    return pl.BlockSpec(shape, lambda b, t, _nd=nd: (0,) * _nd, pipeline_mode=pl.Buffered(1))


def _mixer(x, wcat, gmix, gbias, hng, wpool, pscale, wout, state, *, BB, TT, L, start_pos):
    B, T, _ = x.shape
    zero_state = state is None
    R = BB * TT
    grid = (B // BB, T // TT)
    x_spec = pl.BlockSpec((BB, TT, D_MODEL), lambda b, t: (b, t, 0))
    c_spec = pl.BlockSpec((BB, HEADS, DH, DH), lambda b, t: (b, 0, 0, 0))
    n_spec = pl.BlockSpec((BB, HEADS, DH), lambda b, t: (b, 0, 0))
    m_spec = pl.BlockSpec((BB, HEADS, 1), lambda b, t: (b, 0, 0))
    buf_spec = pl.BlockSpec((BB, POOL_BUF, POOL_WIDTH), lambda b, t: (b, 0, 0))
    weights = (wcat, gmix, gbias, hng, wpool, pscale, wout)
    in_specs = [x_spec] + [_const_spec(w.shape) for w in weights]
    args = [x, *weights]
    if not zero_state:
        in_specs += [c_spec, n_spec, m_spec, buf_spec]
        args += list(state)
    out_shape = (
        jax.ShapeDtypeStruct((B, T, D_MODEL), F32),
        jax.ShapeDtypeStruct((B, HEADS, DH, DH), F32),
        jax.ShapeDtypeStruct((B, HEADS, DH), F32),
        jax.ShapeDtypeStruct((B, HEADS, 1), F32),
        jax.ShapeDtypeStruct((B, POOL_BUF, POOL_WIDTH), F32),
    )
    kern = functools.partial(_mixer_kernel, BB=BB, TT=TT, L=L, start_pos=start_pos, zero_state=zero_state)
    return pl.pallas_call(
        kern,
        grid=grid,
        in_specs=in_specs,
        out_specs=(x_spec, c_spec, n_spec, m_spec, buf_spec),
        out_shape=out_shape,
        scratch_shapes=[
            pltpu.VMEM((R, IN_COLS_PAD), F32),
            pltpu.VMEM((R, GATE_COLS), F32),
            pltpu.VMEM((R, GATE_COLS), F32),
            pltpu.VMEM((R, GATE_COLS), F32),
            pltpu.VMEM((HEADS, DH, 2 * DH), F32),
            pltpu.VMEM((R, D_MODEL), F32),
            pltpu.VMEM((BB, POOL_PAD + TT, POOL_WIDTH), F32),
        ],
        compiler_params=pltpu.CompilerParams(
            dimension_semantics=("arbitrary", "arbitrary"), vmem_limit_bytes=VMEM_LIMIT),
        name="mixer_zero" if zero_state else "mixer_state",
    )(*args)


def _first_index_of_max(work, idx, n, axis):
    mx = jnp.max(work, axis=axis, keepdims=True)
    return jnp.min(jnp.where(work == mx, idx, float(n)), axis=axis, keepdims=True)


def _route(scores_t, bias_t):
    tm = scores_t.shape[1]
    biased = scores_t + bias_t
    b3 = biased.reshape(N_GROUPS, GROUP_SIZE, tm)
    sub = lax.broadcasted_iota(jnp.int32, b3.shape, 1).astype(F32)
    m1 = jnp.max(b3, axis=1, keepdims=True)
    first = jnp.min(jnp.where(b3 == m1, sub, float(GROUP_SIZE)), axis=1, keepdims=True)
    m2 = jnp.max(jnp.where(sub == first, NEG_INF, b3), axis=1, keepdims=True)
    gs = (m1 + m2).reshape(N_GROUPS, tm)
    gidx = lax.broadcasted_iota(jnp.int32, gs.shape, 0).astype(F32)
    gsel = jnp.zeros(gs.shape, F32)
    work = gs
    yield
    for _ in range(TOPK_GROUPS):
        pick = gidx == _first_index_of_max(work, gidx, N_GROUPS, 0)
        gsel = jnp.where(pick, 1.0, gsel)
        work = jnp.where(pick, NEG_INF, work)
    emask = jnp.broadcast_to(gsel.reshape(N_GROUPS, 1, tm), b3.shape).reshape(N_EXPERTS, tm)
    work = jnp.where(emask > 0, biased, NEG_INF)
    eidx = lax.broadcasted_iota(jnp.int32, work.shape, 0).astype(F32)
    mask = jnp.zeros(work.shape, F32)
    yield
    for _ in range(TOP_K):
        pick = eidx == _first_index_of_max(work, eidx, N_EXPERTS, 0)
        mask = jnp.where(pick, 1.0, mask)
        work = jnp.where(pick, NEG_INF, work)
        yield
    sel = mask * scores_t
    return mask, sel / jnp.sum(sel, axis=0, keepdims=True) * ROUTED_SCALE


def _group_specs(block, split):
    return (pl.BlockSpec(block, lambda i, *_: (jnp.minimum(i, split - 1), 0)),
            pl.BlockSpec(block, lambda i, *_: (jnp.maximum(i - split, 0), 0)))


def _router_kernel(x1a_ref, x1b_ref, gffn_ref, rwt_ref, rbias_ref, w1_ref, w3_ref, w2_ref,
                   xn_ref, rank_ref, gates_ref, cnt_ref, x1s_ref, *, split):
    x1 = jnp.where(pl.program_id(0) < split, x1a_ref[...], x1b_ref[...])
    tm = x1.shape[0]
    xn = _rms(x1, gffn_ref[...])
    xb = xn.astype(BF16)
    xn_ref[...] = xb
    logits_t = lax.dot_general(rwt_ref[...], xn, (((1,), (1,)), ((), ())),
                               preferred_element_type=F32, precision=lax.Precision.HIGHEST)
    def routing():
        mask, gates = yield from _route(jax.nn.sigmoid(logits_t), rbias_ref[...])
        gates_ref[...] = gates
        rt = ROUTE_TILE
        before = (lax.broadcasted_iota(jnp.int32, (rt, rt), 0)
                  < lax.broadcasted_iota(jnp.int32, (rt, rt), 1)).astype(BF16)
        for j in range(tm // rt):
            mj = mask[:, j * rt:(j + 1) * rt]
            rank_ref[:, j * rt:(j + 1) * rt] = jnp.where(mj > 0, _dot(mj.astype(BF16), before), -1.0)
            cnt_ref[j * N_EXPERTS:(j + 1) * N_EXPERTS, :] = jnp.broadcast_to(
                jnp.sum(mj, axis=1, keepdims=True), (N_EXPERTS, 128))
            yield

    def shared_expert():
        half = D_MODEL // 2
        a = _dot(xb, w1_ref[...])
        yield
        g = _dot(xb, w3_ref[...])
        yield
        hsh = ((a * jax.nn.sigmoid(a)) * g).astype(BF16)
        yield
        x1s_ref[:, :half] = x1[:, :half] + _dot(hsh, w2_ref[:, :half])
        yield
        x1s_ref[:, half:] = x1[:, half:] + _dot(hsh, w2_ref[:, half:])

    side = [routing(), shared_expert()]
    while side:
        _advance(side, 1)


def _router(x1a, x1b, gffn, rwt, rbias, w1, w3, w2, *, TM):
    N = x1a.shape[0] + x1b.shape[0]
    split = x1a.shape[0] // TM
    tok = pl.BlockSpec((TM, D_MODEL), lambda i: (i, 0))
    per_e = pl.BlockSpec((N_EXPERTS, TM), lambda i: (0, i))
    consts = (gffn, rwt, rbias, w1, w3, w2)
    return pl.pallas_call(
        functools.partial(_router_kernel, split=split),
        grid=(N // TM,),
        in_specs=list(_group_specs((TM, D_MODEL), split)) + [pl.BlockSpec(c.shape, lambda i: (0, 0)) for c in consts],
        out_specs=(tok, per_e, per_e, pl.BlockSpec((TM // ROUTE_TILE * N_EXPERTS, 128), lambda i: (i, 0)), tok),
        out_shape=(jax.ShapeDtypeStruct((N, D_MODEL), BF16),
                   jax.ShapeDtypeStruct((N_EXPERTS, N), F32),
                   jax.ShapeDtypeStruct((N_EXPERTS, N), F32),
                   jax.ShapeDtypeStruct((N // ROUTE_TILE * N_EXPERTS, 128), F32),
                   jax.ShapeDtypeStruct((N, D_MODEL), F32)),
        compiler_params=pltpu.CompilerParams(
            dimension_semantics=("arbitrary",), vmem_limit_bytes=VMEM_LIMIT),
        name="router_shared",
    )(x1a, x1b, *consts)


def _segment_plan(cnt, n_tiles_max):
    seg = (cnt + SEG_ALIGN - 1) // SEG_ALIGN * SEG_ALIGN
    used = jnp.sum(seg, axis=0)
    size = (used + CAP + EXPERT_TILE - 1) // EXPERT_TILE * EXPERT_TILE
    row_end = jnp.cumsum(size)
    row_start = row_end - size
    base = row_start[None, :] + jnp.cumsum(seg, axis=0) - seg
    nwin = jnp.maximum((seg + CAP - 1) // CAP, 1)
    tile_end = row_end // EXPERT_TILE
    n_tiles = tile_end[-1]
    t_ids = jnp.arange(n_tiles_max, dtype=jnp.int32)
    tile_e = jnp.minimum(jnp.sum(t_ids[:, None] >= tile_end[None, :], axis=1), N_EXPERTS - 1)
    i32 = lambda a: a.astype(jnp.int32)
    return (i32(base).reshape(-1), i32(nwin).reshape(-1), i32(row_start + used), i32(row_end),
            i32(tile_e), i32(n_tiles).reshape(1))


def _one_hot_rows(rank_ref, chunk, first_row, values_ref=None):
    tm = rank_ref.shape[1]
    j = (lax.broadcasted_iota(jnp.int32, (CAP, tm), 0) + first_row).astype(F32)
    rows = []
    for k in range(CHUNK_E):
        e = chunk * CHUNK_E + k
        hit = j == rank_ref[e:e + 1, :]
        val = 1.0 if values_ref is None else values_ref[e:e + 1, :]
        rows.append(jnp.where(hit, val, 0.0).astype(BF16))
    return jnp.concatenate(rows, axis=0)


def _window(hbm, base_ref, idx, w):
    start = pl.multiple_of(base_ref[idx] + w * CAP, SEG_ALIGN)
    return hbm.at[pl.ds(start, CAP)]


def _chunk_windows(nwin_ref, tile_idx, c):
    extra = nwin_ref[tile_idx * N_EXPERTS + c * CHUNK_E]
    for k in range(1, CHUNK_E):
        extra = jnp.maximum(extra, nwin_ref[tile_idx * N_EXPERTS + c * CHUNK_E + k])
    return extra


def _tile_windows(nwin_ref, tile_idx):
    extra = _chunk_windows(nwin_ref, tile_idx, 0)
    for c in range(1, N_CHUNKS):
        extra = jnp.maximum(extra, _chunk_windows(nwin_ref, tile_idx, c))
    return extra


def _dispatch_kernel(base_ref, nwin_ref, uend_ref, rend_ref, xn_ref, rank_ref, xs_hbm,
                     stage0, stage1, ostage, zbuf, sem, osem, zsem):
    stage = (stage0, stage1)
    i = pl.program_id(0)
    last = pl.num_programs(0) - 1

    def window_copy(p, e, tile_idx):
        return pltpu.make_async_copy(stage[p].at[pl.ds(e * CAP, CAP)],
                                     _window(xs_hbm, base_ref, tile_idx * N_EXPERTS + e, 0), sem.at[p])

    def wait_tile(p, tile_idx):
        for e in range(N_EXPERTS):
            window_copy(p, e, tile_idx).wait()

    @pl.when(i == 0)
    def _zero_tails():
        zbuf[...] = jnp.zeros(zbuf.shape, BF16)
        sizes = []
        assert 2 * ZERO_ROWS > EXPERT_TILE + CAP
        size = ZERO_ROWS
        while size >= SEG_ALIGN:
            sizes.append(size)
            size //= 2
        pieces = []
        for e in range(N_EXPERTS):
            pos = uend_ref[e]
            length = rend_ref[e] - pos
            for size in sizes:
                take = (length & size) != 0
                cp = pltpu.make_async_copy(zbuf.at[pl.ds(0, size)],
                                           xs_hbm.at[pl.ds(pl.multiple_of(pos, SEG_ALIGN), size)], zsem)
                pieces.append((take, cp))
                pos = pos + jnp.where(take, size, 0)
        for take, cp in pieces:
            pl.when(take)(cp.start)
        for take, cp in pieces:
            pl.when(take)(cp.wait)

    for p in (0, 1):
        @pl.when(i % 2 == p)
        def _step(p=p):
            xb = xn_ref[...]
            rows = CHUNK_E * CAP
            for c in range(N_CHUNKS):
                stage[p][pl.ds(c * rows, rows), :] = _dot(_one_hot_rows(rank_ref, c, 0), xb).astype(BF16)

            @pl.when(i >= 1)
            def _():
                wait_tile(1 - p, i - 1)

            for e in range(N_EXPERTS):
                window_copy(p, e, i).start()

    extra = _tile_windows(nwin_ref, i)

    @pl.when(extra > 1)
    def _long_segments():
        def body(w, carry):
            for c in range(N_CHUNKS):
                @pl.when(w < _chunk_windows(nwin_ref, i, c))
                def _(c=c):
                    ostage[...] = _dot(_one_hot_rows(rank_ref, c, w * CAP), xn_ref[...]).astype(BF16)
                    for k in range(CHUNK_E):
                        idx = i * N_EXPERTS + c * CHUNK_E + k

                        @pl.when(w < nwin_ref[idx])
                        def _(k=k, idx=idx):
                            cp = pltpu.make_async_copy(ostage.at[pl.ds(k * CAP, CAP)],
                                                       _window(xs_hbm, base_ref, idx, w), osem)
                            cp.start()
                            cp.wait()
            return carry

        lax.fori_loop(1, extra, body, 0)

    for p in (0, 1):
        @pl.when((i == last) & (i % 2 == p))
        def _drain(p=p):
            wait_tile(p, i)


def _dispatch(xn, rank_t, base, nwin, used_end, row_end, *, n_rows, TM):
    N = xn.shape[0]
    stage = pltpu.VMEM((N_EXPERTS * CAP, D_MODEL), BF16)
    return pl.pallas_call(
        _dispatch_kernel,
        grid_spec=pltpu.PrefetchScalarGridSpec(
            num_scalar_prefetch=4,
            grid=(N // TM,),
            in_specs=[pl.BlockSpec((TM, D_MODEL), lambda i, *_: (i, 0)),
                      pl.BlockSpec((N_EXPERTS, TM), lambda i, *_: (0, i))],
            out_specs=pl.BlockSpec(memory_space=pl.ANY),
            scratch_shapes=[
                stage, stage,
                pltpu.VMEM((CHUNK_E * CAP, D_MODEL), BF16),
                pltpu.VMEM((ZERO_ROWS, D_MODEL), BF16),
                pltpu.SemaphoreType.DMA((2,)),
                pltpu.SemaphoreType.DMA(()),
                pltpu.SemaphoreType.DMA(()),
            ],
        ),
        out_shape=jax.ShapeDtypeStruct((n_rows, D_MODEL), BF16),
        compiler_params=pltpu.CompilerParams(
            dimension_semantics=("arbitrary",), vmem_limit_bytes=VMEM_LIMIT),
        name="moe_dispatch",
    )(base, nwin, used_end, row_end, xn, rank_t)


def _expert_kernel(tile_e_ref, n_tiles_ref, xs_ref, w1_hbm, w3_hbm, w2_hbm, ys_ref,
                   w1f0, w1f1, w3f0, w3f1, w2f0, w2f1, w1b, w3b, w2b, wsem):
    t = pl.program_id(0)
    valid = t < n_tiles_ref[0]
    e = tile_e_ref[t]
    first_tile = valid & ((t == 0) | (e != tile_e_ref[jnp.maximum(t - 1, 0)]))
    f32_bufs = ((w1f0, w3f0, w2f0), (w1f1, w3f1, w2f1))

    def weight_copies(expert, p):
        return [pltpu.make_async_copy(hbm.at[expert], buf, wsem.at[p, j])
                for j, (hbm, buf) in enumerate(zip((w1_hbm, w3_hbm, w2_hbm), f32_bufs[p]))]

    for p in (0, 1):
        @pl.when(first_tile & (e % 2 == p))
        def _next_expert(p=p):
            @pl.when(t == 0)
            def _():
                for cp in weight_copies(e, p):
                    cp.start()

            for cp in weight_copies(e, p):
                cp.wait()

            @pl.when(e + 1 < N_EXPERTS)
            def _():
                for cp in weight_copies(e + 1, 1 - p):
                    cp.start()

            w1b[...] = f32_bufs[p][0][...].astype(BF16)
            w3b[...] = f32_bufs[p][1][...].astype(BF16)
            w2b[...] = f32_bufs[p][2][...].astype(BF16)

    @pl.when(valid)
    def _compute():
        xb = xs_ref[...]
        a = _dot(xb, w1b[...])
        hb = (a * jax.nn.sigmoid(a)) * _dot(xb, w3b[...])
        ys_ref[...] = _dot(hb.astype(BF16), w2b[...]).astype(BF16)

    @pl.when(t == n_tiles_ref[0])
    def _spare():
        ys_ref[...] = jnp.zeros(ys_ref.shape, BF16)


def _experts(xs, tile_e, n_tiles, w1, w3, w2):
    t_max = xs.shape[0] // EXPERT_TILE
    clamp = lambda t, nt: jnp.minimum(t, nt[0] - 1)
    any_spec = pl.BlockSpec(memory_space=pl.ANY)
    w_in = pltpu.VMEM((D_MODEL, EXPERT_FF), F32)
    w_out = pltpu.VMEM((EXPERT_FF, D_MODEL), F32)
    return pl.pallas_call(
        _expert_kernel,
        grid_spec=pltpu.PrefetchScalarGridSpec(
            num_scalar_prefetch=2,
            grid=(t_max,),
            in_specs=[pl.BlockSpec((EXPERT_TILE, D_MODEL), lambda t, te, nt: (clamp(t, nt), 0)),
                      any_spec, any_spec, any_spec],
            out_specs=pl.BlockSpec((EXPERT_TILE, D_MODEL),
                                   lambda t, te, nt: (jnp.where(t < nt[0], t, t_max), 0)),
            scratch_shapes=[
                w_in, w_in, w_in, w_in, w_out, w_out,
                pltpu.VMEM((D_MODEL, EXPERT_FF), BF16),
                pltpu.VMEM((D_MODEL, EXPERT_FF), BF16),
                pltpu.VMEM((EXPERT_FF, D_MODEL), BF16),
                pltpu.SemaphoreType.DMA((2, 3)),
            ],
        ),
        out_shape=jax.ShapeDtypeStruct(((t_max + 1) * EXPERT_TILE, D_MODEL), BF16),
        compiler_params=pltpu.CompilerParams(
            dimension_semantics=("arbitrary",), vmem_limit_bytes=VMEM_LIMIT),
        name="moe_experts",
    )(tile_e, n_tiles, xs, w1, w3, w2)


def _final_kernel(base_ref, nwin_ref, x1s_ref, rank_ref, gates_ref, ys_hbm, pa_ref, pb_ref, gple_ref,
                  wgate_ref, wproj_ref, gfin_ref, ya_ref, yb_ref, win0, win1, owin, acc_ref, sem, osem,
                  *, split, n_tiles):
    i = pl.program_id(0)
    j = i - 1
    win = (win0, win1)
    parts = 4
    part_chunks = N_CHUNKS // parts
    part_rows = part_chunks * CHUNK_E * CAP
    contract0 = (((0,), (0,)), ((), ()))

    def window_copy(p, e, tile_idx):
        return pltpu.make_async_copy(_window(ys_hbm, base_ref, tile_idx * N_EXPERTS + e, 0),
                                     win[p].at[pl.ds(e * CAP, CAP)], sem.at[p])

    @pl.when(i == 0)
    def _first():
        owin[...] = jnp.zeros(owin.shape, BF16)
        acc_ref[1] = jnp.zeros(acc_ref.shape[1:], F32)
        for e in range(N_EXPERTS):
            window_copy(0, e, i).start()

    def combine_stages(p):
        total = None
        for part in range(parts):
            one_hot = jnp.concatenate(
                [_one_hot_rows(rank_ref, part * part_chunks + c, 0, gates_ref) for c in range(part_chunks)], axis=0)
            d = lax.dot_general(one_hot, win[p][pl.ds(part * part_rows, part_rows), :], contract0,
                                preferred_element_type=F32)
            total = d if total is None else total + d
            yield
        acc_ref[p] = total

    def ple_stages(q):
        half = D_MODEL // 2
        x2 = x1s_ref[...] + acc_ref[q]
        rb = _rms(x2, gple_ref[...]).astype(BF16)
        yield
        g0 = jax.nn.sigmoid(_dot(rb, wgate_ref[:, :half]))
        yield
        g1 = jax.nn.sigmoid(_dot(rb, wgate_ref[:, half:]))
        yield
        pt = jnp.where(j < split, pa_ref[...], pb_ref[...])
        pp = _dot(pt.astype(BF16), wproj_ref[...])
        yield
        y = _rms(x2 + pp * jnp.concatenate([g0, g1], axis=1), gfin_ref[...])

        @pl.when(j < split)
        def _():
            ya_ref[...] = y

        @pl.when(j >= split)
        def _():
            yb_ref[...] = y

    for p in (0, 1):
        @pl.when((i < n_tiles) & (i % 2 == p))
        def _step(p=p):
            for e in range(N_EXPERTS):
                window_copy(p, e, i).wait()

            @pl.when(i + 1 < n_tiles)
            def _():
                for e in range(N_EXPERTS):
                    window_copy(1 - p, e, i + 1).start()

            side = [combine_stages(p), ple_stages(1 - p)]
            while side:
                _advance(side, 1)

    @pl.when(i == n_tiles)
    def _last_tile():
        for _ in ple_stages((n_tiles - 1) % 2):
            pass

    extra = _tile_windows(nwin_ref, jnp.minimum(i, n_tiles - 1))

    @pl.when((i < n_tiles) & (extra > 1))
    def _long_segments():
        def body(w, carry):
            for c in range(N_CHUNKS):
                @pl.when(w < _chunk_windows(nwin_ref, i, c))
                def _(c=c):
                    for k in range(CHUNK_E):
                        idx = i * N_EXPERTS + c * CHUNK_E + k

                        @pl.when(w < nwin_ref[idx])
                        def _(k=k, idx=idx):
                            cp = pltpu.make_async_copy(_window(ys_hbm, base_ref, idx, w),
                                                       owin.at[pl.ds(k * CAP, CAP)], osem)
                            cp.start()
                            cp.wait()
                    acc_ref[i % 2] += lax.dot_general(_one_hot_rows(rank_ref, c, w * CAP, gates_ref), owin[...],
                                                      contract0, preferred_element_type=F32)
            return carry

        lax.fori_loop(1, extra, body, 0)


def _final(x1s, rank_t, gates_t, ys, pa, pb, gple, wgate, wproj, gfin, base, nwin, *, TM):
    N = x1s.shape[0]
    n_tiles = N // TM
    split = pa.shape[0] // TM
    prev = lambda i: jnp.maximum(i - 1, 0)
    tok = pl.BlockSpec((TM, D_MODEL), lambda i, *_: (prev(i), 0))
    per_e = pl.BlockSpec((N_EXPERTS, TM), lambda i, *_: (0, jnp.minimum(i, n_tiles - 1)))
    consts = (gple, wgate, wproj, gfin)
    win = pltpu.VMEM((N_EXPERTS * CAP, D_MODEL), BF16)

    def group_specs(block):
        a, b = _group_specs(block, split)
        return (pl.BlockSpec(block, lambda i, *_: a.index_map(prev(i))),
                pl.BlockSpec(block, lambda i, *_: b.index_map(prev(i))))

    return pl.pallas_call(
        functools.partial(_final_kernel, split=split, n_tiles=n_tiles),
        grid_spec=pltpu.PrefetchScalarGridSpec(
            num_scalar_prefetch=2,
            grid=(n_tiles + 1,),
            in_specs=[tok, per_e, per_e, pl.BlockSpec(memory_space=pl.ANY)]
            + list(group_specs((TM, PLE_DIM)))
            + [pl.BlockSpec(c.shape, lambda i, *_: (0, 0)) for c in consts],
            out_specs=group_specs((TM, D_MODEL)),
            scratch_shapes=[win, win, pltpu.VMEM((CHUNK_E * CAP, D_MODEL), BF16),
                            pltpu.VMEM((2, TM, D_MODEL), F32),
                            pltpu.SemaphoreType.DMA((2,)), pltpu.SemaphoreType.DMA(())],
        ),
        out_shape=(jax.ShapeDtypeStruct((pa.shape[0], D_MODEL), F32),
                   jax.ShapeDtypeStruct((pb.shape[0], D_MODEL), F32)),
        compiler_params=pltpu.CompilerParams(
            dimension_semantics=("arbitrary",), vmem_limit_bytes=VMEM_LIMIT),
        name="combine_ple_final",
    )(base, nwin, x1s, rank_t, gates_t, ys, pa, pb, *consts)


def kernel(x_prompt, x_sample, p_prompt, p_sample, state_C, state_n, state_m, state_pool, norm_mix_g, w_in, b_igate, b_fgate, head_norm_g, w_pool, pool_scale, w_out, norm_ffn_g, router_w, router_bias, ex_w1, ex_w3, ex_w2, sh_w1, sh_w3, sh_w2, norm_ple_g, w_ple_gate, w_ple_proj, final_norm_g):
    depth = norm_mix_g.shape[0]
    assert depth == 1
    l = 0
    B, T, _ = x_prompt.shape
    Bs, Ts, _ = x_sample.shape
    g0 = 4 * MLSTM_WIDTH
    w = w_in[l]
    lane_pad = jnp.zeros((D_MODEL, 128 - HEADS), F32)
    wcat = jnp.concatenate(
        [w[:, :g0], w[:, g0 + 2 * HEADS:], w[:, g0:g0 + HEADS], lane_pad,
         w[:, g0 + HEADS:g0 + 2 * HEADS], lane_pad], axis=1).astype(BF16)
    bias_pad = jnp.zeros((128 - HEADS,), F32)
    gbias = jnp.concatenate([b_igate[l], bias_pad, b_fgate[l], bias_pad])[None, :]
    mixer_w = (wcat, norm_mix_g[l][None, :], gbias, head_norm_g[l][None, :], w_pool[l].astype(BF16),
               pool_scale[l][None, :], w_out[l].astype(BF16))

    x1p, Cp, Np, Mp, Bp = _mixer(x_prompt, *mixer_w, None, BB=1, TT=1024, L=128, start_pos=0)
    state = (state_C[l], state_n[l], state_m[l][..., None], state_pool[l])
    x1s_, Cs, Ns, Ms, Bs_ = _mixer(x_sample, *mixer_w, state, BB=16, TT=Ts, L=Ts, start_pos=PAST_LEN)

    N = B * T + Bs * Ts
    assert (B * T) % ROUTE_TILE == 0 and (Bs * Ts) % ROUTE_TILE == 0
    n_route_tiles = N // ROUTE_TILE

    xn, rank_t, gates_t, cnt, x1sh = _router(
        x1p.reshape(B * T, D_MODEL), x1s_.reshape(Bs * Ts, D_MODEL),
        norm_ffn_g[l][None, :], router_w[l].T, router_bias[l][:, None],
        sh_w1[l].astype(BF16), sh_w3[l].astype(BF16), sh_w2[l].astype(BF16), TM=2 * ROUTE_TILE)

    max_rows = TOP_K * N + (SEG_ALIGN - 1) * n_route_tiles * N_EXPERTS + N_EXPERTS * (CAP + EXPERT_TILE)
    t_max = -(-max_rows // EXPERT_TILE)
    cnt = cnt[:, 0].reshape(n_route_tiles, N_EXPERTS).astype(jnp.int32)
    base, nwin, used_end, row_end, tile_e, n_tiles = _segment_plan(cnt, t_max)

    xs = _dispatch(xn, rank_t, base, nwin, used_end, row_end, n_rows=t_max * EXPERT_TILE, TM=ROUTE_TILE)
    ys = _experts(xs, tile_e, n_tiles, ex_w1[l], ex_w3[l], ex_w2[l])
    y_prompt, y_sample = _final(
        x1sh, rank_t, gates_t, ys, p_prompt[l].reshape(B * T, PLE_DIM), p_sample[l].reshape(Bs * Ts, PLE_DIM),
        norm_ple_g[l][None, :], w_ple_gate[l].astype(BF16), w_ple_proj[l].astype(BF16),
        final_norm_g[None, :], base, nwin, TM=ROUTE_TILE)
    return (y_prompt.reshape(B, T, D_MODEL), y_sample.reshape(Bs, Ts, D_MODEL),
            Cp[None], Np[None], Mp[..., 0][None], Bp[None],
            Cs[None], Ns[None], Ms[..., 0][None], Bs_[None])
```

```python
import functools

import jax
import jax.numpy as jnp
from jax import lax
from jax.experimental import pallas as pl
from jax.experimental.pallas import tpu as pltpu

D_MODEL = 1024
HEADS = 4
DH = 128
MLSTM_WIDTH = HEADS * DH
POOL_WIDTH = 512
POOL_WINDOWS = (2, 4, 8, 16)
POOL_GDIM = 128
POOL_BUF = 15
POOL_PAD = 16
N_EXPERTS = 64
TOP_K = 8
N_GROUPS = 8
GROUP_SIZE = N_EXPERTS // N_GROUPS
TOPK_GROUPS = 4
EXPERT_FF = 256
ROUTED_SCALE = 2.5
NORM_EPS = 1e-6
PLE_DIM = 256
PAST_LEN = 16384

COL_Q, COL_K, COL_V, COL_O, COL_U, COL_I, COL_F = 0, 512, 1024, 1536, 2048, 2560, 2688
IN_COLS_PAD = 2816
GATE_COLS = HEADS * 128

ROUTE_TILE = 256
EXPERT_TILE = 3168
ZERO_ROWS = 2048
SEG_ALIGN = 16
CAP = 48
CHUNK_E = 8
N_CHUNKS = N_EXPERTS // CHUNK_E

VMEM_LIMIT = 56 * 1024 * 1024
F32 = jnp.float32
BF16 = jnp.bfloat16
NEG_INF = float("-inf")


def _rms(x, g):
    return x * lax.rsqrt(jnp.mean(x * x, axis=-1, keepdims=True) + NORM_EPS) * g


def _log_sigmoid(x):
    return jnp.minimum(x, 0.0) - jnp.log1p(jnp.exp(-jnp.abs(x)))


def _dot(a, b):
    return jnp.dot(a, b, preferred_element_type=F32)


def _mlstm_chunk(q, k, v, b_col, i_col, C, n, m, causal, eye):
    L = q.shape[0]
    r_col = i_col - b_col
    r_row = jnp.sum(jnp.where(eye, r_col, 0.0), axis=0, keepdims=True)
    d = jnp.where(causal, b_col + r_row, NEG_INF)
    inter = b_col + m
    qb, kb, vb = q.astype(BF16), k.astype(BF16), v.astype(BF16)
    qk = lax.dot_general(qb, kb, (((1,), (1,)), ((), ())), preferred_element_type=F32)
    qc = _dot(qb, C.astype(BF16))
    qn = jnp.sum(q * n, axis=-1, keepdims=True)
    yield
    m_t = jnp.maximum(inter, jnp.max(d, axis=-1, keepdims=True))
    b_last = b_col[L - 1:L, :]
    m_new = jnp.maximum(b_last + m, jnp.max(b_last + r_row, axis=-1, keepdims=True))
    yield
    w_inter = jnp.exp(inter - m_t)
    s = qk * jnp.exp(d - m_t)
    fw = jnp.exp(b_last + m - m_new)
    iw_col = jnp.exp(b_last + r_col - m_new)
    kw = iw_col * k
    yield
    sv = _dot(s.astype(BF16), vb)
    kv = lax.dot_general(kw.astype(BF16), vb, (((0,), (0,)), ((), ())), preferred_element_type=F32)
    ssum = jnp.sum(s, axis=-1, keepdims=True)
    n_new = fw * n + jnp.sum(kw, axis=0, keepdims=True)
    yield
    num = w_inter * qc + sv
    nq = w_inter * qn + ssum
    h = num / jnp.maximum(jnp.abs(nq), jnp.exp(-m_t))
    C_new = fw * C + kv
    return h, C_new, n_new, m_new


def _interleave(gens):
    results = [None] * len(gens)
    live = list(enumerate(gens))
    while live:
        still = []
        for idx, g in live:
            try:
                next(g)
                still.append((idx, g))
            except StopIteration as stop:
                results[idx] = stop.value
        live = still
    return results


def _mlstm_chunk_square(q, k, v, r, b, mp, CN, m, causal, eye, ones):
    L = q.shape[0]
    r_row = jnp.sum(jnp.where(eye, r, 0.0), axis=0, keepdims=True)
    g = jnp.maximum(mp, m)
    qb, kb = q.astype(BF16), k.astype(BF16)
    v1 = jnp.concatenate([v.astype(BF16), ones], axis=1)
    qk = lax.dot_general(qb, kb, (((1,), (1,)), ((), ())), preferred_element_type=F32)
    qcn = _dot(qb, CN.astype(BF16))
    yield
    w_inter = jnp.exp(m - g)
    p = jnp.where(causal, jnp.exp(r_row - g), 0.0)
    g_last = g[L - 1:L, :]
    m_new = b[L - 1:L, :] + g_last
    fw = jnp.exp(m - g_last)
    kw = (jnp.exp(r - g_last) * k).astype(BF16)
    floor = jnp.exp(-(b + g))
    yield
    s = (qk * p).astype(BF16)
    sv = _dot(s, v1)
    kv = lax.dot_general(kw, v1, (((0,), (0,)), ((), ())), preferred_element_type=F32)
    yield
    num = w_inter * qcn[:, :DH] + sv[:, :DH]
    nq = w_inter * qcn[:, DH:] + sv[:, DH:]
    h = num / jnp.maximum(jnp.abs(nq), floor)
    CN_new = jnp.concatenate([fw, fw], axis=1) * CN + kv
    return h, CN_new, m_new[:, 0:1]


def _chunk_scan(x, L, op, fill):
    pos = lax.broadcasted_iota(jnp.int32, x.shape, 0) & (L - 1)
    k = 1
    while k < L:
        x = op(x, jnp.where(pos >= k, pltpu.roll(x, k, axis=0), fill))
        k *= 2
        yield
    return x


def _advance(gens, steps):
    for g in list(gens):
        for _ in range(steps):
            try:
                next(g)
            except StopIteration:
                gens.remove(g)
                break


def _mixer_kernel(*refs, BB, TT, L, start_pos, zero_state):
    if zero_state:
        (x_ref, wcat_ref, gmix_ref, gbias_ref, hng_ref, wpool_ref, pscale_ref, wout_ref,
         x1_ref, c_ref, n_ref, m_ref, buf_ref,
         z_ref, a_ref, b_ref, mp_ref, cn_ref, mix_ref, ext_ref) = refs
    else:
        (x_ref, wcat_ref, gmix_ref, gbias_ref, hng_ref, wpool_ref, pscale_ref, wout_ref,
         c0_ref, n0_ref, m0_ref, buf0_ref,
         x1_ref, c_ref, n_ref, m_ref, buf_ref,
         z_ref, a_ref, b_ref, mp_ref, cn_ref, mix_ref, ext_ref) = refs
    t = pl.program_id(1)
    R = BB * TT
    n_chunks = TT // L
    square = L == DH
    assert not square or (zero_state and BB == 1)

    @pl.when(t == 0)
    def _init():
        ext_ref[:, 0:POOL_PAD, :] = jnp.zeros((BB, POOL_PAD, POOL_WIDTH), F32)
        if zero_state:
            cn_ref[...] = jnp.zeros(cn_ref.shape, F32)
            c_ref[...] = jnp.zeros(c_ref.shape, F32)
            n_ref[...] = jnp.zeros(n_ref.shape, F32)
            m_ref[...] = jnp.zeros(m_ref.shape, F32)
        else:
            c_ref[...] = c0_ref[...]
            n_ref[...] = n0_ref[...]
            m_ref[...] = m0_ref[...]
            ext_ref[:, 1:POOL_PAD, :] = buf0_ref[...]

    hb = _rms(x_ref[...].reshape(R, D_MODEL), gmix_ref[...]).astype(BF16)

    def project(lo, hi):
        z_ref[:, lo:hi] = _dot(hb, wcat_ref[:, lo:hi])

    def gate_stages():
        gbias = gbias_ref[...]
        gi = z_ref[:, COL_I:COL_I + 128] + gbias[:, :128]
        lf = _log_sigmoid(z_ref[:, COL_F:COL_F + 128] + gbias[:, 128:])
        yield
        b_c = yield from _chunk_scan(lf, L, jnp.add, 0.0)
        if square:
            r_c = gi - b_c
            mp_c = yield from _chunk_scan(r_c, L, jnp.maximum, NEG_INF)
            for hd in range(HEADS):
                lanes = slice(hd * DH, (hd + 1) * DH)
                a_ref[:, lanes] = jnp.broadcast_to(r_c[:, hd:hd + 1], (R, DH))
                b_ref[:, lanes] = jnp.broadcast_to(b_c[:, hd:hd + 1], (R, DH))
                mp_ref[:, lanes] = jnp.broadcast_to(mp_c[:, hd:hd + 1], (R, DH))
                yield
        else:
            a_ref[:, 0:128] = gi
            b_ref[:, 0:128] = b_c

    def pool_stages():
        ext_ref[:, POOL_PAD:POOL_PAD + TT, :] = z_ref[:, COL_U:COL_U + POOL_WIDTH].reshape(BB, TT, POOL_WIDTH)
        pos = start_pos + t * TT + lax.broadcasted_iota(jnp.int32, (1, TT, 1), 1)
        pscale = pscale_ref[...]
        yield
        for gidx, w in enumerate(POOL_WINDOWS):
            lanes = slice(gidx * POOL_GDIM, (gidx + 1) * POOL_GDIM)
            u_g = ext_ref[:, POOL_PAD:POOL_PAD + TT, lanes]
            acc = u_g
            for j in range(1, w):
                acc = acc + ext_ref[:, POOL_PAD - j:POOL_PAD - j + TT, lanes]
                if j % 4 == 3:
                    yield
            cnt = jnp.minimum(pos + 1, w).astype(F32)
            pooled = (acc / cnt - u_g).reshape(R, POOL_GDIM)
            mixed = _dot(pooled.astype(BF16), wpool_ref[gidx]) * pscale[:, lanes]
            mix_ref[:, MLSTM_WIDTH + gidx * POOL_GDIM:MLSTM_WIDTH + (gidx + 1) * POOL_GDIM] = mixed
            yield
        new_buf = ext_ref[:, TT + 1:TT + POOL_PAD, :]
        buf_ref[...] = new_buf
        ext_ref[:, 1:POOL_PAD, :] = new_buf

    project(COL_I, IN_COLS_PAD)
    project(COL_U, COL_U + 256)
    project(COL_U + 256, COL_I)
    side = [gate_stages(), pool_stages()]
    for lo in range(0, COL_U, 256):
        project(lo, lo + 256)
        _advance(side, 3)
    while side:
        _advance(side, 1)

    row = lax.broadcasted_iota(jnp.int32, (L, L), 0)
    col = lax.broadcasted_iota(jnp.int32, (L, L), 1)
    causal = row >= col
    eye = row == col
    ones = jnp.ones((DH, DH), BF16)
    hng = hng_ref[...]

    group = 4 if (not square and n_chunks == 1 and BB % 4 == 0) else 1

    def chunk_body(s, carry):
        work = []
        for j in range(group):
            sj = s * group + j
            rows = pl.ds(pl.multiple_of(sj * L, L), L)
            bb = 0 if BB == 1 else sj // n_chunks
            for hd in range(HEADS):
                lanes = slice(hd * DH, (hd + 1) * DH)
                q = z_ref[rows, COL_Q + hd * DH:COL_Q + (hd + 1) * DH] * (DH ** -0.5)
                k = z_ref[rows, COL_K + hd * DH:COL_K + (hd + 1) * DH]
                v = z_ref[rows, COL_V + hd * DH:COL_V + (hd + 1) * DH]
                o = z_ref[rows, COL_O + hd * DH:COL_O + (hd + 1) * DH]
                m = m_ref[bb, hd:hd + 1, :]
                if square:
                    state = (a_ref[rows, lanes], b_ref[rows, lanes], mp_ref[rows, lanes], cn_ref[hd])
                else:
                    state = (b_ref[rows, hd:hd + 1], a_ref[rows, hd:hd + 1], c_ref[bb, hd],
                             n_ref[bb, hd:hd + 1, :])
                work.append((rows, bb, hd, lanes, q, k, v, o, m, state))
        if square:
            results = _interleave([_mlstm_chunk_square(q, k, v, *state, m, causal, eye, ones)
                                   for _, _, _, _, q, k, v, _, m, state in work])
        else:
            results = _interleave([_mlstm_chunk(q, k, v, *state, m, causal, eye)
                                   for _, _, _, _, q, k, v, _, m, state in work])
        scales = [lax.rsqrt(jnp.mean(res[0] * res[0], axis=-1, keepdims=True) + NORM_EPS) for res in results]
        done = []
        for (rows, bb, hd, lanes, _, _, _, o, _, _), res, scale in zip(work, results, scales):
            out = res[0] * scale * hng[:, lanes] * jax.nn.sigmoid(o)
            done.append((rows, bb, hd, lanes, out, res[-1], res[1:-1]))
        for rows, bb, hd, lanes, out, m_new, new_state in done:
            mix_ref[rows, lanes] = out
            m_ref[bb, hd:hd + 1, :] = m_new
            if square:
                cn_ref[hd] = new_state[0]
            else:
                c_ref[bb, hd] = new_state[0]
                n_ref[bb, hd:hd + 1, :] = new_state[1]
        return carry

    lax.fori_loop(0, BB * n_chunks // group, chunk_body, 0)

    if square:
        @pl.when(t == pl.num_programs(1) - 1)
        def _emit_state():
            for hd in range(HEADS):
                cn = cn_ref[hd]
                c_ref[0, hd] = cn[:, :DH]
                n_ref[0, hd:hd + 1, :] = jnp.sum(jnp.where(eye, cn[:, DH:], 0.0), axis=0, keepdims=True)

    out = x_ref[...].reshape(R, D_MODEL) + _dot(mix_ref[...].astype(BF16), wout_ref[...])
    x1_ref[...] = out.reshape(BB, TT, D_MODEL)


def _const_spec(shape):
    nd = len(shape)
    return pl.BlockSpec(shape, lambda b, t, _nd=nd: (0,) * _nd, pipeline_mode=pl.Buffered(1))


def _mixer(x, wcat, gmix, gbias, hng, wpool, pscale, wout, state, *, BB, TT, L, start_pos):
    B, T, _ = x.shape
    zero_state = state is None
    R = BB * TT
    grid = (B // BB, T // TT)
    x_spec = pl.BlockSpec((BB, TT, D_MODEL), lambda b, t: (b, t, 0))
    c_spec = pl.BlockSpec((BB, HEADS, DH, DH), lambda b, t: (b, 0, 0, 0))
    n_spec = pl.BlockSpec((BB, HEADS, DH), lambda b, t: (b, 0, 0))
    m_spec = pl.BlockSpec((BB, HEADS, 1), lambda b, t: (b, 0, 0))
    buf_spec = pl.BlockSpec((BB, POOL_BUF, POOL_WIDTH), lambda b, t: (b, 0, 0))
    weights = (wcat, gmix, gbias, hng, wpool, pscale, wout)
    in_specs = [x_spec] + [_const_spec(w.shape) for w in weights]
    args = [x, *weights]
    if not zero_state:
        in_specs += [c_spec, n_spec, m_spec, buf_spec]
        args += list(state)
    out_shape = (
        jax.ShapeDtypeStruct((B, T, D_MODEL), F32),
        jax.ShapeDtypeStruct((B, HEADS, DH, DH), F32),
        jax.ShapeDtypeStruct((B, HEADS, DH), F32),
        jax.ShapeDtypeStruct((B, HEADS, 1), F32),
        jax.ShapeDtypeStruct((B, POOL_BUF, POOL_WIDTH), F32),
    )
    kern = functools.partial(_mixer_kernel, BB=BB, TT=TT, L=L, start_pos=start_pos, zero_state=zero_state)
    return pl.pallas_call(
        kern,
        grid=grid,
        in_specs=in_specs,
        out_specs=(x_spec, c_spec, n_spec, m_spec, buf_spec),
        out_shape=out_shape,
        scratch_shapes=[
            pltpu.VMEM((R, IN_COLS_PAD), F32),
            pltpu.VMEM((R, GATE_COLS), F32),
            pltpu.VMEM((R, GATE_COLS), F32),
            pltpu.VMEM((R, GATE_COLS), F32),
            pltpu.VMEM((HEADS, DH, 2 * DH), F32),
            pltpu.VMEM((R, D_MODEL), F32),
            pltpu.VMEM((BB, POOL_PAD + TT, POOL_WIDTH), F32),
        ],
        compiler_params=pltpu.CompilerParams(
            dimension_semantics=("arbitrary", "arbitrary"), vmem_limit_bytes=VMEM_LIMIT),
        name="mixer_zero" if zero_state else "mixer_state",
    )(*args)


def _first_index_of_max(work, idx, n, axis):
    mx = jnp.max(work, axis=axis, keepdims=True)
    return jnp.min(jnp.where(work == mx, idx, float(n)), axis=axis, keepdims=True)


def _route(scores_t, bias_t):
    tm = scores_t.shape[1]
    biased = scores_t + bias_t
    b3 = biased.reshape(N_GROUPS, GROUP_SIZE, tm)
    sub = lax.broadcasted_iota(jnp.int32, b3.shape, 1).astype(F32)
    m1 = jnp.max(b3, axis=1, keepdims=True)
    first = jnp.min(jnp.where(b3 == m1, sub, float(GROUP_SIZE)), axis=1, keepdims=True)
    m2 = jnp.max(jnp.where(sub == first, NEG_INF, b3), axis=1, keepdims=True)
    gs = (m1 + m2).reshape(N_GROUPS, tm)
    gidx = lax.broadcasted_iota(jnp.int32, gs.shape, 0).astype(F32)
    gsel = jnp.zeros(gs.shape, F32)
    work = gs
    yield
    for _ in range(TOPK_GROUPS):
        pick = gidx == _first_index_of_max(work, gidx, N_GROUPS, 0)
        gsel = jnp.where(pick, 1.0, gsel)
        work = jnp.where(pick, NEG_INF, work)
    emask = jnp.broadcast_to(gsel.reshape(N_GROUPS, 1, tm), b3.shape).reshape(N_EXPERTS, tm)
    work = jnp.where(emask > 0, biased, NEG_INF)
    eidx = lax.broadcasted_iota(jnp.int32, work.shape, 0).astype(F32)
    mask = jnp.zeros(work.shape, F32)
    yield
    for _ in range(TOP_K):
        pick = eidx == _first_index_of_max(work, eidx, N_EXPERTS, 0)
        mask = jnp.where(pick, 1.0, mask)
        work = jnp.where(pick, NEG_INF, work)
        yield
    sel = mask * scores_t
    return mask, sel / jnp.sum(sel, axis=0, keepdims=True) * ROUTED_SCALE


def _group_specs(block, split):
    return (pl.BlockSpec(block, lambda i, *_: (jnp.minimum(i, split - 1), 0)),
            pl.BlockSpec(block, lambda i, *_: (jnp.maximum(i - split, 0), 0)))


def _router_kernel(x1a_ref, x1b_ref, gffn_ref, rwt_ref, rbias_ref, w1_ref, w3_ref, w2_ref,
                   xn_ref, rank_ref, gates_ref, cnt_ref, x1s_ref, *, split):
    x1 = jnp.where(pl.program_id(0) < split, x1a_ref[...], x1b_ref[...])
    tm = x1.shape[0]
    xn = _rms(x1, gffn_ref[...])
    xb = xn.astype(BF16)
    xn_ref[...] = xb
    logits_t = lax.dot_general(rwt_ref[...], xn, (((1,), (1,)), ((), ())),
                               preferred_element_type=F32, precision=lax.Precision.HIGHEST)
    def routing():
        mask, gates = yield from _route(jax.nn.sigmoid(logits_t), rbias_ref[...])
        gates_ref[...] = gates
        rt = ROUTE_TILE
        before = (lax.broadcasted_iota(jnp.int32, (rt, rt), 0)
                  < lax.broadcasted_iota(jnp.int32, (rt, rt), 1)).astype(BF16)
        for j in range(tm // rt):
            mj = mask[:, j * rt:(j + 1) * rt]
            rank_ref[:, j * rt:(j + 1) * rt] = jnp.where(mj > 0, _dot(mj.astype(BF16), before), -1.0)
            cnt_ref[j * N_EXPERTS:(j + 1) * N_EXPERTS, :] = jnp.broadcast_to(
                jnp.sum(mj, axis=1, keepdims=True), (N_EXPERTS, 128))
            yield

    def shared_expert():
        half = D_MODEL // 2
        a = _dot(xb, w1_ref[...])
        yield
        g = _dot(xb, w3_ref[...])
        yield
        hsh = ((a * jax.nn.sigmoid(a)) * g).astype(BF16)
        yield
        x1s_ref[:, :half] = x1[:, :half] + _dot(hsh, w2_ref[:, :half])
        yield
        x1s_ref[:, half:] = x1[:, half:] + _dot(hsh, w2_ref[:, half:])

    side = [routing(), shared_expert()]
    while side:
        _advance(side, 1)


def _router(x1a, x1b, gffn, rwt, rbias, w1, w3, w2, *, TM):
    N = x1a.shape[0] + x1b.shape[0]
    split = x1a.shape[0] // TM
    tok = pl.BlockSpec((TM, D_MODEL), lambda i: (i, 0))
    per_e = pl.BlockSpec((N_EXPERTS, TM), lambda i: (0, i))
    consts = (gffn, rwt, rbias, w1, w3, w2)
    return pl.pallas_call(
        functools.partial(_router_kernel, split=split),
        grid=(N // TM,),
        in_specs=list(_group_specs((TM, D_MODEL), split)) + [pl.BlockSpec(c.shape, lambda i: (0, 0)) for c in consts],
        out_specs=(tok, per_e, per_e, pl.BlockSpec((TM // ROUTE_TILE * N_EXPERTS, 128), lambda i: (i, 0)), tok),
        out_shape=(jax.ShapeDtypeStruct((N, D_MODEL), BF16),
                   jax.ShapeDtypeStruct((N_EXPERTS, N), F32),
                   jax.ShapeDtypeStruct((N_EXPERTS, N), F32),
                   jax.ShapeDtypeStruct((N // ROUTE_TILE * N_EXPERTS, 128), F32),
                   jax.ShapeDtypeStruct((N, D_MODEL), F32)),
        compiler_params=pltpu.CompilerParams(
            dimension_semantics=("arbitrary",), vmem_limit_bytes=VMEM_LIMIT),
        name="router_shared",
    )(x1a, x1b, *consts)


def _segment_plan(cnt, n_tiles_max):
    seg = (cnt + SEG_ALIGN - 1) // SEG_ALIGN * SEG_ALIGN
    used = jnp.sum(seg, axis=0)
    size = (used + CAP + EXPERT_TILE - 1) // EXPERT_TILE * EXPERT_TILE
    row_end = jnp.cumsum(size)
    row_start = row_end - size
    base = row_start[None, :] + jnp.cumsum(seg, axis=0) - seg
    nwin = jnp.maximum((seg + CAP - 1) // CAP, 1)
    tile_end = row_end // EXPERT_TILE
    n_tiles = tile_end[-1]
    t_ids = jnp.arange(n_tiles_max, dtype=jnp.int32)
    tile_e = jnp.minimum(jnp.sum(t_ids[:, None] >= tile_end[None, :], axis=1), N_EXPERTS - 1)
    i32 = lambda a: a.astype(jnp.int32)
    return (i32(base).reshape(-1), i32(nwin).reshape(-1), i32(row_start + used), i32(row_end),
            i32(tile_e), i32(n_tiles).reshape(1))


def _one_hot_rows(rank_ref, chunk, first_row, values_ref=None):
    tm = rank_ref.shape[1]
    j = (lax.broadcasted_iota(jnp.int32, (CAP, tm), 0) + first_row).astype(F32)
    rows = []
    for k in range(CHUNK_E):
        e = chunk * CHUNK_E + k
        hit = j == rank_ref[e:e + 1, :]
        val = 1.0 if values_ref is None else values_ref[e:e + 1, :]
        rows.append(jnp.where(hit, val, 0.0).astype(BF16))
    return jnp.concatenate(rows, axis=0)


def _window(hbm, base_ref, idx, w):
    start = pl.multiple_of(base_ref[idx] + w * CAP, SEG_ALIGN)
    return hbm.at[pl.ds(start, CAP)]


def _chunk_windows(nwin_ref, tile_idx, c):
    extra = nwin_ref[tile_idx * N_EXPERTS + c * CHUNK_E]
    for k in range(1, CHUNK_E):
        extra = jnp.maximum(extra, nwin_ref[tile_idx * N_EXPERTS + c * CHUNK_E + k])
    return extra


def _tile_windows(nwin_ref, tile_idx):
    extra = _chunk_windows(nwin_ref, tile_idx, 0)
    for c in range(1, N_CHUNKS):
        extra = jnp.maximum(extra, _chunk_windows(nwin_ref, tile_idx, c))
    return extra


def _dispatch_kernel(base_ref, nwin_ref, uend_ref, rend_ref, xn_ref, rank_ref, xs_hbm,
                     stage0, stage1, ostage, zbuf, sem, osem, zsem):
    stage = (stage0, stage1)
    i = pl.program_id(0)
    last = pl.num_programs(0) - 1

    def window_copy(p, e, tile_idx):
        return pltpu.make_async_copy(stage[p].at[pl.ds(e * CAP, CAP)],
                                     _window(xs_hbm, base_ref, tile_idx * N_EXPERTS + e, 0), sem.at[p])

    def wait_tile(p, tile_idx):
        for e in range(N_EXPERTS):
            window_copy(p, e, tile_idx).wait()

    @pl.when(i == 0)
    def _zero_tails():
        zbuf[...] = jnp.zeros(zbuf.shape, BF16)
        sizes = []
        assert 2 * ZERO_ROWS > EXPERT_TILE + CAP
        size = ZERO_ROWS
        while size >= SEG_ALIGN:
            sizes.append(size)
            size //= 2
        pieces = []
        for e in range(N_EXPERTS):
            pos = uend_ref[e]
            length = rend_ref[e] - pos
            for size in sizes:
                take = (length & size) != 0
                cp = pltpu.make_async_copy(zbuf.at[pl.ds(0, size)],
                                           xs_hbm.at[pl.ds(pl.multiple_of(pos, SEG_ALIGN), size)], zsem)
                pieces.append((take, cp))
                pos = pos + jnp.where(take, size, 0)
        for take, cp in pieces:
            pl.when(take)(cp.start)
        for take, cp in pieces:
            pl.when(take)(cp.wait)

    for p in (0, 1):
        @pl.when(i % 2 == p)
        def _step(p=p):
            xb = xn_ref[...]
            rows = CHUNK_E * CAP
            for c in range(N_CHUNKS):
                stage[p][pl.ds(c * rows, rows), :] = _dot(_one_hot_rows(rank_ref, c, 0), xb).astype(BF16)

            @pl.when(i >= 1)
            def _():
                wait_tile(1 - p, i - 1)

            for e in range(N_EXPERTS):
                window_copy(p, e, i).start()

    extra = _tile_windows(nwin_ref, i)

    @pl.when(extra > 1)
    def _long_segments():
        def body(w, carry):
            for c in range(N_CHUNKS):
                @pl.when(w < _chunk_windows(nwin_ref, i, c))
                def _(c=c):
                    ostage[...] = _dot(_one_hot_rows(rank_ref, c, w * CAP), xn_ref[...]).astype(BF16)
                    for k in range(CHUNK_E):
                        idx = i * N_EXPERTS + c * CHUNK_E + k

                        @pl.when(w < nwin_ref[idx])
                        def _(k=k, idx=idx):
                            cp = pltpu.make_async_copy(ostage.at[pl.ds(k * CAP, CAP)],
                                                       _window(xs_hbm, base_ref, idx, w), osem)
                            cp.start()
                            cp.wait()
            return carry

        lax.fori_loop(1, extra, body, 0)

    for p in (0, 1):
        @pl.when((i == last) & (i % 2 == p))
        def _drain(p=p):
            wait_tile(p, i)


def _dispatch(xn, rank_t, base, nwin, used_end, row_end, *, n_rows, TM):
    N = xn.shape[0]
    stage = pltpu.VMEM((N_EXPERTS * CAP, D_MODEL), BF16)
    return pl.pallas_call(
        _dispatch_kernel,
        grid_spec=pltpu.PrefetchScalarGridSpec(
            num_scalar_prefetch=4,
            grid=(N // TM,),
            in_specs=[pl.BlockSpec((TM, D_MODEL), lambda i, *_: (i, 0)),
                      pl.BlockSpec((N_EXPERTS, TM), lambda i, *_: (0, i))],
            out_specs=pl.BlockSpec(memory_space=pl.ANY),
            scratch_shapes=[
                stage, stage,
                pltpu.VMEM((CHUNK_E * CAP, D_MODEL), BF16),
                pltpu.VMEM((ZERO_ROWS, D_MODEL), BF16),
                pltpu.SemaphoreType.DMA((2,)),
                pltpu.SemaphoreType.DMA(()),
                pltpu.SemaphoreType.DMA(()),
            ],
        ),
        out_shape=jax.ShapeDtypeStruct((n_rows, D_MODEL), BF16),
        compiler_params=pltpu.CompilerParams(
            dimension_semantics=("arbitrary",), vmem_limit_bytes=VMEM_LIMIT),
        name="moe_dispatch",
    )(base, nwin, used_end, row_end, xn, rank_t)


def _expert_kernel(tile_e_ref, n_tiles_ref, xs_ref, w1_hbm, w3_hbm, w2_hbm, ys_ref,
                   w1f0, w1f1, w3f0, w3f1, w2f0, w2f1, w1b, w3b, w2b, wsem):
    t = pl.program_id(0)
    valid = t < n_tiles_ref[0]
    e = tile_e_ref[t]
    first_tile = valid & ((t == 0) | (e != tile_e_ref[jnp.maximum(t - 1, 0)]))
    f32_bufs = ((w1f0, w3f0, w2f0), (w1f1, w3f1, w2f1))

    def weight_copies(expert, p):
        return [pltpu.make_async_copy(hbm.at[expert], buf, wsem.at[p, j])
                for j, (hbm, buf) in enumerate(zip((w1_hbm, w3_hbm, w2_hbm), f32_bufs[p]))]

    for p in (0, 1):
        @pl.when(first_tile & (e % 2 == p))
        def _next_expert(p=p):
            @pl.when(t == 0)
            def _():
                for cp in weight_copies(e, p):
                    cp.start()

            for cp in weight_copies(e, p):
                cp.wait()

            @pl.when(e + 1 < N_EXPERTS)
            def _():
                for cp in weight_copies(e + 1, 1 - p):
                    cp.start()

            w1b[...] = f32_bufs[p][0][...].astype(BF16)
            w3b[...] = f32_bufs[p][1][...].astype(BF16)
            w2b[...] = f32_bufs[p][2][...].astype(BF16)

    @pl.when(valid)
    def _compute():
        xb = xs_ref[...]
        a = _dot(xb, w1b[...])
        hb = (a * jax.nn.sigmoid(a)) * _dot(xb, w3b[...])
        ys_ref[...] = _dot(hb.astype(BF16), w2b[...]).astype(BF16)

    @pl.when(t == n_tiles_ref[0])
    def _spare():
        ys_ref[...] = jnp.zeros(ys_ref.shape, BF16)


def _experts(xs, tile_e, n_tiles, w1, w3, w2):
    t_max = xs.shape[0] // EXPERT_TILE
    clamp = lambda t, nt: jnp.minimum(t, nt[0] - 1)
    any_spec = pl.BlockSpec(memory_space=pl.ANY)
    w_in = pltpu.VMEM((D_MODEL, EXPERT_FF), F32)
    w_out = pltpu.VMEM((EXPERT_FF, D_MODEL), F32)
    return pl.pallas_call(
        _expert_kernel,
        grid_spec=pltpu.PrefetchScalarGridSpec(
            num_scalar_prefetch=2,
            grid=(n_tiles[0],),
            in_specs=[pl.BlockSpec((EXPERT_TILE, D_MODEL), lambda t, te, nt: (clamp(t, nt), 0)),
                      any_spec, any_spec, any_spec],
            out_specs=pl.BlockSpec((EXPERT_TILE, D_MODEL),
                                   lambda t, te, nt: (jnp.where(t < nt[0], t, t_max), 0)),
            scratch_shapes=[
                w_in, w_in, w_in, w_in, w_out, w_out,
                pltpu.VMEM((D_MODEL, EXPERT_FF), BF16),
                pltpu.VMEM((D_MODEL, EXPERT_FF), BF16),
                pltpu.VMEM((EXPERT_FF, D_MODEL), BF16),
                pltpu.SemaphoreType.DMA((2, 3)),
            ],
        ),
        out_shape=jax.ShapeDtypeStruct(((t_max + 1) * EXPERT_TILE, D_MODEL), BF16),
        compiler_params=pltpu.CompilerParams(
            dimension_semantics=("arbitrary",), vmem_limit_bytes=VMEM_LIMIT),
        name="moe_experts",
    )(tile_e, n_tiles, xs, w1, w3, w2)


def _final_kernel(base_ref, nwin_ref, x1s_ref, rank_ref, gates_ref, ys_hbm, pa_ref, pb_ref, gple_ref,
                  wgate_ref, wproj_ref, gfin_ref, ya_ref, yb_ref, win0, win1, owin, acc_ref, sem, osem,
                  *, split, n_tiles):
    i = pl.program_id(0)
    j = i - 1
    win = (win0, win1)
    parts = 4
    part_chunks = N_CHUNKS // parts
    part_rows = part_chunks * CHUNK_E * CAP
    contract0 = (((0,), (0,)), ((), ()))

    def window_copy(p, e, tile_idx):
        return pltpu.make_async_copy(_window(ys_hbm, base_ref, tile_idx * N_EXPERTS + e, 0),
                                     win[p].at[pl.ds(e * CAP, CAP)], sem.at[p])

    @pl.when(i == 0)
    def _first():
        owin[...] = jnp.zeros(owin.shape, BF16)
        acc_ref[1] = jnp.zeros(acc_ref.shape[1:], F32)
        for e in range(N_EXPERTS):
            window_copy(0, e, i).start()

    def combine_stages(p):
        total = None
        for part in range(parts):
            one_hot = jnp.concatenate(
                [_one_hot_rows(rank_ref, part * part_chunks + c, 0, gates_ref) for c in range(part_chunks)], axis=0)
            d = lax.dot_general(one_hot, win[p][pl.ds(part * part_rows, part_rows), :], contract0,
                                preferred_element_type=F32)
            total = d if total is None else total + d
            yield
        acc_ref[p] = total

    def ple_stages(q):
        half = D_MODEL // 2
        x2 = x1s_ref[...] + acc_ref[q]
        rb = _rms(x2, gple_ref[...]).astype(BF16)
        yield
        g0 = jax.nn.sigmoid(_dot(rb, wgate_ref[:, :half]))
        yield
        g1 = jax.nn.sigmoid(_dot(rb, wgate_ref[:, half:]))
        yield
        pt = jnp.where(j < split, pa_ref[...], pb_ref[...])
        pp = _dot(pt.astype(BF16), wproj_ref[...])
        yield
        y = _rms(x2 + pp * jnp.concatenate([g0, g1], axis=1), gfin_ref[...])

        @pl.when(j < split)
        def _():
            ya_ref[...] = y

        @pl.when(j >= split)
        def _():
            yb_ref[...] = y

    for p in (0, 1):
        @pl.when((i < n_tiles) & (i % 2 == p))
        def _step(p=p):
            for e in range(N_EXPERTS):
                window_copy(p, e, i).wait()

            @pl.when(i + 1 < n_tiles)
            def _():
                for e in range(N_EXPERTS):
                    window_copy(1 - p, e, i + 1).start()

            side = [combine_stages(p), ple_stages(1 - p)]
            while side:
                _advance(side, 1)

    @pl.when(i == n_tiles)
    def _last_tile():
        for _ in ple_stages((n_tiles - 1) % 2):
            pass

    extra = _tile_windows(nwin_ref, jnp.minimum(i, n_tiles - 1))

    @pl.when((i < n_tiles) & (extra > 1))
    def _long_segments():
        def body(w, carry):
            for c in range(N_CHUNKS):
                @pl.when(w < _chunk_windows(nwin_ref, i, c))
                def _(c=c):
                    for k in range(CHUNK_E):
                        idx = i * N_EXPERTS + c * CHUNK_E + k

                        @pl.when(w < nwin_ref[idx])
                        def _(k=k, idx=idx):
                            cp = pltpu.make_async_copy(_window(ys_hbm, base_ref, idx, w),
                                                       owin.at[pl.ds(k * CAP, CAP)], osem)
                            cp.start()
                            cp.wait()
                    acc_ref[i % 2] += lax.dot_general(_one_hot_rows(rank_ref, c, w * CAP, gates_ref), owin[...],
                                                      contract0, preferred_element_type=F32)
            return carry

        lax.fori_loop(1, extra, body, 0)


def _final(x1s, rank_t, gates_t, ys, pa, pb, gple, wgate, wproj, gfin, base, nwin, *, TM):
    N = x1s.shape[0]
    n_tiles = N // TM
    split = pa.shape[0] // TM
    prev = lambda i: jnp.maximum(i - 1, 0)
    tok = pl.BlockSpec((TM, D_MODEL), lambda i, *_: (prev(i), 0))
    per_e = pl.BlockSpec((N_EXPERTS, TM), lambda i, *_: (0, jnp.minimum(i, n_tiles - 1)))
    consts = (gple, wgate, wproj, gfin)
    win = pltpu.VMEM((N_EXPERTS * CAP, D_MODEL), BF16)

    def group_specs(block):
        a, b = _group_specs(block, split)
        return (pl.BlockSpec(block, lambda i, *_: a.index_map(prev(i))),
                pl.BlockSpec(block, lambda i, *_: b.index_map(prev(i))))

    return pl.pallas_call(
        functools.partial(_final_kernel, split=split, n_tiles=n_tiles),
        grid_spec=pltpu.PrefetchScalarGridSpec(
            num_scalar_prefetch=2,
            grid=(n_tiles + 1,),
            in_specs=[tok, per_e, per_e, pl.BlockSpec(memory_space=pl.ANY)]
            + list(group_specs((TM, PLE_DIM)))
            + [pl.BlockSpec(c.shape, lambda i, *_: (0, 0)) for c in consts],
            out_specs=group_specs((TM, D_MODEL)),
            scratch_shapes=[win, win, pltpu.VMEM((CHUNK_E * CAP, D_MODEL), BF16),
                            pltpu.VMEM((2, TM, D_MODEL), F32),
                            pltpu.SemaphoreType.DMA((2,)), pltpu.SemaphoreType.DMA(())],
        ),
        out_shape=(jax.ShapeDtypeStruct((pa.shape[0], D_MODEL), F32),
                   jax.ShapeDtypeStruct((pb.shape[0], D_MODEL), F32)),
        compiler_params=pltpu.CompilerParams(
            dimension_semantics=("arbitrary",), vmem_limit_bytes=VMEM_LIMIT),
        name="combine_ple_final",
    )(base, nwin, x1s, rank_t, gates_t, ys, pa, pb, *consts)


def kernel(x_prompt, x_sample, p_prompt, p_sample, state_C, state_n, state_m, state_pool, norm_mix_g, w_in, b_igate, b_fgate, head_norm_g, w_pool, pool_scale, w_out, norm_ffn_g, router_w, router_bias, ex_w1, ex_w3, ex_w2, sh_w1, sh_w3, sh_w2, norm_ple_g, w_ple_gate, w_ple_proj, final_norm_g):
    depth = norm_mix_g.shape[0]
    assert depth == 1
    l = 0
    B, T, _ = x_prompt.shape
    Bs, Ts, _ = x_sample.shape
    g0 = 4 * MLSTM_WIDTH
    w = w_in[l]
    lane_pad = jnp.zeros((D_MODEL, 128 - HEADS), F32)
    wcat = jnp.concatenate(
        [w[:, :g0], w[:, g0 + 2 * HEADS:], w[:, g0:g0 + HEADS], lane_pad,
         w[:, g0 + HEADS:g0 + 2 * HEADS], lane_pad], axis=1).astype(BF16)
    bias_pad = jnp.zeros((128 - HEADS,), F32)
    gbias = jnp.concatenate([b_igate[l], bias_pad, b_fgate[l], bias_pad])[None, :]
    mixer_w = (wcat, norm_mix_g[l][None, :], gbias, head_norm_g[l][None, :], w_pool[l].astype(BF16),
               pool_scale[l][None, :], w_out[l].astype(BF16))

    x1p, Cp, Np, Mp, Bp = _mixer(x_prompt, *mixer_w, None, BB=1, TT=1024, L=128, start_pos=0)
    state = (state_C[l], state_n[l], state_m[l][..., None], state_pool[l])
    x1s_, Cs, Ns, Ms, Bs_ = _mixer(x_sample, *mixer_w, state, BB=16, TT=Ts, L=Ts, start_pos=PAST_LEN)

    N = B * T + Bs * Ts
    assert (B * T) % ROUTE_TILE == 0 and (Bs * Ts) % ROUTE_TILE == 0
    n_route_tiles = N // ROUTE_TILE

    xn, rank_t, gates_t, cnt, x1sh = _router(
        x1p.reshape(B * T, D_MODEL), x1s_.reshape(Bs * Ts, D_MODEL),
        norm_ffn_g[l][None, :], router_w[l].T, router_bias[l][:, None],
        sh_w1[l].astype(BF16), sh_w3[l].astype(BF16), sh_w2[l].astype(BF16), TM=2 * ROUTE_TILE)

    max_rows = TOP_K * N + (SEG_ALIGN - 1) * n_route_tiles * N_EXPERTS + N_EXPERTS * (CAP + EXPERT_TILE)
    t_max = -(-max_rows // EXPERT_TILE)
    cnt = cnt[:, 0].reshape(n_route_tiles, N_EXPERTS).astype(jnp.int32)
    base, nwin, used_end, row_end, tile_e, n_tiles = _segment_plan(cnt, t_max)

    xs = _dispatch(xn, rank_t, base, nwin, used_end, row_end, n_rows=t_max * EXPERT_TILE, TM=ROUTE_TILE)
    ys = _experts(xs, tile_e, n_tiles, ex_w1[l], ex_w3[l], ex_w2[l])
    y_prompt, y_sample = _final(
        x1sh, rank_t, gates_t, ys, p_prompt[l].reshape(B * T, PLE_DIM), p_sample[l].reshape(Bs * Ts, PLE_DIM),
        norm_ple_g[l][None, :], w_ple_gate[l].astype(BF16), w_ple_proj[l].astype(BF16),
        final_norm_g[None, :], base, nwin, TM=ROUTE_TILE)
    return (y_prompt.reshape(B, T, D_MODEL), y_sample.reshape(Bs, Ts, D_MODEL),
            Cp[None], Np[None], Mp[..., 0][None], Bp[None],
            Cs[None], Ns[None], Ms[..., 0][None], Bs_[None])
```
